```python
import math
import jax, jax.numpy as jnp
from jax import lax
import numpy as np

D_MODEL = 1024
BATCH = 4
SEQ = 4096
DEPTH = 2

GRID_W = 64
CTX_LEN = 256
N_GROUPS = 4
BRANCH_W = D_MODEL // N_GROUPS
D_MIX = N_GROUPS * BRANCH_W
HEAD_DIM = 64
EPS = 1e-6
ROPE_BASE = 10000.0
GLA_HEADS = BRANCH_W // HEAD_DIM
GLA_DV = HEAD_DIM
GLA_DK = HEAD_DIM // 2
GLA_RANK = 16
GLA_TAU = 16.0
GLA_CHUNK = 64
FNET_GROUPS = 4
FNET_GW = BRANCH_W // FNET_GROUPS
SWA_HEADS = BRANCH_W // HEAD_DIM
SWA_KV_HEADS = 2
SWA_GROUP = SWA_HEADS // SWA_KV_HEADS
SWA_WINDOW = 128
SWA_BLOCK = 128
NA_HEADS = BRANCH_W // HEAD_DIM
NA_KH_MAX = 8
NA_KW = 16

PROJ_SPLITS = (
    ("a_q", GLA_HEADS * GLA_DK), ("a_k", GLA_HEADS * GLA_DK), ("a_v", GLA_HEADS * GLA_DV),
    ("a_g", BRANCH_W), ("a_lr", 2 * GLA_RANK),
    ("b_v", BRANCH_W), ("b_g", BRANCH_W),
    ("c_q", SWA_HEADS * HEAD_DIM), ("c_k", SWA_KV_HEADS * HEAD_DIM), ("c_v", SWA_KV_HEADS * HEAD_DIM),
    ("c_g", BRANCH_W),
    ("d_q", BRANCH_W), ("d_k", BRANCH_W), ("d_v", BRANCH_W), ("d_g", BRANCH_W),
)
PROJ_WIDTH = (4 * GLA_HEADS * GLA_DK + 2 * GLA_RANK + 2 * BRANCH_W
              + 2 * BRANCH_W
              + 2 * BRANCH_W + 2 * SWA_KV_HEADS * HEAD_DIM
              + 4 * BRANCH_W)

kernel_name = "hybrid_parallel_group_flow_trunk"


def rms_norm(x, w):
    xf = x.astype(jnp.float32)
    y = xf * lax.rsqrt(jnp.mean(xf * xf, axis=-1, keepdims=True) + EPS)
    return (y * w).astype(x.dtype)


def split_proj(p):
    out, off = {}, 0
    for name, w in PROJ_SPLITS:
        out[name] = p[..., off:off + w]
        off += w
    return out


def to_heads(t, n_heads):
    b_, L, w = t.shape
    return t.reshape(b_, L, n_heads, w // n_heads).transpose(0, 2, 1, 3)


def from_heads(t):
    b_, h, L, d = t.shape
    return t.transpose(0, 2, 1, 3).reshape(b_, L, h * d)


def axial_rope_tables(n):
    t = jnp.arange(n)
    row = (t // GRID_W).astype(jnp.float32)
    col = (t % GRID_W).astype(jnp.float32)
    axis_dim = HEAD_DIM // 2
    inv = ROPE_BASE ** (-jnp.arange(0, axis_dim, 2, dtype=jnp.float32) / axis_dim)
    ang = jnp.stack([row[:, None] * inv, col[:, None] * inv], axis=1)
    return jnp.cos(ang), jnp.sin(ang)


def apply_rope(x, cos, sin):
    shp = x.shape
    xr = x.astype(jnp.float32).reshape(shp[:-1] + (2, 2, HEAD_DIM // 4))
    x1, x2 = xr[..., 0, :], xr[..., 1, :]
    o1 = x1 * cos - x2 * sin
    o2 = x2 * cos + x1 * sin
    return jnp.stack([o1, o2], axis=-2).reshape(shp).astype(x.dtype)


def gla_chunk_scan(q, k, v, log_a, s0):
    b_, h_, L, _ = q.shape
    nc = L // GLA_CHUNK

    def to_chunks(t):
        t = t.astype(jnp.float32).reshape(b_, h_, nc, GLA_CHUNK, t.shape[-1])
        return jnp.moveaxis(t, 2, 0)

    causal = jnp.tril(jnp.ones((GLA_CHUNK, GLA_CHUNK), bool))

    def step(s, inp):
        qc, kc, vc, ac = inp
        cum = jnp.cumsum(ac, axis=2)
        inter = jnp.einsum('bhtd,bhde->bhte', qc * jnp.exp(cum), s)
        diff = cum[:, :, :, None, :] - cum[:, :, None, :, :]
        decay = jnp.exp(jnp.where(causal[:, :, None], diff, -jnp.inf))
        scores = jnp.einsum('bhtd,bhsd,bhtsd->bhts', qc, kc, decay)
        intra = jnp.einsum('bhts,bhse->bhte', scores, vc)
        last = cum[:, :, -1:, :]
        s_new = (jnp.exp(last[:, :, 0, :])[..., None] * s
                 + jnp.einsum('bhsd,bhse->bhde', kc * jnp.exp(last - cum), vc))
        return s_new, inter + intra

    s_fin, out = lax.scan(step, s0, (to_chunks(q), to_chunks(k), to_chunks(v), to_chunks(log_a)))
    out = jnp.moveaxis(out, 0, 2).reshape(b_, h_, L, v.shape[-1])
    return out, s_fin


def gla_mixer(pl, pc, dec_w, dec_b, out_norm, need_ctx):
    def prep(p):
        q = to_heads(p['a_q'], GLA_HEADS) * (GLA_DK ** -0.5)
        k = to_heads(p['a_k'], GLA_HEADS)
        v = to_heads(p['a_v'], GLA_HEADS)
        lr = p['a_lr']
        log_a = [to_heads(jax.nn.log_sigmoid(
            (lr[..., d * GLA_RANK:(d + 1) * GLA_RANK] @ dec_w[d] + dec_b[d]).astype(jnp.float32)) / GLA_TAU,
            GLA_HEADS) for d in range(2)]
        return q, k, v, log_a

    ql, kl, vl, al = prep(pl)
    qc, kc, vc, ac = prep(pc)
    b_ = ql.shape[0]
    s0 = jnp.zeros((b_, GLA_HEADS, GLA_DK, GLA_DV), jnp.float32)
    out_l = jnp.zeros(vl.shape, jnp.float32)
    out_c = jnp.zeros(vc.shape, jnp.float32)
    for d in range(2):
        flip = (lambda t: t) if d == 0 else (lambda t: jnp.flip(t, axis=2))
        o_c, s_c = gla_chunk_scan(flip(qc), flip(kc), flip(vc), flip(ac[d]), s0)
        o_l, _ = gla_chunk_scan(flip(ql), flip(kl), flip(vl), flip(al[d]), s_c)
        out_l = out_l + flip(o_l)
        if need_ctx:
            out_c = out_c + flip(o_c)
    y_l = from_heads(rms_norm(out_l, out_norm)).astype(pl['a_v'].dtype)
    y_c = from_heads(rms_norm(out_c, out_norm)).astype(pc['a_v'].dtype) if need_ctx else None
    return y_l, y_c


def fourier_mix(v, w_f):
    b_, L, _ = v.shape
    vg = v.astype(jnp.float32).reshape(b_, L, FNET_GROUPS, FNET_GW)
    f = jnp.fft.fft2(vg, axes=(1, 3), norm='ortho').real
    return (f.reshape(b_, L, BRANCH_W) @ w_f).astype(v.dtype)


def swa_mixer(pl, pc, q_norm, k_norm, sink, cos, sin, need_ctx):
    b_, n, _ = pl['c_q'].shape
    scale = HEAD_DIM ** -0.5
    nb = n // SWA_BLOCK
    q = apply_rope(rms_norm(to_heads(pl['c_q'], SWA_HEADS), q_norm), cos, sin)
    k = apply_rope(rms_norm(to_heads(pl['c_k'], SWA_KV_HEADS), k_norm), cos, sin)
    v = to_heads(pl['c_v'], SWA_KV_HEADS)
    qcx = rms_norm(to_heads(pc['c_q'], SWA_HEADS), q_norm)
    kcx = rms_norm(to_heads(pc['c_k'], SWA_KV_HEADS), k_norm)
    vcx = to_heads(pc['c_v'], SWA_KV_HEADS)

    def band(t):
        tp = jnp.pad(t, ((0, 0), (0, 0), (SWA_BLOCK, SWA_BLOCK), (0, 0)))
        tb = tp.reshape(b_, SWA_KV_HEADS, nb + 2, SWA_BLOCK, HEAD_DIM)
        return jnp.concatenate([tb[:, :, :-2], tb[:, :, 1:-1], tb[:, :, 2:]], axis=3)

    k_win, v_win = band(k), band(v)
    qb = q.reshape(b_, SWA_KV_HEADS, SWA_GROUP, nb, SWA_BLOCK, HEAD_DIM)
    s_loc = jnp.einsum('bkgnqd,bknjd->bkgnqj', qb, k_win).astype(jnp.float32) * scale
    qpos = jnp.arange(nb)[:, None] * SWA_BLOCK + jnp.arange(SWA_BLOCK)[None, :]
    kpos = (jnp.arange(nb)[:, None] - 1) * SWA_BLOCK + jnp.arange(3 * SWA_BLOCK)[None, :]
    valid = ((jnp.abs(qpos[:, :, None] - kpos[:, None, :]) <= SWA_WINDOW)
             & (kpos[:, None, :] >= 0) & (kpos[:, None, :] < n))
    s_loc = jnp.where(valid, s_loc, -jnp.inf)
    s_ctx = jnp.einsum('bkgnqd,bkcd->bkgnqc', qb, kcx).astype(jnp.float32) * scale
    sink_l = jnp.broadcast_to(sink.astype(jnp.float32).reshape(1, SWA_KV_HEADS, SWA_GROUP, 1, 1, 1),
                              s_loc.shape[:-1] + (1,))
    p = jax.nn.softmax(jnp.concatenate([s_loc, s_ctx, sink_l], axis=-1), axis=-1).astype(v.dtype)
    nw = 3 * SWA_BLOCK
    o = (jnp.einsum('bkgnqj,bknjd->bkgnqd', p[..., :nw], v_win)
         + jnp.einsum('bkgnqc,bkcd->bkgnqd', p[..., nw:nw + CTX_LEN], vcx))
    y_l = from_heads(o.reshape(b_, SWA_HEADS, n, HEAD_DIM))
    y_c = None
    if need_ctx:
        qcb = qcx.reshape(b_, SWA_KV_HEADS, SWA_GROUP, CTX_LEN, HEAD_DIM)
        s_cc = jnp.einsum('bkgqd,bkcd->bkgqc', qcb, kcx).astype(jnp.float32) * scale
        sink_c = jnp.broadcast_to(sink.astype(jnp.float32).reshape(1, SWA_KV_HEADS, SWA_GROUP, 1, 1),
                                  s_cc.shape[:-1] + (1,))
        pc_ = jax.nn.softmax(jnp.concatenate([s_cc, sink_c], axis=-1), axis=-1).astype(v.dtype)
        o_c = jnp.einsum('bkgqc,bkcd->bkgqd', pc_[..., :CTX_LEN], vcx)
        y_c = from_heads(o_c.reshape(b_, SWA_HEADS, CTX_LEN, HEAD_DIM))
    return y_l, y_c


def na_mixer(pl, pc, q_norm, k_norm, rel_bias, need_ctx):
    b_, n, _ = pl['d_q'].shape
    rows = n // GRID_W
    kh = min(NA_KH_MAX, rows)
    scale = HEAD_DIM ** -0.5
    q = rms_norm(to_heads(pl['d_q'], NA_HEADS), q_norm)
    k = rms_norm(to_heads(pl['d_k'], NA_HEADS), k_norm)
    v = to_heads(pl['d_v'], NA_HEADS)
    qcx = rms_norm(to_heads(pc['d_q'], NA_HEADS), q_norm)
    kcx = rms_norm(to_heads(pc['d_k'], NA_HEADS), k_norm)
    vcx = to_heads(pc['d_v'], NA_HEADS)

    grid = lambda t: t.reshape(b_, NA_HEADS, rows, GRID_W, HEAD_DIM)
    qg, kg, vg = grid(q), grid(k), grid(v)
    r = jnp.arange(rows)
    row_start = jnp.clip(r - kh // 2, 0, rows - kh)
    row_idx = row_start[:, None] + jnp.arange(kh)[None, :]
    k_rows = kg[:, :, row_idx]
    v_rows = vg[:, :, row_idx]
    s_nb = jnp.einsum('bhrqd,bhrkwd->bhrqkw', qg, k_rows).astype(jnp.float32) * scale
    cq = jnp.arange(GRID_W)
    col_start = jnp.clip(cq - NA_KW // 2, 0, GRID_W - NA_KW)
    col_ok = (cq[None, :] >= col_start[:, None]) & (cq[None, :] < col_start[:, None] + NA_KW)
    dy = row_idx - r[:, None] + (NA_KH_MAX - 1)
    dx = jnp.clip(cq[None, :] - cq[:, None], -(NA_KW - 1), NA_KW - 1) + (NA_KW - 1)
    bias = rel_bias[:, dy[:, None, :, None], dx[None, :, None, :]].astype(jnp.float32)
    s_nb = jnp.where(col_ok[:, None, :], s_nb + bias, -jnp.inf)
    s_nb = s_nb.reshape(b_, NA_HEADS, rows, GRID_W, kh * GRID_W)
    s_ctx = jnp.einsum('bhrqd,bhcd->bhrqc', qg, kcx).astype(jnp.float32) * scale
    p = jax.nn.softmax(jnp.concatenate([s_nb, s_ctx], axis=-1), axis=-1).astype(v.dtype)
    p_nb = p[..., :kh * GRID_W].reshape(b_, NA_HEADS, rows, GRID_W, kh, GRID_W)
    o = (jnp.einsum('bhrqkw,bhrkwd->bhrqd', p_nb, v_rows)
         + jnp.einsum('bhrqc,bhcd->bhrqd', p[..., kh * GRID_W:], vcx))
    y_l = from_heads(o.reshape(b_, NA_HEADS, n, HEAD_DIM))
    y_c = None
    if need_ctx:
        s_cc = jnp.einsum('bhqd,bhcd->bhqc', qcx, kcx).astype(jnp.float32) * scale
        o_c = jnp.einsum('bhqc,bhcd->bhqd', jax.nn.softmax(s_cc, axis=-1).astype(v.dtype), vcx)
        y_c = from_heads(o_c)
    return y_l, y_c


def hybrid_layer(x, ctx, c, c_ctx, norm_w, ada_w, ada_b, w_in, w_out, gla_dec_w, gla_dec_b, gla_out_norm,
                 fnet_w, swa_q_norm, swa_k_norm, swa_sink, na_q_norm, na_k_norm, na_rel_bias,
                 cos, sin, need_ctx):
    shift_l, scale_l, gate_l = jnp.split(jax.nn.silu(c) @ ada_w + ada_b, 3, axis=-1)
    shift_c, scale_c, gate_c = jnp.split(jax.nn.silu(c_ctx) @ ada_w + ada_b, 3, axis=-1)
    h_l = rms_norm(x, norm_w) * (1.0 + scale_l[:, None, :]) + shift_l[:, None, :]
    h_c = rms_norm(ctx, norm_w) * (1.0 + scale_c) + shift_c
    pl = split_proj(h_l @ w_in)
    pc = split_proj(h_c @ w_in)

    a_l, a_c = gla_mixer(pl, pc, gla_dec_w, gla_dec_b, gla_out_norm, need_ctx)
    b_l = fourier_mix(pl['b_v'], fnet_w)
    c_l, c_c = swa_mixer(pl, pc, swa_q_norm, swa_k_norm, swa_sink, cos, sin, need_ctx)
    d_l, d_c = na_mixer(pl, pc, na_q_norm, na_k_norm, na_rel_bias, need_ctx)

    y_l = jnp.concatenate([a_l * jax.nn.silu(pl['a_g']), b_l * jax.nn.silu(pl['b_g']),
                           c_l * jax.nn.silu(pl['c_g']), d_l * jax.nn.silu(pl['d_g'])], axis=-1) @ w_out
    x = x + gate_l[:, None, :] * y_l
    if need_ctx:
        b_c = fourier_mix(pc['b_v'], fnet_w)
        y_c = jnp.concatenate([a_c * jax.nn.silu(pc['a_g']), b_c * jax.nn.silu(pc['b_g']),
                               c_c * jax.nn.silu(pc['c_g']), d_c * jax.nn.silu(pc['d_g'])], axis=-1) @ w_out
        ctx = ctx + gate_c * y_c
    return x, ctx


def setup_inputs(seed: int = 0) -> dict:
    key = jax.random.key(seed)
    ks = jax.random.split(key, 20)

    def nrm(k, shape, s):
        return jax.random.normal(k, shape, jnp.float32) * s

    return {
        "x": nrm(ks[0], (BATCH, SEQ, D_MODEL), 1.0),
        "c": nrm(ks[1], (BATCH, D_MODEL), 1.0),
        "ctx": nrm(ks[2], (BATCH, CTX_LEN, D_MODEL), 1.0),
        "c_ctx": nrm(ks[3], (D_MODEL,), 1.0),
        "norm_w": 1.0 + nrm(ks[4], (DEPTH, D_MODEL), 0.02),
        "ada_w": nrm(ks[5], (DEPTH, D_MODEL, 3 * D_MODEL), 0.5 * D_MODEL ** -0.5),
        "ada_b": nrm(ks[6], (DEPTH, 3 * D_MODEL), 0.02),
        "w_in": nrm(ks[7], (DEPTH, D_MODEL, PROJ_WIDTH), D_MODEL ** -0.5),
        "w_out": nrm(ks[8], (DEPTH, D_MIX, D_MODEL), D_MIX ** -0.5),
        "gla_dec_w": nrm(ks[9], (DEPTH, 2, GLA_RANK, GLA_HEADS * GLA_DK), GLA_RANK ** -0.5),
        "gla_dec_b": nrm(ks[10], (DEPTH, 2, GLA_HEADS * GLA_DK), 0.5),
        "gla_out_norm": 1.0 + nrm(ks[11], (DEPTH, GLA_DV), 0.02),
        "fnet_w": nrm(ks[12], (DEPTH, BRANCH_W, BRANCH_W), BRANCH_W ** -0.5),
        "swa_q_norm": 1.0 + nrm(ks[13], (DEPTH, HEAD_DIM), 0.02),
        "swa_k_norm": 1.0 + nrm(ks[14], (DEPTH, HEAD_DIM), 0.02),
        "swa_sink": nrm(ks[15], (DEPTH, SWA_HEADS), 0.5),
        "na_q_norm": 1.0 + nrm(ks[16], (DEPTH, HEAD_DIM), 0.02),
        "na_k_norm": 1.0 + nrm(ks[17], (DEPTH, HEAD_DIM), 0.02),
        "na_rel_bias": nrm(ks[18], (DEPTH, NA_HEADS, 2 * NA_KH_MAX - 1, 2 * NA_KW - 1), 0.1),
    }


def reference(x, c, ctx, c_ctx, norm_w, ada_w, ada_b, w_in, w_out, gla_dec_w, gla_dec_b, gla_out_norm,
              fnet_w, swa_q_norm, swa_k_norm, swa_sink, na_q_norm, na_k_norm, na_rel_bias):
    cos, sin = axial_rope_tables(x.shape[1])
    for i in range(DEPTH):
        x, ctx = hybrid_layer(x, ctx, c, c_ctx, norm_w[i], ada_w[i], ada_b[i], w_in[i], w_out[i],
                              gla_dec_w[i], gla_dec_b[i], gla_out_norm[i], fnet_w[i],
                              swa_q_norm[i], swa_k_norm[i], swa_sink[i],
                              na_q_norm[i], na_k_norm[i], na_rel_bias[i],
                              cos, sin, need_ctx=(i < DEPTH - 1))
    return x
```

```python
import functools
import math

import numpy as np
import jax
import jax.numpy as jnp
from jax import lax
from jax.experimental import pallas as pl
from jax.experimental.pallas import tpu as pltpu

F32 = jnp.float32
BF16 = jnp.bfloat16

GRID_W = 64
CTX_LEN = 256
HEAD_DIM = 64
BRANCH_W = 256
EPS = 1e-6
ROPE_BASE = 10000.0
GLA_HEADS = 4
GLA_DK = 32
GLA_DV = 64
GLA_RANK = 16
GLA_TAU = 16.0
GLA_CHUNK = 64
GLA_SUB = 16
FNET_GW = 64
SWA_BLOCK = 128
SWA_WINDOW = 128
NA_KH = 8
NA_KW = 16
NA_HEADS = 4

TM = 256
VMEM_LIMIT = 48 * 1024 * 1024

C_A = 0
C_BV = 512
C_CQ = 768
C_CK = 1024
C_CV = 1280
C_DQ = 1536
C_DK = 1792
C_DV = 2048
C_G = 2304
C_LR = 3328
W_COLS = 3456


def _dot(a, b):
    return jnp.dot(a, b, preferred_element_type=F32)


def _dot_nt(a, b):
    return lax.dot_general(a, b, (((1,), (1,)), ((), ())), preferred_element_type=F32)


def _dot_tn(a, b):
    return lax.dot_general(a, b, (((0,), (0,)), ((), ())), preferred_element_type=F32)


def _cparams(sem, vmem=None):
    return pltpu.CompilerParams(dimension_semantics=sem, vmem_limit_bytes=vmem)


def _ada_kernel(c_ref, w_ref, b_ref, o_ref):
    c = c_ref[...]
    s = c * (1.0 / (1.0 + jnp.exp(-c)))
    o_ref[...] = _dot(s.astype(BF16), w_ref[...]) + b_ref[...]


def _ada(cc, ada_w, ada_b):
    r, d = cc.shape
    n = ada_w.shape[1]
    return pl.pallas_call(
        _ada_kernel,
        out_shape=jax.ShapeDtypeStruct((r, n), F32),
        grid=(1,),
        in_specs=[pl.BlockSpec((r, d), lambda i: (0, 0)),
                  pl.BlockSpec((d, n), lambda i: (0, 0)),
                  pl.BlockSpec((1, n), lambda i: (0, 0))],
        out_specs=pl.BlockSpec((r, n), lambda i: (0, 0)),
        compiler_params=_cparams(("arbitrary",), VMEM_LIMIT),
        name="ada",
    )(cc, ada_w, ada_b)


def _head_rms(x, seg_ones, w):
    ms = _dot((x * x).astype(BF16), seg_ones) * (1.0 / HEAD_DIM)
    return x * lax.rsqrt(ms + EPS) * w


def _rope(y, cos, sin_signed):
    lane = lax.broadcasted_iota(jnp.int32, y.shape, 1)
    first_half = (lane % 32) < 16
    n = y.shape[1]
    swapped = jnp.where(first_half, pltpu.roll(y, n - 16, 1), pltpu.roll(y, 16, 1))
    return y * cos + swapped * sin_signed


def _inproj_kernel(x_ref, ctx_ref, mod_ref, nw_ref, w_ref, cos_ref, sin_ref, hn_ref, seg_ref,
                   wdec_ref, bdec_ref,
                   a_ref, la_ref, bv_ref, cq_ref, ck_ref, cv_ref, dq_ref, dk_ref, dv_ref, g_ref,
                   xs_ref):
    i = pl.program_id(1)

    @pl.when(i == 0)
    def _():
        xs_ref[...] = ctx_ref[0]

    @pl.when(i > 0)
    def _():
        xs_ref[...] = x_ref[0]

    x = xs_ref[...]
    ms = jnp.mean(x * x, axis=-1, keepdims=True)
    y = x * lax.rsqrt(ms + EPS) * nw_ref[...]
    h = y * (1.0 + mod_ref[0, 1:2, :]) + mod_ref[0, 0:1, :]
    hb = h.astype(BF16)

    def proj(c0, width):
        return _dot(hb, w_ref[:, c0:c0 + width])

    seg = seg_ref[...]
    cos = cos_ref[...]
    sin = sin_ref[...]

    a = proj(C_A, 512)
    a_ref[0, :, 0:128] = a[:, 0:128] * (GLA_DK ** -0.5)
    a_ref[0, :, 128:512] = a[:, 128:512]
    bv_ref[0] = proj(C_BV, 256).astype(BF16)
    cq_ref[0] = _rope(_head_rms(proj(C_CQ, 256), seg, hn_ref[0:1, :]), cos, sin).astype(BF16)
    ck_ref[0] = _rope(_head_rms(proj(C_CK, 256), seg, hn_ref[1:2, :]), cos, sin).astype(BF16)
    cv_ref[0] = proj(C_CV, 256).astype(BF16)
    dq_ref[0] = _head_rms(proj(C_DQ, 256), seg, hn_ref[2:3, :]).astype(BF16)
    dk_ref[0] = _head_rms(proj(C_DK, 256), seg, hn_ref[3:4, :]).astype(BF16)
    dv_ref[0] = proj(C_DV, 256).astype(BF16)
    g = proj(C_G, 1024)
    g_ref[0] = g * (1.0 / (1.0 + jnp.exp(-g)))
    lr = proj(C_LR, 128)
    z = _dot(lr.astype(BF16), wdec_ref[...]) + bdec_ref[...]
    la_ref[0] = (jnp.minimum(z, 0.0) - jnp.log(1.0 + jnp.exp(-jnp.abs(z)))) * (1.0 / GLA_TAU)


def _inproj(x, ctx, mod, norm_w, w_perm, cos_t, sin_t, head_norms, seg_ones, wdec, bdec):
    b, seq, d = x.shape
    t_all = CTX_LEN + seq
    nblk = t_all // TM

    def tok(width):
        return pl.BlockSpec((1, TM, width), lambda bi, i: (bi, i, 0))

    def const(shape):
        return pl.BlockSpec(shape, lambda bi, i: (0,) * len(shape))

    def out(width, dtype):
        return jax.ShapeDtypeStruct((b, t_all, width), dtype)

    return pl.pallas_call(
        _inproj_kernel,
        out_shape=(out(512, F32), out(256, F32), out(256, BF16), out(256, BF16), out(256, BF16),
                   out(256, BF16), out(256, BF16), out(256, BF16), out(256, BF16), out(1024, F32)),
        grid=(b, nblk),
        in_specs=[
            pl.BlockSpec((1, TM, d), lambda bi, i: (bi, jnp.maximum(i - 1, 0), 0)),
            pl.BlockSpec((1, TM, d), lambda bi, i: (bi, 0, 0)),
            pl.BlockSpec((1, 3, d), lambda bi, i: (jnp.where(i == 0, b, bi), 0, 0)),
            const((1, d)),
            const((d, W_COLS)),
            pl.BlockSpec((TM, 256), lambda bi, i: (i, 0)),
            pl.BlockSpec((TM, 256), lambda bi, i: (i, 0)),
            const((8, 256)),
            const((256, 256)),
            const((128, 256)),
            const((1, 256)),
        ],
        out_specs=(tok(512), tok(256), tok(256), tok(256), tok(256), tok(256), tok(256), tok(256),
                   tok(256), tok(1024)),
        scratch_shapes=[pltpu.VMEM((TM, d), F32)],
        compiler_params=_cparams(("parallel", "arbitrary"), VMEM_LIMIT),
        name="inproj",
    )(x, ctx, mod, norm_w, w_perm, cos_t, sin_t, head_norms, seg_ones, wdec, bdec)


def _gla_kernel(a_ref, la_ref, ones2_ref, o_ref, st_ref, *, rev):
    c, sb = GLA_CHUNK, GLA_SUB
    nsub = c // sb
    hk = GLA_HEADS * GLA_DK
    hv = GLA_HEADS * GLA_DV

    @pl.when(pl.program_id(1) == 0)
    def _():
        st_ref[...] = jnp.zeros_like(st_ref)

    q = a_ref[0, :, 0:hk]
    k = a_ref[0, :, hk:2 * hk]
    v = a_ref[0, :, 2 * hk:2 * hk + hv]
    la = la_ref[0]

    r_i = lax.broadcasted_iota(jnp.int32, (c, c), 0)
    c_i = lax.broadcasted_iota(jnp.int32, (c, c), 1)
    tri = jnp.where((c_i >= r_i) if rev else (c_i <= r_i), 1.0, 0.0).astype(BF16)
    la_hi = la.astype(BF16)
    la_lo = (la - la_hi.astype(F32)).astype(BF16)
    cum = _dot(tri, la_hi) + _dot(tri, la_lo)
    cumx = cum - la
    end = 0 if rev else c - 1
    cum_end = cum[end:end + 1, :]

    def first_row(blk):
        return sb * blk + (sb - 1 if rev else 0)

    def blk_of(p):
        return nsub - 1 - p if rev else p

    cref = [cumx[first_row(blk):first_row(blk) + 1, :] for blk in range(nsub)]
    cref_rows = jnp.concatenate([jnp.broadcast_to(cr, (sb, hk)) for cr in cref], axis=0)

    st = st_ref[...]
    qi = q * jnp.exp(cum)
    inter = _dot_nt(qi.astype(BF16), st.astype(BF16))

    qd = q * jnp.exp(cum - cref_rows)
    lane_k = lax.broadcasted_iota(jnp.int32, (c, hk), 1) // GLA_DK
    q_st = jnp.concatenate([jnp.where(lane_k == h, qd, 0.0) for h in range(GLA_HEADS)], axis=0)
    kt, vt = [], []
    for p in range(1, nsub):
        blk = blk_of(p)
        lo, hi = (c - sb * p, c) if rev else (0, sb * p)
        kt.append(k[lo:hi] * jnp.exp(cref[blk] - cum[lo:hi]))
        vt.append(v[lo:hi])
    npast = sb * (nsub * (nsub - 1) // 2)
    npad = 128 - npast
    kt.append(jnp.zeros((npad, hk), F32))
    vt.append(jnp.zeros((npad, hv), F32))
    kt = jnp.concatenate(kt, axis=0).astype(BF16)
    vt = jnp.concatenate(vt, axis=0).astype(BF16)
    sc = _dot_nt(q_st.astype(BF16), kt)
    row = lax.broadcasted_iota(jnp.int32, sc.shape, 0)
    col = lax.broadcasted_iota(jnp.int32, sc.shape, 1)
    row_blk = (row % c) // sb
    row_p = (nsub - 1 - row_blk) if rev else row_blk
    col_p = jnp.where(col < sb, 1, jnp.where(col < 3 * sb, 2, jnp.where(col < 6 * sb, 3, -1)))
    sc = jnp.where(row_p == col_p, sc, 0.0)
    off_st = _dot(sc.astype(BF16), vt)
    lane_v = lax.broadcasted_iota(jnp.int32, (c, hv), 1) // GLA_DV
    off = jnp.where(lane_v == 0, off_st[0:c], 0.0)
    for h in range(1, GLA_HEADS):
        off = off + jnp.where(lane_v == h, off_st[h * c:(h + 1) * c], 0.0)

    ones2 = ones2_ref[...]
    t_loc = lax.broadcasted_iota(jnp.int32, (sb, hk), 0)
    for blk in range(nsub):
        r0 = sb * blk
        cb = cum[r0:r0 + sb]
        qb = q[r0:r0 + sb]
        es = []
        for s in range(sb):
            valid = (t_loc <= s) if rev else (t_loc >= s)
            diff = jnp.where(valid, cb - cum[r0 + s:r0 + s + 1, :], -jnp.inf)
            es.append(jnp.exp(diff) * (qb * k[r0 + s:r0 + s + 1, :]))
        e = jnp.concatenate(es, axis=0).astype(BF16)
        r = _dot(e, ones2)
        dg = r[0:sb] * v[r0:r0 + 1, :]
        for s in range(1, sb):
            dg = dg + r[s * sb:(s + 1) * sb] * v[r0 + s:r0 + s + 1, :]
        o_ref[0, r0:r0 + sb, :] = inter[r0:r0 + sb] + off[r0:r0 + sb] + dg

    kd = k * jnp.exp(cum_end - cum)
    kv = _dot_tn(v.astype(BF16), kd.astype(BF16))
    row_h = lax.broadcasted_iota(jnp.int32, (hv, hk), 0) // GLA_DV
    col_h = lax.broadcasted_iota(jnp.int32, (hv, hk), 1) // GLA_DK
    st_ref[...] = st * jnp.exp(cum_end) + jnp.where(row_h == col_h, kv, 0.0)


def _gla(a, la, ones2, rev):
    b, t_all, _ = a.shape
    nch = t_all // GLA_CHUNK
    nctx = CTX_LEN // GLA_CHUNK
    direction = 1 if rev else 0

    def chunk(j):
        if not rev:
            return j
        return jnp.where(j < nctx, nctx - 1 - j, nch - 1 - (j - nctx))

    return pl.pallas_call(
        functools.partial(_gla_kernel, rev=rev),
        out_shape=jax.ShapeDtypeStruct((b, t_all, 256), F32),
        grid=(b, nch),
        in_specs=[pl.BlockSpec((1, GLA_CHUNK, 512), lambda bi, j: (bi, chunk(j), 0)),
                  pl.BlockSpec((1, GLA_CHUNK, 128), lambda bi, j: (bi, chunk(j), direction)),
                  pl.BlockSpec((128, 256), lambda bi, j: (0, 0))],
        out_specs=pl.BlockSpec((1, GLA_CHUNK, 256), lambda bi, j: (bi, chunk(j), 0)),
        scratch_shapes=[pltpu.VMEM((256, 128), F32)],
        compiler_params=_cparams(("parallel", "arbitrary")),
        name="gla_bwd" if rev else "gla_fwd",
    )(a, la, ones2)


def _fnet_chan_kernel(v_ref, wc_ref, ws_ref, o_ref):
    v = v_ref[0]
    o_ref[0, :, :] = _dot(v, wc_ref[...]).astype(BF16)
    o_ref[1, :, :] = _dot(v, ws_ref[...]).astype(BF16)


def _fnet_chan(bv, wc, ws, row_blk_off, length):
    b = bv.shape[0]
    tl = min(TM, length)
    return pl.pallas_call(
        _fnet_chan_kernel,
        out_shape=jax.ShapeDtypeStruct((2, length, b * 256), BF16),
        grid=(b, length // tl),
        in_specs=[pl.BlockSpec((1, tl, 256), lambda bi, i: (bi, i + row_blk_off, 0)),
                  pl.BlockSpec((256, 256), lambda bi, i: (0, 0)),
                  pl.BlockSpec((256, 256), lambda bi, i: (0, 0))],
        out_specs=pl.BlockSpec((2, tl, 256), lambda bi, i: (0, i, bi)),
        compiler_params=_cparams(("parallel", "arbitrary")),
        name="fnet_chan",
    )(bv, wc, ws)


def _fnet_pos_kernel(d_ref, y_ref, wf_ref, o_ref, acc_ref, *, nb):
    kk = pl.program_id(1)

    @pl.when(kk == 0)
    def _():
        acc_ref[...] = jnp.zeros_like(acc_ref)

    acc_ref[...] += _dot(d_ref[...], y_ref[...])

    @pl.when(kk == pl.num_programs(1) - 1)
    def _():
        wf = wf_ref[...]
        for bi in range(nb):
            o_ref[bi] = _dot(acc_ref[:, bi * 256:(bi + 1) * 256].astype(BF16), wf)


def _fnet_pos(dmat, y2, wf, nb):
    length, k2 = dmat.shape
    tm = min(512, length)
    tk = min(1024, k2)
    return pl.pallas_call(
        functools.partial(_fnet_pos_kernel, nb=nb),
        out_shape=jax.ShapeDtypeStruct((nb, length, 256), F32),
        grid=(length // tm, k2 // tk),
        in_specs=[pl.BlockSpec((tm, tk), lambda i, kk: (i, kk)),
                  pl.BlockSpec((tk, nb * 256), lambda i, kk: (kk, 0)),
                  pl.BlockSpec((256, 256), lambda i, kk: (0, 0))],
        out_specs=pl.BlockSpec((nb, tm, 256), lambda i, kk: (0, i, 0)),
        scratch_shapes=[pltpu.VMEM((tm, nb * 256), F32)],
        compiler_params=_cparams(("parallel", "arbitrary"), VMEM_LIMIT),
        name="fnet_pos",
    )(dmat, y2, wf)


def _dft_tables(length):
    j = jnp.arange(length, dtype=jnp.int32)
    jk = (j[:, None] * j[None, :]) % length
    ang = jk.astype(F32) * (2.0 * math.pi / length)
    s = 1.0 / math.sqrt(length)
    return jnp.concatenate([jnp.cos(ang) * s, -jnp.sin(ang) * s], axis=1).astype(BF16)


def _chan_tables():
    j = np.arange(FNET_GW)
    ang = 2.0 * np.pi * ((j[:, None] * j[None, :]) % FNET_GW) / FNET_GW
    s = 1.0 / math.sqrt(FNET_GW)
    eye = np.eye(BRANCH_W // FNET_GW)
    wc = np.kron(eye, np.cos(ang) * s)
    ws = np.kron(eye, np.sin(ang) * s)
    return jnp.asarray(wc, F32).astype(BF16), jnp.asarray(ws, F32).astype(BF16)


def _fnet(bv, wf, row_blk_off, length):
    b = bv.shape[0]
    wc, ws = _chan_tables()
    y = _fnet_chan(bv, wc, ws, row_blk_off, length)
    y2 = y.reshape(2 * length, b * 256)
    return _fnet_pos(_dft_tables(length), y2, wf, b)


def _stack_heads(qg):
    lane = lax.broadcasted_iota(jnp.int32, qg.shape, 1)
    zero = jnp.zeros_like(qg)
    return jnp.concatenate([jnp.where(lane < HEAD_DIM, qg, zero),
                            jnp.where(lane >= HEAD_DIM, qg, zero)], axis=0)


def _unstack_heads(o, rows):
    lane = lax.broadcasted_iota(jnp.int32, (rows, 2 * HEAD_DIM), 1)
    return jnp.where(lane < HEAD_DIM, o[0:rows], o[rows:2 * rows])


def _swa_kernel(sink_ref, q_ref, kp_ref, kc_ref, kn_ref, kx_ref, vp_ref, vc_ref, vn_ref, vx_ref,
                o_ref):
    n = pl.program_id(1)
    nb = pl.num_programs(1)
    blk = SWA_BLOCK
    rows = 2 * blk
    row = lax.broadcasted_iota(jnp.int32, (rows, 3 * blk), 0) % blk
    col = lax.broadcasted_iota(jnp.int32, (rows, 3 * blk), 1)
    rel = col - blk - row
    col_lo = jnp.where(n > 0, 0, blk)
    col_hi = jnp.where(n < nb - 1, 3 * blk, 2 * blk)
    valid = (jnp.abs(rel) <= SWA_WINDOW) & (col >= col_lo) & (col < col_hi)
    head_row = lax.broadcasted_iota(jnp.int32, (rows, 1), 0) < blk
    for g in range(2):
        ls = slice(128 * g, 128 * (g + 1))
        q_st = _stack_heads(q_ref[0, :, ls])
        kw = jnp.concatenate([kp_ref[0, :, ls], kc_ref[0, :, ls], kn_ref[0, :, ls]], axis=0)
        vw = jnp.concatenate([vp_ref[0, :, ls], vc_ref[0, :, ls], vn_ref[0, :, ls]], axis=0)
        s_loc = jnp.where(valid, _dot_nt(q_st, kw), -jnp.inf)
        s_ctx = _dot_nt(q_st, kx_ref[0, :, ls])
        sink = jnp.where(head_row, sink_ref[2 * g], sink_ref[2 * g + 1])
        m = jnp.maximum(jnp.maximum(jnp.max(s_loc, axis=-1, keepdims=True),
                                    jnp.max(s_ctx, axis=-1, keepdims=True)), sink)
        p_loc = jnp.exp(s_loc - m)
        p_ctx = jnp.exp(s_ctx - m)
        den = (jnp.sum(p_loc, axis=-1, keepdims=True) + jnp.sum(p_ctx, axis=-1, keepdims=True)
               + jnp.exp(sink - m))
        o = _dot(p_loc.astype(BF16), vw) + _dot(p_ctx.astype(BF16), vx_ref[0, :, ls])
        o_ref[0, :, ls] = _unstack_heads(o / den, blk)


def _swa(sink, cq, ck, cv, seq):
    b = cq.shape[0]
    nb = seq // SWA_BLOCK
    off = CTX_LEN // SWA_BLOCK

    def at(f):
        return pl.BlockSpec((1, SWA_BLOCK, 256), lambda bi, n: (bi, f(n) + off, 0))

    prev = at(lambda n: jnp.maximum(n - 1, 0))
    cur = at(lambda n: n)
    nxt = at(lambda n: jnp.minimum(n + 1, nb - 1))
    cx = pl.BlockSpec((1, CTX_LEN, 256), lambda bi, n: (bi, 0, 0))
    return pl.pallas_call(
        _swa_kernel,
        out_shape=jax.ShapeDtypeStruct((b, seq, 256), F32),
        grid=(b, nb),
        in_specs=[pl.BlockSpec(memory_space=pltpu.SMEM), cur, prev, cur, nxt, cx, prev, cur, nxt, cx],
        out_specs=pl.BlockSpec((1, SWA_BLOCK, 256), lambda bi, n: (bi, n, 0)),
        compiler_params=_cparams(("parallel", "arbitrary")),
        name="swa",
    )(sink, cq, ck, ck, ck, ck, cv, cv, cv, cv)


def _na_kernel(q_ref, k_ref, v_ref, bias_ref, o_ref, *, rows_total):
    r = pl.program_id(1)
    kh = NA_KH
    row_start = jnp.clip(r - kh // 2, 0, rows_total - kh)
    k0 = pl.multiple_of(CTX_LEN + row_start * GRID_W, GRID_W)
    nk = kh * GRID_W
    for g in range(2):
        ls = slice(128 * g, 128 * (g + 1))
        q_st = _stack_heads(q_ref[0, :, ls])
        s_nb = _dot_nt(q_st, k_ref[0, pl.ds(k0, nk), ls]) + bias_ref[0, g]
        s_ctx = _dot_nt(q_st, k_ref[0, 0:CTX_LEN, ls])
        m = jnp.maximum(jnp.max(s_nb, axis=-1, keepdims=True), jnp.max(s_ctx, axis=-1, keepdims=True))
        p_nb = jnp.exp(s_nb - m)
        p_ctx = jnp.exp(s_ctx - m)
        den = jnp.sum(p_nb, axis=-1, keepdims=True) + jnp.sum(p_ctx, axis=-1, keepdims=True)
        o = (_dot(p_nb.astype(BF16), v_ref[0, pl.ds(k0, nk), ls])
             + _dot(p_ctx.astype(BF16), v_ref[0, 0:CTX_LEN, ls]))
        o_ref[0, :, ls] = _unstack_heads(o / den, GRID_W)


def _na_bias_table(rel_bias):
    cq = np.arange(GRID_W)
    col_start = np.clip(cq - NA_KW // 2, 0, GRID_W - NA_KW)
    col_ok = (cq[None, :] >= col_start[:, None]) & (cq[None, :] < col_start[:, None] + NA_KW)
    dx = np.clip(cq[None, :] - cq[:, None], -(NA_KW - 1), NA_KW - 1) + (NA_KW - 1)
    tabs = []
    for c in range(NA_KH):
        dy = -c + np.arange(NA_KH) + (NA_KH - 1)
        t = rel_bias[:, dy[None, :, None], dx[:, None, :]].astype(F32)
        t = jnp.where(col_ok[None, :, None, :], t, -jnp.inf)
        tabs.append(t.reshape(NA_HEADS // 2, 2 * GRID_W, NA_KH * GRID_W))
    return jnp.stack(tabs)


def _na(dq, dk, dv, bias_tab, seq):
    b, t_all, _ = dq.shape
    rows_total = seq // GRID_W

    def cls(r):
        return r - jnp.clip(r - NA_KH // 2, 0, rows_total - NA_KH)

    off = CTX_LEN // GRID_W
    full = pl.BlockSpec((1, t_all, 256), lambda bi, r: (bi, 0, 0))
    return pl.pallas_call(
        functools.partial(_na_kernel, rows_total=rows_total),
        out_shape=jax.ShapeDtypeStruct((b, seq, 256), F32),
        grid=(b, rows_total),
        in_specs=[pl.BlockSpec((1, GRID_W, 256), lambda bi, r: (bi, r + off, 0)), full, full,
                  pl.BlockSpec((1, 2, 2 * GRID_W, NA_KH * GRID_W), lambda bi, r: (cls(r), 0, 0, 0))],
        out_specs=pl.BlockSpec((1, GRID_W, 256), lambda bi, r: (bi, r, 0)),
        compiler_params=_cparams(("parallel", "arbitrary"), VMEM_LIMIT),
        name="na",
    )(dq, dk, dv, bias_tab)


def _ctx_attn_kernel(sink_ref, q_ref, k_ref, v_ref, o_ref, *, has_sink):
    n = CTX_LEN
    head_row = lax.broadcasted_iota(jnp.int32, (2 * n, 1), 0) < n
    for g in range(2):
        ls = slice(128 * g, 128 * (g + 1))
        q_st = _stack_heads(q_ref[0, :, ls])
        s = _dot_nt(q_st, k_ref[0, :, ls])
        m = jnp.max(s, axis=-1, keepdims=True)
        if has_sink:
            sink = jnp.where(head_row, sink_ref[2 * g], sink_ref[2 * g + 1])
            m = jnp.maximum(m, sink)
        p = jnp.exp(s - m)
        den = jnp.sum(p, axis=-1, keepdims=True)
        if has_sink:
            den = den + jnp.exp(sink - m)
        o = _dot(p.astype(BF16), v_ref[0, :, ls])
        o_ref[0, :, ls] = _unstack_heads(o / den, n)


def _ctx_attn(sink, q, k, v, has_sink):
    b = q.shape[0]
    blk = pl.BlockSpec((1, CTX_LEN, 256), lambda bi: (bi, 0, 0))
    return pl.pallas_call(
        functools.partial(_ctx_attn_kernel, has_sink=has_sink),
        out_shape=jax.ShapeDtypeStruct((b, CTX_LEN, 256), F32),
        grid=(b,),
        in_specs=[pl.BlockSpec(memory_space=pltpu.SMEM), blk, blk, blk],
        out_specs=blk,
        compiler_params=_cparams(("parallel",)),
        name="ctx_attn_sink" if has_sink else "ctx_attn",
    )(sink, q, k, v)


def _outproj_kernel(x_ref, of_ref, ob_ref, fz_ref, co_ref, do_ref, g_ref, mod_ref, gn_ref, seg_ref,
                    w_ref, o_ref):
    a = of_ref[0] + ob_ref[0]
    ms = _dot((a * a).astype(BF16), seg_ref[...]) * (1.0 / GLA_DV)
    a = a * lax.rsqrt(ms + EPS) * gn_ref[...]
    g = g_ref[0]
    mixed = jnp.concatenate([a * g[:, 0:256], fz_ref[0] * g[:, 256:512],
                             co_ref[0] * g[:, 512:768], do_ref[0] * g[:, 768:1024]], axis=1)
    y = _dot(mixed.astype(BF16), w_ref[...])
    o_ref[0] = x_ref[0] + mod_ref[0, 2:3, :] * y


def _outproj(x, o_f, o_b, fz, c_o, d_o, g, mod, gla_norm, seg_ones, w_out, is_ctx):
    b, rows, d = x.shape
    nblk = rows // TM
    comb = 0 if is_ctx else CTX_LEN // TM
    nbatch = b

    def own(width):
        return pl.BlockSpec((1, TM, width), lambda bi, i: (bi, i, 0))

    def combined(width):
        return pl.BlockSpec((1, TM, width), lambda bi, i: (bi, i + comb, 0))

    def const(shape):
        return pl.BlockSpec(shape, lambda bi, i: (0,) * len(shape))

    mod_spec = pl.BlockSpec((1, 3, d), (lambda bi, i: (nbatch, 0, 0)) if is_ctx else (lambda bi, i: (bi, 0, 0)))
    return pl.pallas_call(
        _outproj_kernel,
        out_shape=jax.ShapeDtypeStruct((b, rows, d), F32),
        grid=(b, nblk),
        in_specs=[own(d), combined(256), combined(256), own(256), own(256), own(256), combined(1024),
                  mod_spec, const((1, 256)), const((256, 256)), const((d, d))],
        out_specs=own(d),
        compiler_params=_cparams(("parallel", "arbitrary"), VMEM_LIMIT),
        name="outproj_ctx" if is_ctx else "outproj",
    )(x, o_f, o_b, fz, c_o, d_o, g, mod, gla_norm, seg_ones, w_out)


def _permute_w_in(w):
    s = lambda lo, hi: w[:, lo:hi]
    ck, cv = s(1568, 1696), s(1696, 1824)
    dup = lambda t: jnp.concatenate([t[:, :64], t[:, :64], t[:, 64:], t[:, 64:]], axis=1)
    pad = jnp.zeros((w.shape[0], 128 - 2 * GLA_RANK), w.dtype)
    return jnp.concatenate([
        s(0, 512),
        s(800, 1056),
        s(1312, 1568), dup(ck), dup(cv),
        s(2080, 2336), s(2336, 2592), s(2592, 2848),
        s(512, 768), s(1056, 1312), s(1824, 2080), s(2848, 3104),
        s(768, 800), pad,
    ], axis=1)


def _rope_tables(seq):
    t = np.arange(seq)
    axis_dim = HEAD_DIM // 2
    inv = ROPE_BASE ** (-np.arange(0, axis_dim, 2, dtype=np.float64) / axis_dim)
    row = (t // GRID_W).astype(np.float64)
    col = (t % GRID_W).astype(np.float64)
    ang = np.concatenate([row[:, None] * inv, row[:, None] * inv, col[:, None] * inv, col[:, None] * inv], axis=1)
    cos = np.cos(ang)
    sign = np.concatenate([-np.ones(16), np.ones(16), -np.ones(16), np.ones(16)])
    sin = np.sin(ang) * sign
    cos = np.concatenate([np.ones((CTX_LEN, HEAD_DIM)), cos], axis=0)
    sin = np.concatenate([np.zeros((CTX_LEN, HEAD_DIM)), sin], axis=0)
    return (jnp.asarray(np.tile(cos, (1, 4)), F32), jnp.asarray(np.tile(sin, (1, 4)), F32))


def _seg_ones(width, seg):
    i = np.arange(width)
    return jnp.asarray((i[:, None] // seg) == (i[None, :] // seg), BF16)


def _gla_ones2():
    r = np.arange(GLA_HEADS * GLA_DK) // GLA_DK
    c = np.arange(GLA_HEADS * GLA_DV) // GLA_DV
    return jnp.asarray(r[:, None] == c[None, :], BF16)


def _layer(x, ctx, cc, need_ctx, tables, norm_w, ada_w, ada_b, w_in, w_out, gla_dec_w, gla_dec_b,
           gla_out_norm, fnet_w, swa_q_norm, swa_k_norm, swa_sink, na_q_norm, na_k_norm, na_rel_bias):
    b, seq, d = x.shape
    cos_t, sin_t, seg64, ones2 = tables
    mod = _ada(cc, ada_w.astype(BF16), ada_b[None, :])[:b + 1].reshape(b + 1, 3, d)

    qs = HEAD_DIM ** -0.5
    head_norms = jnp.stack([jnp.tile(swa_q_norm, 4) * qs, jnp.tile(swa_k_norm, 4),
                            jnp.tile(na_q_norm, 4) * qs, jnp.tile(na_k_norm, 4)])
    head_norms = jnp.concatenate([head_norms, jnp.zeros((4, 256), F32)], axis=0)
    wdec = jnp.zeros((128, 256), F32)
    wdec = wdec.at[0:GLA_RANK, 0:128].set(gla_dec_w[0]).at[GLA_RANK:2 * GLA_RANK, 128:256].set(gla_dec_w[1])
    bdec = gla_dec_b.reshape(1, 256)

    a, la, bv, cq, ck, cv, dq, dk, dv, g = _inproj(
        x, ctx, mod, norm_w[None, :], _permute_w_in(w_in).astype(BF16), cos_t, sin_t, head_norms, seg64,
        wdec.astype(BF16), bdec)

    o_f = _gla(a, la, ones2, rev=False)
    o_b = _gla(a, la, ones2, rev=True)
    wf = fnet_w.astype(BF16)
    fz = _fnet(bv, wf, CTX_LEN // TM, seq)
    c_o = _swa(swa_sink, cq, ck, cv, seq)
    d_o = _na(dq, dk, dv, _na_bias_table(na_rel_bias), seq)

    gn = jnp.tile(gla_out_norm, 4)[None, :]
    wo = w_out.astype(BF16)
    x_new = _outproj(x, o_f, o_b, fz, c_o, d_o, g, mod, gn, seg64, wo, is_ctx=False)
    ctx_new = None
    if need_ctx:
        fzc = _fnet(bv, wf, 0, CTX_LEN)
        c_oc = _ctx_attn(swa_sink, cq, ck, cv, has_sink=True)
        d_oc = _ctx_attn(swa_sink, dq, dk, dv, has_sink=False)
        ctx_new = _outproj(ctx, o_f, o_b, fzc, c_oc, d_oc, g, mod, gn, seg64, wo, is_ctx=True)
    return x_new, ctx_new


def kernel(x, c, ctx, c_ctx, norm_w, ada_w, ada_b, w_in, w_out, gla_dec_w, gla_dec_b, gla_out_norm,
           fnet_w, swa_q_norm, swa_k_norm, swa_sink, na_q_norm, na_k_norm, na_rel_bias):
    b, seq, d = x.shape
    depth = norm_w.shape[0]
    assert ctx.shape[1] == CTX_LEN and seq % TM == 0 and seq % (NA_KH * GRID_W) == 0
    tables = _rope_tables(seq) + (_seg_ones(256, HEAD_DIM), _gla_ones2())
    cc = jnp.concatenate([c, c_ctx[None, :], jnp.zeros((8 - (b + 1) % 8, d), F32)], axis=0)
    for i in range(depth):
        x, ctx = _layer(x, ctx, cc, i < depth - 1, tables, norm_w[i], ada_w[i], ada_b[i], w_in[i], w_out[i],
                        gla_dec_w[i], gla_dec_b[i], gla_out_norm[i], fnet_w[i], swa_q_norm[i],
                        swa_k_norm[i], swa_sink[i], na_q_norm[i], na_k_norm[i], na_rel_bias[i])
    return x
```

```python
import functools
import math

import numpy as np
import jax
import jax.numpy as jnp
from jax import lax
from jax.experimental import pallas as pl
from jax.experimental.pallas import tpu as pltpu

F32 = jnp.float32
BF16 = jnp.bfloat16

GRID_W = 64
CTX_LEN = 256
HEAD_DIM = 64
BRANCH_W = 256
EPS = 1e-6
ROPE_BASE = 10000.0
GLA_HEADS = 4
GLA_DK = 32
GLA_DV = 64
GLA_RANK = 16
GLA_TAU = 16.0
GLA_CHUNK = 64
GLA_SUB = 16
FNET_GW = 64
SWA_BLOCK = 128
SWA_WINDOW = 128
NA_KH = 8
NA_KW = 16
NA_HEADS = 4

TM = 256
VMEM_LIMIT = 48 * 1024 * 1024

C_A = 0
C_BV = 512
C_CQ = 768
C_CK = 1024
C_CV = 1280
C_DQ = 1536
C_DK = 1792
C_DV = 2048
C_G = 2304
C_LR = 3328
W_COLS = 3456


def _dot(a, b):
    return jnp.dot(a, b, preferred_element_type=F32)


def _dot_nt(a, b):
    return lax.dot_general(a, b, (((1,), (1,)), ((), ())), preferred_element_type=F32)


def _dot_tn(a, b):
    return lax.dot_general(a, b, (((0,), (0,)), ((), ())), preferred_element_type=F32)


def _cparams(sem, vmem=None):
    return pltpu.CompilerParams(dimension_semantics=sem, vmem_limit_bytes=vmem)


def _ada_kernel(c_ref, w_ref, b_ref, o_ref):
    c = c_ref[...]
    s = c * (1.0 / (1.0 + jnp.exp(-c)))
    o_ref[...] = _dot(s.astype(BF16), w_ref[...]) + b_ref[...]


def _ada(cc, ada_w, ada_b):
    r, d = cc.shape
    n = ada_w.shape[1]
    return pl.pallas_call(
        _ada_kernel,
        out_shape=jax.ShapeDtypeStruct((r, n), F32),
        grid=(1,),
        in_specs=[pl.BlockSpec((r, d), lambda i: (0, 0)),
                  pl.BlockSpec((d, n), lambda i: (0, 0)),
                  pl.BlockSpec((1, n), lambda i: (0, 0))],
        out_specs=pl.BlockSpec((r, n), lambda i: (0, 0)),
        compiler_params=_cparams(("arbitrary",), VMEM_LIMIT),
        name="ada",
    )(cc, ada_w, ada_b)


def _head_rms(x, seg_ones, w):
    ms = _dot((x * x).astype(BF16), seg_ones) * (1.0 / HEAD_DIM)
    return x * lax.rsqrt(ms + EPS) * w


def _rope(y, cos, sin_signed):
    lane = lax.broadcasted_iota(jnp.int32, y.shape, 1)
    first_half = (lane % 32) < 16
    n = y.shape[1]
    swapped = jnp.where(first_half, pltpu.roll(y, n - 16, 1), pltpu.roll(y, 16, 1))
    return y * cos + swapped * sin_signed


def _inproj_kernel(x_ref, ctx_ref, mod_ref, nw_ref, w_ref, cos_ref, sin_ref, hn_ref, seg_ref,
                   wdec_ref, bdec_ref,
                   a_ref, la_ref, bv_ref, cq_ref, ck_ref, cv_ref, dq_ref, dk_ref, dv_ref, g_ref,
                   xs_ref):
    i = pl.program_id(1)

    @pl.when(i == 0)
    def _():
        xs_ref[...] = ctx_ref[0]

    @pl.when(i > 0)
    def _():
        xs_ref[...] = x_ref[0]

    x = xs_ref[...]
    ms = jnp.mean(x * x, axis=-1, keepdims=True)
    y = x * lax.rsqrt(ms + EPS) * nw_ref[...]
    h = y * (1.0 + mod_ref[0, 1:2, :]) + mod_ref[0, 0:1, :]
    hb = h.astype(BF16)

    def proj(c0, width):
        return _dot(hb, w_ref[:, c0:c0 + width])

    seg = seg_ref[...]
    cos = cos_ref[...]
    sin = sin_ref[...]

    a = proj(C_A, 512)
    a_ref[0, :, 0:128] = a[:, 0:128] * (GLA_DK ** -0.5)
    a_ref[0, :, 128:512] = a[:, 128:512]
    bv_ref[0] = proj(C_BV, 256).astype(BF16)
    cq_ref[0] = _rope(_head_rms(proj(C_CQ, 256), seg, hn_ref[0:1, :]), cos, sin).astype(BF16)
    ck_ref[0] = _rope(_head_rms(proj(C_CK, 256), seg, hn_ref[1:2, :]), cos, sin).astype(BF16)
    cv_ref[0] = proj(C_CV, 256).astype(BF16)
    dq_ref[0] = _head_rms(proj(C_DQ, 256), seg, hn_ref[2:3, :]).astype(BF16)
    dk_ref[0] = _head_rms(proj(C_DK, 256), seg, hn_ref[3:4, :]).astype(BF16)
    dv_ref[0] = proj(C_DV, 256).astype(BF16)
    g = proj(C_G, 1024)
    g_ref[0] = g * (1.0 / (1.0 + jnp.exp(-g)))
    lr = proj(C_LR, 128)
    z = _dot(lr.astype(BF16), wdec_ref[...]) + bdec_ref[...]
    la_ref[0] = (jnp.minimum(z, 0.0) - jnp.log(1.0 + jnp.exp(-jnp.abs(z)))) * (1.0 / GLA_TAU)


def _inproj(x, ctx, mod, norm_w, w_perm, cos_t, sin_t, head_norms, seg_ones, wdec, bdec):
    b, seq, d = x.shape
    t_all = CTX_LEN + seq
    nblk = t_all // TM

    def tok(width):
        return pl.BlockSpec((1, TM, width), lambda bi, i: (bi, i, 0))

    def const(shape):
        return pl.BlockSpec(shape, lambda bi, i: (0,) * len(shape))

    def out(width, dtype):
        return jax.ShapeDtypeStruct((b, t_all, width), dtype)

    return pl.pallas_call(
        _inproj_kernel,
        out_shape=(out(512, F32), out(256, F32), out(256, BF16), out(256, BF16), out(256, BF16),
                   out(256, BF16), out(256, BF16), out(256, BF16), out(256, BF16), out(1024, F32)),
        grid=(b, nblk),
        in_specs=[
            pl.BlockSpec((1, TM, d), lambda bi, i: (bi, jnp.maximum(i - 1, 0), 0)),
            pl.BlockSpec((1, TM, d), lambda bi, i: (bi, 0, 0)),
            pl.BlockSpec((1, 3, d), lambda bi, i: (jnp.where(i == 0, b, bi), 0, 0)),
            const((1, d)),
            const((d, W_COLS)),
            pl.BlockSpec((TM, 256), lambda bi, i: (i, 0)),
            pl.BlockSpec((TM, 256), lambda bi, i: (i, 0)),
            const((8, 256)),
            const((256, 256)),
            const((128, 256)),
            const((1, 256)),
        ],
        out_specs=(tok(512), tok(256), tok(256), tok(256), tok(256), tok(256), tok(256), tok(256),
                   tok(256), tok(1024)),
        scratch_shapes=[pltpu.VMEM((TM, d), F32)],
        compiler_params=_cparams(("parallel", "arbitrary"), VMEM_LIMIT),
        name="inproj",
    )(x, ctx, mod, norm_w, w_perm, cos_t, sin_t, head_norms, seg_ones, wdec, bdec)


def _gla_kernel(a_ref, la_ref, ones2_ref, o_ref, st_ref, *, rev, nbatch):
    @pl.when(pl.program_id(0) == 0)
    def _():
        st_ref[...] = jnp.zeros_like(st_ref)

    for bi in range(nbatch):
        _gla_chunk(a_ref, la_ref, ones2_ref, o_ref, st_ref, bi, rev)


def _gla_chunk(a_ref, la_ref, ones2_ref, o_ref, st_ref, bi, rev):
    c, sb = GLA_CHUNK, GLA_SUB
    nsub = c // sb
    hk = GLA_HEADS * GLA_DK
    hv = GLA_HEADS * GLA_DV

    q = a_ref[bi, :, 0:hk]
    k = a_ref[bi, :, hk:2 * hk]
    v = a_ref[bi, :, 2 * hk:2 * hk + hv]
    la = la_ref[bi]

    r_i = lax.broadcasted_iota(jnp.int32, (c, c), 0)
    c_i = lax.broadcasted_iota(jnp.int32, (c, c), 1)
    tri = jnp.where((c_i >= r_i) if rev else (c_i <= r_i), 1.0, 0.0).astype(BF16)
    la_hi = la.astype(BF16)
    la_lo = (la - la_hi.astype(F32)).astype(BF16)
    cum = _dot(tri, la_hi) + _dot(tri, la_lo)
    cumx = cum - la
    end = 0 if rev else c - 1
    cum_end = cum[end:end + 1, :]

    def first_row(blk):
        return sb * blk + (sb - 1 if rev else 0)

    def blk_of(p):
        return nsub - 1 - p if rev else p

    cref = [cumx[first_row(blk):first_row(blk) + 1, :] for blk in range(nsub)]
    cref_rows = jnp.concatenate([jnp.broadcast_to(cr, (sb, hk)) for cr in cref], axis=0)

    st = st_ref[bi]
    qi = q * jnp.exp(cum)
    inter = _dot_nt(qi.astype(BF16), st.astype(BF16))

    qd = q * jnp.exp(cum - cref_rows)
    lane_k = lax.broadcasted_iota(jnp.int32, (c, hk), 1) // GLA_DK
    q_st = jnp.concatenate([jnp.where(lane_k == h, qd, 0.0) for h in range(GLA_HEADS)], axis=0)
    kt, vt = [], []
    for p in range(1, nsub):
        blk = blk_of(p)
        lo, hi = (c - sb * p, c) if rev else (0, sb * p)
        kt.append(k[lo:hi] * jnp.exp(cref[blk] - cum[lo:hi]))
        vt.append(v[lo:hi])
    npast = sb * (nsub * (nsub - 1) // 2)
    npad = 128 - npast
    kt.append(jnp.zeros((npad, hk), F32))
    vt.append(jnp.zeros((npad, hv), F32))
    kt = jnp.concatenate(kt, axis=0).astype(BF16)
    vt = jnp.concatenate(vt, axis=0).astype(BF16)
    sc = _dot_nt(q_st.astype(BF16), kt)
    row = lax.broadcasted_iota(jnp.int32, sc.shape, 0)
    col = lax.broadcasted_iota(jnp.int32, sc.shape, 1)
    row_blk = (row % c) // sb
    row_p = (nsub - 1 - row_blk) if rev else row_blk
    col_p = jnp.where(col < sb, 1, jnp.where(col < 3 * sb, 2, jnp.where(col < 6 * sb, 3, -1)))
    sc = jnp.where(row_p == col_p, sc, 0.0)
    off_st = _dot(sc.astype(BF16), vt)
    lane_v = lax.broadcasted_iota(jnp.int32, (c, hv), 1) // GLA_DV
    off = jnp.where(lane_v == 0, off_st[0:c], 0.0)
    for h in range(1, GLA_HEADS):
        off = off + jnp.where(lane_v == h, off_st[h * c:(h + 1) * c], 0.0)

    ones2 = ones2_ref[...]
    t_loc = lax.broadcasted_iota(jnp.int32, (sb, hk), 0)
    for blk in range(nsub):
        r0 = sb * blk
        cb = cum[r0:r0 + sb]
        qb = q[r0:r0 + sb]
        es = []
        for s in range(sb):
            valid = (t_loc <= s) if rev else (t_loc >= s)
            diff = jnp.where(valid, cb - cum[r0 + s:r0 + s + 1, :], -jnp.inf)
            es.append(jnp.exp(diff) * (qb * k[r0 + s:r0 + s + 1, :]))
        e = jnp.concatenate(es, axis=0).astype(BF16)
        r = _dot(e, ones2)
        dg = r[0:sb] * v[r0:r0 + 1, :]
        for s in range(1, sb):
            dg = dg + r[s * sb:(s + 1) * sb] * v[r0 + s:r0 + s + 1, :]
        o_ref[bi, r0:r0 + sb, :] = inter[r0:r0 + sb] + off[r0:r0 + sb] + dg

    kd = k * jnp.exp(cum_end - cum)
    kv = _dot_tn(v.astype(BF16), kd.astype(BF16))
    row_h = lax.broadcasted_iota(jnp.int32, (hv, hk), 0) // GLA_DV
    col_h = lax.broadcasted_iota(jnp.int32, (hv, hk), 1) // GLA_DK
    st_ref[bi] = st * jnp.exp(cum_end) + jnp.where(row_h == col_h, kv, 0.0)


def _gla(a, la, ones2, rev):
    b, t_all, _ = a.shape
    nch = t_all // GLA_CHUNK
    nctx = CTX_LEN // GLA_CHUNK
    direction = 1 if rev else 0

    def chunk(j):
        if not rev:
            return j
        return jnp.where(j < nctx, nctx - 1 - j, nch - 1 - (j - nctx))

    return pl.pallas_call(
        functools.partial(_gla_kernel, rev=rev, nbatch=b),
        out_shape=jax.ShapeDtypeStruct((b, t_all, 256), F32),
        grid=(nch,),
        in_specs=[pl.BlockSpec((b, GLA_CHUNK, 512), lambda j: (0, chunk(j), 0)),
                  pl.BlockSpec((b, GLA_CHUNK, 128), lambda j: (0, chunk(j), direction)),
                  pl.BlockSpec((128, 256), lambda j: (0, 0))],
        out_specs=pl.BlockSpec((b, GLA_CHUNK, 256), lambda j: (0, chunk(j), 0)),
        scratch_shapes=[pltpu.VMEM((b, 256, 128), F32)],
        compiler_params=_cparams(("arbitrary",)),
        name="gla_bwd" if rev else "gla_fwd",
    )(a, la, ones2)


def _fnet_chan_kernel(v_ref, wc_ref, ws_ref, o_ref):
    v = v_ref[0]
    o_ref[0, :, :] = _dot(v, wc_ref[...]).astype(BF16)
    o_ref[1, :, :] = _dot(v, ws_ref[...]).astype(BF16)


def _fnet_chan(bv, wc, ws, row_blk_off, length):
    b = bv.shape[0]
    tl = min(TM, length)
    return pl.pallas_call(
        _fnet_chan_kernel,
        out_shape=jax.ShapeDtypeStruct((2, length, b * 256), BF16),
        grid=(b, length // tl),
        in_specs=[pl.BlockSpec((1, tl, 256), lambda bi, i: (bi, i + row_blk_off, 0)),
                  pl.BlockSpec((256, 256), lambda bi, i: (0, 0)),
                  pl.BlockSpec((256, 256), lambda bi, i: (0, 0))],
        out_specs=pl.BlockSpec((2, tl, 256), lambda bi, i: (0, i, bi)),
        compiler_params=_cparams(("parallel", "arbitrary")),
        name="fnet_chan",
    )(bv, wc, ws)


def _fnet_pos_kernel(d_ref, y_ref, wf_ref, o_ref, acc_ref, *, nb):
    kk = pl.program_id(1)

    @pl.when(kk == 0)
    def _():
        acc_ref[...] = jnp.zeros_like(acc_ref)

    acc_ref[...] += _dot(d_ref[...], y_ref[...])

    @pl.when(kk == pl.num_programs(1) - 1)
    def _():
        wf = wf_ref[...]
        for bi in range(nb):
            o_ref[bi] = _dot(acc_ref[:, bi * 256:(bi + 1) * 256].astype(BF16), wf)


def _fnet_pos(dmat, y2, wf, nb):
    length, k2 = dmat.shape
    tm = min(512, length)
    tk = min(1024, k2)
    return pl.pallas_call(
        functools.partial(_fnet_pos_kernel, nb=nb),
        out_shape=jax.ShapeDtypeStruct((nb, length, 256), F32),
        grid=(length // tm, k2 // tk),
        in_specs=[pl.BlockSpec((tm, tk), lambda i, kk: (i, kk)),
                  pl.BlockSpec((tk, nb * 256), lambda i, kk: (kk, 0)),
                  pl.BlockSpec((256, 256), lambda i, kk: (0, 0))],
        out_specs=pl.BlockSpec((nb, tm, 256), lambda i, kk: (0, i, 0)),
        scratch_shapes=[pltpu.VMEM((tm, nb * 256), F32)],
        compiler_params=_cparams(("parallel", "arbitrary"), VMEM_LIMIT),
        name="fnet_pos",
    )(dmat, y2, wf)


def _dft_tables(length):
    j = jnp.arange(length, dtype=jnp.int32)
    jk = (j[:, None] * j[None, :]) % length
    ang = jk.astype(F32) * (2.0 * math.pi / length)
    s = 1.0 / math.sqrt(length)
    return jnp.concatenate([jnp.cos(ang) * s, -jnp.sin(ang) * s], axis=1).astype(BF16)


def _chan_tables():
    j = np.arange(FNET_GW)
    ang = 2.0 * np.pi * ((j[:, None] * j[None, :]) % FNET_GW) / FNET_GW
    s = 1.0 / math.sqrt(FNET_GW)
    eye = np.eye(BRANCH_W // FNET_GW)
    wc = np.kron(eye, np.cos(ang) * s)
    ws = np.kron(eye, np.sin(ang) * s)
    return jnp.asarray(wc, F32).astype(BF16), jnp.asarray(ws, F32).astype(BF16)


def _fnet(bv, wf, row_blk_off, length):
    b = bv.shape[0]
    wc, ws = _chan_tables()
    y = _fnet_chan(bv, wc, ws, row_blk_off, length)
    y2 = y.reshape(2 * length, b * 256)
    return _fnet_pos(_dft_tables(length), y2, wf, b)


def _fnet_grid_kernel(v_ref, cg_ref, sg_ref, w1_ref, w2_ref, wf_ref, o_ref, z_ref, f_ref, *, rows, row_off):
    length = rows * GRID_W
    v = v_ref[0, row_off:row_off + length, :]
    zr = _dot(v, cg_ref[...])
    zi = _dot(v, sg_ref[...])
    z_ref[0] = zr[:, 0:128]
    z_ref[1] = zr[:, 128:256]
    z_ref[2] = zi[:, 0:128]
    z_ref[3] = zi[:, 128:256]
    w1 = w1_ref[...]

    def gather_rows(sl):
        return jnp.concatenate([jnp.concatenate([z_ref[0, sl, :], z_ref[1, sl, :]], axis=1),
                                jnp.concatenate([z_ref[2, sl, :], z_ref[3, sl, :]], axis=1)], axis=0).astype(BF16)

    def stage_r(w, carry):
        sl = pl.ds(w, rows, stride=GRID_W)
        g = _dot(w1, gather_rows(sl))
        z_ref[0, sl, :] = g[0:rows, 0:128]
        z_ref[1, sl, :] = g[0:rows, 128:256]
        z_ref[2, sl, :] = g[rows:2 * rows, 0:128]
        z_ref[3, sl, :] = g[rows:2 * rows, 128:256]
        return carry

    lax.fori_loop(0, GRID_W, stage_r, 0)
    wf = wf_ref[...]

    def stage_w(ka, carry):
        r0 = pl.multiple_of(ka * GRID_W, GRID_W)
        f = _dot(w2_ref[pl.ds(r0, GRID_W), :], gather_rows(pl.ds(r0, GRID_W)))
        y = _dot(f.astype(BF16), wf)
        f_ref[0, pl.ds(ka, GRID_W, stride=rows), :] = y[:, 0:128]
        f_ref[1, pl.ds(ka, GRID_W, stride=rows), :] = y[:, 128:256]
        return carry

    lax.fori_loop(0, rows, stage_w, 0)
    o_ref[0, :, 0:128] = f_ref[0]
    o_ref[0, :, 128:256] = f_ref[1]


def _fnet_grid_tables(rows):
    length = rows * GRID_W
    j = np.arange(rows)
    ang = 2.0 * np.pi * ((j[:, None] * j[None, :]) % rows) / rows
    c, s = np.cos(ang), np.sin(ang)
    w1 = jnp.asarray(np.block([[c, -s], [-s, -c]]), F32).astype(BF16)
    ka = jnp.arange(rows, dtype=jnp.int32)[:, None, None]
    kb = jnp.arange(GRID_W, dtype=jnp.int32)[None, :, None]
    w = jnp.arange(GRID_W, dtype=jnp.int32)[None, None, :]
    ang2 = ((w * (ka + rows * kb)) % length).astype(F32) * (2.0 * math.pi / length)
    scale = 1.0 / math.sqrt(length)
    w2 = jnp.concatenate([jnp.cos(ang2) * scale, jnp.sin(ang2) * scale], axis=2)
    return w1, w2.reshape(length, 2 * GRID_W).astype(BF16)


def _fnet_grid(bv, wf, seq):
    b, t_all, _ = bv.shape
    rows = seq // GRID_W
    wc, ws = _chan_tables()
    w1, w2 = _fnet_grid_tables(rows)

    def const(shape):
        return pl.BlockSpec(shape, lambda bi: (0,) * len(shape))

    return pl.pallas_call(
        functools.partial(_fnet_grid_kernel, rows=rows, row_off=CTX_LEN),
        out_shape=jax.ShapeDtypeStruct((b, seq, 256), F32),
        grid=(b,),
        in_specs=[pl.BlockSpec((1, t_all, 256), lambda bi: (bi, 0, 0)), const((256, 256)), const((256, 256)),
                  const((2 * rows, 2 * rows)), const((seq, 2 * GRID_W)), const((256, 256))],
        out_specs=pl.BlockSpec((1, seq, 256), lambda bi: (bi, 0, 0)),
        scratch_shapes=[pltpu.VMEM((4, seq, 128), F32), pltpu.VMEM((2, seq, 128), F32)],
        compiler_params=_cparams(("parallel",), VMEM_LIMIT),
        name="fnet_grid",
    )(bv, wc, ws, w1, w2, wf)


def _stack_heads(qg):
    lane = lax.broadcasted_iota(jnp.int32, qg.shape, 1)
    zero = jnp.zeros_like(qg)
    return jnp.concatenate([jnp.where(lane < HEAD_DIM, qg, zero),
                            jnp.where(lane >= HEAD_DIM, qg, zero)], axis=0)


def _unstack_heads(o, rows):
    lane = lax.broadcasted_iota(jnp.int32, (rows, 2 * HEAD_DIM), 1)
    return jnp.where(lane < HEAD_DIM, o[0:rows], o[rows:2 * rows])


def _swa_kernel(sink_ref, q_ref, kp_ref, kc_ref, kn_ref, kx_ref, vp_ref, vc_ref, vn_ref, vx_ref,
                o_ref):
    n = pl.program_id(1)
    nb = pl.num_programs(1)
    blk = SWA_BLOCK
    rows = 2 * blk
    row = lax.broadcasted_iota(jnp.int32, (rows, 3 * blk), 0) % blk
    col = lax.broadcasted_iota(jnp.int32, (rows, 3 * blk), 1)
    rel = col - blk - row
    col_lo = jnp.where(n > 0, 0, blk)
    col_hi = jnp.where(n < nb - 1, 3 * blk, 2 * blk)
    valid = (jnp.abs(rel) <= SWA_WINDOW) & (col >= col_lo) & (col < col_hi)
    head_row = lax.broadcasted_iota(jnp.int32, (rows, 1), 0) < blk
    for g in range(2):
        ls = slice(128 * g, 128 * (g + 1))
        q_st = _stack_heads(q_ref[0, :, ls])
        kw = jnp.concatenate([kp_ref[0, :, ls], kc_ref[0, :, ls], kn_ref[0, :, ls]], axis=0)
        vw = jnp.concatenate([vp_ref[0, :, ls], vc_ref[0, :, ls], vn_ref[0, :, ls]], axis=0)
        s_loc = jnp.where(valid, _dot_nt(q_st, kw), -jnp.inf)
        s_ctx = _dot_nt(q_st, kx_ref[0, :, ls])
        sink = jnp.where(head_row, sink_ref[2 * g], sink_ref[2 * g + 1])
        m = jnp.maximum(jnp.maximum(jnp.max(s_loc, axis=-1, keepdims=True),
                                    jnp.max(s_ctx, axis=-1, keepdims=True)), sink)
        p_loc = jnp.exp(s_loc - m)
        p_ctx = jnp.exp(s_ctx - m)
        den = (jnp.sum(p_loc, axis=-1, keepdims=True) + jnp.sum(p_ctx, axis=-1, keepdims=True)
               + jnp.exp(sink - m))
        o = _dot(p_loc.astype(BF16), vw) + _dot(p_ctx.astype(BF16), vx_ref[0, :, ls])
        o_ref[0, :, ls] = _unstack_heads(o / den, blk)


def _swa(sink, cq, ck, cv, seq):
    b = cq.shape[0]
    nb = seq // SWA_BLOCK
    off = CTX_LEN // SWA_BLOCK

    def at(f):
        return pl.BlockSpec((1, SWA_BLOCK, 256), lambda bi, n: (bi, f(n) + off, 0))

    prev = at(lambda n: jnp.maximum(n - 1, 0))
    cur = at(lambda n: n)
    nxt = at(lambda n: jnp.minimum(n + 1, nb - 1))
    cx = pl.BlockSpec((1, CTX_LEN, 256), lambda bi, n: (bi, 0, 0))
    return pl.pallas_call(
        _swa_kernel,
        out_shape=jax.ShapeDtypeStruct((b, seq, 256), F32),
        grid=(b, nb),
        in_specs=[pl.BlockSpec(memory_space=pltpu.SMEM), cur, prev, cur, nxt, cx, prev, cur, nxt, cx],
        out_specs=pl.BlockSpec((1, SWA_BLOCK, 256), lambda bi, n: (bi, n, 0)),
        compiler_params=_cparams(("parallel", "arbitrary")),
        name="swa",
    )(sink, cq, ck, ck, ck, ck, cv, cv, cv, cv)


NA_ROWS_PER_STEP = 4


def _na_kernel(q_ref, k_ref, v_ref, bias_ref, o_ref, *, rows_total):
    kh = NA_KH
    nk = kh * GRID_W
    for rr in range(NA_ROWS_PER_STEP):
        r = pl.program_id(1) * NA_ROWS_PER_STEP + rr
        row_start = jnp.clip(r - kh // 2, 0, rows_total - kh)
        cls = r - row_start
        k0 = pl.multiple_of(CTX_LEN + row_start * GRID_W, GRID_W)
        qs = slice(GRID_W * rr, GRID_W * (rr + 1))
        for g in range(2):
            ls = slice(128 * g, 128 * (g + 1))
            q_st = _stack_heads(q_ref[0, qs, ls])
            s_nb = _dot_nt(q_st, k_ref[0, pl.ds(k0, nk), ls]) + bias_ref[cls, g]
            s_ctx = _dot_nt(q_st, k_ref[0, 0:CTX_LEN, ls])
            m = jnp.maximum(jnp.max(s_nb, axis=-1, keepdims=True), jnp.max(s_ctx, axis=-1, keepdims=True))
            p_nb = jnp.exp(s_nb - m)
            p_ctx = jnp.exp(s_ctx - m)
            den = jnp.sum(p_nb, axis=-1, keepdims=True) + jnp.sum(p_ctx, axis=-1, keepdims=True)
            o = (_dot(p_nb.astype(BF16), v_ref[0, pl.ds(k0, nk), ls])
                 + _dot(p_ctx.astype(BF16), v_ref[0, 0:CTX_LEN, ls]))
            o_ref[0, qs, ls] = _unstack_heads(o / den, GRID_W)


def _bias_expand_kernel(rb_ref, oh_ref, o_ref):
    x = rb_ref[...]
    oh = oh_ref[...]
    hi = x.astype(BF16)
    r1 = x - hi.astype(F32)
    mid = r1.astype(BF16)
    lo = (r1 - mid.astype(F32)).astype(BF16)
    o_ref[...] = _dot(hi, oh) + _dot(mid, oh) + _dot(lo, oh)


def _na_bias_table(rel_bias):
    ndy, ndx = 2 * NA_KH - 1, 2 * NA_KW - 1
    cq = np.arange(GRID_W)
    col_start = np.clip(cq - NA_KW // 2, 0, GRID_W - NA_KW)
    col_ok = (cq[None, :] >= col_start[:, None]) & (cq[None, :] < col_start[:, None] + NA_KW)
    qi = jnp.arange(GRID_W, dtype=jnp.int32)
    dx = jnp.clip(qi[None, :] - qi[:, None], -(NA_KW - 1), NA_KW - 1) + (NA_KW - 1)
    onehot = (jnp.arange(128, dtype=jnp.int32)[:, None] == dx.reshape(1, GRID_W * GRID_W)).astype(BF16)
    rb = jnp.zeros((64, 128), F32).at[0:NA_HEADS * ndy, 0:ndx].set(rel_bias.reshape(NA_HEADS * ndy, ndx))
    p = pl.pallas_call(
        _bias_expand_kernel,
        out_shape=jax.ShapeDtypeStruct((64, GRID_W * GRID_W), F32),
        name="na_bias_expand",
    )(rb, onehot)
    p = p[0:NA_HEADS * ndy].reshape(NA_HEADS, ndy, GRID_W, GRID_W)
    p = jnp.where(col_ok[None, None], p, -jnp.inf)
    tabs = jnp.stack([p[:, NA_KH - 1 - c:2 * NA_KH - 1 - c] for c in range(NA_KH)])
    tabs = tabs.transpose(0, 1, 3, 2, 4)
    return tabs.reshape(NA_KH, NA_HEADS // 2, 2 * GRID_W, NA_KH * GRID_W)


def _na(dq, dk, dv, bias_tab, seq):
    b, t_all, _ = dq.shape
    rows_total = seq // GRID_W
    qrows = NA_ROWS_PER_STEP * GRID_W
    off = CTX_LEN // qrows
    full = pl.BlockSpec((1, t_all, 256), lambda bi, r: (bi, 0, 0))
    return pl.pallas_call(
        functools.partial(_na_kernel, rows_total=rows_total),
        out_shape=jax.ShapeDtypeStruct((b, seq, 256), F32),
        grid=(b, rows_total // NA_ROWS_PER_STEP),
        in_specs=[pl.BlockSpec((1, qrows, 256), lambda bi, r: (bi, r + off, 0)), full, full,
                  pl.BlockSpec(bias_tab.shape, lambda bi, r: (0, 0, 0, 0))],
        out_specs=pl.BlockSpec((1, qrows, 256), lambda bi, r: (bi, r, 0)),
        compiler_params=_cparams(("parallel", "arbitrary"), VMEM_LIMIT),
        name="na",
    )(dq, dk, dv, bias_tab)


def _ctx_attn_kernel(sink_ref, q_ref, k_ref, v_ref, o_ref, *, has_sink):
    n = CTX_LEN
    head_row = lax.broadcasted_iota(jnp.int32, (2 * n, 1), 0) < n
    for g in range(2):
        ls = slice(128 * g, 128 * (g + 1))
        q_st = _stack_heads(q_ref[0, :, ls])
        s = _dot_nt(q_st, k_ref[0, :, ls])
        m = jnp.max(s, axis=-1, keepdims=True)
        if has_sink:
            sink = jnp.where(head_row, sink_ref[2 * g], sink_ref[2 * g + 1])
            m = jnp.maximum(m, sink)
        p = jnp.exp(s - m)
        den = jnp.sum(p, axis=-1, keepdims=True)
        if has_sink:
            den = den + jnp.exp(sink - m)
        o = _dot(p.astype(BF16), v_ref[0, :, ls])
        o_ref[0, :, ls] = _unstack_heads(o / den, n)


def _ctx_attn(sink, q, k, v, has_sink):
    b = q.shape[0]
    blk = pl.BlockSpec((1, CTX_LEN, 256), lambda bi: (bi, 0, 0))
    return pl.pallas_call(
        functools.partial(_ctx_attn_kernel, has_sink=has_sink),
        out_shape=jax.ShapeDtypeStruct((b, CTX_LEN, 256), F32),
        grid=(b,),
        in_specs=[pl.BlockSpec(memory_space=pltpu.SMEM), blk, blk, blk],
        out_specs=blk,
        compiler_params=_cparams(("parallel",)),
        name="ctx_attn_sink" if has_sink else "ctx_attn",
    )(sink, q, k, v)


def _outproj_kernel(x_ref, of_ref, ob_ref, fz_ref, co_ref, do_ref, g_ref, mod_ref, gn_ref, seg_ref,
                    w_ref, o_ref):
    a = of_ref[0] + ob_ref[0]
    ms = _dot((a * a).astype(BF16), seg_ref[...]) * (1.0 / GLA_DV)
    a = a * lax.rsqrt(ms + EPS) * gn_ref[...]
    g = g_ref[0]
    mixed = jnp.concatenate([a * g[:, 0:256], fz_ref[0] * g[:, 256:512],
                             co_ref[0] * g[:, 512:768], do_ref[0] * g[:, 768:1024]], axis=1)
    y = _dot(mixed.astype(BF16), w_ref[...])
    o_ref[0] = x_ref[0] + mod_ref[0, 2:3, :] * y


def _outproj(x, o_f, o_b, fz, c_o, d_o, g, mod, gla_norm, seg_ones, w_out, is_ctx):
    b, rows, d = x.shape
    nblk = rows // TM
    comb = 0 if is_ctx else CTX_LEN // TM
    nbatch = b

    def own(width):
        return pl.BlockSpec((1, TM, width), lambda bi, i: (bi, i, 0))

    def combined(width):
        return pl.BlockSpec((1, TM, width), lambda bi, i: (bi, i + comb, 0))

    def const(shape):
        return pl.BlockSpec(shape, lambda bi, i: (0,) * len(shape))

    mod_spec = pl.BlockSpec((1, 3, d), (lambda bi, i: (nbatch, 0, 0)) if is_ctx else (lambda bi, i: (bi, 0, 0)))
    return pl.pallas_call(
        _outproj_kernel,
        out_shape=jax.ShapeDtypeStruct((b, rows, d), F32),
        grid=(b, nblk),
        in_specs=[own(d), combined(256), combined(256), own(256), own(256), own(256), combined(1024),
                  mod_spec, const((1, 256)), const((256, 256)), const((d, d))],
        out_specs=own(d),
        compiler_params=_cparams(("parallel", "arbitrary"), VMEM_LIMIT),
        name="outproj_ctx" if is_ctx else "outproj",
    )(x, o_f, o_b, fz, c_o, d_o, g, mod, gla_norm, seg_ones, w_out)


def _permute_w_in(w):
    s = lambda lo, hi: w[:, lo:hi]
    ck, cv = s(1568, 1696), s(1696, 1824)
    dup = lambda t: jnp.concatenate([t[:, :64], t[:, :64], t[:, 64:], t[:, 64:]], axis=1)
    pad = jnp.zeros((w.shape[0], 128 - 2 * GLA_RANK), w.dtype)
    return jnp.concatenate([
        s(0, 512),
        s(800, 1056),
        s(1312, 1568), dup(ck), dup(cv),
        s(2080, 2336), s(2336, 2592), s(2592, 2848),
        s(512, 768), s(1056, 1312), s(1824, 2080), s(2848, 3104),
        s(768, 800), pad,
    ], axis=1)


def _rope_tables(seq):
    t = np.arange(seq)
    axis_dim = HEAD_DIM // 2
    inv = ROPE_BASE ** (-np.arange(0, axis_dim, 2, dtype=np.float64) / axis_dim)
    row = (t // GRID_W).astype(np.float64)
    col = (t % GRID_W).astype(np.float64)
    ang = np.concatenate([row[:, None] * inv, row[:, None] * inv, col[:, None] * inv, col[:, None] * inv], axis=1)
    cos = np.cos(ang)
    sign = np.concatenate([-np.ones(16), np.ones(16), -np.ones(16), np.ones(16)])
    sin = np.sin(ang) * sign
    cos = np.concatenate([np.ones((CTX_LEN, HEAD_DIM)), cos], axis=0)
    sin = np.concatenate([np.zeros((CTX_LEN, HEAD_DIM)), sin], axis=0)
    return (jnp.asarray(np.tile(cos, (1, 4)), F32), jnp.asarray(np.tile(sin, (1, 4)), F32))


def _seg_ones(width, seg):
    i = np.arange(width)
    return jnp.asarray((i[:, None] // seg) == (i[None, :] // seg), BF16)


def _gla_ones2():
    r = np.arange(GLA_HEADS * GLA_DK) // GLA_DK
    c = np.arange(GLA_HEADS * GLA_DV) // GLA_DV
    return jnp.asarray(r[:, None] == c[None, :], BF16)


def _layer(x, ctx, cc, need_ctx, tables, norm_w, ada_w, ada_b, w_in, w_out, gla_dec_w, gla_dec_b,
           gla_out_norm, fnet_w, swa_q_norm, swa_k_norm, swa_sink, na_q_norm, na_k_norm, na_rel_bias):
    b, seq, d = x.shape
    cos_t, sin_t, seg64, ones2 = tables
    mod = _ada(cc, ada_w.astype(BF16), ada_b[None, :])[:b + 1].reshape(b + 1, 3, d)

    qs = HEAD_DIM ** -0.5
    head_norms = jnp.stack([jnp.tile(swa_q_norm, 4) * qs, jnp.tile(swa_k_norm, 4),
                            jnp.tile(na_q_norm, 4) * qs, jnp.tile(na_k_norm, 4)])
    head_norms = jnp.concatenate([head_norms, jnp.zeros((4, 256), F32)], axis=0)
    wdec = jnp.zeros((128, 256), F32)
    wdec = wdec.at[0:GLA_RANK, 0:128].set(gla_dec_w[0]).at[GLA_RANK:2 * GLA_RANK, 128:256].set(gla_dec_w[1])
    bdec = gla_dec_b.reshape(1, 256)

    a, la, bv, cq, ck, cv, dq, dk, dv, g = _inproj(
        x, ctx, mod, norm_w[None, :], _permute_w_in(w_in).astype(BF16), cos_t, sin_t, head_norms, seg64,
        wdec.astype(BF16), bdec)

    o_f = _gla(a, la, ones2, rev=False)
    o_b = _gla(a, la, ones2, rev=True)
    wf = fnet_w.astype(BF16)
    fz = _fnet_grid(bv, wf, seq)
    c_o = _swa(swa_sink, cq, ck, cv, seq)
    d_o = _na(dq, dk, dv, _na_bias_table(na_rel_bias), seq)

    gn = jnp.tile(gla_out_norm, 4)[None, :]
    wo = w_out.astype(BF16)
    x_new = _outproj(x, o_f, o_b, fz, c_o, d_o, g, mod, gn, seg64, wo, is_ctx=False)
    ctx_new = None
    if need_ctx:
        fzc = _fnet(bv, wf, 0, CTX_LEN)
        c_oc = _ctx_attn(swa_sink, cq, ck, cv, has_sink=True)
        d_oc = _ctx_attn(swa_sink, dq, dk, dv, has_sink=False)
        ctx_new = _outproj(ctx, o_f, o_b, fzc, c_oc, d_oc, g, mod, gn, seg64, wo, is_ctx=True)
    return x_new, ctx_new


def kernel(x, c, ctx, c_ctx, norm_w, ada_w, ada_b, w_in, w_out, gla_dec_w, gla_dec_b, gla_out_norm,
           fnet_w, swa_q_norm, swa_k_norm, swa_sink, na_q_norm, na_k_norm, na_rel_bias):
    b, seq, d = x.shape
    depth = norm_w.shape[0]
    assert ctx.shape[1] == CTX_LEN and seq % TM == 0 and seq % (NA_KH * GRID_W) == 0
    tables = _rope_tables(seq) + (_seg_ones(256, HEAD_DIM), _gla_ones2())
    cc = jnp.concatenate([c, c_ctx[None, :], jnp.zeros((8 - (b + 1) % 8, d), F32)], axis=0)
    for i in range(depth):
        x, ctx = _layer(x, ctx, cc, i < depth - 1, tables, norm_w[i], ada_w[i], ada_b[i], w_in[i], w_out[i],
                        gla_dec_w[i], gla_dec_b[i], gla_out_norm[i], fnet_w[i], swa_q_norm[i],
                        swa_k_norm[i], swa_sink[i], na_q_norm[i], na_k_norm[i], na_rel_bias[i])
    return x
```

```python
import functools
import math

import numpy as np
import jax
import jax.numpy as jnp
from jax import lax
from jax.experimental import pallas as pl
from jax.experimental.pallas import tpu as pltpu

F32 = jnp.float32
BF16 = jnp.bfloat16

GRID_W = 64
CTX_LEN = 256
HEAD_DIM = 64
BRANCH_W = 256
EPS = 1e-6
ROPE_BASE = 10000.0
GLA_HEADS = 4
GLA_DK = 32
GLA_DV = 64
GLA_RANK = 16
GLA_TAU = 16.0
GLA_CHUNK = 64
GLA_SUB = 8
FNET_GW = 64
SWA_BLOCK = 128
SWA_WINDOW = 128
NA_KH = 8
NA_KW = 16
NA_HEADS = 4

TM = 256
VMEM_LIMIT = 48 * 1024 * 1024

C_A = 0
C_BV = 512
C_CQ = 768
C_CK = 1024
C_CV = 1280
C_DQ = 1536
C_DK = 1792
C_DV = 2048
C_G = 2304
C_LR = 3328
W_COLS = 3456


def _dot(a, b):
    return jnp.dot(a, b, preferred_element_type=F32)


def _dot_nt(a, b):
    return lax.dot_general(a, b, (((1,), (1,)), ((), ())), preferred_element_type=F32)


def _dot_tn(a, b):
    return lax.dot_general(a, b, (((0,), (0,)), ((), ())), preferred_element_type=F32)


def _cparams(sem, vmem=None):
    return pltpu.CompilerParams(dimension_semantics=sem, vmem_limit_bytes=vmem)


def _ada_kernel(c_ref, w_ref, b_ref, o_ref):
    c = c_ref[...]
    s = c * (1.0 / (1.0 + jnp.exp(-c)))
    o_ref[...] = _dot(s.astype(BF16), w_ref[...]) + b_ref[...]


def _ada(cc, ada_w, ada_b):
    r, d = cc.shape
    n = ada_w.shape[1]
    return pl.pallas_call(
        _ada_kernel,
        out_shape=jax.ShapeDtypeStruct((r, n), F32),
        grid=(1,),
        in_specs=[pl.BlockSpec((r, d), lambda i: (0, 0)),
                  pl.BlockSpec((d, n), lambda i: (0, 0)),
                  pl.BlockSpec((1, n), lambda i: (0, 0))],
        out_specs=pl.BlockSpec((r, n), lambda i: (0, 0)),
        compiler_params=_cparams(("arbitrary",), VMEM_LIMIT),
        name="ada",
    )(cc, ada_w, ada_b)


def _head_rms(x, seg_ones, w):
    ms = _dot((x * x).astype(BF16), seg_ones) * (1.0 / HEAD_DIM)
    return x * lax.rsqrt(ms + EPS) * w


def _rope(y, cos, sin_signed):
    lane = lax.broadcasted_iota(jnp.int32, y.shape, 1)
    first_half = (lane % 32) < 16
    n = y.shape[1]
    swapped = jnp.where(first_half, pltpu.roll(y, n - 16, 1), pltpu.roll(y, 16, 1))
    return y * cos + swapped * sin_signed


def _inproj_kernel(x_ref, ctx_ref, mod_ref, nw_ref, w_ref, cos_ref, sin_ref, hn_ref, seg_ref,
                   wdec_ref, bdec_ref,
                   a_ref, la_ref, bv_ref, cq_ref, ck_ref, cv_ref, dq_ref, dk_ref, dv_ref, g_ref,
                   xs_ref):
    i = pl.program_id(1)

    @pl.when(i == 0)
    def _():
        xs_ref[...] = ctx_ref[0]

    @pl.when(i > 0)
    def _():
        xs_ref[...] = x_ref[0]

    x = xs_ref[...]
    ms = jnp.mean(x * x, axis=-1, keepdims=True)
    y = x * lax.rsqrt(ms + EPS) * nw_ref[...]
    h = y * (1.0 + mod_ref[0, 1:2, :]) + mod_ref[0, 0:1, :]
    hb = h.astype(BF16)

    def proj(c0, width):
        return _dot(hb, w_ref[:, c0:c0 + width])

    seg = seg_ref[...]
    cos = cos_ref[...]
    sin = sin_ref[...]

    a = proj(C_A, 512)
    a_ref[0, :, 0:128] = a[:, 0:128] * (GLA_DK ** -0.5)
    a_ref[0, :, 128:512] = a[:, 128:512]
    bv_ref[0] = proj(C_BV, 256).astype(BF16)
    cq_ref[0] = _rope(_head_rms(proj(C_CQ, 256), seg, hn_ref[0:1, :]), cos, sin).astype(BF16)
    ck_ref[0] = _rope(_head_rms(proj(C_CK, 256), seg, hn_ref[1:2, :]), cos, sin).astype(BF16)
    cv_ref[0] = proj(C_CV, 256).astype(BF16)
    dq_ref[0] = _head_rms(proj(C_DQ, 256), seg, hn_ref[2:3, :]).astype(BF16)
    dk_ref[0] = _head_rms(proj(C_DK, 256), seg, hn_ref[3:4, :]).astype(BF16)
    dv_ref[0] = proj(C_DV, 256).astype(BF16)
    g = proj(C_G, 1024)
    g_ref[0] = (g * (1.0 / (1.0 + jnp.exp(-g)))).astype(BF16)
    lr = proj(C_LR, 128)
    z = _dot(lr.astype(BF16), wdec_ref[...]) + bdec_ref[...]
    la_ref[0] = (jnp.minimum(z, 0.0) - jnp.log(1.0 + jnp.exp(-jnp.abs(z)))) * (1.0 / GLA_TAU)


def _inproj(x, ctx, mod, norm_w, w_perm, cos_t, sin_t, head_norms, seg_ones, wdec, bdec):
    b, seq, d = x.shape
    t_all = CTX_LEN + seq
    nblk = t_all // TM

    def tok(width):
        return pl.BlockSpec((1, TM, width), lambda bi, i: (bi, i, 0))

    def const(shape):
        return pl.BlockSpec(shape, lambda bi, i: (0,) * len(shape))

    def out(width, dtype):
        return jax.ShapeDtypeStruct((b, t_all, width), dtype)

    return pl.pallas_call(
        _inproj_kernel,
        out_shape=(out(512, F32), out(256, F32), out(256, BF16), out(256, BF16), out(256, BF16),
                   out(256, BF16), out(256, BF16), out(256, BF16), out(256, BF16), out(1024, BF16)),
        grid=(b, nblk),
        in_specs=[
            pl.BlockSpec((1, TM, d), lambda bi, i: (bi, jnp.maximum(i - 1, 0), 0)),
            pl.BlockSpec((1, TM, d), lambda bi, i: (bi, 0, 0)),
            pl.BlockSpec((1, 3, d), lambda bi, i: (jnp.where(i == 0, b, bi), 0, 0)),
            const((1, d)),
            const((d, W_COLS)),
            pl.BlockSpec((TM, 256), lambda bi, i: (i, 0)),
            pl.BlockSpec((TM, 256), lambda bi, i: (i, 0)),
            const((8, 256)),
            const((256, 256)),
            const((128, 256)),
            const((1, 256)),
        ],
        out_specs=(tok(512), tok(256), tok(256), tok(256), tok(256), tok(256), tok(256), tok(256),
                   tok(256), tok(1024)),
        scratch_shapes=[pltpu.VMEM((TM, d), F32)],
        compiler_params=_cparams(("parallel", "arbitrary"), VMEM_LIMIT),
        name="inproj",
    )(x, ctx, mod, norm_w, w_perm, cos_t, sin_t, head_norms, seg_ones, wdec, bdec)


def _gla_kernel(a_ref, la_ref, ones2_ref, tri_ref, pmask_ref, qmask_ref, vmask_ref, stmask_ref,
                o_ref, st_ref, rb_ref, *, rev, nbatch):
    @pl.when(pl.program_id(0) == 0)
    def _():
        st_ref[...] = jnp.zeros_like(st_ref)

    consts = (ones2_ref[...], tri_ref[...], pmask_ref[...], qmask_ref[...], vmask_ref[...], stmask_ref[...])
    for bi in range(nbatch):
        _gla_chunk(a_ref, la_ref, o_ref, st_ref, rb_ref, bi, rev, consts)


def _gla_chunk(a_ref, la_ref, o_ref, st_ref, rb_ref, bi, rev, consts):
    ones2, tri, pmask, qmask, vmask, stmask = consts
    c, sb = GLA_CHUNK, GLA_SUB
    nsub = c // sb
    hk = GLA_HEADS * GLA_DK
    hv = GLA_HEADS * GLA_DV

    q = a_ref[bi, :, 0:hk]
    k = a_ref[bi, :, hk:2 * hk]
    v = a_ref[bi, :, 2 * hk:2 * hk + hv]
    la = la_ref[bi]

    la_hi = la.astype(BF16)
    la_lo = (la - la_hi.astype(F32)).astype(BF16)
    cum = _dot(tri, la_hi) + _dot(tri, la_lo)
    cumx = cum - la

    rb_ref[bi, 0] = cum
    rb_ref[bi, 1] = cumx
    rb_ref[bi, 2] = k
    rb_ref[bi, 3] = v[:, 0:hk]
    rb_ref[bi, 4] = v[:, hk:hv]

    def brow(plane, r, n=sb):
        return jnp.broadcast_to(rb_ref[bi, plane, r:r + 1, :], (n, hk))

    end = 0 if rev else c - 1

    def first_row(blk):
        return sb * blk + (sb - 1 if rev else 0)

    def blk_of(p):
        return nsub - 1 - p if rev else p

    cref_rows = jnp.concatenate([brow(1, first_row(blk)) for blk in range(nsub)], axis=0)

    st = st_ref[bi]
    qi = q * jnp.exp(cum)
    inter = _dot_nt(qi.astype(BF16), st.astype(BF16))

    qd = q * jnp.exp(cum - cref_rows)
    q_st = jnp.concatenate([qd] * GLA_HEADS, axis=0) * qmask
    kt, vt = [], []
    for p in range(1, nsub):
        lo, hi = (c - sb * p, c) if rev else (0, sb * p)
        kt.append(k[lo:hi] * jnp.exp(brow(1, first_row(blk_of(p)), hi - lo) - cum[lo:hi]))
        vt.append(v[lo:hi])
    npad = pmask.shape[1] - sb * (nsub * (nsub - 1) // 2)
    kt.append(jnp.zeros((npad, hk), F32))
    vt.append(jnp.zeros((npad, hv), F32))
    kt = jnp.concatenate(kt, axis=0).astype(BF16)
    vt = jnp.concatenate(vt, axis=0).astype(BF16)
    sc = _dot_nt(q_st.astype(BF16), kt) * pmask
    off_st = _dot(sc.astype(BF16), vt) * vmask
    off = off_st[0:c]
    for h in range(1, GLA_HEADS):
        off = off + off_st[h * c:(h + 1) * c]

    t_loc = lax.broadcasted_iota(jnp.int32, (sb, hk), 0)
    valid = [(t_loc <= s) if rev else (t_loc >= s) for s in range(sb)]
    es = []
    for blk in range(nsub):
        r0 = sb * blk
        cb = cum[r0:r0 + sb]
        qb = q[r0:r0 + sb]
        for s in range(sb):
            diff = jnp.where(valid[s], cb - brow(0, r0 + s), -jnp.inf)
            es.append(jnp.exp(diff) * (qb * brow(2, r0 + s)))
    e = jnp.concatenate(es, axis=0).astype(BF16)
    r = _dot(e, ones2)
    dgs = []
    for blk in range(nsub):
        acc = None
        for s in range(sb):
            row0 = (blk * sb + s) * sb
            vrow = jnp.concatenate([brow(3, blk * sb + s), brow(4, blk * sb + s)], axis=1)
            term = r[row0:row0 + sb] * vrow
            acc = term if acc is None else acc + term
        dgs.append(acc)
    o_ref[bi] = inter + off + jnp.concatenate(dgs, axis=0)

    cum_end = rb_ref[bi, 0, end:end + 1, :]
    kd = k * jnp.exp(cum_end - cum)
    kv = _dot_tn(v.astype(BF16), kd.astype(BF16))
    st_ref[bi] = st * jnp.exp(cum_end) + kv * stmask


def _gla_consts(rev):
    c, sb = GLA_CHUNK, GLA_SUB
    nsub = c // sb
    t = np.arange(c)
    tri = (t[None, :] >= t[:, None]) if rev else (t[None, :] <= t[:, None])
    blk = t // sb
    p_of_t = (nsub - 1 - blk) if rev else blk
    col_p = np.concatenate([np.full(sb * p, p) for p in range(1, nsub)])
    col_p = np.concatenate([col_p, np.full(-col_p.size % 128, -1)])
    pmask = np.tile(p_of_t, GLA_HEADS)[:, None] == col_p[None, :]
    row_h = np.repeat(np.arange(GLA_HEADS), c)
    qmask = row_h[:, None] == (np.arange(GLA_HEADS * GLA_DK) // GLA_DK)[None, :]
    vmask = row_h[:, None] == (np.arange(GLA_HEADS * GLA_DV) // GLA_DV)[None, :]
    stmask = (np.arange(GLA_HEADS * GLA_DV) // GLA_DV)[:, None] == (np.arange(GLA_HEADS * GLA_DK) // GLA_DK)[None, :]
    f = lambda m: jnp.asarray(m, F32)
    return f(tri).astype(BF16), f(pmask), f(qmask), f(vmask), f(stmask)


def _gla(a, la, ones2, rev):
    b, t_all, _ = a.shape
    nch = t_all // GLA_CHUNK
    nctx = CTX_LEN // GLA_CHUNK
    direction = 1 if rev else 0
    consts = (ones2,) + _gla_consts(rev)

    def chunk(j):
        if not rev:
            return j
        return jnp.where(j < nctx, nctx - 1 - j, nch - 1 - (j - nctx))

    return pl.pallas_call(
        functools.partial(_gla_kernel, rev=rev, nbatch=b),
        out_shape=jax.ShapeDtypeStruct((b, t_all, 256), F32),
        grid=(nch,),
        in_specs=[pl.BlockSpec((b, GLA_CHUNK, 512), lambda j: (0, chunk(j), 0)),
                  pl.BlockSpec((b, GLA_CHUNK, 128), lambda j: (0, chunk(j), direction))]
                 + [pl.BlockSpec(t.shape, lambda j: (0, 0)) for t in consts],
        out_specs=pl.BlockSpec((b, GLA_CHUNK, 256), lambda j: (0, chunk(j), 0)),
        scratch_shapes=[pltpu.VMEM((b, 256, 128), F32), pltpu.VMEM((b, 5, GLA_CHUNK, 128), F32)],
        compiler_params=_cparams(("arbitrary",)),
        name="gla_bwd" if rev else "gla_fwd",
    )(a, la, *consts)


def _fnet_chan_kernel(v_ref, wc_ref, ws_ref, o_ref):
    v = v_ref[0]
    o_ref[0, :, :] = _dot(v, wc_ref[...]).astype(BF16)
    o_ref[1, :, :] = _dot(v, ws_ref[...]).astype(BF16)


def _fnet_chan(bv, wc, ws, row_blk_off, length):
    b = bv.shape[0]
    tl = min(TM, length)
    return pl.pallas_call(
        _fnet_chan_kernel,
        out_shape=jax.ShapeDtypeStruct((2, length, b * 256), BF16),
        grid=(b, length // tl),
        in_specs=[pl.BlockSpec((1, tl, 256), lambda bi, i: (bi, i + row_blk_off, 0)),
                  pl.BlockSpec((256, 256), lambda bi, i: (0, 0)),
                  pl.BlockSpec((256, 256), lambda bi, i: (0, 0))],
        out_specs=pl.BlockSpec((2, tl, 256), lambda bi, i: (0, i, bi)),
        compiler_params=_cparams(("parallel", "arbitrary")),
        name="fnet_chan",
    )(bv, wc, ws)


def _fnet_pos_kernel(d_ref, y_ref, wf_ref, o_ref, acc_ref, *, nb):
    kk = pl.program_id(1)

    @pl.when(kk == 0)
    def _():
        acc_ref[...] = jnp.zeros_like(acc_ref)

    acc_ref[...] += _dot(d_ref[...], y_ref[...])

    @pl.when(kk == pl.num_programs(1) - 1)
    def _():
        wf = wf_ref[...]
        for bi in range(nb):
            o_ref[bi] = _dot(acc_ref[:, bi * 256:(bi + 1) * 256].astype(BF16), wf).astype(BF16)


def _fnet_pos(dmat, y2, wf, nb):
    length, k2 = dmat.shape
    tm = min(512, length)
    tk = min(1024, k2)
    return pl.pallas_call(
        functools.partial(_fnet_pos_kernel, nb=nb),
        out_shape=jax.ShapeDtypeStruct((nb, length, 256), BF16),
        grid=(length // tm, k2 // tk),
        in_specs=[pl.BlockSpec((tm, tk), lambda i, kk: (i, kk)),
                  pl.BlockSpec((tk, nb * 256), lambda i, kk: (kk, 0)),
                  pl.BlockSpec((256, 256), lambda i, kk: (0, 0))],
        out_specs=pl.BlockSpec((nb, tm, 256), lambda i, kk: (0, i, 0)),
        scratch_shapes=[pltpu.VMEM((tm, nb * 256), F32)],
        compiler_params=_cparams(("parallel", "arbitrary"), VMEM_LIMIT),
        name="fnet_pos",
    )(dmat, y2, wf)


def _dft_tables(length):
    j = jnp.arange(length, dtype=jnp.int32)
    jk = (j[:, None] * j[None, :]) % length
    ang = jk.astype(F32) * (2.0 * math.pi / length)
    s = 1.0 / math.sqrt(length)
    return jnp.concatenate([jnp.cos(ang) * s, -jnp.sin(ang) * s], axis=1).astype(BF16)


def _chan_tables():
    j = np.arange(FNET_GW)
    ang = 2.0 * np.pi * ((j[:, None] * j[None, :]) % FNET_GW) / FNET_GW
    s = 1.0 / math.sqrt(FNET_GW)
    eye = np.eye(BRANCH_W // FNET_GW)
    wc = np.kron(eye, np.cos(ang) * s)
    ws = np.kron(eye, np.sin(ang) * s)
    return jnp.asarray(wc, F32).astype(BF16), jnp.asarray(ws, F32).astype(BF16)


def _fnet(bv, wf, row_blk_off, length):
    b = bv.shape[0]
    wc, ws = _chan_tables()
    y = _fnet_chan(bv, wc, ws, row_blk_off, length)
    y2 = y.reshape(2 * length, b * 256)
    return _fnet_pos(_dft_tables(length), y2, wf, b)


FNET_UNROLL = 8


def _fnet_grid_kernel(v_ref, cg_ref, sg_ref, w1_ref, w2_ref, wf_ref, o_ref, z_ref, g_ref, f_ref, *, rows, row_off):
    length = rows * GRID_W
    v = v_ref[0, row_off:row_off + length, :]
    wf = wf_ref[...]
    zr = _dot(v, _dot(cg_ref[...], wf).astype(BF16))
    zi = _dot(v, _dot(sg_ref[...], wf).astype(BF16))
    z_ref[0] = zr[:, 0:128].reshape(rows, GRID_W, 128)
    z_ref[1] = zr[:, 128:256].reshape(rows, GRID_W, 128)
    z_ref[2] = zi[:, 0:128].reshape(rows, GRID_W, 128)
    z_ref[3] = zi[:, 128:256].reshape(rows, GRID_W, 128)

    n8 = rows * 8
    for w0 in range(0, GRID_W, 8):
        def slab(c):
            return z_ref[c, :, w0:w0 + 8, :].reshape(n8, 128)
        rhs = jnp.concatenate([jnp.concatenate([slab(0), slab(1)], axis=1),
                               jnp.concatenate([slab(2), slab(3)], axis=1)], axis=0).astype(BF16)
        g = _dot(w1_ref[...], rhs)
        g_ref[0, :, w0:w0 + 8, :] = g[0:n8, 0:128].reshape(rows, 8, 128)
        g_ref[1, :, w0:w0 + 8, :] = g[0:n8, 128:256].reshape(rows, 8, 128)
        g_ref[2, :, w0:w0 + 8, :] = g[n8:2 * n8, 0:128].reshape(rows, 8, 128)
        g_ref[3, :, w0:w0 + 8, :] = g[n8:2 * n8, 128:256].reshape(rows, 8, 128)

    def stage_w(ka, carry):
        r0 = pl.multiple_of(ka * GRID_W, GRID_W)
        rhs = jnp.concatenate([jnp.concatenate([g_ref[0, ka], g_ref[1, ka]], axis=1),
                               jnp.concatenate([g_ref[2, ka], g_ref[3, ka]], axis=1)], axis=0).astype(BF16)
        y = _dot(w2_ref[pl.ds(r0, GRID_W), :], rhs)
        f_ref[0, pl.ds(ka, GRID_W, stride=rows), :] = y[:, 0:128]
        f_ref[1, pl.ds(ka, GRID_W, stride=rows), :] = y[:, 128:256]
        return carry

    lax.fori_loop(0, rows, stage_w, 0, unroll=FNET_UNROLL)
    o_ref[0, :, 0:128] = f_ref[0].astype(BF16)
    o_ref[0, :, 128:256] = f_ref[1].astype(BF16)


def _fnet_grid_tables(rows):
    length = rows * GRID_W
    j = np.arange(rows)
    ang = 2.0 * np.pi * ((j[:, None] * j[None, :]) % rows) / rows
    c, s = np.cos(ang), np.sin(ang)
    w1 = jnp.asarray(np.block([[c, -s], [-s, -c]]), F32)
    w1 = jnp.kron(w1, jnp.eye(8, dtype=F32)).astype(BF16)
    ka = jnp.arange(rows, dtype=jnp.int32)[:, None, None]
    kb = jnp.arange(GRID_W, dtype=jnp.int32)[None, :, None]
    w = jnp.arange(GRID_W, dtype=jnp.int32)[None, None, :]
    ang2 = ((w * (ka + rows * kb)) % length).astype(F32) * (2.0 * math.pi / length)
    scale = 1.0 / math.sqrt(length)
    w2 = jnp.concatenate([jnp.cos(ang2) * scale, jnp.sin(ang2) * scale], axis=2)
    return w1, w2.reshape(length, 2 * GRID_W).astype(BF16)


def _fnet_grid(bv, wf, seq):
    b, t_all, _ = bv.shape
    rows = seq // GRID_W
    wc, ws = _chan_tables()
    w1, w2 = _fnet_grid_tables(rows)

    def const(shape):
        return pl.BlockSpec(shape, lambda bi: (0,) * len(shape))

    return pl.pallas_call(
        functools.partial(_fnet_grid_kernel, rows=rows, row_off=CTX_LEN),
        out_shape=jax.ShapeDtypeStruct((b, seq, 256), BF16),
        grid=(b,),
        in_specs=[pl.BlockSpec((1, t_all, 256), lambda bi: (bi, 0, 0)), const((256, 256)), const((256, 256)),
                  const((16 * rows, 16 * rows)), const((seq, 2 * GRID_W)), const((256, 256))],
        out_specs=pl.BlockSpec((1, seq, 256), lambda bi: (bi, 0, 0)),
        scratch_shapes=[pltpu.VMEM((4, rows, GRID_W, 128), F32), pltpu.VMEM((4, rows, GRID_W, 128), F32),
                        pltpu.VMEM((2, seq, 128), F32)],
        compiler_params=_cparams(("parallel",), VMEM_LIMIT),
        name="fnet_grid",
    )(bv, wc, ws, w1, w2, wf)


def _stack_heads(qg):
    lane = lax.broadcasted_iota(jnp.int32, qg.shape, 1)
    zero = jnp.zeros_like(qg)
    return jnp.concatenate([jnp.where(lane < HEAD_DIM, qg, zero),
                            jnp.where(lane >= HEAD_DIM, qg, zero)], axis=0)


def _unstack_heads(o, rows):
    lane = lax.broadcasted_iota(jnp.int32, (rows, 2 * HEAD_DIM), 1)
    return jnp.where(lane < HEAD_DIM, o[0:rows], o[rows:2 * rows])


def _swa_kernel(sink_ref, q_ref, kp_ref, kc_ref, kn_ref, kx_ref, vp_ref, vc_ref, vn_ref, vx_ref,
                o_ref):
    n = pl.program_id(1)
    nb = pl.num_programs(1)
    blk = SWA_BLOCK
    rows = 2 * blk
    row = lax.broadcasted_iota(jnp.int32, (rows, 3 * blk), 0) % blk
    col = lax.broadcasted_iota(jnp.int32, (rows, 3 * blk), 1)
    rel = col - blk - row
    col_lo = jnp.where(n > 0, 0, blk)
    col_hi = jnp.where(n < nb - 1, 3 * blk, 2 * blk)
    valid = (jnp.abs(rel) <= SWA_WINDOW) & (col >= col_lo) & (col < col_hi)
    head_row = lax.broadcasted_iota(jnp.int32, (rows, 1), 0) < blk
    for g in range(2):
        ls = slice(128 * g, 128 * (g + 1))
        q_st = _stack_heads(q_ref[0, :, ls])
        kw = jnp.concatenate([kp_ref[0, :, ls], kc_ref[0, :, ls], kn_ref[0, :, ls]], axis=0)
        vw = jnp.concatenate([vp_ref[0, :, ls], vc_ref[0, :, ls], vn_ref[0, :, ls]], axis=0)
        s_loc = jnp.where(valid, _dot_nt(q_st, kw), -jnp.inf)
        s_ctx = _dot_nt(q_st, kx_ref[0, :, ls])
        sink = jnp.where(head_row, sink_ref[2 * g], sink_ref[2 * g + 1])
        m = jnp.maximum(jnp.maximum(jnp.max(s_loc, axis=-1, keepdims=True),
                                    jnp.max(s_ctx, axis=-1, keepdims=True)), sink)
        p_loc = jnp.exp(s_loc - m)
        p_ctx = jnp.exp(s_ctx - m)
        den = (jnp.sum(p_loc, axis=-1, keepdims=True) + jnp.sum(p_ctx, axis=-1, keepdims=True)
               + jnp.exp(sink - m))
        o = _dot(p_loc.astype(BF16), vw) + _dot(p_ctx.astype(BF16), vx_ref[0, :, ls])
        o_ref[0, :, ls] = _unstack_heads(o / den, blk).astype(BF16)


def _swa(sink, cq, ck, cv, seq):
    b = cq.shape[0]
    nb = seq // SWA_BLOCK
    off = CTX_LEN // SWA_BLOCK

    def at(f):
        return pl.BlockSpec((1, SWA_BLOCK, 256), lambda bi, n: (bi, f(n) + off, 0))

    prev = at(lambda n: jnp.maximum(n - 1, 0))
    cur = at(lambda n: n)
    nxt = at(lambda n: jnp.minimum(n + 1, nb - 1))
    cx = pl.BlockSpec((1, CTX_LEN, 256), lambda bi, n: (bi, 0, 0))
    return pl.pallas_call(
        _swa_kernel,
        out_shape=jax.ShapeDtypeStruct((b, seq, 256), BF16),
        grid=(b, nb),
        in_specs=[pl.BlockSpec(memory_space=pltpu.SMEM), cur, prev, cur, nxt, cx, prev, cur, nxt, cx],
        out_specs=pl.BlockSpec((1, SWA_BLOCK, 256), lambda bi, n: (bi, n, 0)),
        compiler_params=_cparams(("parallel", "arbitrary")),
        name="swa",
    )(sink, cq, ck, ck, ck, ck, cv, cv, cv, cv)


NA_ROWS_PER_STEP = 4


def _na_kernel(q_ref, k_ref, v_ref, bias_ref, o_ref, *, rows_total):
    kh = NA_KH
    nk = kh * GRID_W
    for rr in range(NA_ROWS_PER_STEP):
        r = pl.program_id(1) * NA_ROWS_PER_STEP + rr
        row_start = jnp.clip(r - kh // 2, 0, rows_total - kh)
        cls = r - row_start
        k0 = pl.multiple_of(CTX_LEN + row_start * GRID_W, GRID_W)
        qs = slice(GRID_W * rr, GRID_W * (rr + 1))
        for g in range(2):
            ls = slice(128 * g, 128 * (g + 1))
            q_st = _stack_heads(q_ref[0, qs, ls])
            s_nb = _dot_nt(q_st, k_ref[0, pl.ds(k0, nk), ls]) + bias_ref[cls, g]
            s_ctx = _dot_nt(q_st, k_ref[0, 0:CTX_LEN, ls])
            m = jnp.maximum(jnp.max(s_nb, axis=-1, keepdims=True), jnp.max(s_ctx, axis=-1, keepdims=True))
            p_nb = jnp.exp(s_nb - m)
            p_ctx = jnp.exp(s_ctx - m)
            den = jnp.sum(p_nb, axis=-1, keepdims=True) + jnp.sum(p_ctx, axis=-1, keepdims=True)
            o = (_dot(p_nb.astype(BF16), v_ref[0, pl.ds(k0, nk), ls])
                 + _dot(p_ctx.astype(BF16), v_ref[0, 0:CTX_LEN, ls]))
            o_ref[0, qs, ls] = _unstack_heads(o / den, GRID_W).astype(BF16)


def _bias_expand_kernel(rb_ref, oh_ref, o_ref):
    x = rb_ref[...]
    oh = oh_ref[...]
    hi = x.astype(BF16)
    r1 = x - hi.astype(F32)
    mid = r1.astype(BF16)
    lo = (r1 - mid.astype(F32)).astype(BF16)
    o_ref[...] = _dot(hi, oh) + _dot(mid, oh) + _dot(lo, oh)


def _na_bias_table(rel_bias):
    ndy, ndx = 2 * NA_KH - 1, 2 * NA_KW - 1
    cq = np.arange(GRID_W)
    col_start = np.clip(cq - NA_KW // 2, 0, GRID_W - NA_KW)
    col_ok = (cq[None, :] >= col_start[:, None]) & (cq[None, :] < col_start[:, None] + NA_KW)
    qi = jnp.arange(GRID_W, dtype=jnp.int32)
    dx = jnp.clip(qi[None, :] - qi[:, None], -(NA_KW - 1), NA_KW - 1) + (NA_KW - 1)
    onehot = (jnp.arange(128, dtype=jnp.int32)[:, None] == dx.reshape(1, GRID_W * GRID_W)).astype(BF16)
    rb = jnp.zeros((64, 128), F32).at[0:NA_HEADS * ndy, 0:ndx].set(rel_bias.reshape(NA_HEADS * ndy, ndx))
    p = pl.pallas_call(
        _bias_expand_kernel,
        out_shape=jax.ShapeDtypeStruct((64, GRID_W * GRID_W), F32),
        name="na_bias_expand",
    )(rb, onehot)
    p = p[0:NA_HEADS * ndy].reshape(NA_HEADS, ndy, GRID_W, GRID_W)
    p = jnp.where(col_ok[None, None], p, -jnp.inf)
    tabs = jnp.stack([p[:, NA_KH - 1 - c:2 * NA_KH - 1 - c] for c in range(NA_KH)])
    tabs = tabs.transpose(0, 1, 3, 2, 4)
    return tabs.reshape(NA_KH, NA_HEADS // 2, 2 * GRID_W, NA_KH * GRID_W)


def _na(dq, dk, dv, bias_tab, seq):
    b, t_all, _ = dq.shape
    rows_total = seq // GRID_W
    qrows = NA_ROWS_PER_STEP * GRID_W
    off = CTX_LEN // qrows
    full = pl.BlockSpec((1, t_all, 256), lambda bi, r: (bi, 0, 0))
    return pl.pallas_call(
        functools.partial(_na_kernel, rows_total=rows_total),
        out_shape=jax.ShapeDtypeStruct((b, seq, 256), BF16),
        grid=(b, rows_total // NA_ROWS_PER_STEP),
        in_specs=[pl.BlockSpec((1, qrows, 256), lambda bi, r: (bi, r + off, 0)), full, full,
                  pl.BlockSpec(bias_tab.shape, lambda bi, r: (0, 0, 0, 0))],
        out_specs=pl.BlockSpec((1, qrows, 256), lambda bi, r: (bi, r, 0)),
        compiler_params=_cparams(("parallel", "arbitrary"), VMEM_LIMIT),
        name="na",
    )(dq, dk, dv, bias_tab)


def _ctx_attn_kernel(sink_ref, q_ref, k_ref, v_ref, o_ref, *, has_sink):
    n = CTX_LEN
    head_row = lax.broadcasted_iota(jnp.int32, (2 * n, 1), 0) < n
    for g in range(2):
        ls = slice(128 * g, 128 * (g + 1))
        q_st = _stack_heads(q_ref[0, :, ls])
        s = _dot_nt(q_st, k_ref[0, :, ls])
        m = jnp.max(s, axis=-1, keepdims=True)
        if has_sink:
            sink = jnp.where(head_row, sink_ref[2 * g], sink_ref[2 * g + 1])
            m = jnp.maximum(m, sink)
        p = jnp.exp(s - m)
        den = jnp.sum(p, axis=-1, keepdims=True)
        if has_sink:
            den = den + jnp.exp(sink - m)
        o = _dot(p.astype(BF16), v_ref[0, :, ls])
        o_ref[0, :, ls] = _unstack_heads(o / den, n).astype(BF16)


def _ctx_attn(sink, q, k, v, has_sink):
    b = q.shape[0]
    blk = pl.BlockSpec((1, CTX_LEN, 256), lambda bi: (bi, 0, 0))
    return pl.pallas_call(
        functools.partial(_ctx_attn_kernel, has_sink=has_sink),
        out_shape=jax.ShapeDtypeStruct((b, CTX_LEN, 256), BF16),
        grid=(b,),
        in_specs=[pl.BlockSpec(memory_space=pltpu.SMEM), blk, blk, blk],
        out_specs=blk,
        compiler_params=_cparams(("parallel",)),
        name="ctx_attn_sink" if has_sink else "ctx_attn",
    )(sink, q, k, v)


def _outproj_kernel(x_ref, of_ref, ob_ref, fz_ref, co_ref, do_ref, g_ref, mod_ref, gn_ref, seg_ref,
                    w_ref, o_ref):
    a = of_ref[0] + ob_ref[0]
    ms = _dot((a * a).astype(BF16), seg_ref[...]) * (1.0 / GLA_DV)
    a = a * lax.rsqrt(ms + EPS) * gn_ref[...]
    g = g_ref[0].astype(F32)
    mixed = jnp.concatenate([a * g[:, 0:256], fz_ref[0].astype(F32) * g[:, 256:512],
                             co_ref[0].astype(F32) * g[:, 512:768], do_ref[0].astype(F32) * g[:, 768:1024]],
                            axis=1)
    y = _dot(mixed.astype(BF16), w_ref[...])
    o_ref[0] = x_ref[0] + mod_ref[0, 2:3, :] * y


def _outproj(x, o_f, o_b, fz, c_o, d_o, g, mod, gla_norm, seg_ones, w_out, is_ctx):
    b, rows, d = x.shape
    nblk = rows // TM
    comb = 0 if is_ctx else CTX_LEN // TM
    nbatch = b

    def own(width):
        return pl.BlockSpec((1, TM, width), lambda bi, i: (bi, i, 0))

    def combined(width):
        return pl.BlockSpec((1, TM, width), lambda bi, i: (bi, i + comb, 0))

    def const(shape):
        return pl.BlockSpec(shape, lambda bi, i: (0,) * len(shape))

    mod_spec = pl.BlockSpec((1, 3, d), (lambda bi, i: (nbatch, 0, 0)) if is_ctx else (lambda bi, i: (bi, 0, 0)))
    return pl.pallas_call(
        _outproj_kernel,
        out_shape=jax.ShapeDtypeStruct((b, rows, d), F32),
        grid=(b, nblk),
        in_specs=[own(d), combined(256), combined(256), own(256), own(256), own(256), combined(1024),
                  mod_spec, const((1, 256)), const((256, 256)), const((d, d))],
        out_specs=own(d),
        compiler_params=_cparams(("parallel", "arbitrary"), VMEM_LIMIT),
        name="outproj_ctx" if is_ctx else "outproj",
    )(x, o_f, o_b, fz, c_o, d_o, g, mod, gla_norm, seg_ones, w_out)


def _permute_w_in(w):
    s = lambda lo, hi: w[:, lo:hi]
    ck, cv = s(1568, 1696), s(1696, 1824)
    dup = lambda t: jnp.concatenate([t[:, :64], t[:, :64], t[:, 64:], t[:, 64:]], axis=1)
    pad = jnp.zeros((w.shape[0], 128 - 2 * GLA_RANK), w.dtype)
    return jnp.concatenate([
        s(0, 512),
        s(800, 1056),
        s(1312, 1568), dup(ck), dup(cv),
        s(2080, 2336), s(2336, 2592), s(2592, 2848),
        s(512, 768), s(1056, 1312), s(1824, 2080), s(2848, 3104),
        s(768, 800), pad,
    ], axis=1)


def _rope_tables(seq):
    t = np.arange(seq)
    axis_dim = HEAD_DIM // 2
    inv = ROPE_BASE ** (-np.arange(0, axis_dim, 2, dtype=np.float64) / axis_dim)
    row = (t // GRID_W).astype(np.float64)
    col = (t % GRID_W).astype(np.float64)
    ang = np.concatenate([row[:, None] * inv, row[:, None] * inv, col[:, None] * inv, col[:, None] * inv], axis=1)
    cos = np.cos(ang)
    sign = np.concatenate([-np.ones(16), np.ones(16), -np.ones(16), np.ones(16)])
    sin = np.sin(ang) * sign
    cos = np.concatenate([np.ones((CTX_LEN, HEAD_DIM)), cos], axis=0)
    sin = np.concatenate([np.zeros((CTX_LEN, HEAD_DIM)), sin], axis=0)
    return (jnp.asarray(np.tile(cos, (1, 4)), F32), jnp.asarray(np.tile(sin, (1, 4)), F32))


def _seg_ones(width, seg):
    i = np.arange(width)
    return jnp.asarray((i[:, None] // seg) == (i[None, :] // seg), BF16)


def _gla_ones2():
    r = np.arange(GLA_HEADS * GLA_DK) // GLA_DK
    c = np.arange(GLA_HEADS * GLA_DV) // GLA_DV
    return jnp.asarray(r[:, None] == c[None, :], BF16)


def _layer(x, ctx, cc, need_ctx, tables, norm_w, ada_w, ada_b, w_in, w_out, gla_dec_w, gla_dec_b,
           gla_out_norm, fnet_w, swa_q_norm, swa_k_norm, swa_sink, na_q_norm, na_k_norm, na_rel_bias):
    b, seq, d = x.shape
    cos_t, sin_t, seg64, ones2 = tables
    mod = _ada(cc, ada_w.astype(BF16), ada_b[None, :])[:b + 1].reshape(b + 1, 3, d)

    qs = HEAD_DIM ** -0.5
    head_norms = jnp.stack([jnp.tile(swa_q_norm, 4) * qs, jnp.tile(swa_k_norm, 4),
                            jnp.tile(na_q_norm, 4) * qs, jnp.tile(na_k_norm, 4)])
    head_norms = jnp.concatenate([head_norms, jnp.zeros((4, 256), F32)], axis=0)
    wdec = jnp.zeros((128, 256), F32)
    wdec = wdec.at[0:GLA_RANK, 0:128].set(gla_dec_w[0]).at[GLA_RANK:2 * GLA_RANK, 128:256].set(gla_dec_w[1])
    bdec = gla_dec_b.reshape(1, 256)

    a, la, bv, cq, ck, cv, dq, dk, dv, g = _inproj(
        x, ctx, mod, norm_w[None, :], _permute_w_in(w_in).astype(BF16), cos_t, sin_t, head_norms, seg64,
        wdec.astype(BF16), bdec)

    o_f = _gla(a, la, ones2, rev=False)
    o_b = _gla(a, la, ones2, rev=True)
    wf = fnet_w.astype(BF16)
    fz = _fnet_grid(bv, wf, seq)
    c_o = _swa(swa_sink, cq, ck, cv, seq)
    d_o = _na(dq, dk, dv, _na_bias_table(na_rel_bias), seq)

    gn = jnp.tile(gla_out_norm, 4)[None, :]
    wo = w_out.astype(BF16)
    x_new = _outproj(x, o_f, o_b, fz, c_o, d_o, g, mod, gn, seg64, wo, is_ctx=False)
    ctx_new = None
    if need_ctx:
        fzc = _fnet(bv, wf, 0, CTX_LEN)
        c_oc = _ctx_attn(swa_sink, cq, ck, cv, has_sink=True)
        d_oc = _ctx_attn(swa_sink, dq, dk, dv, has_sink=False)
        ctx_new = _outproj(ctx, o_f, o_b, fzc, c_oc, d_oc, g, mod, gn, seg64, wo, is_ctx=True)
    return x_new, ctx_new


def kernel(x, c, ctx, c_ctx, norm_w, ada_w, ada_b, w_in, w_out, gla_dec_w, gla_dec_b, gla_out_norm,
           fnet_w, swa_q_norm, swa_k_norm, swa_sink, na_q_norm, na_k_norm, na_rel_bias):
    b, seq, d = x.shape
    depth = norm_w.shape[0]
    assert ctx.shape[1] == CTX_LEN and seq % TM == 0 and seq % (NA_KH * GRID_W) == 0
    tables = _rope_tables(seq) + (_seg_ones(256, HEAD_DIM), _gla_ones2())
    cc = jnp.concatenate([c, c_ctx[None, :], jnp.zeros((8 - (b + 1) % 8, d), F32)], axis=0)
    for i in range(depth):
        x, ctx = _layer(x, ctx, cc, i < depth - 1, tables, norm_w[i], ada_w[i], ada_b[i], w_in[i], w_out[i],
                        gla_dec_w[i], gla_dec_b[i], gla_out_norm[i], fnet_w[i], swa_q_norm[i],
                        swa_k_norm[i], swa_sink[i], na_q_norm[i], na_k_norm[i], na_rel_bias[i])
    return x
```

```python
import functools
import math

import numpy as np
import jax
import jax.numpy as jnp
from jax import lax
from jax.experimental import pallas as pl
from jax.experimental.pallas import tpu as pltpu

F32 = jnp.float32
BF16 = jnp.bfloat16

GRID_W = 64
CTX_LEN = 256
HEAD_DIM = 64
BRANCH_W = 256
EPS = 1e-6
ROPE_BASE = 10000.0
GLA_HEADS = 4
GLA_DK = 32
GLA_DV = 64
GLA_RANK = 16
GLA_TAU = 16.0
GLA_CHUNK = 64
GLA_SUB = 8
FNET_GW = 64
SWA_BLOCK = 128
SWA_WINDOW = 128
NA_KH = 8
NA_KW = 16
NA_HEADS = 4

TM = 256
VMEM_LIMIT = 48 * 1024 * 1024

C_A = 0
C_BV = 512
C_CQ = 768
C_CK = 1024
C_CV = 1280
C_DQ = 1536
C_DK = 1792
C_DV = 2048
C_G = 2304
C_LR = 3328
W_COLS = 3456


def _dot(a, b):
    return jnp.dot(a, b, preferred_element_type=F32)


def _dot_nt(a, b):
    return lax.dot_general(a, b, (((1,), (1,)), ((), ())), preferred_element_type=F32)


def _dot_tn(a, b):
    return lax.dot_general(a, b, (((0,), (0,)), ((), ())), preferred_element_type=F32)


def _cparams(sem, vmem=None):
    return pltpu.CompilerParams(dimension_semantics=sem, vmem_limit_bytes=vmem)


def _ada_kernel(c_ref, w_ref, b_ref, o_ref):
    c = c_ref[...]
    s = c * (1.0 / (1.0 + jnp.exp(-c)))
    o_ref[...] = _dot(s.astype(BF16), w_ref[...]) + b_ref[...]


def _ada(cc, ada_w, ada_b):
    r, d = cc.shape
    n = ada_w.shape[1]
    return pl.pallas_call(
        _ada_kernel,
        out_shape=jax.ShapeDtypeStruct((r, n), F32),
        grid=(1,),
        in_specs=[pl.BlockSpec((r, d), lambda i: (0, 0)),
                  pl.BlockSpec((d, n), lambda i: (0, 0)),
                  pl.BlockSpec((1, n), lambda i: (0, 0))],
        out_specs=pl.BlockSpec((r, n), lambda i: (0, 0)),
        compiler_params=_cparams(("arbitrary",), VMEM_LIMIT),
        name="ada",
    )(cc, ada_w, ada_b)


def _head_rms(x, seg_ones, w):
    ms = _dot((x * x).astype(BF16), seg_ones) * (1.0 / HEAD_DIM)
    return x * lax.rsqrt(ms + EPS) * w


def _rope(y, cos, sin_signed):
    lane = lax.broadcasted_iota(jnp.int32, y.shape, 1)
    first_half = (lane % 32) < 16
    n = y.shape[1]
    swapped = jnp.where(first_half, pltpu.roll(y, n - 16, 1), pltpu.roll(y, 16, 1))
    return y * cos + swapped * sin_signed


def _inproj_kernel(x_ref, ctx_ref, mod_ref, nw_ref, w_ref, cos_ref, sin_ref, hn_ref, seg_ref,
                   wdec_ref, bdec_ref,
                   a_ref, la_ref, bv_ref, cq_ref, ck_ref, cv_ref, dq_ref, dk_ref, dv_ref, g_ref,
                   xs_ref):
    i = pl.program_id(1)

    @pl.when(i == 0)
    def _():
        xs_ref[...] = ctx_ref[0]

    @pl.when(i > 0)
    def _():
        xs_ref[...] = x_ref[0]

    x = xs_ref[...]
    ms = jnp.mean(x * x, axis=-1, keepdims=True)
    y = x * lax.rsqrt(ms + EPS) * nw_ref[...]
    h = y * (1.0 + mod_ref[0, 1:2, :]) + mod_ref[0, 0:1, :]
    hb = h.astype(BF16)

    def proj(c0, width):
        return _dot(hb, w_ref[:, c0:c0 + width])

    seg = seg_ref[...]
    cos = cos_ref[...]
    sin = sin_ref[...]

    a = proj(C_A, 512)
    a_ref[0, :, 0:128] = a[:, 0:128] * (GLA_DK ** -0.5)
    a_ref[0, :, 128:512] = a[:, 128:512]
    bv_ref[0] = proj(C_BV, 256).astype(BF16)
    cq_ref[0] = _rope(_head_rms(proj(C_CQ, 256), seg, hn_ref[0:1, :]), cos, sin).astype(BF16)
    ck_ref[0] = _rope(_head_rms(proj(C_CK, 256), seg, hn_ref[1:2, :]), cos, sin).astype(BF16)
    cv_ref[0] = proj(C_CV, 256).astype(BF16)
    dq_ref[0] = _head_rms(proj(C_DQ, 256), seg, hn_ref[2:3, :]).astype(BF16)
    dk_ref[0] = _head_rms(proj(C_DK, 256), seg, hn_ref[3:4, :]).astype(BF16)
    dv_ref[0] = proj(C_DV, 256).astype(BF16)
    g = proj(C_G, 1024)
    g_ref[0] = (g * (1.0 / (1.0 + jnp.exp(-g)))).astype(BF16)
    lr = proj(C_LR, 128)
    z = _dot(lr.astype(BF16), wdec_ref[...]) + bdec_ref[...]
    la_ref[0] = (jnp.minimum(z, 0.0) - jnp.log(1.0 + jnp.exp(-jnp.abs(z)))) * (1.0 / GLA_TAU)


def _inproj(x, ctx, mod, norm_w, w_perm, cos_t, sin_t, head_norms, seg_ones, wdec, bdec):
    b, seq, d = x.shape
    t_all = CTX_LEN + seq
    nblk = t_all // TM

    def tok(width):
        return pl.BlockSpec((1, TM, width), lambda bi, i: (bi, i, 0))

    def const(shape):
        return pl.BlockSpec(shape, lambda bi, i: (0,) * len(shape))

    def out(width, dtype):
        return jax.ShapeDtypeStruct((b, t_all, width), dtype)

    return pl.pallas_call(
        _inproj_kernel,
        out_shape=(out(512, F32), out(256, F32), out(256, BF16), out(256, BF16), out(256, BF16),
                   out(256, BF16), out(256, BF16), out(256, BF16), out(256, BF16), out(1024, BF16)),
        grid=(b, nblk),
        in_specs=[
            pl.BlockSpec((1, TM, d), lambda bi, i: (bi, jnp.maximum(i - 1, 0), 0)),
            pl.BlockSpec((1, TM, d), lambda bi, i: (bi, 0, 0)),
            pl.BlockSpec((1, 3, d), lambda bi, i: (jnp.where(i == 0, b, bi), 0, 0)),
            const((1, d)),
            const((d, W_COLS)),
            pl.BlockSpec((TM, 256), lambda bi, i: (i, 0)),
            pl.BlockSpec((TM, 256), lambda bi, i: (i, 0)),
            const((8, 256)),
            const((256, 256)),
            const((128, 256)),
            const((1, 256)),
        ],
        out_specs=(tok(512), tok(256), tok(256), tok(256), tok(256), tok(256), tok(256), tok(256),
                   tok(256), tok(1024)),
        scratch_shapes=[pltpu.VMEM((TM, d), F32)],
        compiler_params=_cparams(("parallel", "arbitrary"), VMEM_LIMIT),
        name="inproj",
    )(x, ctx, mod, norm_w, w_perm, cos_t, sin_t, head_norms, seg_ones, wdec, bdec)


def _gla_kernel(af_ref, ab_ref, laf_ref, lab_ref, ones2_ref, trif_ref, trib_ref, pmf_ref, pmb_ref,
                qmask_ref, vmask_ref, stmask_ref, of_ref, ob_ref, st_ref, rb_ref, *, nbatch):
    @pl.when(pl.program_id(0) == 0)
    def _():
        st_ref[...] = jnp.zeros_like(st_ref)

    shared = (qmask_ref[...], vmask_ref[...], stmask_ref[...])
    consts_f = (ones2_ref[...], trif_ref[...], pmf_ref[...]) + shared
    consts_b = (ones2_ref[...], trib_ref[...], pmb_ref[...]) + shared
    for bi in range(nbatch):
        _gla_chunk(af_ref, laf_ref, of_ref, st_ref.at[0], rb_ref.at[0], bi, False, consts_f)
        _gla_chunk(ab_ref, lab_ref, ob_ref, st_ref.at[1], rb_ref.at[1], bi, True, consts_b)


def _gla_chunk(a_ref, la_ref, o_ref, st_ref, rb_ref, bi, rev, consts):
    ones2, tri, pmask, qmask, vmask, stmask = consts
    c, sb = GLA_CHUNK, GLA_SUB
    nsub = c // sb
    hk = GLA_HEADS * GLA_DK
    hv = GLA_HEADS * GLA_DV

    q = a_ref[bi, :, 0:hk]
    k = a_ref[bi, :, hk:2 * hk]
    v = a_ref[bi, :, 2 * hk:2 * hk + hv]
    la = la_ref[bi]

    la_hi = la.astype(BF16)
    la_lo = (la - la_hi.astype(F32)).astype(BF16)
    cum = _dot(tri, la_hi) + _dot(tri, la_lo)
    cumx = cum - la

    rb_ref[bi, 0] = cum
    rb_ref[bi, 1] = cumx
    rb_ref[bi, 2] = k
    rb_ref[bi, 3] = v[:, 0:hk]
    rb_ref[bi, 4] = v[:, hk:hv]

    def brow(plane, r, n=sb):
        return jnp.broadcast_to(rb_ref[bi, plane, r:r + 1, :], (n, hk))

    end = 0 if rev else c - 1

    def first_row(blk):
        return sb * blk + (sb - 1 if rev else 0)

    def blk_of(p):
        return nsub - 1 - p if rev else p

    cref_rows = jnp.concatenate([brow(1, first_row(blk)) for blk in range(nsub)], axis=0)

    st = st_ref[bi]
    qi = q * jnp.exp(cum)
    inter = _dot_nt(qi.astype(BF16), st.astype(BF16))

    qd = q * jnp.exp(cum - cref_rows)
    q_st = jnp.concatenate([qd] * GLA_HEADS, axis=0) * qmask
    kt, vt = [], []
    for p in range(1, nsub):
        lo, hi = (c - sb * p, c) if rev else (0, sb * p)
        kt.append(k[lo:hi] * jnp.exp(brow(1, first_row(blk_of(p)), hi - lo) - cum[lo:hi]))
        vt.append(v[lo:hi])
    npad = pmask.shape[1] - sb * (nsub * (nsub - 1) // 2)
    kt.append(jnp.zeros((npad, hk), F32))
    vt.append(jnp.zeros((npad, hv), F32))
    kt = jnp.concatenate(kt, axis=0).astype(BF16)
    vt = jnp.concatenate(vt, axis=0).astype(BF16)
    sc = _dot_nt(q_st.astype(BF16), kt) * pmask
    off_st = _dot(sc.astype(BF16), vt) * vmask
    off = off_st[0:c]
    for h in range(1, GLA_HEADS):
        off = off + off_st[h * c:(h + 1) * c]

    t_loc = lax.broadcasted_iota(jnp.int32, (sb, hk), 0)
    valid = [(t_loc <= s) if rev else (t_loc >= s) for s in range(sb)]
    es = []
    for blk in range(nsub):
        r0 = sb * blk
        cb = cum[r0:r0 + sb]
        qb = q[r0:r0 + sb]
        for s in range(sb):
            diff = jnp.where(valid[s], cb - brow(0, r0 + s), -jnp.inf)
            es.append(jnp.exp(diff) * (qb * brow(2, r0 + s)))
    e = jnp.concatenate(es, axis=0).astype(BF16)
    r = _dot(e, ones2)
    dgs = []
    for blk in range(nsub):
        acc = None
        for s in range(sb):
            row0 = (blk * sb + s) * sb
            vrow = jnp.concatenate([brow(3, blk * sb + s), brow(4, blk * sb + s)], axis=1)
            term = r[row0:row0 + sb] * vrow
            acc = term if acc is None else acc + term
        dgs.append(acc)
    o_ref[bi] = (inter + off + jnp.concatenate(dgs, axis=0)).astype(BF16)

    cum_end = rb_ref[bi, 0, end:end + 1, :]
    kd = k * jnp.exp(cum_end - cum)
    kv = _dot_tn(v.astype(BF16), kd.astype(BF16))
    st_ref[bi] = st * jnp.exp(cum_end) + kv * stmask


def _gla_consts(rev):
    c, sb = GLA_CHUNK, GLA_SUB
    nsub = c // sb
    t = np.arange(c)
    tri = (t[None, :] >= t[:, None]) if rev else (t[None, :] <= t[:, None])
    blk = t // sb
    p_of_t = (nsub - 1 - blk) if rev else blk
    col_p = np.concatenate([np.full(sb * p, p) for p in range(1, nsub)])
    col_p = np.concatenate([col_p, np.full(-col_p.size % 128, -1)])
    pmask = np.tile(p_of_t, GLA_HEADS)[:, None] == col_p[None, :]
    row_h = np.repeat(np.arange(GLA_HEADS), c)
    qmask = row_h[:, None] == (np.arange(GLA_HEADS * GLA_DK) // GLA_DK)[None, :]
    vmask = row_h[:, None] == (np.arange(GLA_HEADS * GLA_DV) // GLA_DV)[None, :]
    stmask = (np.arange(GLA_HEADS * GLA_DV) // GLA_DV)[:, None] == (np.arange(GLA_HEADS * GLA_DK) // GLA_DK)[None, :]
    f = lambda m: jnp.asarray(m, F32)
    return f(tri).astype(BF16), f(pmask), f(qmask), f(vmask), f(stmask)


def _gla(a, la, ones2):
    b, t_all, _ = a.shape
    nch = t_all // GLA_CHUNK
    nctx = CTX_LEN // GLA_CHUNK
    tri_f, pm_f, qmask, vmask, stmask = _gla_consts(False)
    tri_b, pm_b, _, _, _ = _gla_consts(True)
    consts = (ones2, tri_f, tri_b, pm_f, pm_b, qmask, vmask, stmask)

    def fwd(j):
        return j

    def bwd(j):
        return jnp.where(j < nctx, nctx - 1 - j, nch - 1 - (j - nctx))

    out = jax.ShapeDtypeStruct((b, t_all, 256), BF16)
    return pl.pallas_call(
        functools.partial(_gla_kernel, nbatch=b),
        out_shape=(out, out),
        grid=(nch,),
        in_specs=[pl.BlockSpec((b, GLA_CHUNK, 512), lambda j: (0, fwd(j), 0)),
                  pl.BlockSpec((b, GLA_CHUNK, 512), lambda j: (0, bwd(j), 0)),
                  pl.BlockSpec((b, GLA_CHUNK, 128), lambda j: (0, fwd(j), 0)),
                  pl.BlockSpec((b, GLA_CHUNK, 128), lambda j: (0, bwd(j), 1))]
                 + [pl.BlockSpec(t.shape, lambda j: (0, 0)) for t in consts],
        out_specs=(pl.BlockSpec((b, GLA_CHUNK, 256), lambda j: (0, fwd(j), 0)),
                   pl.BlockSpec((b, GLA_CHUNK, 256), lambda j: (0, bwd(j), 0))),
        scratch_shapes=[pltpu.VMEM((2, b, 256, 128), F32), pltpu.VMEM((2, b, 5, GLA_CHUNK, 128), F32)],
        compiler_params=_cparams(("arbitrary",)),
        name="gla",
    )(a, a, la, la, *consts)


def _fnet_chan_kernel(v_ref, wc_ref, ws_ref, o_ref):
    v = v_ref[0]
    o_ref[0, :, :] = _dot(v, wc_ref[...]).astype(BF16)
    o_ref[1, :, :] = _dot(v, ws_ref[...]).astype(BF16)


def _fnet_chan(bv, wc, ws, row_blk_off, length):
    b = bv.shape[0]
    tl = min(TM, length)
    return pl.pallas_call(
        _fnet_chan_kernel,
        out_shape=jax.ShapeDtypeStruct((2, length, b * 256), BF16),
        grid=(b, length // tl),
        in_specs=[pl.BlockSpec((1, tl, 256), lambda bi, i: (bi, i + row_blk_off, 0)),
                  pl.BlockSpec((256, 256), lambda bi, i: (0, 0)),
                  pl.BlockSpec((256, 256), lambda bi, i: (0, 0))],
        out_specs=pl.BlockSpec((2, tl, 256), lambda bi, i: (0, i, bi)),
        compiler_params=_cparams(("parallel", "arbitrary")),
        name="fnet_chan",
    )(bv, wc, ws)


def _fnet_pos_kernel(d_ref, y_ref, wf_ref, o_ref, acc_ref, *, nb):
    kk = pl.program_id(1)

    @pl.when(kk == 0)
    def _():
        acc_ref[...] = jnp.zeros_like(acc_ref)

    acc_ref[...] += _dot(d_ref[...], y_ref[...])

    @pl.when(kk == pl.num_programs(1) - 1)
    def _():
        wf = wf_ref[...]
        for bi in range(nb):
            o_ref[bi] = _dot(acc_ref[:, bi * 256:(bi + 1) * 256].astype(BF16), wf).astype(BF16)


def _fnet_pos(dmat, y2, wf, nb):
    length, k2 = dmat.shape
    tm = min(512, length)
    tk = min(1024, k2)
    return pl.pallas_call(
        functools.partial(_fnet_pos_kernel, nb=nb),
        out_shape=jax.ShapeDtypeStruct((nb, length, 256), BF16),
        grid=(length // tm, k2 // tk),
        in_specs=[pl.BlockSpec((tm, tk), lambda i, kk: (i, kk)),
                  pl.BlockSpec((tk, nb * 256), lambda i, kk: (kk, 0)),
                  pl.BlockSpec((256, 256), lambda i, kk: (0, 0))],
        out_specs=pl.BlockSpec((nb, tm, 256), lambda i, kk: (0, i, 0)),
        scratch_shapes=[pltpu.VMEM((tm, nb * 256), F32)],
        compiler_params=_cparams(("parallel", "arbitrary"), VMEM_LIMIT),
        name="fnet_pos",
    )(dmat, y2, wf)


def _dft_tables(length):
    j = jnp.arange(length, dtype=jnp.int32)
    jk = (j[:, None] * j[None, :]) % length
    ang = jk.astype(F32) * (2.0 * math.pi / length)
    s = 1.0 / math.sqrt(length)
    return jnp.concatenate([jnp.cos(ang) * s, -jnp.sin(ang) * s], axis=1).astype(BF16)


def _chan_tables():
    j = np.arange(FNET_GW)
    ang = 2.0 * np.pi * ((j[:, None] * j[None, :]) % FNET_GW) / FNET_GW
    s = 1.0 / math.sqrt(FNET_GW)
    eye = np.eye(BRANCH_W // FNET_GW)
    wc = np.kron(eye, np.cos(ang) * s)
    ws = np.kron(eye, np.sin(ang) * s)
    return jnp.asarray(wc, F32).astype(BF16), jnp.asarray(ws, F32).astype(BF16)


def _fnet(bv, wf, row_blk_off, length):
    b = bv.shape[0]
    wc, ws = _chan_tables()
    y = _fnet_chan(bv, wc, ws, row_blk_off, length)
    y2 = y.reshape(2 * length, b * 256)
    return _fnet_pos(_dft_tables(length), y2, wf, b)


FNET_UNROLL = 8


def _fnet_grid_kernel(v_ref, cg_ref, sg_ref, w1_ref, w2_ref, wf_ref, o_ref, z_ref, g_ref, f_ref, *, rows, row_off):
    length = rows * GRID_W
    v = v_ref[0, row_off:row_off + length, :]
    wf = wf_ref[...]
    zr = _dot(v, _dot(cg_ref[...], wf).astype(BF16))
    zi = _dot(v, _dot(sg_ref[...], wf).astype(BF16))
    z_ref[0] = zr[:, 0:128].reshape(rows, GRID_W, 128)
    z_ref[1] = zr[:, 128:256].reshape(rows, GRID_W, 128)
    z_ref[2] = zi[:, 0:128].reshape(rows, GRID_W, 128)
    z_ref[3] = zi[:, 128:256].reshape(rows, GRID_W, 128)

    n8 = rows * 8
    for w0 in range(0, GRID_W, 8):
        def slab(c):
            return z_ref[c, :, w0:w0 + 8, :].reshape(n8, 128)
        rhs = jnp.concatenate([jnp.concatenate([slab(0), slab(1)], axis=1),
                               jnp.concatenate([slab(2), slab(3)], axis=1)], axis=0).astype(BF16)
        g = _dot(w1_ref[...], rhs)
        g_ref[0, :, w0:w0 + 8, :] = g[0:n8, 0:128].reshape(rows, 8, 128)
        g_ref[1, :, w0:w0 + 8, :] = g[0:n8, 128:256].reshape(rows, 8, 128)
        g_ref[2, :, w0:w0 + 8, :] = g[n8:2 * n8, 0:128].reshape(rows, 8, 128)
        g_ref[3, :, w0:w0 + 8, :] = g[n8:2 * n8, 128:256].reshape(rows, 8, 128)

    def stage_w(ka, carry):
        r0 = pl.multiple_of(ka * GRID_W, GRID_W)
        rhs = jnp.concatenate([jnp.concatenate([g_ref[0, ka], g_ref[1, ka]], axis=1),
                               jnp.concatenate([g_ref[2, ka], g_ref[3, ka]], axis=1)], axis=0).astype(BF16)
        y = _dot(w2_ref[pl.ds(r0, GRID_W), :], rhs)
        f_ref[0, pl.ds(ka, GRID_W, stride=rows), :] = y[:, 0:128]
        f_ref[1, pl.ds(ka, GRID_W, stride=rows), :] = y[:, 128:256]
        return carry

    lax.fori_loop(0, rows, stage_w, 0, unroll=FNET_UNROLL)
    o_ref[0, :, 0:128] = f_ref[0].astype(BF16)
    o_ref[0, :, 128:256] = f_ref[1].astype(BF16)


def _fnet_grid_tables(rows):
    length = rows * GRID_W
    j = np.arange(rows)
    ang = 2.0 * np.pi * ((j[:, None] * j[None, :]) % rows) / rows
    c, s = np.cos(ang), np.sin(ang)
    w1 = np.kron(np.block([[c, -s], [-s, -c]]), np.eye(8))
    w1 = jnp.asarray(w1, F32).astype(BF16)
    ka = jnp.arange(rows, dtype=jnp.int32)[:, None, None]
    kb = jnp.arange(GRID_W, dtype=jnp.int32)[None, :, None]
    w = jnp.arange(GRID_W, dtype=jnp.int32)[None, None, :]
    ang2 = ((w * (ka + rows * kb)) % length).astype(F32) * (2.0 * math.pi / length)
    scale = 1.0 / math.sqrt(length)
    w2 = jnp.concatenate([jnp.cos(ang2) * scale, jnp.sin(ang2) * scale], axis=2)
    return w1, w2.reshape(length, 2 * GRID_W).astype(BF16)


def _fnet_grid(bv, wf, seq):
    b, t_all, _ = bv.shape
    rows = seq // GRID_W
    wc, ws = _chan_tables()
    w1, w2 = _fnet_grid_tables(rows)

    def const(shape):
        return pl.BlockSpec(shape, lambda bi: (0,) * len(shape))

    return pl.pallas_call(
        functools.partial(_fnet_grid_kernel, rows=rows, row_off=CTX_LEN),
        out_shape=jax.ShapeDtypeStruct((b, seq, 256), BF16),
        grid=(b,),
        in_specs=[pl.BlockSpec((1, t_all, 256), lambda bi: (bi, 0, 0)), const((256, 256)), const((256, 256)),
                  const((16 * rows, 16 * rows)), const((seq, 2 * GRID_W)), const((256, 256))],
        out_specs=pl.BlockSpec((1, seq, 256), lambda bi: (bi, 0, 0)),
        scratch_shapes=[pltpu.VMEM((4, rows, GRID_W, 128), F32), pltpu.VMEM((4, rows, GRID_W, 128), F32),
                        pltpu.VMEM((2, seq, 128), F32)],
        compiler_params=_cparams(("parallel",), VMEM_LIMIT),
        name="fnet_grid",
    )(bv, wc, ws, w1, w2, wf)


def _stack_heads(qg):
    lane = lax.broadcasted_iota(jnp.int32, qg.shape, 1)
    zero = jnp.zeros_like(qg)
    return jnp.concatenate([jnp.where(lane < HEAD_DIM, qg, zero),
                            jnp.where(lane >= HEAD_DIM, qg, zero)], axis=0)


def _unstack_heads(o, rows):
    lane = lax.broadcasted_iota(jnp.int32, (rows, 2 * HEAD_DIM), 1)
    return jnp.where(lane < HEAD_DIM, o[0:rows], o[rows:2 * rows])


def _swa_kernel(sink_ref, band_ref, q_ref, kp_ref, kc_ref, kn_ref, kx_ref, vp_ref, vc_ref, vn_ref, vx_ref,
                o_ref, *, nbatch):
    n = pl.program_id(0)
    nb = pl.num_programs(0)
    blk = SWA_BLOCK
    rows = 2 * blk
    col = lax.broadcasted_iota(jnp.int32, (1, 3 * blk), 1)
    col_lo = jnp.where(n > 0, 0, blk)
    col_hi = jnp.where(n < nb - 1, 3 * blk, 2 * blk)
    edge = jnp.where((col >= col_lo) & (col < col_hi), 0.0, -jnp.inf)
    bias = band_ref[...] + edge
    head_row = lax.broadcasted_iota(jnp.int32, (rows, 1), 0) < blk
    for bi in range(nbatch):
        for g in range(2):
            ls = slice(128 * g, 128 * (g + 1))
            q_st = _stack_heads(q_ref[bi, :, ls])
            kw = jnp.concatenate([kp_ref[bi, :, ls], kc_ref[bi, :, ls], kn_ref[bi, :, ls]], axis=0)
            vw = jnp.concatenate([vp_ref[bi, :, ls], vc_ref[bi, :, ls], vn_ref[bi, :, ls]], axis=0)
            s_loc = _dot_nt(q_st, kw) + bias
            s_ctx = _dot_nt(q_st, kx_ref[bi, :, ls])
            sink = jnp.where(head_row, sink_ref[2 * g], sink_ref[2 * g + 1])
            m = jnp.maximum(jnp.maximum(jnp.max(s_loc, axis=-1, keepdims=True),
                                        jnp.max(s_ctx, axis=-1, keepdims=True)), sink)
            p_loc = jnp.exp(s_loc - m)
            p_ctx = jnp.exp(s_ctx - m)
            den = (jnp.sum(p_loc, axis=-1, keepdims=True) + jnp.sum(p_ctx, axis=-1, keepdims=True)
                   + jnp.exp(sink - m))
            o = _dot(p_loc.astype(BF16), vw) + _dot(p_ctx.astype(BF16), vx_ref[bi, :, ls])
            o_ref[bi, :, ls] = _unstack_heads(o / den, blk).astype(BF16)


def _swa_band():
    row = np.arange(2 * SWA_BLOCK)[:, None] % SWA_BLOCK
    col = np.arange(3 * SWA_BLOCK)[None, :]
    ok = np.abs(col - SWA_BLOCK - row) <= SWA_WINDOW
    return jnp.where(jnp.asarray(ok), 0.0, -jnp.inf).astype(F32)


def _swa(sink, cq, ck, cv, seq):
    b = cq.shape[0]
    nb = seq // SWA_BLOCK
    off = CTX_LEN // SWA_BLOCK

    def at(f):
        return pl.BlockSpec((b, SWA_BLOCK, 256), lambda n: (0, f(n) + off, 0))

    prev = at(lambda n: jnp.maximum(n - 1, 0))
    cur = at(lambda n: n)
    nxt = at(lambda n: jnp.minimum(n + 1, nb - 1))
    cx = pl.BlockSpec((b, CTX_LEN, 256), lambda n: (0, 0, 0))
    band = pl.BlockSpec((2 * SWA_BLOCK, 3 * SWA_BLOCK), lambda n: (0, 0))
    return pl.pallas_call(
        functools.partial(_swa_kernel, nbatch=b),
        out_shape=jax.ShapeDtypeStruct((b, seq, 256), BF16),
        grid=(nb,),
        in_specs=[pl.BlockSpec(memory_space=pltpu.SMEM), band, cur, prev, cur, nxt, cx, prev, cur, nxt, cx],
        out_specs=pl.BlockSpec((b, SWA_BLOCK, 256), lambda n: (0, n, 0)),
        compiler_params=_cparams(("arbitrary",)),
        name="swa",
    )(sink, _swa_band(), cq, ck, ck, ck, ck, cv, cv, cv, cv)


NA_ROWS_PER_STEP = 4


def _na_kernel(q_ref, k_ref, v_ref, bias_ref, o_ref, *, rows_total):
    kh = NA_KH
    nk = kh * GRID_W
    for rr in range(NA_ROWS_PER_STEP):
        r = pl.program_id(1) * NA_ROWS_PER_STEP + rr
        row_start = jnp.clip(r - kh // 2, 0, rows_total - kh)
        cls = r - row_start
        k0 = pl.multiple_of(CTX_LEN + row_start * GRID_W, GRID_W)
        qs = slice(GRID_W * rr, GRID_W * (rr + 1))
        for g in range(2):
            ls = slice(128 * g, 128 * (g + 1))
            q_st = _stack_heads(q_ref[0, qs, ls])
            s_nb = _dot_nt(q_st, k_ref[0, pl.ds(k0, nk), ls]) + bias_ref[cls, g]
            s_ctx = _dot_nt(q_st, k_ref[0, 0:CTX_LEN, ls])
            m = jnp.maximum(jnp.max(s_nb, axis=-1, keepdims=True), jnp.max(s_ctx, axis=-1, keepdims=True))
            p_nb = jnp.exp(s_nb - m)
            p_ctx = jnp.exp(s_ctx - m)
            den = jnp.sum(p_nb, axis=-1, keepdims=True) + jnp.sum(p_ctx, axis=-1, keepdims=True)
            o = (_dot(p_nb.astype(BF16), v_ref[0, pl.ds(k0, nk), ls])
                 + _dot(p_ctx.astype(BF16), v_ref[0, 0:CTX_LEN, ls]))
            o_ref[0, qs, ls] = _unstack_heads(o / den, GRID_W).astype(BF16)


def _bias_expand_kernel(rb_ref, oh_ref, o_ref):
    x = rb_ref[...]
    oh = oh_ref[...]
    hi = x.astype(BF16)
    r1 = x - hi.astype(F32)
    mid = r1.astype(BF16)
    lo = (r1 - mid.astype(F32)).astype(BF16)
    o_ref[...] = _dot(hi, oh) + _dot(mid, oh) + _dot(lo, oh)


def _na_bias_table(rel_bias):
    ndy, ndx = 2 * NA_KH - 1, 2 * NA_KW - 1
    cq = np.arange(GRID_W)
    col_start = np.clip(cq - NA_KW // 2, 0, GRID_W - NA_KW)
    col_ok = (cq[None, :] >= col_start[:, None]) & (cq[None, :] < col_start[:, None] + NA_KW)
    qi = jnp.arange(GRID_W, dtype=jnp.int32)
    dx = jnp.clip(qi[None, :] - qi[:, None], -(NA_KW - 1), NA_KW - 1) + (NA_KW - 1)
    onehot = (jnp.arange(128, dtype=jnp.int32)[:, None] == dx.reshape(1, GRID_W * GRID_W)).astype(BF16)
    rb = jnp.zeros((64, 128), F32).at[0:NA_HEADS * ndy, 0:ndx].set(rel_bias.reshape(NA_HEADS * ndy, ndx))
    p = pl.pallas_call(
        _bias_expand_kernel,
        out_shape=jax.ShapeDtypeStruct((64, GRID_W * GRID_W), F32),
        name="na_bias_expand",
    )(rb, onehot)
    p = p[0:NA_HEADS * ndy].reshape(NA_HEADS, ndy, GRID_W, GRID_W)
    p = jnp.where(col_ok[None, None], p, -jnp.inf)
    tabs = jnp.stack([p[:, NA_KH - 1 - c:2 * NA_KH - 1 - c] for c in range(NA_KH)])
    tabs = tabs.transpose(0, 1, 3, 2, 4)
    return tabs.reshape(NA_KH, NA_HEADS // 2, 2 * GRID_W, NA_KH * GRID_W)


def _na(dq, dk, dv, bias_tab, seq):
    b, t_all, _ = dq.shape
    rows_total = seq // GRID_W
    qrows = NA_ROWS_PER_STEP * GRID_W
    assert CTX_LEN % qrows == 0 and rows_total % NA_ROWS_PER_STEP == 0
    off = CTX_LEN // qrows
    full = pl.BlockSpec((1, t_all, 256), lambda bi, r: (bi, 0, 0))
    return pl.pallas_call(
        functools.partial(_na_kernel, rows_total=rows_total),
        out_shape=jax.ShapeDtypeStruct((b, seq, 256), BF16),
        grid=(b, rows_total // NA_ROWS_PER_STEP),
        in_specs=[pl.BlockSpec((1, qrows, 256), lambda bi, r: (bi, r + off, 0)), full, full,
                  pl.BlockSpec(bias_tab.shape, lambda bi, r: (0, 0, 0, 0))],
        out_specs=pl.BlockSpec((1, qrows, 256), lambda bi, r: (bi, r, 0)),
        compiler_params=_cparams(("parallel", "arbitrary"), VMEM_LIMIT),
        name="na",
    )(dq, dk, dv, bias_tab)


def _ctx_attn_kernel(sink_ref, q_ref, k_ref, v_ref, o_ref, *, has_sink):
    n = CTX_LEN
    head_row = lax.broadcasted_iota(jnp.int32, (2 * n, 1), 0) < n
    for g in range(2):
        ls = slice(128 * g, 128 * (g + 1))
        q_st = _stack_heads(q_ref[0, :, ls])
        s = _dot_nt(q_st, k_ref[0, :, ls])
        m = jnp.max(s, axis=-1, keepdims=True)
        if has_sink:
            sink = jnp.where(head_row, sink_ref[2 * g], sink_ref[2 * g + 1])
            m = jnp.maximum(m, sink)
        p = jnp.exp(s - m)
        den = jnp.sum(p, axis=-1, keepdims=True)
        if has_sink:
            den = den + jnp.exp(sink - m)
        o = _dot(p.astype(BF16), v_ref[0, :, ls])
        o_ref[0, :, ls] = _unstack_heads(o / den, n).astype(BF16)


def _ctx_attn(sink, q, k, v, has_sink):
    b = q.shape[0]
    blk = pl.BlockSpec((1, CTX_LEN, 256), lambda bi: (bi, 0, 0))
    return pl.pallas_call(
        functools.partial(_ctx_attn_kernel, has_sink=has_sink),
        out_shape=jax.ShapeDtypeStruct((b, CTX_LEN, 256), BF16),
        grid=(b,),
        in_specs=[pl.BlockSpec(memory_space=pltpu.SMEM), blk, blk, blk],
        out_specs=blk,
        compiler_params=_cparams(("parallel",)),
        name="ctx_attn_sink" if has_sink else "ctx_attn",
    )(sink, q, k, v)


def _outproj_kernel(x_ref, of_ref, ob_ref, fz_ref, co_ref, do_ref, g_ref, mod_ref, gn_ref, seg_ref,
                    w_ref, o_ref):
    a = of_ref[0].astype(F32) + ob_ref[0].astype(F32)
    ms = _dot((a * a).astype(BF16), seg_ref[...]) * (1.0 / GLA_DV)
    a = a * lax.rsqrt(ms + EPS) * gn_ref[...]
    g = g_ref[0].astype(F32)
    mixed = jnp.concatenate([a * g[:, 0:256], fz_ref[0].astype(F32) * g[:, 256:512],
                             co_ref[0].astype(F32) * g[:, 512:768], do_ref[0].astype(F32) * g[:, 768:1024]],
                            axis=1)
    y = _dot(mixed.astype(BF16), w_ref[...])
    o_ref[0] = x_ref[0] + mod_ref[0, 2:3, :] * y


def _outproj(x, o_f, o_b, fz, c_o, d_o, g, mod, gla_norm, seg_ones, w_out, is_ctx):
    b, rows, d = x.shape
    nblk = rows // TM
    comb = 0 if is_ctx else CTX_LEN // TM
    nbatch = b

    def own(width):
        return pl.BlockSpec((1, TM, width), lambda bi, i: (bi, i, 0))

    def combined(width):
        return pl.BlockSpec((1, TM, width), lambda bi, i: (bi, i + comb, 0))

    def const(shape):
        return pl.BlockSpec(shape, lambda bi, i: (0,) * len(shape))

    mod_spec = pl.BlockSpec((1, 3, d), (lambda bi, i: (nbatch, 0, 0)) if is_ctx else (lambda bi, i: (bi, 0, 0)))
    return pl.pallas_call(
        _outproj_kernel,
        out_shape=jax.ShapeDtypeStruct((b, rows, d), F32),
        grid=(b, nblk),
        in_specs=[own(d), combined(256), combined(256), own(256), own(256), own(256), combined(1024),
                  mod_spec, const((1, 256)), const((256, 256)), const((d, d))],
        out_specs=own(d),
        compiler_params=_cparams(("parallel", "arbitrary"), VMEM_LIMIT),
        name="outproj_ctx" if is_ctx else "outproj",
    )(x, o_f, o_b, fz, c_o, d_o, g, mod, gla_norm, seg_ones, w_out)


def _permute_w_in(w):
    s = lambda lo, hi: w[:, lo:hi]
    ck, cv = s(1568, 1696), s(1696, 1824)
    dup = lambda t: jnp.concatenate([t[:, :64], t[:, :64], t[:, 64:], t[:, 64:]], axis=1)
    pad = jnp.zeros((w.shape[0], 128 - 2 * GLA_RANK), w.dtype)
    return jnp.concatenate([
        s(0, 512),
        s(800, 1056),
        s(1312, 1568), dup(ck), dup(cv),
        s(2080, 2336), s(2336, 2592), s(2592, 2848),
        s(512, 768), s(1056, 1312), s(1824, 2080), s(2848, 3104),
        s(768, 800), pad,
    ], axis=1)


def _rope_tables(seq):
    t = np.arange(seq)
    axis_dim = HEAD_DIM // 2
    inv = ROPE_BASE ** (-np.arange(0, axis_dim, 2, dtype=np.float64) / axis_dim)
    row = (t // GRID_W).astype(np.float64)
    col = (t % GRID_W).astype(np.float64)
    ang = np.concatenate([row[:, None] * inv, row[:, None] * inv, col[:, None] * inv, col[:, None] * inv], axis=1)
    cos = np.cos(ang)
    sign = np.concatenate([-np.ones(16), np.ones(16), -np.ones(16), np.ones(16)])
    sin = np.sin(ang) * sign
    cos = np.concatenate([np.ones((CTX_LEN, HEAD_DIM)), cos], axis=0)
    sin = np.concatenate([np.zeros((CTX_LEN, HEAD_DIM)), sin], axis=0)
    return (jnp.asarray(np.tile(cos, (1, 4)), F32), jnp.asarray(np.tile(sin, (1, 4)), F32))


def _seg_ones(width, seg):
    i = np.arange(width)
    return jnp.asarray((i[:, None] // seg) == (i[None, :] // seg), BF16)


def _gla_ones2():
    r = np.arange(GLA_HEADS * GLA_DK) // GLA_DK
    c = np.arange(GLA_HEADS * GLA_DV) // GLA_DV
    return jnp.asarray(r[:, None] == c[None, :], BF16)


def _layer(x, ctx, cc, need_ctx, tables, norm_w, ada_w, ada_b, w_in, w_out, gla_dec_w, gla_dec_b,
           gla_out_norm, fnet_w, swa_q_norm, swa_k_norm, swa_sink, na_q_norm, na_k_norm, na_rel_bias):
    b, seq, d = x.shape
    cos_t, sin_t, seg64, ones2 = tables
    mod = _ada(cc, ada_w.astype(BF16), ada_b[None, :])[:b + 1].reshape(b + 1, 3, d)

    qs = HEAD_DIM ** -0.5
    head_norms = jnp.stack([jnp.tile(swa_q_norm, 4) * qs, jnp.tile(swa_k_norm, 4),
                            jnp.tile(na_q_norm, 4) * qs, jnp.tile(na_k_norm, 4)])
    head_norms = jnp.concatenate([head_norms, jnp.zeros((4, 256), F32)], axis=0)
    wdec = jnp.zeros((128, 256), F32)
    wdec = wdec.at[0:GLA_RANK, 0:128].set(gla_dec_w[0]).at[GLA_RANK:2 * GLA_RANK, 128:256].set(gla_dec_w[1])
    bdec = gla_dec_b.reshape(1, 256)

    a, la, bv, cq, ck, cv, dq, dk, dv, g = _inproj(
        x, ctx, mod, norm_w[None, :], _permute_w_in(w_in).astype(BF16), cos_t, sin_t, head_norms, seg64,
        wdec.astype(BF16), bdec)

    o_f, o_b = _gla(a, la, ones2)
    wf = fnet_w.astype(BF16)
    fz = _fnet_grid(bv, wf, seq)
    c_o = _swa(swa_sink, cq, ck, cv, seq)
    d_o = _na(dq, dk, dv, _na_bias_table(na_rel_bias), seq)

    gn = jnp.tile(gla_out_norm, 4)[None, :]
    wo = w_out.astype(BF16)
    x_new = _outproj(x, o_f, o_b, fz, c_o, d_o, g, mod, gn, seg64, wo, is_ctx=False)
    ctx_new = None
    if need_ctx:
        fzc = _fnet(bv, wf, 0, CTX_LEN)
        c_oc = _ctx_attn(swa_sink, cq, ck, cv, has_sink=True)
        d_oc = _ctx_attn(swa_sink, dq, dk, dv, has_sink=False)
        ctx_new = _outproj(ctx, o_f, o_b, fzc, c_oc, d_oc, g, mod, gn, seg64, wo, is_ctx=True)
    return x_new, ctx_new


def kernel(x, c, ctx, c_ctx, norm_w, ada_w, ada_b, w_in, w_out, gla_dec_w, gla_dec_b, gla_out_norm,
           fnet_w, swa_q_norm, swa_k_norm, swa_sink, na_q_norm, na_k_norm, na_rel_bias):
    b, seq, d = x.shape
    depth = norm_w.shape[0]
    assert ctx.shape[1] == CTX_LEN and seq % TM == 0 and seq % (NA_KH * GRID_W) == 0
    tables = _rope_tables(seq) + (_seg_ones(256, HEAD_DIM), _gla_ones2())
    cc = jnp.concatenate([c, c_ctx[None, :], jnp.zeros((8 - (b + 1) % 8, d), F32)], axis=0)
    for i in range(depth):
        x, ctx = _layer(x, ctx, cc, i < depth - 1, tables, norm_w[i], ada_w[i], ada_b[i], w_in[i], w_out[i],
                        gla_dec_w[i], gla_dec_b[i], gla_out_norm[i], fnet_w[i], swa_q_norm[i],
                        swa_k_norm[i], swa_sink[i], na_q_norm[i], na_k_norm[i], na_rel_bias[i])
    return x
```

```python
import functools
import math

import numpy as np
import jax
import jax.numpy as jnp
from jax import lax
from jax.experimental import pallas as pl
from jax.experimental.pallas import tpu as pltpu

F32 = jnp.float32
BF16 = jnp.bfloat16

GRID_W = 64
CTX_LEN = 256
HEAD_DIM = 64
BRANCH_W = 256
EPS = 1e-6
ROPE_BASE = 10000.0
GLA_HEADS = 4
GLA_DK = 32
GLA_DV = 64
GLA_RANK = 16
GLA_TAU = 16.0
GLA_CHUNK = 64
GLA_SUB = 8
FNET_GW = 64
SWA_BLOCK = 128
SWA_WINDOW = 128
NA_KH = 8
NA_KW = 16
NA_HEADS = 4

TM = 512
CTX_PAD = TM
VMEM_LIMIT = 48 * 1024 * 1024

C_A = 0
C_BV = 512
C_CQ = 768
C_CK = 1024
C_CV = 1280
C_DQ = 1536
C_DK = 1792
C_DV = 2048
C_G = 2304
C_LR = 3328
W_COLS = 3456


def _dot(a, b):
    return jnp.dot(a, b, preferred_element_type=F32)


def _dot_nt(a, b):
    return lax.dot_general(a, b, (((1,), (1,)), ((), ())), preferred_element_type=F32)


def _dot_tn(a, b):
    return lax.dot_general(a, b, (((0,), (0,)), ((), ())), preferred_element_type=F32)


def _cparams(sem, vmem=None):
    return pltpu.CompilerParams(dimension_semantics=sem, vmem_limit_bytes=vmem)


def _ada_kernel(c_ref, w_ref, b_ref, o_ref):
    c = c_ref[...]
    s = c * (1.0 / (1.0 + jnp.exp(-c)))
    o_ref[...] = _dot(s.astype(BF16), w_ref[...]) + b_ref[...]


def _ada(cc, ada_w, ada_b):
    r, d = cc.shape
    n = ada_w.shape[1]
    return pl.pallas_call(
        _ada_kernel,
        out_shape=jax.ShapeDtypeStruct((r, n), F32),
        grid=(1,),
        in_specs=[pl.BlockSpec((r, d), lambda i: (0, 0)),
                  pl.BlockSpec((d, n), lambda i: (0, 0)),
                  pl.BlockSpec((1, n), lambda i: (0, 0))],
        out_specs=pl.BlockSpec((r, n), lambda i: (0, 0)),
        compiler_params=_cparams(("arbitrary",), VMEM_LIMIT),
        name="ada",
    )(cc, ada_w, ada_b)


def _head_rms(x, seg_ones, w):
    ms = _dot((x * x).astype(BF16), seg_ones) * (1.0 / HEAD_DIM)
    return x * lax.rsqrt(ms + EPS) * w


def _rope(y, cos, sin_signed):
    lane = lax.broadcasted_iota(jnp.int32, y.shape, 1)
    first_half = (lane % 32) < 16
    n = y.shape[1]
    swapped = jnp.where(first_half, pltpu.roll(y, n - 16, 1), pltpu.roll(y, 16, 1))
    return y * cos + swapped * sin_signed


def _inproj_kernel(x_ref, ctx_ref, mod_ref, nw_ref, w_ref, cos_ref, sin_ref, hn_ref, seg_ref,
                   wdec_ref, bdec_ref,
                   a_ref, la_ref, bv_ref, cq_ref, ck_ref, cv_ref, dq_ref, dk_ref, dv_ref, g_ref,
                   xs_ref):
    i = pl.program_id(1)

    @pl.when(i == 0)
    def _():
        xs_ref[0:CTX_LEN, :] = ctx_ref[0]
        xs_ref[CTX_LEN:, :] = jnp.zeros((TM - CTX_LEN, xs_ref.shape[1]), F32)

    @pl.when(i > 0)
    def _():
        xs_ref[...] = x_ref[0]

    x = xs_ref[...]
    ms = jnp.mean(x * x, axis=-1, keepdims=True)
    y = x * lax.rsqrt(ms + EPS) * nw_ref[...]
    h = y * (1.0 + mod_ref[0, 1:2, :]) + mod_ref[0, 0:1, :]
    hb = h.astype(BF16)

    def proj(c0, width):
        return _dot(hb, w_ref[:, c0:c0 + width])

    seg = seg_ref[...]
    cos = cos_ref[...]
    sin = sin_ref[...]

    a = proj(C_A, 512)
    a_ref[0, :, 0:128] = a[:, 0:128] * (GLA_DK ** -0.5)
    a_ref[0, :, 128:512] = a[:, 128:512]
    bv_ref[0] = proj(C_BV, 256).astype(BF16)
    cq_ref[0] = _rope(_head_rms(proj(C_CQ, 256), seg, hn_ref[0:1, :]), cos, sin).astype(BF16)
    ck_ref[0] = _rope(_head_rms(proj(C_CK, 256), seg, hn_ref[1:2, :]), cos, sin).astype(BF16)
    cv_ref[0] = proj(C_CV, 256).astype(BF16)
    dq_ref[0] = _head_rms(proj(C_DQ, 256), seg, hn_ref[2:3, :]).astype(BF16)
    dk_ref[0] = _head_rms(proj(C_DK, 256), seg, hn_ref[3:4, :]).astype(BF16)
    dv_ref[0] = proj(C_DV, 256).astype(BF16)
    g = proj(C_G, 1024)
    g_ref[0] = (g * (1.0 / (1.0 + jnp.exp(-g)))).astype(BF16)
    lr = proj(C_LR, 128)
    z = _dot(lr.astype(BF16), wdec_ref[...]) + bdec_ref[...]
    la_ref[0] = (jnp.minimum(z, 0.0) - jnp.log(1.0 + jnp.exp(-jnp.abs(z)))) * (1.0 / GLA_TAU)


def _inproj(x, ctx, mod, norm_w, w_perm, cos_t, sin_t, head_norms, seg_ones, wdec, bdec):
    b, seq, d = x.shape
    t_all = CTX_PAD + seq
    nblk = t_all // TM

    def tok(width):
        return pl.BlockSpec((1, TM, width), lambda bi, i: (bi, i, 0))

    def const(shape):
        return pl.BlockSpec(shape, lambda bi, i: (0,) * len(shape))

    def out(width, dtype):
        return jax.ShapeDtypeStruct((b, t_all, width), dtype)

    return pl.pallas_call(
        _inproj_kernel,
        out_shape=(out(512, F32), out(256, F32), out(256, BF16), out(256, BF16), out(256, BF16),
                   out(256, BF16), out(256, BF16), out(256, BF16), out(256, BF16), out(1024, BF16)),
        grid=(b, nblk),
        in_specs=[
            pl.BlockSpec((1, TM, d), lambda bi, i: (bi, jnp.maximum(i - 1, 0), 0)),
            pl.BlockSpec((1, CTX_LEN, d), lambda bi, i: (bi, 0, 0)),
            pl.BlockSpec((1, 3, d), lambda bi, i: (jnp.where(i == 0, b, bi), 0, 0)),
            const((1, d)),
            const((d, W_COLS)),
            pl.BlockSpec((TM, 256), lambda bi, i: (i, 0)),
            pl.BlockSpec((TM, 256), lambda bi, i: (i, 0)),
            const((8, 256)),
            const((256, 256)),
            const((128, 256)),
            const((1, 256)),
        ],
        out_specs=(tok(512), tok(256), tok(256), tok(256), tok(256), tok(256), tok(256), tok(256),
                   tok(256), tok(1024)),
        scratch_shapes=[pltpu.VMEM((TM, d), F32)],
        compiler_params=_cparams(("parallel", "arbitrary"), VMEM_LIMIT),
        name="inproj",
    )(x, ctx, mod, norm_w, w_perm, cos_t, sin_t, head_norms, seg_ones, wdec, bdec)


def _gla_kernel(af_ref, ab_ref, laf_ref, lab_ref, ones2_ref, trif_ref, trib_ref, pmf_ref, pmb_ref,
                qmask_ref, vmask_ref, stmask_ref, of_ref, ob_ref, st_ref, rb_ref, *, nbatch):
    @pl.when(pl.program_id(0) == 0)
    def _():
        st_ref[...] = jnp.zeros_like(st_ref)

    shared = (qmask_ref[...], vmask_ref[...], stmask_ref[...])
    consts_f = (ones2_ref[...], trif_ref[...], pmf_ref[...]) + shared
    consts_b = (ones2_ref[...], trib_ref[...], pmb_ref[...]) + shared
    for bi in range(nbatch):
        _gla_chunk(af_ref, laf_ref, of_ref, st_ref.at[0], rb_ref.at[0], bi, False, consts_f)
        _gla_chunk(ab_ref, lab_ref, ob_ref, st_ref.at[1], rb_ref.at[1], bi, True, consts_b)


def _gla_chunk(a_ref, la_ref, o_ref, st_ref, rb_ref, bi, rev, consts):
    ones2, tri, pmask, qmask, vmask, stmask = consts
    c, sb = GLA_CHUNK, GLA_SUB
    nsub = c // sb
    hk = GLA_HEADS * GLA_DK
    hv = GLA_HEADS * GLA_DV

    q = a_ref[bi, :, 0:hk]
    k = a_ref[bi, :, hk:2 * hk]
    v = a_ref[bi, :, 2 * hk:2 * hk + hv]
    la = la_ref[bi]

    la_hi = la.astype(BF16)
    la_lo = (la - la_hi.astype(F32)).astype(BF16)
    cum = _dot(tri, la_hi) + _dot(tri, la_lo)
    cumx = cum - la

    rb_ref[bi, 0] = cum
    rb_ref[bi, 1] = cumx
    rb_ref[bi, 2] = k
    rb_ref[bi, 3] = v[:, 0:hk]
    rb_ref[bi, 4] = v[:, hk:hv]

    def brow(plane, r, n=sb):
        return jnp.broadcast_to(rb_ref[bi, plane, r:r + 1, :], (n, hk))

    end = 0 if rev else c - 1

    def first_row(blk):
        return sb * blk + (sb - 1 if rev else 0)

    def blk_of(p):
        return nsub - 1 - p if rev else p

    cref_rows = jnp.concatenate([brow(1, first_row(blk)) for blk in range(nsub)], axis=0)

    st = st_ref[bi]
    qi = q * jnp.exp(cum)
    inter = _dot_nt(qi.astype(BF16), st.astype(BF16))

    qd = q * jnp.exp(cum - cref_rows)
    q_st = jnp.concatenate([qd] * GLA_HEADS, axis=0) * qmask
    kt, vt = [], []
    for p in range(1, nsub):
        lo, hi = (c - sb * p, c) if rev else (0, sb * p)
        kt.append(k[lo:hi] * jnp.exp(brow(1, first_row(blk_of(p)), hi - lo) - cum[lo:hi]))
        vt.append(v[lo:hi])
    npad = pmask.shape[1] - sb * (nsub * (nsub - 1) // 2)
    kt.append(jnp.zeros((npad, hk), F32))
    vt.append(jnp.zeros((npad, hv), F32))
    kt = jnp.concatenate(kt, axis=0).astype(BF16)
    vt = jnp.concatenate(vt, axis=0).astype(BF16)
    sc = _dot_nt(q_st.astype(BF16), kt) * pmask
    off_st = _dot(sc.astype(BF16), vt) * vmask
    off = off_st[0:c]
    for h in range(1, GLA_HEADS):
        off = off + off_st[h * c:(h + 1) * c]

    t_loc = lax.broadcasted_iota(jnp.int32, (sb, hk), 0)
    valid = [(t_loc <= s) if rev else (t_loc >= s) for s in range(sb)]
    es = []
    for blk in range(nsub):
        r0 = sb * blk
        cb = cum[r0:r0 + sb]
        qb = q[r0:r0 + sb]
        for s in range(sb):
            diff = jnp.where(valid[s], cb - brow(0, r0 + s), -jnp.inf)
            es.append(jnp.exp(diff) * (qb * brow(2, r0 + s)))
    e = jnp.concatenate(es, axis=0).astype(BF16)
    r = _dot(e, ones2)
    dgs = []
    for blk in range(nsub):
        acc = None
        for s in range(sb):
            row0 = (blk * sb + s) * sb
            vrow = jnp.concatenate([brow(3, blk * sb + s), brow(4, blk * sb + s)], axis=1)
            term = r[row0:row0 + sb] * vrow
            acc = term if acc is None else acc + term
        dgs.append(acc)
    o_ref[bi] = (inter + off + jnp.concatenate(dgs, axis=0)).astype(BF16)

    cum_end = rb_ref[bi, 0, end:end + 1, :]
    kd = k * jnp.exp(cum_end - cum)
    kv = _dot_tn(v.astype(BF16), kd.astype(BF16))
    st_ref[bi] = st * jnp.exp(cum_end) + kv * stmask


def _gla_consts(rev):
    c, sb = GLA_CHUNK, GLA_SUB
    nsub = c // sb
    t = np.arange(c)
    tri = (t[None, :] >= t[:, None]) if rev else (t[None, :] <= t[:, None])
    blk = t // sb
    p_of_t = (nsub - 1 - blk) if rev else blk
    col_p = np.concatenate([np.full(sb * p, p) for p in range(1, nsub)])
    col_p = np.concatenate([col_p, np.full(-col_p.size % 128, -1)])
    pmask = np.tile(p_of_t, GLA_HEADS)[:, None] == col_p[None, :]
    row_h = np.repeat(np.arange(GLA_HEADS), c)
    qmask = row_h[:, None] == (np.arange(GLA_HEADS * GLA_DK) // GLA_DK)[None, :]
    vmask = row_h[:, None] == (np.arange(GLA_HEADS * GLA_DV) // GLA_DV)[None, :]
    stmask = (np.arange(GLA_HEADS * GLA_DV) // GLA_DV)[:, None] == (np.arange(GLA_HEADS * GLA_DK) // GLA_DK)[None, :]
    f = lambda m: jnp.asarray(m, F32)
    return f(tri).astype(BF16), f(pmask), f(qmask), f(vmask), f(stmask)


def _gla(a, la, ones2):
    b, t_all, _ = a.shape
    nch = t_all // GLA_CHUNK
    nctx = CTX_LEN // GLA_CHUNK
    nskip = (CTX_PAD - CTX_LEN) // GLA_CHUNK
    tri_f, pm_f, qmask, vmask, stmask = _gla_consts(False)
    tri_b, pm_b, _, _, _ = _gla_consts(True)
    consts = (ones2, tri_f, tri_b, pm_f, pm_b, qmask, vmask, stmask)

    def fwd(j):
        return jnp.where(j < nctx, j, j + nskip)

    def bwd(j):
        return jnp.where(j < nctx, nctx - 1 - j, nch - 1 - (j - nctx))

    out = jax.ShapeDtypeStruct((b, t_all, 256), BF16)
    return pl.pallas_call(
        functools.partial(_gla_kernel, nbatch=b),
        out_shape=(out, out),
        grid=(nch - nskip,),
        in_specs=[pl.BlockSpec((b, GLA_CHUNK, 512), lambda j: (0, fwd(j), 0)),
                  pl.BlockSpec((b, GLA_CHUNK, 512), lambda j: (0, bwd(j), 0)),
                  pl.BlockSpec((b, GLA_CHUNK, 128), lambda j: (0, fwd(j), 0)),
                  pl.BlockSpec((b, GLA_CHUNK, 128), lambda j: (0, bwd(j), 1))]
                 + [pl.BlockSpec(t.shape, lambda j: (0, 0)) for t in consts],
        out_specs=(pl.BlockSpec((b, GLA_CHUNK, 256), lambda j: (0, fwd(j), 0)),
                   pl.BlockSpec((b, GLA_CHUNK, 256), lambda j: (0, bwd(j), 0))),
        scratch_shapes=[pltpu.VMEM((2, b, 256, 128), F32), pltpu.VMEM((2, b, 5, GLA_CHUNK, 128), F32)],
        compiler_params=_cparams(("arbitrary",)),
        name="gla",
    )(a, a, la, la, *consts)


def _fnet_chan_kernel(v_ref, wc_ref, ws_ref, o_ref):
    v = v_ref[0]
    o_ref[0, :, :] = _dot(v, wc_ref[...]).astype(BF16)
    o_ref[1, :, :] = _dot(v, ws_ref[...]).astype(BF16)


def _fnet_chan(bv, wc, ws, row_blk_off, length):
    b = bv.shape[0]
    tl = min(TM, length)
    return pl.pallas_call(
        _fnet_chan_kernel,
        out_shape=jax.ShapeDtypeStruct((2, length, b * 256), BF16),
        grid=(b, length // tl),
        in_specs=[pl.BlockSpec((1, tl, 256), lambda bi, i: (bi, i + row_blk_off, 0)),
                  pl.BlockSpec((256, 256), lambda bi, i: (0, 0)),
                  pl.BlockSpec((256, 256), lambda bi, i: (0, 0))],
        out_specs=pl.BlockSpec((2, tl, 256), lambda bi, i: (0, i, bi)),
        compiler_params=_cparams(("parallel", "arbitrary")),
        name="fnet_chan",
    )(bv, wc, ws)


def _fnet_pos_kernel(d_ref, y_ref, wf_ref, o_ref, acc_ref, *, nb):
    kk = pl.program_id(1)

    @pl.when(kk == 0)
    def _():
        acc_ref[...] = jnp.zeros_like(acc_ref)

    acc_ref[...] += _dot(d_ref[...], y_ref[...])

    @pl.when(kk == pl.num_programs(1) - 1)
    def _():
        wf = wf_ref[...]
        for bi in range(nb):
            o_ref[bi] = _dot(acc_ref[:, bi * 256:(bi + 1) * 256].astype(BF16), wf).astype(BF16)


def _fnet_pos(dmat, y2, wf, nb):
    length, k2 = dmat.shape
    tm = min(512, length)
    tk = min(1024, k2)
    return pl.pallas_call(
        functools.partial(_fnet_pos_kernel, nb=nb),
        out_shape=jax.ShapeDtypeStruct((nb, length, 256), BF16),
        grid=(length // tm, k2 // tk),
        in_specs=[pl.BlockSpec((tm, tk), lambda i, kk: (i, kk)),
                  pl.BlockSpec((tk, nb * 256), lambda i, kk: (kk, 0)),
                  pl.BlockSpec((256, 256), lambda i, kk: (0, 0))],
        out_specs=pl.BlockSpec((nb, tm, 256), lambda i, kk: (0, i, 0)),
        scratch_shapes=[pltpu.VMEM((tm, nb * 256), F32)],
        compiler_params=_cparams(("parallel", "arbitrary"), VMEM_LIMIT),
        name="fnet_pos",
    )(dmat, y2, wf)


def _dft_tables(length):
    j = jnp.arange(length, dtype=jnp.int32)
    jk = (j[:, None] * j[None, :]) % length
    ang = jk.astype(F32) * (2.0 * math.pi / length)
    s = 1.0 / math.sqrt(length)
    return jnp.concatenate([jnp.cos(ang) * s, -jnp.sin(ang) * s], axis=1).astype(BF16)


def _chan_tables():
    j = np.arange(FNET_GW)
    ang = 2.0 * np.pi * ((j[:, None] * j[None, :]) % FNET_GW) / FNET_GW
    s = 1.0 / math.sqrt(FNET_GW)
    eye = np.eye(BRANCH_W // FNET_GW)
    wc = np.kron(eye, np.cos(ang) * s)
    ws = np.kron(eye, np.sin(ang) * s)
    return jnp.asarray(wc, F32).astype(BF16), jnp.asarray(ws, F32).astype(BF16)


def _fnet(bv, wf, row_blk_off, length):
    b = bv.shape[0]
    wc, ws = _chan_tables()
    y = _fnet_chan(bv, wc, ws, row_blk_off, length)
    y2 = y.reshape(2 * length, b * 256)
    return _fnet_pos(_dft_tables(length), y2, wf, b)


FNET_UNROLL = 8


def _fnet_grid_kernel(v_ref, cg_ref, sg_ref, w1_ref, w2_ref, wf_ref, o_ref, z_ref, g_ref, f_ref, *, rows, row_off):
    length = rows * GRID_W
    v = v_ref[0, row_off:row_off + length, :]
    wf = wf_ref[...]
    zr = _dot(v, _dot(cg_ref[...], wf).astype(BF16))
    zi = _dot(v, _dot(sg_ref[...], wf).astype(BF16))
    z_ref[0] = zr[:, 0:128].reshape(rows, GRID_W, 128)
    z_ref[1] = zr[:, 128:256].reshape(rows, GRID_W, 128)
    z_ref[2] = zi[:, 0:128].reshape(rows, GRID_W, 128)
    z_ref[3] = zi[:, 128:256].reshape(rows, GRID_W, 128)

    n8 = rows * 8
    for w0 in range(0, GRID_W, 8):
        def slab(c):
            return z_ref[c, :, w0:w0 + 8, :].reshape(n8, 128)
        rhs = jnp.concatenate([jnp.concatenate([slab(0), slab(1)], axis=1),
                               jnp.concatenate([slab(2), slab(3)], axis=1)], axis=0).astype(BF16)
        g = _dot(w1_ref[...], rhs)
        g_ref[0, :, w0:w0 + 8, :] = g[0:n8, 0:128].reshape(rows, 8, 128)
        g_ref[1, :, w0:w0 + 8, :] = g[0:n8, 128:256].reshape(rows, 8, 128)
        g_ref[2, :, w0:w0 + 8, :] = g[n8:2 * n8, 0:128].reshape(rows, 8, 128)
        g_ref[3, :, w0:w0 + 8, :] = g[n8:2 * n8, 128:256].reshape(rows, 8, 128)

    def stage_w(ka, carry):
        r0 = pl.multiple_of(ka * GRID_W, GRID_W)
        rhs = jnp.concatenate([jnp.concatenate([g_ref[0, ka], g_ref[1, ka]], axis=1),
                               jnp.concatenate([g_ref[2, ka], g_ref[3, ka]], axis=1)], axis=0).astype(BF16)
        y = _dot(w2_ref[pl.ds(r0, GRID_W), :], rhs)
        f_ref[0, pl.ds(ka, GRID_W, stride=rows), :] = y[:, 0:128]
        f_ref[1, pl.ds(ka, GRID_W, stride=rows), :] = y[:, 128:256]
        return carry

    lax.fori_loop(0, rows, stage_w, 0, unroll=FNET_UNROLL)
    o_ref[0, :, 0:128] = f_ref[0].astype(BF16)
    o_ref[0, :, 128:256] = f_ref[1].astype(BF16)


def _fnet_grid_tables(rows):
    length = rows * GRID_W
    j = np.arange(rows)
    ang = 2.0 * np.pi * ((j[:, None] * j[None, :]) % rows) / rows
    c, s = np.cos(ang), np.sin(ang)
    w1 = np.kron(np.block([[c, -s], [-s, -c]]), np.eye(8))
    w1 = jnp.asarray(w1, F32).astype(BF16)
    ka = jnp.arange(rows, dtype=jnp.int32)[:, None, None]
    kb = jnp.arange(GRID_W, dtype=jnp.int32)[None, :, None]
    w = jnp.arange(GRID_W, dtype=jnp.int32)[None, None, :]
    ang2 = ((w * (ka + rows * kb)) % length).astype(F32) * (2.0 * math.pi / length)
    scale = 1.0 / math.sqrt(length)
    w2 = jnp.concatenate([jnp.cos(ang2) * scale, jnp.sin(ang2) * scale], axis=2)
    return w1, w2.reshape(length, 2 * GRID_W).astype(BF16)


def _fnet_grid(bv, wf, seq):
    b, t_all, _ = bv.shape
    rows = seq // GRID_W
    wc, ws = _chan_tables()
    w1, w2 = _fnet_grid_tables(rows)

    def const(shape):
        return pl.BlockSpec(shape, lambda bi: (0,) * len(shape))

    return pl.pallas_call(
        functools.partial(_fnet_grid_kernel, rows=rows, row_off=CTX_PAD),
        out_shape=jax.ShapeDtypeStruct((b, seq, 256), BF16),
        grid=(b,),
        in_specs=[pl.BlockSpec((1, t_all, 256), lambda bi: (bi, 0, 0)), const((256, 256)), const((256, 256)),
                  const((16 * rows, 16 * rows)), const((seq, 2 * GRID_W)), const((256, 256))],
        out_specs=pl.BlockSpec((1, seq, 256), lambda bi: (bi, 0, 0)),
        scratch_shapes=[pltpu.VMEM((4, rows, GRID_W, 128), F32), pltpu.VMEM((4, rows, GRID_W, 128), F32),
                        pltpu.VMEM((2, seq, 128), F32)],
        compiler_params=_cparams(("parallel",), VMEM_LIMIT),
        name="fnet_grid",
    )(bv, wc, ws, w1, w2, wf)


def _stack_heads(qg):
    lane = lax.broadcasted_iota(jnp.int32, qg.shape, 1)
    zero = jnp.zeros_like(qg)
    return jnp.concatenate([jnp.where(lane < HEAD_DIM, qg, zero),
                            jnp.where(lane >= HEAD_DIM, qg, zero)], axis=0)


def _unstack_heads(o, rows):
    lane = lax.broadcasted_iota(jnp.int32, (rows, 2 * HEAD_DIM), 1)
    return jnp.where(lane < HEAD_DIM, o[0:rows], o[rows:2 * rows])


def _swa_kernel(sink_ref, band_ref, q_ref, kp_ref, kc_ref, kn_ref, kx_ref, vp_ref, vc_ref, vn_ref, vx_ref,
                o_ref, *, nbatch):
    n = pl.program_id(0)
    nb = pl.num_programs(0)
    blk = SWA_BLOCK
    rows = 2 * blk
    col = lax.broadcasted_iota(jnp.int32, (1, 3 * blk), 1)
    col_lo = jnp.where(n > 0, 0, blk)
    col_hi = jnp.where(n < nb - 1, 3 * blk, 2 * blk)
    edge = jnp.where((col >= col_lo) & (col < col_hi), 0.0, -jnp.inf)
    bias = band_ref[...] + edge
    head_row = lax.broadcasted_iota(jnp.int32, (rows, 1), 0) < blk
    for bi in range(nbatch):
        for g in range(2):
            ls = slice(128 * g, 128 * (g + 1))
            q_st = _stack_heads(q_ref[bi, :, ls])
            kw = jnp.concatenate([kp_ref[bi, :, ls], kc_ref[bi, :, ls], kn_ref[bi, :, ls]], axis=0)
            vw = jnp.concatenate([vp_ref[bi, :, ls], vc_ref[bi, :, ls], vn_ref[bi, :, ls]], axis=0)
            s_loc = _dot_nt(q_st, kw) + bias
            s_ctx = _dot_nt(q_st, kx_ref[bi, :, ls])
            sink = jnp.where(head_row, sink_ref[2 * g], sink_ref[2 * g + 1])
            m = jnp.maximum(jnp.maximum(jnp.max(s_loc, axis=-1, keepdims=True),
                                        jnp.max(s_ctx, axis=-1, keepdims=True)), sink)
            p_loc = jnp.exp(s_loc - m)
            p_ctx = jnp.exp(s_ctx - m)
            den = (jnp.sum(p_loc, axis=-1, keepdims=True) + jnp.sum(p_ctx, axis=-1, keepdims=True)
                   + jnp.exp(sink - m))
            o = _dot(p_loc.astype(BF16), vw) + _dot(p_ctx.astype(BF16), vx_ref[bi, :, ls])
            o_ref[bi, :, ls] = _unstack_heads(o / den, blk).astype(BF16)


def _swa_band():
    row = np.arange(2 * SWA_BLOCK)[:, None] % SWA_BLOCK
    col = np.arange(3 * SWA_BLOCK)[None, :]
    ok = np.abs(col - SWA_BLOCK - row) <= SWA_WINDOW
    return jnp.where(jnp.asarray(ok), 0.0, -jnp.inf).astype(F32)


def _swa(sink, cq, ck, cv, seq):
    b = cq.shape[0]
    nb = seq // SWA_BLOCK
    off = CTX_PAD // SWA_BLOCK

    def at(f):
        return pl.BlockSpec((b, SWA_BLOCK, 256), lambda n: (0, f(n) + off, 0))

    prev = at(lambda n: jnp.maximum(n - 1, 0))
    cur = at(lambda n: n)
    nxt = at(lambda n: jnp.minimum(n + 1, nb - 1))
    cx = pl.BlockSpec((b, CTX_LEN, 256), lambda n: (0, 0, 0))
    band = pl.BlockSpec((2 * SWA_BLOCK, 3 * SWA_BLOCK), lambda n: (0, 0))
    return pl.pallas_call(
        functools.partial(_swa_kernel, nbatch=b),
        out_shape=jax.ShapeDtypeStruct((b, seq, 256), BF16),
        grid=(nb,),
        in_specs=[pl.BlockSpec(memory_space=pltpu.SMEM), band, cur, prev, cur, nxt, cx, prev, cur, nxt, cx],
        out_specs=pl.BlockSpec((b, SWA_BLOCK, 256), lambda n: (0, n, 0)),
        compiler_params=_cparams(("arbitrary",)),
        name="swa",
    )(sink, _swa_band(), cq, ck, ck, ck, ck, cv, cv, cv, cv)


NA_ROWS_PER_STEP = 8


def _na_kernel(q_ref, k_ref, v_ref, bias_ref, o_ref, *, rows_total):
    kh = NA_KH
    nk = kh * GRID_W
    for rr in range(NA_ROWS_PER_STEP):
        r = pl.program_id(1) * NA_ROWS_PER_STEP + rr
        row_start = jnp.clip(r - kh // 2, 0, rows_total - kh)
        cls = r - row_start
        k0 = pl.multiple_of(CTX_PAD + row_start * GRID_W, GRID_W)
        qs = slice(GRID_W * rr, GRID_W * (rr + 1))
        for g in range(2):
            ls = slice(128 * g, 128 * (g + 1))
            q_st = _stack_heads(q_ref[0, qs, ls])
            s_nb = _dot_nt(q_st, k_ref[0, pl.ds(k0, nk), ls]) + bias_ref[cls, g]
            s_ctx = _dot_nt(q_st, k_ref[0, 0:CTX_LEN, ls])
            m = jnp.maximum(jnp.max(s_nb, axis=-1, keepdims=True), jnp.max(s_ctx, axis=-1, keepdims=True))
            p_nb = jnp.exp(s_nb - m)
            p_ctx = jnp.exp(s_ctx - m)
            den = jnp.sum(p_nb, axis=-1, keepdims=True) + jnp.sum(p_ctx, axis=-1, keepdims=True)
            o = (_dot(p_nb.astype(BF16), v_ref[0, pl.ds(k0, nk), ls])
                 + _dot(p_ctx.astype(BF16), v_ref[0, 0:CTX_LEN, ls]))
            o_ref[0, qs, ls] = _unstack_heads(o / den, GRID_W).astype(BF16)


def _bias_expand_kernel(rb_ref, oh_ref, o_ref):
    x = rb_ref[...]
    oh = oh_ref[...]
    hi = x.astype(BF16)
    r1 = x - hi.astype(F32)
    mid = r1.astype(BF16)
    lo = (r1 - mid.astype(F32)).astype(BF16)
    o_ref[...] = _dot(hi, oh) + _dot(mid, oh) + _dot(lo, oh)


def _na_bias_table(rel_bias):
    ndy, ndx = 2 * NA_KH - 1, 2 * NA_KW - 1
    cq = np.arange(GRID_W)
    col_start = np.clip(cq - NA_KW // 2, 0, GRID_W - NA_KW)
    col_ok = (cq[None, :] >= col_start[:, None]) & (cq[None, :] < col_start[:, None] + NA_KW)
    qi = jnp.arange(GRID_W, dtype=jnp.int32)
    dx = jnp.clip(qi[None, :] - qi[:, None], -(NA_KW - 1), NA_KW - 1) + (NA_KW - 1)
    onehot = (jnp.arange(128, dtype=jnp.int32)[:, None] == dx.reshape(1, GRID_W * GRID_W)).astype(BF16)
    rb = jnp.zeros((64, 128), F32).at[0:NA_HEADS * ndy, 0:ndx].set(rel_bias.reshape(NA_HEADS * ndy, ndx))
    p = pl.pallas_call(
        _bias_expand_kernel,
        out_shape=jax.ShapeDtypeStruct((64, GRID_W * GRID_W), F32),
        name="na_bias_expand",
    )(rb, onehot)
    p = p[0:NA_HEADS * ndy].reshape(NA_HEADS, ndy, GRID_W, GRID_W)
    p = jnp.where(col_ok[None, None], p, -jnp.inf)
    tabs = jnp.stack([p[:, NA_KH - 1 - c:2 * NA_KH - 1 - c] for c in range(NA_KH)])
    tabs = tabs.transpose(0, 1, 3, 2, 4)
    return tabs.reshape(NA_KH, NA_HEADS // 2, 2 * GRID_W, NA_KH * GRID_W)


def _na(dq, dk, dv, bias_tab, seq):
    b, t_all, _ = dq.shape
    rows_total = seq // GRID_W
    qrows = NA_ROWS_PER_STEP * GRID_W
    assert CTX_PAD % qrows == 0 and rows_total % NA_ROWS_PER_STEP == 0
    off = CTX_PAD // qrows
    full = pl.BlockSpec((1, t_all, 256), lambda bi, r: (bi, 0, 0))
    return pl.pallas_call(
        functools.partial(_na_kernel, rows_total=rows_total),
        out_shape=jax.ShapeDtypeStruct((b, seq, 256), BF16),
        grid=(b, rows_total // NA_ROWS_PER_STEP),
        in_specs=[pl.BlockSpec((1, qrows, 256), lambda bi, r: (bi, r + off, 0)), full, full,
                  pl.BlockSpec(bias_tab.shape, lambda bi, r: (0, 0, 0, 0))],
        out_specs=pl.BlockSpec((1, qrows, 256), lambda bi, r: (bi, r, 0)),
        compiler_params=_cparams(("parallel", "arbitrary"), VMEM_LIMIT),
        name="na",
    )(dq, dk, dv, bias_tab)


def _ctx_attn_kernel(sink_ref, q_ref, k_ref, v_ref, o_ref, *, has_sink):
    n = CTX_LEN
    head_row = lax.broadcasted_iota(jnp.int32, (2 * n, 1), 0) < n
    for g in range(2):
        ls = slice(128 * g, 128 * (g + 1))
        q_st = _stack_heads(q_ref[0, :, ls])
        s = _dot_nt(q_st, k_ref[0, :, ls])
        m = jnp.max(s, axis=-1, keepdims=True)
        if has_sink:
            sink = jnp.where(head_row, sink_ref[2 * g], sink_ref[2 * g + 1])
            m = jnp.maximum(m, sink)
        p = jnp.exp(s - m)
        den = jnp.sum(p, axis=-1, keepdims=True)
        if has_sink:
            den = den + jnp.exp(sink - m)
        o = _dot(p.astype(BF16), v_ref[0, :, ls])
        o_ref[0, :, ls] = _unstack_heads(o / den, n).astype(BF16)


def _ctx_attn(sink, q, k, v, has_sink):
    b = q.shape[0]
    blk = pl.BlockSpec((1, CTX_LEN, 256), lambda bi: (bi, 0, 0))
    return pl.pallas_call(
        functools.partial(_ctx_attn_kernel, has_sink=has_sink),
        out_shape=jax.ShapeDtypeStruct((b, CTX_LEN, 256), BF16),
        grid=(b,),
        in_specs=[pl.BlockSpec(memory_space=pltpu.SMEM), blk, blk, blk],
        out_specs=blk,
        compiler_params=_cparams(("parallel",)),
        name="ctx_attn_sink" if has_sink else "ctx_attn",
    )(sink, q, k, v)


def _outproj_kernel(x_ref, of_ref, ob_ref, fz_ref, co_ref, do_ref, g_ref, mod_ref, gn_ref, seg_ref,
                    w_ref, o_ref):
    a = of_ref[0].astype(F32) + ob_ref[0].astype(F32)
    ms = _dot((a * a).astype(BF16), seg_ref[...]) * (1.0 / GLA_DV)
    a = a * lax.rsqrt(ms + EPS) * gn_ref[...]
    g = g_ref[0].astype(F32)
    mixed = jnp.concatenate([a * g[:, 0:256], fz_ref[0].astype(F32) * g[:, 256:512],
                             co_ref[0].astype(F32) * g[:, 512:768], do_ref[0].astype(F32) * g[:, 768:1024]],
                            axis=1)
    y = _dot(mixed.astype(BF16), w_ref[...])
    o_ref[0] = x_ref[0] + mod_ref[0, 2:3, :] * y


def _outproj(x, o_f, o_b, fz, c_o, d_o, g, mod, gla_norm, seg_ones, w_out, is_ctx):
    b, rows, d = x.shape
    tm = CTX_LEN if is_ctx else TM
    nblk = rows // tm
    comb = 0 if is_ctx else CTX_PAD // tm
    nbatch = b

    def own(width):
        return pl.BlockSpec((1, tm, width), lambda bi, i: (bi, i, 0))

    def combined(width):
        return pl.BlockSpec((1, tm, width), lambda bi, i: (bi, i + comb, 0))

    def const(shape):
        return pl.BlockSpec(shape, lambda bi, i: (0,) * len(shape))

    mod_spec = pl.BlockSpec((1, 3, d), (lambda bi, i: (nbatch, 0, 0)) if is_ctx else (lambda bi, i: (bi, 0, 0)))
    return pl.pallas_call(
        _outproj_kernel,
        out_shape=jax.ShapeDtypeStruct((b, rows, d), F32),
        grid=(b, nblk),
        in_specs=[own(d), combined(256), combined(256), own(256), own(256), own(256), combined(1024),
                  mod_spec, const((1, 256)), const((256, 256)), const((d, d))],
        out_specs=own(d),
        compiler_params=_cparams(("parallel", "arbitrary"), VMEM_LIMIT),
        name="outproj_ctx" if is_ctx else "outproj",
    )(x, o_f, o_b, fz, c_o, d_o, g, mod, gla_norm, seg_ones, w_out)


def _permute_w_in(w):
    s = lambda lo, hi: w[:, lo:hi]
    ck, cv = s(1568, 1696), s(1696, 1824)
    dup = lambda t: jnp.concatenate([t[:, :64], t[:, :64], t[:, 64:], t[:, 64:]], axis=1)
    pad = jnp.zeros((w.shape[0], 128 - 2 * GLA_RANK), w.dtype)
    return jnp.concatenate([
        s(0, 512),
        s(800, 1056),
        s(1312, 1568), dup(ck), dup(cv),
        s(2080, 2336), s(2336, 2592), s(2592, 2848),
        s(512, 768), s(1056, 1312), s(1824, 2080), s(2848, 3104),
        s(768, 800), pad,
    ], axis=1)


def _rope_tables(seq):
    t = np.arange(seq)
    axis_dim = HEAD_DIM // 2
    inv = ROPE_BASE ** (-np.arange(0, axis_dim, 2, dtype=np.float64) / axis_dim)
    row = (t // GRID_W).astype(np.float64)
    col = (t % GRID_W).astype(np.float64)
    ang = np.concatenate([row[:, None] * inv, row[:, None] * inv, col[:, None] * inv, col[:, None] * inv], axis=1)
    cos = np.cos(ang)
    sign = np.concatenate([-np.ones(16), np.ones(16), -np.ones(16), np.ones(16)])
    sin = np.sin(ang) * sign
    cos = np.concatenate([np.ones((CTX_PAD, HEAD_DIM)), cos], axis=0)
    sin = np.concatenate([np.zeros((CTX_PAD, HEAD_DIM)), sin], axis=0)
    return (jnp.asarray(np.tile(cos, (1, 4)), F32), jnp.asarray(np.tile(sin, (1, 4)), F32))


def _seg_ones(width, seg):
    i = np.arange(width)
    return jnp.asarray((i[:, None] // seg) == (i[None, :] // seg), BF16)


def _gla_ones2():
    r = np.arange(GLA_HEADS * GLA_DK) // GLA_DK
    c = np.arange(GLA_HEADS * GLA_DV) // GLA_DV
    return jnp.asarray(r[:, None] == c[None, :], BF16)


def _layer(x, ctx, cc, need_ctx, tables, norm_w, ada_w, ada_b, w_in, w_out, gla_dec_w, gla_dec_b,
           gla_out_norm, fnet_w, swa_q_norm, swa_k_norm, swa_sink, na_q_norm, na_k_norm, na_rel_bias):
    b, seq, d = x.shape
    cos_t, sin_t, seg64, ones2 = tables
    mod = _ada(cc, ada_w.astype(BF16), ada_b[None, :])[:b + 1].reshape(b + 1, 3, d)

    qs = HEAD_DIM ** -0.5
    head_norms = jnp.stack([jnp.tile(swa_q_norm, 4) * qs, jnp.tile(swa_k_norm, 4),
                            jnp.tile(na_q_norm, 4) * qs, jnp.tile(na_k_norm, 4)])
    head_norms = jnp.concatenate([head_norms, jnp.zeros((4, 256), F32)], axis=0)
    wdec = jnp.zeros((128, 256), F32)
    wdec = wdec.at[0:GLA_RANK, 0:128].set(gla_dec_w[0]).at[GLA_RANK:2 * GLA_RANK, 128:256].set(gla_dec_w[1])
    bdec = gla_dec_b.reshape(1, 256)

    a, la, bv, cq, ck, cv, dq, dk, dv, g = _inproj(
        x, ctx, mod, norm_w[None, :], _permute_w_in(w_in).astype(BF16), cos_t, sin_t, head_norms, seg64,
        wdec.astype(BF16), bdec)

    o_f, o_b = _gla(a, la, ones2)
    wf = fnet_w.astype(BF16)
    fz = _fnet_grid(bv, wf, seq)
    c_o = _swa(swa_sink, cq, ck, cv, seq)
    d_o = _na(dq, dk, dv, _na_bias_table(na_rel_bias), seq)

    gn = jnp.tile(gla_out_norm, 4)[None, :]
    wo = w_out.astype(BF16)
    x_new = _outproj(x, o_f, o_b, fz, c_o, d_o, g, mod, gn, seg64, wo, is_ctx=False)
    ctx_new = None
    if need_ctx:
        fzc = _fnet(bv, wf, 0, CTX_LEN)
        c_oc = _ctx_attn(swa_sink, cq, ck, cv, has_sink=True)
        d_oc = _ctx_attn(swa_sink, dq, dk, dv, has_sink=False)
        ctx_new = _outproj(ctx, o_f, o_b, fzc, c_oc, d_oc, g, mod, gn, seg64, wo, is_ctx=True)
    return x_new, ctx_new


def kernel(x, c, ctx, c_ctx, norm_w, ada_w, ada_b, w_in, w_out, gla_dec_w, gla_dec_b, gla_out_norm,
           fnet_w, swa_q_norm, swa_k_norm, swa_sink, na_q_norm, na_k_norm, na_rel_bias):
    b, seq, d = x.shape
    depth = norm_w.shape[0]
    assert ctx.shape[1] == CTX_LEN and seq % TM == 0 and seq % (NA_KH * GRID_W) == 0
    tables = _rope_tables(seq) + (_seg_ones(256, HEAD_DIM), _gla_ones2())
    cc = jnp.concatenate([c, c_ctx[None, :], jnp.zeros((8 - (b + 1) % 8, d), F32)], axis=0)
    for i in range(depth):
        x, ctx = _layer(x, ctx, cc, i < depth - 1, tables, norm_w[i], ada_w[i], ada_b[i], w_in[i], w_out[i],
                        gla_dec_w[i], gla_dec_b[i], gla_out_norm[i], fnet_w[i], swa_q_norm[i],
                        swa_k_norm[i], swa_sink[i], na_q_norm[i], na_k_norm[i], na_rel_bias[i])
    return x
```

```python
import functools
import math

import numpy as np
import jax
import jax.numpy as jnp
from jax import lax
from jax.experimental import pallas as pl
from jax.experimental.pallas import tpu as pltpu

F32 = jnp.float32
BF16 = jnp.bfloat16

GRID_W = 64
CTX_LEN = 256
HEAD_DIM = 64
BRANCH_W = 256
EPS = 1e-6
ROPE_BASE = 10000.0
GLA_HEADS = 4
GLA_DK = 32
GLA_DV = 64
GLA_RANK = 16
GLA_TAU = 16.0
GLA_CHUNK = 64
GLA_SUB = 8
FNET_GW = 64
SWA_BLOCK = 128
SWA_WINDOW = 128
NA_KH = 8
NA_KW = 16
NA_HEADS = 4

TM = 512
CTX_PAD = TM
VMEM_LIMIT = 48 * 1024 * 1024

C_A = 0
C_BV = 512
C_CQ = 768
C_CK = 1024
C_CV = 1280
C_DQ = 1536
C_DK = 1792
C_DV = 2048
C_G = 2304
C_LR = 3328
W_COLS = 3456


def _dot(a, b):
    return jnp.dot(a, b, preferred_element_type=F32)


def _dot_nt(a, b):
    return lax.dot_general(a, b, (((1,), (1,)), ((), ())), preferred_element_type=F32)


def _dot_tn(a, b):
    return lax.dot_general(a, b, (((0,), (0,)), ((), ())), preferred_element_type=F32)


INPROJ_GROUP = 2
ATTN_GROUP = 4
_DONE = object()


def _interleave(units, group=None):
    group = group or len(units)
    for i in range(0, len(units), group):
        live = units[i:i + group]
        while live:
            live = [u for u in live if next(u, _DONE) is not _DONE]


def _skew(units, depth):
    started = []
    for u in units:
        next(u)
        started.append(u)
        if len(started) > depth:
            next(started.pop(0), _DONE)
    for u in started:
        next(u, _DONE)


def _cparams(sem, vmem=None):
    return pltpu.CompilerParams(dimension_semantics=sem, vmem_limit_bytes=vmem)


def _ada_kernel(c_ref, w_ref, b_ref, o_ref):
    c = c_ref[...]
    s = c * (1.0 / (1.0 + jnp.exp(-c)))
    o_ref[...] = _dot(s.astype(BF16), w_ref[...]) + b_ref[...]


def _ada(cc, ada_w, ada_b):
    r, d = cc.shape
    n = ada_w.shape[1]
    return pl.pallas_call(
        _ada_kernel,
        out_shape=jax.ShapeDtypeStruct((r, n), F32),
        grid=(1,),
        in_specs=[pl.BlockSpec((r, d), lambda i: (0, 0)),
                  pl.BlockSpec((d, n), lambda i: (0, 0)),
                  pl.BlockSpec((1, n), lambda i: (0, 0))],
        out_specs=pl.BlockSpec((r, n), lambda i: (0, 0)),
        compiler_params=_cparams(("arbitrary",), VMEM_LIMIT),
        name="ada",
    )(cc, ada_w, ada_b)


def _head_rms(x, seg_ones, w):
    ms = _dot((x * x).astype(BF16), seg_ones) * (1.0 / HEAD_DIM)
    return x * lax.rsqrt(ms + EPS) * w


def _rope(y, cos, sin_signed):
    lane = lax.broadcasted_iota(jnp.int32, y.shape, 1)
    first_half = (lane % 32) < 16
    n = y.shape[1]
    swapped = jnp.where(first_half, pltpu.roll(y, n - 16, 1), pltpu.roll(y, 16, 1))
    return y * cos + swapped * sin_signed


def _inproj_kernel(x_ref, ctx_ref, mod_ref, nw_ref, w_ref, cos_ref, sin_ref, hn_ref, seg_ref,
                   wdec_ref, bdec_ref,
                   a_ref, la_ref, bv_ref, cq_ref, ck_ref, cv_ref, dq_ref, dk_ref, dv_ref, g_ref,
                   xs_ref):
    i = pl.program_id(1)

    @pl.when(i == 0)
    def _():
        xs_ref[0:CTX_LEN, :] = ctx_ref[0]
        xs_ref[CTX_LEN:, :] = jnp.zeros((TM - CTX_LEN, xs_ref.shape[1]), F32)

    @pl.when(i > 0)
    def _():
        xs_ref[...] = x_ref[0]

    x = xs_ref[...]
    ms = jnp.mean(x * x, axis=-1, keepdims=True)
    y = x * lax.rsqrt(ms + EPS) * nw_ref[...]
    h = y * (1.0 + mod_ref[0, 1:2, :]) + mod_ref[0, 0:1, :]
    hb = h.astype(BF16)

    def proj(c0, width):
        return _dot(hb, w_ref[:, c0:c0 + width])

    seg = seg_ref[...]
    cos = cos_ref[...]
    sin = sin_ref[...]

    def gla_qkv():
        a = proj(C_A, 512)
        yield
        a_ref[0, :, 0:128] = a[:, 0:128] * (GLA_DK ** -0.5)
        a_ref[0, :, 128:512] = a[:, 128:512]

    def plain(c0, ref):
        r = proj(c0, 256)
        yield
        ref[0] = r.astype(BF16)

    def normed(c0, ref, row, rope):
        r = proj(c0, 256)
        yield
        y = _head_rms(r, seg, hn_ref[row:row + 1, :])
        ref[0] = (_rope(y, cos, sin) if rope else y).astype(BF16)

    def gates(j):
        g = proj(C_G + 256 * j, 256)
        yield
        g_ref[0, :, 256 * j:256 * (j + 1)] = (g * (1.0 / (1.0 + jnp.exp(-g)))).astype(BF16)

    def decay():
        lr = proj(C_LR, 128)
        yield
        z = _dot(lr.astype(BF16), wdec_ref[...]) + bdec_ref[...]
        la_ref[0] = (jnp.minimum(z, 0.0) - jnp.log(1.0 + jnp.exp(-jnp.abs(z)))) * (1.0 / GLA_TAU)

    _skew([decay(), normed(C_CQ, cq_ref, 0, True), normed(C_CK, ck_ref, 1, True),
                 normed(C_DQ, dq_ref, 2, False), normed(C_DK, dk_ref, 3, False),
                 gates(0), gates(1), gates(2), gates(3), gla_qkv(),
                 plain(C_BV, bv_ref), plain(C_CV, cv_ref), plain(C_DV, dv_ref)], INPROJ_GROUP)


def _inproj(x, ctx, mod, norm_w, w_perm, cos_t, sin_t, head_norms, seg_ones, wdec, bdec):
    b, seq, d = x.shape
    t_all = CTX_PAD + seq
    nblk = t_all // TM

    def tok(width):
        return pl.BlockSpec((1, TM, width), lambda bi, i: (bi, i, 0))

    def const(shape):
        return pl.BlockSpec(shape, lambda bi, i: (0,) * len(shape))

    def out(width, dtype):
        return jax.ShapeDtypeStruct((b, t_all, width), dtype)

    return pl.pallas_call(
        _inproj_kernel,
        out_shape=(out(512, F32), out(256, F32), out(256, BF16), out(256, BF16), out(256, BF16),
                   out(256, BF16), out(256, BF16), out(256, BF16), out(256, BF16), out(1024, BF16)),
        grid=(b, nblk),
        in_specs=[
            pl.BlockSpec((1, TM, d), lambda bi, i: (bi, jnp.maximum(i - 1, 0), 0)),
            pl.BlockSpec((1, CTX_LEN, d), lambda bi, i: (bi, 0, 0)),
            pl.BlockSpec((1, 3, d), lambda bi, i: (jnp.where(i == 0, b, bi), 0, 0)),
            const((1, d)),
            const((d, W_COLS)),
            pl.BlockSpec((TM, 256), lambda bi, i: (i, 0)),
            pl.BlockSpec((TM, 256), lambda bi, i: (i, 0)),
            const((8, 256)),
            const((256, 256)),
            const((128, 256)),
            const((1, 256)),
        ],
        out_specs=(tok(512), tok(256), tok(256), tok(256), tok(256), tok(256), tok(256), tok(256),
                   tok(256), tok(1024)),
        scratch_shapes=[pltpu.VMEM((TM, d), F32)],
        compiler_params=_cparams(("parallel", "arbitrary"), VMEM_LIMIT),
        name="inproj",
    )(x, ctx, mod, norm_w, w_perm, cos_t, sin_t, head_norms, seg_ones, wdec, bdec)


def _gla_kernel(af_ref, ab_ref, laf_ref, lab_ref, ones2_ref, trif_ref, trib_ref, pmf_ref, pmb_ref,
                qmask_ref, vmask_ref, stmask_ref, of_ref, ob_ref, st_ref, rb_ref, *, nbatch, nreal):
    @pl.when(pl.program_id(0) == 0)
    def _():
        st_ref[...] = jnp.zeros_like(st_ref)

    @pl.when(pl.program_id(0) < nreal)
    def _():
        shared = (qmask_ref[...], vmask_ref[...], stmask_ref[...])
        consts_f = (ones2_ref[...], trif_ref[...], pmf_ref[...]) + shared
        consts_b = (ones2_ref[...], trib_ref[...], pmb_ref[...]) + shared
        units = []
        for bi in range(nbatch):
            units.append(_gla_chunk(af_ref, laf_ref, of_ref, st_ref.at[0], rb_ref.at[0], bi, False, consts_f))
            units.append(_gla_chunk(ab_ref, lab_ref, ob_ref, st_ref.at[1], rb_ref.at[1], bi, True, consts_b))
        _interleave(units)

    @pl.when(pl.program_id(0) >= nreal)
    def _():
        of_ref[...] = jnp.zeros_like(of_ref)
        ob_ref[...] = jnp.zeros_like(ob_ref)


def _gla_chunk(a_ref, la_ref, o_ref, st_ref, rb_ref, bi, rev, consts):
    ones2, tri, pmask, qmask, vmask, stmask = consts
    c, sb = GLA_CHUNK, GLA_SUB
    nsub = c // sb
    hk = GLA_HEADS * GLA_DK
    hv = GLA_HEADS * GLA_DV

    q = a_ref[bi, :, 0:hk]
    k = a_ref[bi, :, hk:2 * hk]
    v = a_ref[bi, :, 2 * hk:2 * hk + hv]
    la = la_ref[bi]

    la_hi = la.astype(BF16)
    la_lo = (la - la_hi.astype(F32)).astype(BF16)
    cum = _dot(tri, la_hi) + _dot(tri, la_lo)
    cumx = cum - la

    rb_ref[bi, 0] = cum
    rb_ref[bi, 1] = cumx
    rb_ref[bi, 2] = k
    rb_ref[bi, 3] = v[:, 0:hk]
    rb_ref[bi, 4] = v[:, hk:hv]
    yield

    def brow(plane, r, n=sb):
        return jnp.broadcast_to(rb_ref[bi, plane, r:r + 1, :], (n, hk))

    end = 0 if rev else c - 1

    def first_row(blk):
        return sb * blk + (sb - 1 if rev else 0)

    def blk_of(p):
        return nsub - 1 - p if rev else p

    cref_rows = jnp.concatenate([brow(1, first_row(blk)) for blk in range(nsub)], axis=0)

    st = st_ref[bi]
    qi = q * jnp.exp(cum)
    inter = _dot_nt(qi.astype(BF16), st.astype(BF16))

    qd = q * jnp.exp(cum - cref_rows)
    q_st = jnp.concatenate([qd] * GLA_HEADS, axis=0) * qmask
    kt, vt = [], []
    for p in range(1, nsub):
        lo, hi = (c - sb * p, c) if rev else (0, sb * p)
        kt.append(k[lo:hi] * jnp.exp(brow(1, first_row(blk_of(p)), hi - lo) - cum[lo:hi]))
        vt.append(v[lo:hi])
    npad = pmask.shape[1] - sb * (nsub * (nsub - 1) // 2)
    kt.append(jnp.zeros((npad, hk), F32))
    vt.append(jnp.zeros((npad, hv), F32))
    kt = jnp.concatenate(kt, axis=0).astype(BF16)
    vt = jnp.concatenate(vt, axis=0).astype(BF16)
    sc = _dot_nt(q_st.astype(BF16), kt)

    t_loc = lax.broadcasted_iota(jnp.int32, (sb, hk), 0)
    valid = [(t_loc <= s) if rev else (t_loc >= s) for s in range(sb)]
    es = []
    for blk in range(nsub):
        r0 = sb * blk
        cb = cum[r0:r0 + sb]
        qb = q[r0:r0 + sb]
        for s in range(sb):
            diff = jnp.where(valid[s], cb - brow(0, r0 + s), -jnp.inf)
            es.append(jnp.exp(diff) * (qb * brow(2, r0 + s)))
    e = jnp.concatenate(es, axis=0).astype(BF16)
    r = _dot(e, ones2)
    cum_end = rb_ref[bi, 0, end:end + 1, :]
    kd = k * jnp.exp(cum_end - cum)
    kv = _dot_tn(v.astype(BF16), kd.astype(BF16))
    yield

    off_st = _dot((sc * pmask).astype(BF16), vt)
    yield

    off_st = off_st * vmask
    off = off_st[0:c]
    for h in range(1, GLA_HEADS):
        off = off + off_st[h * c:(h + 1) * c]
    dgs = []
    for blk in range(nsub):
        acc = None
        for s in range(sb):
            row0 = (blk * sb + s) * sb
            vrow = jnp.concatenate([brow(3, blk * sb + s), brow(4, blk * sb + s)], axis=1)
            term = r[row0:row0 + sb] * vrow
            acc = term if acc is None else acc + term
        dgs.append(acc)
    o_ref[bi] = (inter + off + jnp.concatenate(dgs, axis=0)).astype(BF16)

    st_ref[bi] = st * jnp.exp(cum_end) + kv * stmask


def _gla_consts(rev):
    c, sb = GLA_CHUNK, GLA_SUB
    nsub = c // sb
    t = np.arange(c)
    tri = (t[None, :] >= t[:, None]) if rev else (t[None, :] <= t[:, None])
    blk = t // sb
    p_of_t = (nsub - 1 - blk) if rev else blk
    col_p = np.concatenate([np.full(sb * p, p) for p in range(1, nsub)])
    col_p = np.concatenate([col_p, np.full(-col_p.size % 128, -1)])
    pmask = np.tile(p_of_t, GLA_HEADS)[:, None] == col_p[None, :]
    row_h = np.repeat(np.arange(GLA_HEADS), c)
    qmask = row_h[:, None] == (np.arange(GLA_HEADS * GLA_DK) // GLA_DK)[None, :]
    vmask = row_h[:, None] == (np.arange(GLA_HEADS * GLA_DV) // GLA_DV)[None, :]
    stmask = (np.arange(GLA_HEADS * GLA_DV) // GLA_DV)[:, None] == (np.arange(GLA_HEADS * GLA_DK) // GLA_DK)[None, :]
    f = lambda m: jnp.asarray(m, F32)
    return f(tri).astype(BF16), f(pmask), f(qmask), f(vmask), f(stmask)


def _gla(a, la, ones2):
    b, t_all, _ = a.shape
    nch = t_all // GLA_CHUNK
    nctx = CTX_LEN // GLA_CHUNK
    nskip = (CTX_PAD - CTX_LEN) // GLA_CHUNK
    tri_f, pm_f, qmask, vmask, stmask = _gla_consts(False)
    tri_b, pm_b, _, _, _ = _gla_consts(True)
    consts = (ones2, tri_f, tri_b, pm_f, pm_b, qmask, vmask, stmask)

    nreal = nch - nskip

    def fwd(j):
        return jnp.where(j < nctx, j, jnp.where(j < nreal, j + nskip, nctx + (j - nreal)))

    def bwd(j):
        return jnp.where(j < nctx, nctx - 1 - j, jnp.where(j < nreal, nch - 1 - (j - nctx), nctx + (j - nreal)))

    out = jax.ShapeDtypeStruct((b, t_all, 256), BF16)
    return pl.pallas_call(
        functools.partial(_gla_kernel, nbatch=b, nreal=nreal),
        out_shape=(out, out),
        grid=(nch,),
        in_specs=[pl.BlockSpec((b, GLA_CHUNK, 512), lambda j: (0, fwd(j), 0)),
                  pl.BlockSpec((b, GLA_CHUNK, 512), lambda j: (0, bwd(j), 0)),
                  pl.BlockSpec((b, GLA_CHUNK, 128), lambda j: (0, fwd(j), 0)),
                  pl.BlockSpec((b, GLA_CHUNK, 128), lambda j: (0, bwd(j), 1))]
                 + [pl.BlockSpec(t.shape, lambda j: (0, 0)) for t in consts],
        out_specs=(pl.BlockSpec((b, GLA_CHUNK, 256), lambda j: (0, fwd(j), 0)),
                   pl.BlockSpec((b, GLA_CHUNK, 256), lambda j: (0, bwd(j), 0))),
        scratch_shapes=[pltpu.VMEM((2, b, 256, 128), F32), pltpu.VMEM((2, b, 5, GLA_CHUNK, 128), F32)],
        compiler_params=_cparams(("arbitrary",)),
        name="gla",
    )(a, a, la, la, *consts)


def _fnet_chan_kernel(v_ref, wc_ref, ws_ref, o_ref):
    v = v_ref[0]
    o_ref[0, :, :] = _dot(v, wc_ref[...]).astype(BF16)
    o_ref[1, :, :] = _dot(v, ws_ref[...]).astype(BF16)


def _fnet_chan(bv, wc, ws, row_blk_off, length):
    b = bv.shape[0]
    tl = min(TM, length)
    return pl.pallas_call(
        _fnet_chan_kernel,
        out_shape=jax.ShapeDtypeStruct((2, length, b * 256), BF16),
        grid=(b, length // tl),
        in_specs=[pl.BlockSpec((1, tl, 256), lambda bi, i: (bi, i + row_blk_off, 0)),
                  pl.BlockSpec((256, 256), lambda bi, i: (0, 0)),
                  pl.BlockSpec((256, 256), lambda bi, i: (0, 0))],
        out_specs=pl.BlockSpec((2, tl, 256), lambda bi, i: (0, i, bi)),
        compiler_params=_cparams(("parallel", "arbitrary")),
        name="fnet_chan",
    )(bv, wc, ws)


def _fnet_pos_kernel(d_ref, y_ref, wf_ref, o_ref, acc_ref, *, nb):
    kk = pl.program_id(1)

    @pl.when(kk == 0)
    def _():
        acc_ref[...] = jnp.zeros_like(acc_ref)

    acc_ref[...] += _dot(d_ref[...], y_ref[...])

    @pl.when(kk == pl.num_programs(1) - 1)
    def _():
        wf = wf_ref[...]
        for bi in range(nb):
            o_ref[bi] = _dot(acc_ref[:, bi * 256:(bi + 1) * 256].astype(BF16), wf).astype(BF16)


def _fnet_pos(dmat, y2, wf, nb):
    length, k2 = dmat.shape
    tm = min(512, length)
    tk = min(1024, k2)
    return pl.pallas_call(
        functools.partial(_fnet_pos_kernel, nb=nb),
        out_shape=jax.ShapeDtypeStruct((nb, length, 256), BF16),
        grid=(length // tm, k2 // tk),
        in_specs=[pl.BlockSpec((tm, tk), lambda i, kk: (i, kk)),
                  pl.BlockSpec((tk, nb * 256), lambda i, kk: (kk, 0)),
                  pl.BlockSpec((256, 256), lambda i, kk: (0, 0))],
        out_specs=pl.BlockSpec((nb, tm, 256), lambda i, kk: (0, i, 0)),
        scratch_shapes=[pltpu.VMEM((tm, nb * 256), F32)],
        compiler_params=_cparams(("parallel", "arbitrary"), VMEM_LIMIT),
        name="fnet_pos",
    )(dmat, y2, wf)


def _dft_tables(length):
    j = jnp.arange(length, dtype=jnp.int32)
    jk = (j[:, None] * j[None, :]) % length
    ang = jk.astype(F32) * (2.0 * math.pi / length)
    s = 1.0 / math.sqrt(length)
    return jnp.concatenate([jnp.cos(ang) * s, -jnp.sin(ang) * s], axis=1).astype(BF16)


def _chan_tables():
    j = np.arange(FNET_GW)
    ang = 2.0 * np.pi * ((j[:, None] * j[None, :]) % FNET_GW) / FNET_GW
    s = 1.0 / math.sqrt(FNET_GW)
    eye = np.eye(BRANCH_W // FNET_GW)
    wc = np.kron(eye, np.cos(ang) * s)
    ws = np.kron(eye, np.sin(ang) * s)
    return jnp.asarray(wc, F32).astype(BF16), jnp.asarray(ws, F32).astype(BF16)


def _fnet(bv, wf, row_blk_off, length):
    b = bv.shape[0]
    wc, ws = _chan_tables()
    y = _fnet_chan(bv, wc, ws, row_blk_off, length)
    y2 = y.reshape(2 * length, b * 256)
    return _fnet_pos(_dft_tables(length), y2, wf, b)


FNET_UNROLL = 8


def _fnet_grid_kernel(v_ref, cg_ref, sg_ref, w1_ref, w2_ref, wf_ref, o_ref, z_ref, g_ref, f_ref, *, rows, row_off):
    length = rows * GRID_W
    v = v_ref[0, row_off:row_off + length, :]
    wf = wf_ref[...]
    zr = _dot(v, _dot(cg_ref[...], wf).astype(BF16))
    zi = _dot(v, _dot(sg_ref[...], wf).astype(BF16))
    z_ref[0] = zr[:, 0:128].reshape(rows, GRID_W, 128)
    z_ref[1] = zr[:, 128:256].reshape(rows, GRID_W, 128)
    z_ref[2] = zi[:, 0:128].reshape(rows, GRID_W, 128)
    z_ref[3] = zi[:, 128:256].reshape(rows, GRID_W, 128)

    n8 = rows * 8
    for w0 in range(0, GRID_W, 8):
        def slab(c):
            return z_ref[c, :, w0:w0 + 8, :].reshape(n8, 128)
        rhs = jnp.concatenate([jnp.concatenate([slab(0), slab(1)], axis=1),
                               jnp.concatenate([slab(2), slab(3)], axis=1)], axis=0).astype(BF16)
        g = _dot(w1_ref[...], rhs)
        g_ref[0, :, w0:w0 + 8, :] = g[0:n8, 0:128].reshape(rows, 8, 128)
        g_ref[1, :, w0:w0 + 8, :] = g[0:n8, 128:256].reshape(rows, 8, 128)
        g_ref[2, :, w0:w0 + 8, :] = g[n8:2 * n8, 0:128].reshape(rows, 8, 128)
        g_ref[3, :, w0:w0 + 8, :] = g[n8:2 * n8, 128:256].reshape(rows, 8, 128)

    def stage_w(ka, carry):
        r0 = pl.multiple_of(ka * GRID_W, GRID_W)
        rhs = jnp.concatenate([jnp.concatenate([g_ref[0, ka], g_ref[1, ka]], axis=1),
                               jnp.concatenate([g_ref[2, ka], g_ref[3, ka]], axis=1)], axis=0).astype(BF16)
        y = _dot(w2_ref[pl.ds(r0, GRID_W), :], rhs)
        f_ref[0, pl.ds(ka, GRID_W, stride=rows), :] = y[:, 0:128]
        f_ref[1, pl.ds(ka, GRID_W, stride=rows), :] = y[:, 128:256]
        return carry

    lax.fori_loop(0, rows, stage_w, 0, unroll=FNET_UNROLL)
    o_ref[0, :, 0:128] = f_ref[0].astype(BF16)
    o_ref[0, :, 128:256] = f_ref[1].astype(BF16)


def _fnet_grid_tables(rows):
    length = rows * GRID_W
    j = np.arange(rows)
    ang = 2.0 * np.pi * ((j[:, None] * j[None, :]) % rows) / rows
    c, s = np.cos(ang), np.sin(ang)
    w1 = np.kron(np.block([[c, -s], [-s, -c]]), np.eye(8))
    w1 = jnp.asarray(w1, F32).astype(BF16)
    ka = jnp.arange(rows, dtype=jnp.int32)[:, None, None]
    kb = jnp.arange(GRID_W, dtype=jnp.int32)[None, :, None]
    w = jnp.arange(GRID_W, dtype=jnp.int32)[None, None, :]
    ang2 = ((w * (ka + rows * kb)) % length).astype(F32) * (2.0 * math.pi / length)
    scale = 1.0 / math.sqrt(length)
    w2 = jnp.concatenate([jnp.cos(ang2) * scale, jnp.sin(ang2) * scale], axis=2)
    return w1, w2.reshape(length, 2 * GRID_W).astype(BF16)


def _fnet_grid(bv, wf, seq):
    b, t_all, _ = bv.shape
    rows = seq // GRID_W
    wc, ws = _chan_tables()
    w1, w2 = _fnet_grid_tables(rows)

    def const(shape):
        return pl.BlockSpec(shape, lambda bi: (0,) * len(shape))

    return pl.pallas_call(
        functools.partial(_fnet_grid_kernel, rows=rows, row_off=CTX_PAD),
        out_shape=jax.ShapeDtypeStruct((b, seq, 256), BF16),
        grid=(b,),
        in_specs=[pl.BlockSpec((1, t_all, 256), lambda bi: (bi, 0, 0)), const((256, 256)), const((256, 256)),
                  const((16 * rows, 16 * rows)), const((seq, 2 * GRID_W)), const((256, 256))],
        out_specs=pl.BlockSpec((1, seq, 256), lambda bi: (bi, 0, 0)),
        scratch_shapes=[pltpu.VMEM((4, rows, GRID_W, 128), F32), pltpu.VMEM((4, rows, GRID_W, 128), F32),
                        pltpu.VMEM((2, seq, 128), F32)],
        compiler_params=_cparams(("parallel",), VMEM_LIMIT),
        name="fnet_grid",
    )(bv, wc, ws, w1, w2, wf)


def _stack_heads(qg):
    lane = lax.broadcasted_iota(jnp.int32, qg.shape, 1)
    zero = jnp.zeros_like(qg)
    return jnp.concatenate([jnp.where(lane < HEAD_DIM, qg, zero),
                            jnp.where(lane >= HEAD_DIM, qg, zero)], axis=0)


def _unstack_heads(o, rows):
    lane = lax.broadcasted_iota(jnp.int32, (rows, 2 * HEAD_DIM), 1)
    return jnp.where(lane < HEAD_DIM, o[0:rows], o[rows:2 * rows])


def _swa_kernel(sink_ref, band_ref, q_ref, kp_ref, kc_ref, kn_ref, kx_ref, vp_ref, vc_ref, vn_ref, vx_ref,
                o_ref, *, nbatch):
    n = pl.program_id(0)
    nb = pl.num_programs(0)
    blk = SWA_BLOCK
    rows = 2 * blk
    col = lax.broadcasted_iota(jnp.int32, (1, 3 * blk), 1)
    col_lo = jnp.where(n > 0, 0, blk)
    col_hi = jnp.where(n < nb - 1, 3 * blk, 2 * blk)
    edge = jnp.where((col >= col_lo) & (col < col_hi), 0.0, -jnp.inf)
    bias = band_ref[...] + edge
    head_row = lax.broadcasted_iota(jnp.int32, (rows, 1), 0) < blk

    def unit(bi, g):
        ls = slice(128 * g, 128 * (g + 1))
        q_st = _stack_heads(q_ref[bi, :, ls])
        kw = jnp.concatenate([kp_ref[bi, :, ls], kc_ref[bi, :, ls], kn_ref[bi, :, ls]], axis=0)
        vw = jnp.concatenate([vp_ref[bi, :, ls], vc_ref[bi, :, ls], vn_ref[bi, :, ls]], axis=0)
        s_loc = _dot_nt(q_st, kw) + bias
        s_ctx = _dot_nt(q_st, kx_ref[bi, :, ls])
        yield
        sink =jnp.where(head_row, sink_ref[2 * g], sink_ref[2 * g + 1])
        m = jnp.maximum(jnp.maximum(jnp.max(s_loc, axis=-1, keepdims=True),
                                    jnp.max(s_ctx, axis=-1, keepdims=True)), sink)
        p_loc = jnp.exp(s_loc - m)
        p_ctx = jnp.exp(s_ctx - m)
        den = (jnp.sum(p_loc, axis=-1, keepdims=True) + jnp.sum(p_ctx, axis=-1, keepdims=True)
               + jnp.exp(sink - m))
        o = _dot(p_loc.astype(BF16), vw) + _dot(p_ctx.astype(BF16), vx_ref[bi, :, ls])
        yield
        o_ref[bi, :, ls] = _unstack_heads(o / den, blk).astype(BF16)

    _interleave([unit(bi, g) for bi in range(nbatch) for g in range(2)], group=ATTN_GROUP)


def _swa_band():
    row = np.arange(2 * SWA_BLOCK)[:, None] % SWA_BLOCK
    col = np.arange(3 * SWA_BLOCK)[None, :]
    ok = np.abs(col - SWA_BLOCK - row) <= SWA_WINDOW
    return jnp.where(jnp.asarray(ok), 0.0, -jnp.inf).astype(F32)


def _swa(sink, cq, ck, cv, seq):
    b = cq.shape[0]
    nb = seq // SWA_BLOCK
    off = CTX_PAD // SWA_BLOCK

    def at(f):
        return pl.BlockSpec((b, SWA_BLOCK, 256), lambda n: (0, f(n) + off, 0))

    prev = at(lambda n: jnp.maximum(n - 1, 0))
    cur = at(lambda n: n)
    nxt = at(lambda n: jnp.minimum(n + 1, nb - 1))
    cx = pl.BlockSpec((b, CTX_LEN, 256), lambda n: (0, 0, 0))
    band = pl.BlockSpec((2 * SWA_BLOCK, 3 * SWA_BLOCK), lambda n: (0, 0))
    return pl.pallas_call(
        functools.partial(_swa_kernel, nbatch=b),
        out_shape=jax.ShapeDtypeStruct((b, seq, 256), BF16),
        grid=(nb,),
        in_specs=[pl.BlockSpec(memory_space=pltpu.SMEM), band, cur, prev, cur, nxt, cx, prev, cur, nxt, cx],
        out_specs=pl.BlockSpec((b, SWA_BLOCK, 256), lambda n: (0, n, 0)),
        compiler_params=_cparams(("arbitrary",)),
        name="swa",
    )(sink, _swa_band(), cq, ck, ck, ck, ck, cv, cv, cv, cv)


NA_ROWS_PER_STEP = 8


def _na_kernel(q_ref, k_ref, v_ref, bias_ref, o_ref, *, rows_total):
    kh = NA_KH
    nk = kh * GRID_W
    def unit(rr, g):
        r = pl.program_id(1) * NA_ROWS_PER_STEP + rr
        row_start = jnp.clip(r - kh // 2, 0, rows_total - kh)
        cls = r - row_start
        k0 = pl.multiple_of(CTX_PAD + row_start * GRID_W, GRID_W)
        qs = slice(GRID_W * rr, GRID_W * (rr + 1))
        ls = slice(128 * g, 128 * (g + 1))
        q_st = _stack_heads(q_ref[0, qs, ls])
        s_nb = _dot_nt(q_st, k_ref[0, pl.ds(k0, nk), ls]) + bias_ref[cls, g]
        s_ctx = _dot_nt(q_st, k_ref[0, 0:CTX_LEN, ls])
        yield
        m =jnp.maximum(jnp.max(s_nb, axis=-1, keepdims=True), jnp.max(s_ctx, axis=-1, keepdims=True))
        p_nb = jnp.exp(s_nb - m)
        p_ctx = jnp.exp(s_ctx - m)
        den = jnp.sum(p_nb, axis=-1, keepdims=True) + jnp.sum(p_ctx, axis=-1, keepdims=True)
        o = (_dot(p_nb.astype(BF16), v_ref[0, pl.ds(k0, nk), ls])
             + _dot(p_ctx.astype(BF16), v_ref[0, 0:CTX_LEN, ls]))
        yield
        o_ref[0, qs, ls] = _unstack_heads(o / den, GRID_W).astype(BF16)

    _interleave([unit(rr, g) for rr in range(NA_ROWS_PER_STEP) for g in range(2)], group=ATTN_GROUP)


def _bias_expand_kernel(rb_ref, oh_ref, o_ref):
    x = rb_ref[...]
    oh = oh_ref[...]
    hi = x.astype(BF16)
    r1 = x - hi.astype(F32)
    mid = r1.astype(BF16)
    lo = (r1 - mid.astype(F32)).astype(BF16)
    o_ref[...] = _dot(hi, oh) + _dot(mid, oh) + _dot(lo, oh)


def _na_bias_table(rel_bias):
    ndy, ndx = 2 * NA_KH - 1, 2 * NA_KW - 1
    cq = np.arange(GRID_W)
    col_start = np.clip(cq - NA_KW // 2, 0, GRID_W - NA_KW)
    col_ok = (cq[None, :] >= col_start[:, None]) & (cq[None, :] < col_start[:, None] + NA_KW)
    qi = jnp.arange(GRID_W, dtype=jnp.int32)
    dx = jnp.clip(qi[None, :] - qi[:, None], -(NA_KW - 1), NA_KW - 1) + (NA_KW - 1)
    onehot = (jnp.arange(128, dtype=jnp.int32)[:, None] == dx.reshape(1, GRID_W * GRID_W)).astype(BF16)
    rb = jnp.zeros((64, 128), F32).at[0:NA_HEADS * ndy, 0:ndx].set(rel_bias.reshape(NA_HEADS * ndy, ndx))
    p = pl.pallas_call(
        _bias_expand_kernel,
        out_shape=jax.ShapeDtypeStruct((64, GRID_W * GRID_W), F32),
        name="na_bias_expand",
    )(rb, onehot)
    p = p[0:NA_HEADS * ndy].reshape(NA_HEADS, ndy, GRID_W, GRID_W)
    p = jnp.where(col_ok[None, None], p, -jnp.inf)
    tabs = jnp.stack([p[:, NA_KH - 1 - c:2 * NA_KH - 1 - c] for c in range(NA_KH)])
    tabs = tabs.transpose(0, 1, 3, 2, 4)
    return tabs.reshape(NA_KH, NA_HEADS // 2, 2 * GRID_W, NA_KH * GRID_W)


def _na(dq, dk, dv, bias_tab, seq):
    b, t_all, _ = dq.shape
    rows_total = seq // GRID_W
    qrows = NA_ROWS_PER_STEP * GRID_W
    assert CTX_PAD % qrows == 0 and rows_total % NA_ROWS_PER_STEP == 0
    off = CTX_PAD // qrows
    full = pl.BlockSpec((1, t_all, 256), lambda bi, r: (bi, 0, 0))
    return pl.pallas_call(
        functools.partial(_na_kernel, rows_total=rows_total),
        out_shape=jax.ShapeDtypeStruct((b, seq, 256), BF16),
        grid=(b, rows_total // NA_ROWS_PER_STEP),
        in_specs=[pl.BlockSpec((1, qrows, 256), lambda bi, r: (bi, r + off, 0)), full, full,
                  pl.BlockSpec(bias_tab.shape, lambda bi, r: (0, 0, 0, 0))],
        out_specs=pl.BlockSpec((1, qrows, 256), lambda bi, r: (bi, r, 0)),
        compiler_params=_cparams(("parallel", "arbitrary"), VMEM_LIMIT),
        name="na",
    )(dq, dk, dv, bias_tab)


def _ctx_attn_kernel(sink_ref, q_ref, k_ref, v_ref, o_ref, *, has_sink):
    n = CTX_LEN
    head_row = lax.broadcasted_iota(jnp.int32, (2 * n, 1), 0) < n
    for g in range(2):
        ls = slice(128 * g, 128 * (g + 1))
        q_st = _stack_heads(q_ref[0, :, ls])
        s = _dot_nt(q_st, k_ref[0, :, ls])
        m = jnp.max(s, axis=-1, keepdims=True)
        if has_sink:
            sink = jnp.where(head_row, sink_ref[2 * g], sink_ref[2 * g + 1])
            m = jnp.maximum(m, sink)
        p = jnp.exp(s - m)
        den = jnp.sum(p, axis=-1, keepdims=True)
        if has_sink:
            den = den + jnp.exp(sink - m)
        o = _dot(p.astype(BF16), v_ref[0, :, ls])
        o_ref[0, :, ls] = _unstack_heads(o / den, n).astype(BF16)


def _ctx_attn(sink, q, k, v, has_sink):
    b = q.shape[0]
    blk = pl.BlockSpec((1, CTX_LEN, 256), lambda bi: (bi, 0, 0))
    return pl.pallas_call(
        functools.partial(_ctx_attn_kernel, has_sink=has_sink),
        out_shape=jax.ShapeDtypeStruct((b, CTX_LEN, 256), BF16),
        grid=(b,),
        in_specs=[pl.BlockSpec(memory_space=pltpu.SMEM), blk, blk, blk],
        out_specs=blk,
        compiler_params=_cparams(("parallel",)),
        name="ctx_attn_sink" if has_sink else "ctx_attn",
    )(sink, q, k, v)


def _outproj_kernel(x_ref, of_ref, ob_ref, fz_ref, co_ref, do_ref, g_ref, mod_ref, gn_ref, seg_ref,
                    w_ref, o_ref):
    a = of_ref[0].astype(F32) + ob_ref[0].astype(F32)
    ms = _dot((a * a).astype(BF16), seg_ref[...]) * (1.0 / GLA_DV)
    a = a * lax.rsqrt(ms + EPS) * gn_ref[...]
    g = g_ref[0].astype(F32)
    mixed = jnp.concatenate([a * g[:, 0:256], fz_ref[0].astype(F32) * g[:, 256:512],
                             co_ref[0].astype(F32) * g[:, 512:768], do_ref[0].astype(F32) * g[:, 768:1024]],
                            axis=1)
    y = _dot(mixed.astype(BF16), w_ref[...])
    o_ref[0] = x_ref[0] + mod_ref[0, 2:3, :] * y


def _outproj(x, o_f, o_b, fz, c_o, d_o, g, mod, gla_norm, seg_ones, w_out, is_ctx):
    b, rows, d = x.shape
    tm = CTX_LEN if is_ctx else TM
    nblk = rows // tm
    comb = 0 if is_ctx else CTX_PAD // tm
    nbatch = b

    def own(width):
        return pl.BlockSpec((1, tm, width), lambda bi, i: (bi, i, 0))

    def combined(width):
        return pl.BlockSpec((1, tm, width), lambda bi, i: (bi, i + comb, 0))

    def const(shape):
        return pl.BlockSpec(shape, lambda bi, i: (0,) * len(shape))

    mod_spec = pl.BlockSpec((1, 3, d), (lambda bi, i: (nbatch, 0, 0)) if is_ctx else (lambda bi, i: (bi, 0, 0)))
    return pl.pallas_call(
        _outproj_kernel,
        out_shape=jax.ShapeDtypeStruct((b, rows, d), F32),
        grid=(b, nblk),
        in_specs=[own(d), combined(256), combined(256), own(256), own(256), own(256), combined(1024),
                  mod_spec, const((1, 256)), const((256, 256)), const((d, d))],
        out_specs=own(d),
        compiler_params=_cparams(("parallel", "arbitrary"), VMEM_LIMIT),
        name="outproj_ctx" if is_ctx else "outproj",
    )(x, o_f, o_b, fz, c_o, d_o, g, mod, gla_norm, seg_ones, w_out)


def _permute_w_in(w):
    s = lambda lo, hi: w[:, lo:hi]
    ck, cv = s(1568, 1696), s(1696, 1824)
    dup = lambda t: jnp.concatenate([t[:, :64], t[:, :64], t[:, 64:], t[:, 64:]], axis=1)
    pad = jnp.zeros((w.shape[0], 128 - 2 * GLA_RANK), w.dtype)
    return jnp.concatenate([
        s(0, 512),
        s(800, 1056),
        s(1312, 1568), dup(ck), dup(cv),
        s(2080, 2336), s(2336, 2592), s(2592, 2848),
        s(512, 768), s(1056, 1312), s(1824, 2080), s(2848, 3104),
        s(768, 800), pad,
    ], axis=1)


def _rope_tables(seq):
    t = np.arange(seq)
    axis_dim = HEAD_DIM // 2
    inv = ROPE_BASE ** (-np.arange(0, axis_dim, 2, dtype=np.float64) / axis_dim)
    row = (t // GRID_W).astype(np.float64)
    col = (t % GRID_W).astype(np.float64)
    ang = np.concatenate([row[:, None] * inv, row[:, None] * inv, col[:, None] * inv, col[:, None] * inv], axis=1)
    cos = np.cos(ang)
    sign = np.concatenate([-np.ones(16), np.ones(16), -np.ones(16), np.ones(16)])
    sin = np.sin(ang) * sign
    cos = np.concatenate([np.ones((CTX_PAD, HEAD_DIM)), cos], axis=0)
    sin = np.concatenate([np.zeros((CTX_PAD, HEAD_DIM)), sin], axis=0)
    return (jnp.asarray(np.tile(cos, (1, 4)), F32), jnp.asarray(np.tile(sin, (1, 4)), F32))


def _seg_ones(width, seg):
    i = np.arange(width)
    return jnp.asarray((i[:, None] // seg) == (i[None, :] // seg), BF16)


def _gla_ones2():
    r = np.arange(GLA_HEADS * GLA_DK) // GLA_DK
    c = np.arange(GLA_HEADS * GLA_DV) // GLA_DV
    return jnp.asarray(r[:, None] == c[None, :], BF16)


def _layer(x, ctx, cc, need_ctx, tables, norm_w, ada_w, ada_b, w_in, w_out, gla_dec_w, gla_dec_b,
           gla_out_norm, fnet_w, swa_q_norm, swa_k_norm, swa_sink, na_q_norm, na_k_norm, na_rel_bias):
    b, seq, d = x.shape
    cos_t, sin_t, seg64, ones2 = tables
    mod = _ada(cc, ada_w.astype(BF16), ada_b[None, :])[:b + 1].reshape(b + 1, 3, d)

    qs = HEAD_DIM ** -0.5
    head_norms = jnp.stack([jnp.tile(swa_q_norm, 4) * qs, jnp.tile(swa_k_norm, 4),
                            jnp.tile(na_q_norm, 4) * qs, jnp.tile(na_k_norm, 4)])
    head_norms = jnp.concatenate([head_norms, jnp.zeros((4, 256), F32)], axis=0)
    wdec = jnp.zeros((128, 256), F32)
    wdec = wdec.at[0:GLA_RANK, 0:128].set(gla_dec_w[0]).at[GLA_RANK:2 * GLA_RANK, 128:256].set(gla_dec_w[1])
    bdec = gla_dec_b.reshape(1, 256)

    a, la, bv, cq, ck, cv, dq, dk, dv, g = _inproj(
        x, ctx, mod, norm_w[None, :], _permute_w_in(w_in).astype(BF16), cos_t, sin_t, head_norms, seg64,
        wdec.astype(BF16), bdec)

    o_f, o_b = _gla(a, la, ones2)
    wf = fnet_w.astype(BF16)
    fz = _fnet_grid(bv, wf, seq)
    c_o = _swa(swa_sink, cq, ck, cv, seq)
    d_o = _na(dq, dk, dv, _na_bias_table(na_rel_bias), seq)

    gn = jnp.tile(gla_out_norm, 4)[None, :]
    wo = w_out.astype(BF16)
    x_new = _outproj(x, o_f, o_b, fz, c_o, d_o, g, mod, gn, seg64, wo, is_ctx=False)
    ctx_new = None
    if need_ctx:
        fzc = _fnet(bv, wf, 0, CTX_LEN)
        c_oc = _ctx_attn(swa_sink, cq, ck, cv, has_sink=True)
        d_oc = _ctx_attn(swa_sink, dq, dk, dv, has_sink=False)
        ctx_new = _outproj(ctx, o_f, o_b, fzc, c_oc, d_oc, g, mod, gn, seg64, wo, is_ctx=True)
    return x_new, ctx_new


def kernel(x, c, ctx, c_ctx, norm_w, ada_w, ada_b, w_in, w_out, gla_dec_w, gla_dec_b, gla_out_norm,
           fnet_w, swa_q_norm, swa_k_norm, swa_sink, na_q_norm, na_k_norm, na_rel_bias):
    b, seq, d = x.shape
    depth = norm_w.shape[0]
    assert ctx.shape[1] == CTX_LEN and seq % TM == 0 and seq % (NA_KH * GRID_W) == 0
    tables = _rope_tables(seq) + (_seg_ones(256, HEAD_DIM), _gla_ones2())
    cc = jnp.concatenate([c, c_ctx[None, :], jnp.zeros((8 - (b + 1) % 8, d), F32)], axis=0)
    for i in range(depth):
        x, ctx = _layer(x, ctx, cc, i < depth - 1, tables, norm_w[i], ada_w[i], ada_b[i], w_in[i], w_out[i],
                        gla_dec_w[i], gla_dec_b[i], gla_out_norm[i], fnet_w[i], swa_q_norm[i],
                        swa_k_norm[i], swa_sink[i], na_q_norm[i], na_k_norm[i], na_rel_bias[i])
    return x
```

```python
import functools
import math

import numpy as np
import jax
import jax.numpy as jnp
from jax import lax
from jax.experimental import pallas as pl
from jax.experimental.pallas import tpu as pltpu

F32 = jnp.float32
BF16 = jnp.bfloat16

GRID_W = 64
CTX_LEN = 256
HEAD_DIM = 64
BRANCH_W = 256
EPS = 1e-6
ROPE_BASE = 10000.0
GLA_HEADS = 4
GLA_DK = 32
GLA_DV = 64
GLA_RANK = 16
GLA_TAU = 16.0
GLA_CHUNK = 64
GLA_SUB = 8
LOG2E = 1.4426950408889634
FNET_GW = 64
SWA_BLOCK = 128
SWA_WINDOW = 128
NA_KH = 8
NA_KW = 16
NA_HEADS = 4

TM = 512
CTX_PAD = TM
VMEM_LIMIT = 48 * 1024 * 1024

C_A = 0
C_BV = 512
C_CQ = 768
C_CK = 1024
C_CV = 1280
C_DQ = 1536
C_DK = 1792
C_DV = 2048
C_G = 2304
C_LR = 3328
W_COLS = 3456


def _dot(a, b):
    return jnp.dot(a, b, preferred_element_type=F32)


def _dot_nt(a, b):
    return lax.dot_general(a, b, (((1,), (1,)), ((), ())), preferred_element_type=F32)


def _dot_tn(a, b):
    return lax.dot_general(a, b, (((0,), (0,)), ((), ())), preferred_element_type=F32)


INPROJ_GROUP = 2
ATTN_GROUP = 4
_DONE = object()


def _interleave(units, group=None):
    group = group or len(units)
    for i in range(0, len(units), group):
        live = units[i:i + group]
        while live:
            live = [u for u in live if next(u, _DONE) is not _DONE]


def _skew(units, depth):
    started = []
    for u in units:
        next(u)
        started.append(u)
        if len(started) > depth:
            next(started.pop(0), _DONE)
    for u in started:
        next(u, _DONE)


def _cparams(sem, vmem=None):
    return pltpu.CompilerParams(dimension_semantics=sem, vmem_limit_bytes=vmem)


def _ada_kernel(c_ref, w_ref, b_ref, o_ref):
    c = c_ref[...]
    s = c * (1.0 / (1.0 + jnp.exp(-c)))
    d = c.shape[1]
    for j in range(3):
        cols = slice(j * d, (j + 1) * d)
        o_ref[0, :, j, :] = _dot(s.astype(BF16), w_ref[0, :, cols].astype(BF16)) + b_ref[0, :, cols]


def _ada(cc, ada_w, ada_b):
    r, d = cc.shape
    depth, _, n = ada_w.shape
    return pl.pallas_call(
        _ada_kernel,
        out_shape=jax.ShapeDtypeStruct((depth, r, 3, d), F32),
        grid=(depth,),
        in_specs=[pl.BlockSpec((r, d), lambda l: (0, 0)),
                  pl.BlockSpec((1, d, n), lambda l: (l, 0, 0)),
                  pl.BlockSpec((1, 1, n), lambda l: (l, 0, 0))],
        out_specs=pl.BlockSpec((1, r, 3, d), lambda l: (l, 0, 0, 0)),
        compiler_params=_cparams(("parallel",), VMEM_LIMIT),
        name="ada",
    )(cc, ada_w, ada_b)


def _head_rms(x, seg_ones, w):
    ms = _dot((x * x).astype(BF16), seg_ones) * (1.0 / HEAD_DIM)
    return x * lax.rsqrt(ms + EPS) * w


def _rope(y, cos, sin_signed):
    lane = lax.broadcasted_iota(jnp.int32, y.shape, 1)
    first_half = (lane % 32) < 16
    n = y.shape[1]
    swapped = jnp.where(first_half, pltpu.roll(y, n - 16, 1), pltpu.roll(y, 16, 1))
    return y * cos + swapped * sin_signed


def _inproj_kernel(x_ref, ctx_ref, mod_ref, nw_ref, w_ref, cos_ref, sin_ref, hn_ref, seg_ref,
                   wdec_ref, bdec_ref,
                   a_ref, la_ref, bv_ref, cq_ref, ck_ref, cv_ref, dq_ref, dk_ref, dv_ref, g_ref,
                   xs_ref):
    i = pl.program_id(1)

    @pl.when(i == 0)
    def _():
        xs_ref[0:CTX_LEN, :] = ctx_ref[0]
        xs_ref[CTX_LEN:, :] = jnp.zeros((TM - CTX_LEN, xs_ref.shape[1]), F32)

    @pl.when(i > 0)
    def _():
        xs_ref[...] = x_ref[0]

    x = xs_ref[...]
    ms = jnp.mean(x * x, axis=-1, keepdims=True)
    y = x * lax.rsqrt(ms + EPS) * nw_ref[...]
    h = y * (1.0 + mod_ref[0, 1:2, :]) + mod_ref[0, 0:1, :]
    hb = h.astype(BF16)

    def proj(c0, width):
        return _dot(hb, w_ref[:, c0:c0 + width])

    seg = seg_ref[...]
    cos = cos_ref[...]
    sin = sin_ref[...]

    def gla_qkv():
        a = proj(C_A, 512)
        yield
        a_ref[0, :, 0:128] = a[:, 0:128] * (GLA_DK ** -0.5)
        a_ref[0, :, 128:512] = a[:, 128:512]

    def plain(c0, ref):
        r = proj(c0, 256)
        yield
        ref[0] = r.astype(BF16)

    def normed(c0, ref, row, rope):
        r = proj(c0, 256)
        yield
        y = _head_rms(r, seg, hn_ref[row:row + 1, :])
        ref[0] = (_rope(y, cos, sin) if rope else y).astype(BF16)

    def gates(j):
        g = proj(C_G + 256 * j, 256)
        yield
        g_ref[0, :, 256 * j:256 * (j + 1)] = (g * (1.0 / (1.0 + jnp.exp(-g)))).astype(BF16)

    def decay():
        lr = proj(C_LR, 128)
        yield
        z = _dot(lr.astype(BF16), wdec_ref[...]) + bdec_ref[...]
        la_ref[0] = (jnp.minimum(z, 0.0) - jnp.log(1.0 + jnp.exp(-jnp.abs(z)))) * (LOG2E / GLA_TAU)

    _skew([decay(), normed(C_CQ, cq_ref, 0, True), normed(C_CK, ck_ref, 1, True),
                 normed(C_DQ, dq_ref, 2, False), normed(C_DK, dk_ref, 3, False),
                 gates(0), gates(1), gates(2), gates(3), gla_qkv(),
                 plain(C_BV, bv_ref), plain(C_CV, cv_ref), plain(C_DV, dv_ref)], INPROJ_GROUP)


def _inproj(x, ctx, mod, norm_w, w_perm, cos_t, sin_t, head_norms, seg_ones, wdec, bdec):
    b, seq, d = x.shape
    t_all = CTX_PAD + seq
    nblk = t_all // TM

    def tok(width):
        return pl.BlockSpec((1, TM, width), lambda bi, i: (bi, i, 0))

    def const(shape):
        return pl.BlockSpec(shape, lambda bi, i: (0,) * len(shape))

    def out(width, dtype):
        return jax.ShapeDtypeStruct((b, t_all, width), dtype)

    return pl.pallas_call(
        _inproj_kernel,
        out_shape=(out(512, F32), out(256, F32), out(256, BF16), out(256, BF16), out(256, BF16),
                   out(256, BF16), out(256, BF16), out(256, BF16), out(256, BF16), out(1024, BF16)),
        grid=(b, nblk),
        in_specs=[
            pl.BlockSpec((1, TM, d), lambda bi, i: (bi, jnp.maximum(i - 1, 0), 0)),
            pl.BlockSpec((1, CTX_LEN, d), lambda bi, i: (bi, 0, 0)),
            pl.BlockSpec((1, 3, d), lambda bi, i: (jnp.where(i == 0, b, bi), 0, 0)),
            const((1, d)),
            const((d, W_COLS)),
            pl.BlockSpec((TM, 256), lambda bi, i: (i, 0)),
            pl.BlockSpec((TM, 256), lambda bi, i: (i, 0)),
            const((8, 256)),
            const((256, 256)),
            const((128, 256)),
            const((1, 256)),
        ],
        out_specs=(tok(512), tok(256), tok(256), tok(256), tok(256), tok(256), tok(256), tok(256),
                   tok(256), tok(1024)),
        scratch_shapes=[pltpu.VMEM((TM, d), F32)],
        compiler_params=_cparams(("parallel", "arbitrary"), VMEM_LIMIT),
        name="inproj",
    )(x, ctx, mod, norm_w, w_perm, cos_t, sin_t, head_norms, seg_ones, wdec, bdec)


def _gla_kernel(af_ref, ab_ref, laf_ref, lab_ref, ones2_ref, trif_ref, trib_ref, pmf_ref, pmb_ref,
                qmask_ref, stmask_ref, of_ref, ob_ref, st_ref, rb_ref, *, nbatch, nreal):
    @pl.when(pl.program_id(0) == 0)
    def _():
        st_ref[...] = jnp.zeros_like(st_ref)

    @pl.when(pl.program_id(0) < nreal)
    def _():
        lane_head = lax.broadcasted_iota(jnp.int32, (GLA_CHUNK, GLA_HEADS * GLA_DV), 1) // GLA_DV
        shared = (qmask_ref[...], lane_head, stmask_ref[...])
        consts_f = (ones2_ref[...], trif_ref[...], pmf_ref[...]) + shared
        consts_b = (ones2_ref[...], trib_ref[...], pmb_ref[...]) + shared
        units = []
        for bi in range(nbatch):
            units.append(_gla_chunk(af_ref, laf_ref, of_ref, st_ref.at[0], rb_ref.at[0], bi, False, consts_f))
            units.append(_gla_chunk(ab_ref, lab_ref, ob_ref, st_ref.at[1], rb_ref.at[1], bi, True, consts_b))
        _interleave(units)

    @pl.when(pl.program_id(0) >= nreal)
    def _():
        of_ref[...] = jnp.zeros_like(of_ref)
        ob_ref[...] = jnp.zeros_like(ob_ref)


def _gla_chunk(a_ref, la_ref, o_ref, st_ref, rb_ref, bi, rev, consts):
    ones2, tri, pmask, qmask, lane_head, stmask = consts
    c, sb = GLA_CHUNK, GLA_SUB
    nsub = c // sb
    hk = GLA_HEADS * GLA_DK
    hv = GLA_HEADS * GLA_DV

    q = a_ref[bi, :, 0:hk]
    k = a_ref[bi, :, hk:2 * hk]
    v = a_ref[bi, :, 2 * hk:2 * hk + hv]
    la = la_ref[bi]

    la_hi = la.astype(BF16)
    la_lo = (la - la_hi.astype(F32)).astype(BF16)
    cum = _dot(tri, la_hi) + _dot(tri, la_lo)
    cumx = cum - la

    rb_ref[bi, 0] = cum
    rb_ref[bi, 1] = cumx
    rb_ref[bi, 2] = k
    rb_ref[bi, 3] = v[:, 0:hk]
    rb_ref[bi, 4] = v[:, hk:hv]
    yield

    def brow(plane, r, n=sb):
        return jnp.broadcast_to(rb_ref[bi, plane, r:r + 1, :], (n, hk))

    end = 0 if rev else c - 1

    def first_row(blk):
        return sb * blk + (sb - 1 if rev else 0)

    def blk_of(p):
        return nsub - 1 - p if rev else p

    cref_rows = jnp.concatenate([brow(1, first_row(blk)) for blk in range(nsub)], axis=0)

    st = st_ref[bi]
    qi = q * jnp.exp2(cum)
    inter = _dot_nt(qi.astype(BF16), st.astype(BF16))

    qd = q * jnp.exp2(cum - cref_rows)
    q_st = jnp.concatenate([qd] * GLA_HEADS, axis=0) * qmask
    kt, vt = [], []
    for p in range(1, nsub):
        lo, hi = (c - sb * p, c) if rev else (0, sb * p)
        kt.append(k[lo:hi] * jnp.exp2(brow(1, first_row(blk_of(p)), hi - lo) - cum[lo:hi]))
        vt.append(v[lo:hi])
    npad = pmask.shape[1] - sb * (nsub * (nsub - 1) // 2)
    kt.append(jnp.zeros((npad, hk), F32))
    vt.append(jnp.zeros((npad, hv), F32))
    kt = jnp.concatenate(kt, axis=0).astype(BF16)
    vt = jnp.concatenate(vt, axis=0).astype(BF16)
    sc = _dot_nt(q_st.astype(BF16), kt)

    t_loc = lax.broadcasted_iota(jnp.int32, (sb, hk), 0)
    valid = [(t_loc <= s) if rev else (t_loc >= s) for s in range(sb)]
    es = []
    for blk in range(nsub):
        r0 = sb * blk
        cb = cum[r0:r0 + sb]
        qb = q[r0:r0 + sb]
        for s in range(sb):
            diff = jnp.where(valid[s], cb - brow(0, r0 + s), -jnp.inf)
            es.append(jnp.exp2(diff) * (qb * brow(2, r0 + s)))
    e = jnp.concatenate(es, axis=0).astype(BF16)
    r = _dot(e, ones2)
    cum_end = rb_ref[bi, 0, end:end + 1, :]
    kd = k * jnp.exp2(cum_end - cum)
    kv = _dot_tn(v.astype(BF16), kd.astype(BF16))
    yield

    off_st = _dot(sc.astype(BF16) * pmask, vt)
    yield

    off = off_st[(GLA_HEADS - 1) * c:GLA_HEADS * c]
    for h in range(GLA_HEADS - 2, -1, -1):
        off = jnp.where(lane_head == h, off_st[h * c:(h + 1) * c], off)
    dgs = []
    for blk in range(nsub):
        acc = None
        for s in range(sb):
            row0 = (blk * sb + s) * sb
            vrow = jnp.concatenate([brow(3, blk * sb + s), brow(4, blk * sb + s)], axis=1)
            term = r[row0:row0 + sb] * vrow
            acc = term if acc is None else acc + term
        dgs.append(acc)
    o_ref[bi] = (inter + off + jnp.concatenate(dgs, axis=0)).astype(BF16)

    st_ref[bi] = st * jnp.exp2(cum_end) + kv * stmask


def _gla_consts(rev):
    c, sb = GLA_CHUNK, GLA_SUB
    nsub = c // sb
    t = np.arange(c)
    tri = (t[None, :] >= t[:, None]) if rev else (t[None, :] <= t[:, None])
    blk = t // sb
    p_of_t = (nsub - 1 - blk) if rev else blk
    col_p = np.concatenate([np.full(sb * p, p) for p in range(1, nsub)])
    col_p = np.concatenate([col_p, np.full(-col_p.size % 128, -1)])
    pmask = np.tile(p_of_t, GLA_HEADS)[:, None] == col_p[None, :]
    row_h = np.repeat(np.arange(GLA_HEADS), c)
    qmask = row_h[:, None] == (np.arange(GLA_HEADS * GLA_DK) // GLA_DK)[None, :]
    stmask = (np.arange(GLA_HEADS * GLA_DV) // GLA_DV)[:, None] == (np.arange(GLA_HEADS * GLA_DK) // GLA_DK)[None, :]
    f = lambda m: jnp.asarray(m, F32)
    return f(tri).astype(BF16), f(pmask).astype(BF16), f(qmask), f(stmask)


def _gla(a, la, ones2):
    b, t_all, _ = a.shape
    nch = t_all // GLA_CHUNK
    nctx = CTX_LEN // GLA_CHUNK
    nskip = (CTX_PAD - CTX_LEN) // GLA_CHUNK
    tri_f, pm_f, qmask, stmask = _gla_consts(False)
    tri_b, pm_b, _, _ = _gla_consts(True)
    consts = (ones2, tri_f, tri_b, pm_f, pm_b, qmask, stmask)

    nreal = nch - nskip

    def fwd(j):
        return jnp.where(j < nctx, j, jnp.where(j < nreal, j + nskip, nctx + (j - nreal)))

    def bwd(j):
        return jnp.where(j < nctx, nctx - 1 - j, jnp.where(j < nreal, nch - 1 - (j - nctx), nctx + (j - nreal)))

    out = jax.ShapeDtypeStruct((b, t_all, 256), BF16)
    return pl.pallas_call(
        functools.partial(_gla_kernel, nbatch=b, nreal=nreal),
        out_shape=(out, out),
        grid=(nch,),
        in_specs=[pl.BlockSpec((b, GLA_CHUNK, 512), lambda j: (0, fwd(j), 0)),
                  pl.BlockSpec((b, GLA_CHUNK, 512), lambda j: (0, bwd(j), 0)),
                  pl.BlockSpec((b, GLA_CHUNK, 128), lambda j: (0, fwd(j), 0)),
                  pl.BlockSpec((b, GLA_CHUNK, 128), lambda j: (0, bwd(j), 1))]
                 + [pl.BlockSpec(t.shape, lambda j: (0, 0)) for t in consts],
        out_specs=(pl.BlockSpec((b, GLA_CHUNK, 256), lambda j: (0, fwd(j), 0)),
                   pl.BlockSpec((b, GLA_CHUNK, 256), lambda j: (0, bwd(j), 0))),
        scratch_shapes=[pltpu.VMEM((2, b, 256, 128), F32), pltpu.VMEM((2, b, 5, GLA_CHUNK, 128), F32)],
        compiler_params=_cparams(("arbitrary",)),
        name="gla",
    )(a, a, la, la, *consts)


def _fnet_chan_kernel(v_ref, wc_ref, ws_ref, o_ref):
    v = v_ref[0]
    o_ref[0, :, :] = _dot(v, wc_ref[...]).astype(BF16)
    o_ref[1, :, :] = _dot(v, ws_ref[...]).astype(BF16)


def _fnet_chan(bv, wc, ws, row_blk_off, length):
    b = bv.shape[0]
    tl = min(TM, length)
    return pl.pallas_call(
        _fnet_chan_kernel,
        out_shape=jax.ShapeDtypeStruct((2, length, b * 256), BF16),
        grid=(b, length // tl),
        in_specs=[pl.BlockSpec((1, tl, 256), lambda bi, i: (bi, i + row_blk_off, 0)),
                  pl.BlockSpec((256, 256), lambda bi, i: (0, 0)),
                  pl.BlockSpec((256, 256), lambda bi, i: (0, 0))],
        out_specs=pl.BlockSpec((2, tl, 256), lambda bi, i: (0, i, bi)),
        compiler_params=_cparams(("parallel", "arbitrary")),
        name="fnet_chan",
    )(bv, wc, ws)


def _fnet_pos_kernel(d_ref, y_ref, wf_ref, o_ref, acc_ref, *, nb):
    kk = pl.program_id(1)

    @pl.when(kk == 0)
    def _():
        acc_ref[...] = jnp.zeros_like(acc_ref)

    acc_ref[...] += _dot(d_ref[...], y_ref[...])

    @pl.when(kk == pl.num_programs(1) - 1)
    def _():
        wf = wf_ref[...]
        for bi in range(nb):
            o_ref[bi] = _dot(acc_ref[:, bi * 256:(bi + 1) * 256].astype(BF16), wf).astype(BF16)


def _fnet_pos(dmat, y2, wf, nb):
    length, k2 = dmat.shape
    tm = min(512, length)
    tk = min(1024, k2)
    return pl.pallas_call(
        functools.partial(_fnet_pos_kernel, nb=nb),
        out_shape=jax.ShapeDtypeStruct((nb, length, 256), BF16),
        grid=(length // tm, k2 // tk),
        in_specs=[pl.BlockSpec((tm, tk), lambda i, kk: (i, kk)),
                  pl.BlockSpec((tk, nb * 256), lambda i, kk: (kk, 0)),
                  pl.BlockSpec((256, 256), lambda i, kk: (0, 0))],
        out_specs=pl.BlockSpec((nb, tm, 256), lambda i, kk: (0, i, 0)),
        scratch_shapes=[pltpu.VMEM((tm, nb * 256), F32)],
        compiler_params=_cparams(("parallel", "arbitrary"), VMEM_LIMIT),
        name="fnet_pos",
    )(dmat, y2, wf)


def _dft_tables(length):
    j = jnp.arange(length, dtype=jnp.int32)
    jk = (j[:, None] * j[None, :]) % length
    ang = jk.astype(F32) * (2.0 * math.pi / length)
    s = 1.0 / math.sqrt(length)
    return jnp.concatenate([jnp.cos(ang) * s, -jnp.sin(ang) * s], axis=1).astype(BF16)


def _chan_tables():
    j = np.arange(FNET_GW)
    ang = 2.0 * np.pi * ((j[:, None] * j[None, :]) % FNET_GW) / FNET_GW
    s = 1.0 / math.sqrt(FNET_GW)
    eye = np.eye(BRANCH_W // FNET_GW)
    wc = np.kron(eye, np.cos(ang) * s)
    ws = np.kron(eye, np.sin(ang) * s)
    return jnp.asarray(wc, F32).astype(BF16), jnp.asarray(ws, F32).astype(BF16)


def _fnet(bv, wf, row_blk_off, length):
    b = bv.shape[0]
    wc, ws = _chan_tables()
    y = _fnet_chan(bv, wc, ws, row_blk_off, length)
    y2 = y.reshape(2 * length, b * 256)
    return _fnet_pos(_dft_tables(length), y2, wf, b)


FNET_UNROLL = 8


def _fnet_grid_kernel(v_ref, cg_ref, sg_ref, w1_ref, w2_ref, wf_ref, o_ref, z_ref, g_ref, f_ref, *, rows, row_off):
    length = rows * GRID_W
    v = v_ref[0, row_off:row_off + length, :]
    wf = wf_ref[...]
    zr = _dot(v, _dot(cg_ref[...], wf).astype(BF16))
    zi = _dot(v, _dot(sg_ref[...], wf).astype(BF16))
    z_ref[0] = zr[:, 0:128].reshape(rows, GRID_W, 128)
    z_ref[1] = zr[:, 128:256].reshape(rows, GRID_W, 128)
    z_ref[2] = zi[:, 0:128].reshape(rows, GRID_W, 128)
    z_ref[3] = zi[:, 128:256].reshape(rows, GRID_W, 128)

    n8 = rows * 8
    for w0 in range(0, GRID_W, 8):
        def slab(c):
            return z_ref[c, :, w0:w0 + 8, :].reshape(n8, 128)
        rhs = jnp.concatenate([jnp.concatenate([slab(0), slab(1)], axis=1),
                               jnp.concatenate([slab(2), slab(3)], axis=1)], axis=0).astype(BF16)
        g = _dot(w1_ref[...], rhs)
        g_ref[0, :, w0:w0 + 8, :] = g[0:n8, 0:128].reshape(rows, 8, 128)
        g_ref[1, :, w0:w0 + 8, :] = g[0:n8, 128:256].reshape(rows, 8, 128)
        g_ref[2, :, w0:w0 + 8, :] = g[n8:2 * n8, 0:128].reshape(rows, 8, 128)
        g_ref[3, :, w0:w0 + 8, :] = g[n8:2 * n8, 128:256].reshape(rows, 8, 128)

    def stage_w(ka, carry):
        r0 = pl.multiple_of(ka * GRID_W, GRID_W)
        rhs = jnp.concatenate([jnp.concatenate([g_ref[0, ka], g_ref[1, ka]], axis=1),
                               jnp.concatenate([g_ref[2, ka], g_ref[3, ka]], axis=1)], axis=0).astype(BF16)
        y = _dot(w2_ref[pl.ds(r0, GRID_W), :], rhs)
        f_ref[0, pl.ds(ka, GRID_W, stride=rows), :] = y[:, 0:128]
        f_ref[1, pl.ds(ka, GRID_W, stride=rows), :] = y[:, 128:256]
        return carry

    lax.fori_loop(0, rows, stage_w, 0, unroll=FNET_UNROLL)
    o_ref[0, :, 0:128] = f_ref[0].astype(BF16)
    o_ref[0, :, 128:256] = f_ref[1].astype(BF16)


def _fnet_grid_tables(rows):
    length = rows * GRID_W
    j = np.arange(rows)
    ang = 2.0 * np.pi * ((j[:, None] * j[None, :]) % rows) / rows
    c, s = np.cos(ang), np.sin(ang)
    w1 = np.kron(np.block([[c, -s], [-s, -c]]), np.eye(8))
    w1 = jnp.asarray(w1, F32).astype(BF16)
    ka = jnp.arange(rows, dtype=jnp.int32)[:, None, None]
    kb = jnp.arange(GRID_W, dtype=jnp.int32)[None, :, None]
    w = jnp.arange(GRID_W, dtype=jnp.int32)[None, None, :]
    ang2 = ((w * (ka + rows * kb)) % length).astype(F32) * (2.0 * math.pi / length)
    scale = 1.0 / math.sqrt(length)
    w2 = jnp.concatenate([jnp.cos(ang2) * scale, jnp.sin(ang2) * scale], axis=2)
    return w1, w2.reshape(length, 2 * GRID_W).astype(BF16)


def _fnet_grid(bv, wf, seq):
    b, t_all, _ = bv.shape
    rows = seq // GRID_W
    wc, ws = _chan_tables()
    w1, w2 = _fnet_grid_tables(rows)

    def const(shape):
        return pl.BlockSpec(shape, lambda bi: (0,) * len(shape))

    return pl.pallas_call(
        functools.partial(_fnet_grid_kernel, rows=rows, row_off=CTX_PAD),
        out_shape=jax.ShapeDtypeStruct((b, seq, 256), BF16),
        grid=(b,),
        in_specs=[pl.BlockSpec((1, t_all, 256), lambda bi: (bi, 0, 0)), const((256, 256)), const((256, 256)),
                  const((16 * rows, 16 * rows)), const((seq, 2 * GRID_W)), const((256, 256))],
        out_specs=pl.BlockSpec((1, seq, 256), lambda bi: (bi, 0, 0)),
        scratch_shapes=[pltpu.VMEM((4, rows, GRID_W, 128), F32), pltpu.VMEM((4, rows, GRID_W, 128), F32),
                        pltpu.VMEM((2, seq, 128), F32)],
        compiler_params=_cparams(("parallel",), VMEM_LIMIT),
        name="fnet_grid",
    )(bv, wc, ws, w1, w2, wf)


def _stack_heads(qg):
    lane = lax.broadcasted_iota(jnp.int32, qg.shape, 1)
    zero = jnp.zeros_like(qg)
    return jnp.concatenate([jnp.where(lane < HEAD_DIM, qg, zero),
                            jnp.where(lane >= HEAD_DIM, qg, zero)], axis=0)


def _unstack_heads(o, rows):
    lane = lax.broadcasted_iota(jnp.int32, (rows, 2 * HEAD_DIM), 1)
    return jnp.where(lane < HEAD_DIM, o[0:rows], o[rows:2 * rows])


def _swa_kernel(sink_ref, band_ref, q_ref, kp_ref, kc_ref, kn_ref, kx_ref, vp_ref, vc_ref, vn_ref, vx_ref,
                o_ref, *, nbatch):
    n = pl.program_id(0)
    nb = pl.num_programs(0)
    blk = SWA_BLOCK
    rows = 2 * blk
    col = lax.broadcasted_iota(jnp.int32, (1, 3 * blk), 1)
    col_lo = jnp.where(n > 0, 0, blk)
    col_hi = jnp.where(n < nb - 1, 3 * blk, 2 * blk)
    edge = jnp.where((col >= col_lo) & (col < col_hi), 0.0, -jnp.inf)
    bias = band_ref[...] + edge
    head_row = lax.broadcasted_iota(jnp.int32, (rows, 1), 0) < blk

    def unit(bi, g):
        ls = slice(128 * g, 128 * (g + 1))
        q_st = _stack_heads(q_ref[bi, :, ls])
        kw = jnp.concatenate([kp_ref[bi, :, ls], kc_ref[bi, :, ls], kn_ref[bi, :, ls]], axis=0)
        vw = jnp.concatenate([vp_ref[bi, :, ls], vc_ref[bi, :, ls], vn_ref[bi, :, ls]], axis=0)
        s_loc = _dot_nt(q_st, kw) + bias
        s_ctx = _dot_nt(q_st, kx_ref[bi, :, ls])
        yield
        sink =jnp.where(head_row, sink_ref[2 * g], sink_ref[2 * g + 1])
        m = jnp.maximum(jnp.maximum(jnp.max(s_loc, axis=-1, keepdims=True),
                                    jnp.max(s_ctx, axis=-1, keepdims=True)), sink)
        p_loc = jnp.exp2(s_loc - m)
        p_ctx = jnp.exp2(s_ctx - m)
        den = (jnp.sum(p_loc, axis=-1, keepdims=True) + jnp.sum(p_ctx, axis=-1, keepdims=True)
               + jnp.exp2(sink - m))
        o = _dot(p_loc.astype(BF16), vw) + _dot(p_ctx.astype(BF16), vx_ref[bi, :, ls])
        yield
        o_ref[bi, :, ls] = _unstack_heads(o / den, blk).astype(BF16)

    _interleave([unit(bi, g) for bi in range(nbatch) for g in range(2)], group=ATTN_GROUP)


def _swa_band():
    row = np.arange(2 * SWA_BLOCK)[:, None] % SWA_BLOCK
    col = np.arange(3 * SWA_BLOCK)[None, :]
    ok = np.abs(col - SWA_BLOCK - row) <= SWA_WINDOW
    return jnp.where(jnp.asarray(ok), 0.0, -jnp.inf).astype(F32)


def _swa(sink, cq, ck, cv, seq):
    b = cq.shape[0]
    nb = seq // SWA_BLOCK
    off = CTX_PAD // SWA_BLOCK

    def at(f):
        return pl.BlockSpec((b, SWA_BLOCK, 256), lambda n: (0, f(n) + off, 0))

    prev = at(lambda n: jnp.maximum(n - 1, 0))
    cur = at(lambda n: n)
    nxt = at(lambda n: jnp.minimum(n + 1, nb - 1))
    cx = pl.BlockSpec((b, CTX_LEN, 256), lambda n: (0, 0, 0))
    band = pl.BlockSpec((2 * SWA_BLOCK, 3 * SWA_BLOCK), lambda n: (0, 0))
    return pl.pallas_call(
        functools.partial(_swa_kernel, nbatch=b),
        out_shape=jax.ShapeDtypeStruct((b, seq, 256), BF16),
        grid=(nb,),
        in_specs=[pl.BlockSpec(memory_space=pltpu.SMEM), band, cur, prev, cur, nxt, cx, prev, cur, nxt, cx],
        out_specs=pl.BlockSpec((b, SWA_BLOCK, 256), lambda n: (0, n, 0)),
        compiler_params=_cparams(("arbitrary",)),
        name="swa",
    )(sink, _swa_band(), cq, ck, ck, ck, ck, cv, cv, cv, cv)


NA_ROWS_PER_STEP = 8


def _na_kernel(q_ref, k_ref, v_ref, bias_ref, o_ref, *, rows_total):
    kh = NA_KH
    nk = kh * GRID_W
    def unit(rr, g):
        r = pl.program_id(1) * NA_ROWS_PER_STEP + rr
        row_start = jnp.clip(r - kh // 2, 0, rows_total - kh)
        cls = r - row_start
        k0 = pl.multiple_of(CTX_PAD + row_start * GRID_W, GRID_W)
        qs = slice(GRID_W * rr, GRID_W * (rr + 1))
        ls = slice(128 * g, 128 * (g + 1))
        q_st = _stack_heads(q_ref[0, qs, ls])
        s_nb = _dot_nt(q_st, k_ref[0, pl.ds(k0, nk), ls]) + bias_ref[cls, g]
        s_ctx = _dot_nt(q_st, k_ref[0, 0:CTX_LEN, ls])
        yield
        m =jnp.maximum(jnp.max(s_nb, axis=-1, keepdims=True), jnp.max(s_ctx, axis=-1, keepdims=True))
        p_nb = jnp.exp2(s_nb - m)
        p_ctx = jnp.exp2(s_ctx - m)
        den = jnp.sum(p_nb, axis=-1, keepdims=True) + jnp.sum(p_ctx, axis=-1, keepdims=True)
        o = (_dot(p_nb.astype(BF16), v_ref[0, pl.ds(k0, nk), ls])
             + _dot(p_ctx.astype(BF16), v_ref[0, 0:CTX_LEN, ls]))
        yield
        o_ref[0, qs, ls] = _unstack_heads(o / den, GRID_W).astype(BF16)

    _interleave([unit(rr, g) for rr in range(NA_ROWS_PER_STEP) for g in range(2)], group=ATTN_GROUP)


def _bias_expand_kernel(rb_ref, oh_ref, o_ref):
    x = rb_ref[...]
    oh = oh_ref[...]
    hi = x.astype(BF16)
    r1 = x - hi.astype(F32)
    mid = r1.astype(BF16)
    lo = (r1 - mid.astype(F32)).astype(BF16)
    o_ref[...] = (_dot(hi, oh) + _dot(mid, oh) + _dot(lo, oh)) * LOG2E


def _na_bias_table(rel_bias):
    ndy, ndx = 2 * NA_KH - 1, 2 * NA_KW - 1
    cq = np.arange(GRID_W)
    col_start = np.clip(cq - NA_KW // 2, 0, GRID_W - NA_KW)
    col_ok = (cq[None, :] >= col_start[:, None]) & (cq[None, :] < col_start[:, None] + NA_KW)
    qi = jnp.arange(GRID_W, dtype=jnp.int32)
    dx = jnp.clip(qi[None, :] - qi[:, None], -(NA_KW - 1), NA_KW - 1) + (NA_KW - 1)
    onehot = (jnp.arange(128, dtype=jnp.int32)[:, None] == dx.reshape(1, GRID_W * GRID_W)).astype(BF16)
    rb = jnp.zeros((64, 128), F32).at[0:NA_HEADS * ndy, 0:ndx].set(rel_bias.reshape(NA_HEADS * ndy, ndx))
    p = pl.pallas_call(
        _bias_expand_kernel,
        out_shape=jax.ShapeDtypeStruct((64, GRID_W * GRID_W), F32),
        name="na_bias_expand",
    )(rb, onehot)
    p = p[0:NA_HEADS * ndy].reshape(NA_HEADS, ndy, GRID_W, GRID_W)
    p = jnp.where(col_ok[None, None], p, -jnp.inf)
    tabs = jnp.stack([p[:, NA_KH - 1 - c:2 * NA_KH - 1 - c] for c in range(NA_KH)])
    tabs = tabs.transpose(0, 1, 3, 2, 4)
    return tabs.reshape(NA_KH, NA_HEADS // 2, 2 * GRID_W, NA_KH * GRID_W)


def _na(dq, dk, dv, bias_tab, seq):
    b, t_all, _ = dq.shape
    rows_total = seq // GRID_W
    qrows = NA_ROWS_PER_STEP * GRID_W
    assert CTX_PAD % qrows == 0 and rows_total % NA_ROWS_PER_STEP == 0
    off = CTX_PAD // qrows
    full = pl.BlockSpec((1, t_all, 256), lambda bi, r: (bi, 0, 0))
    return pl.pallas_call(
        functools.partial(_na_kernel, rows_total=rows_total),
        out_shape=jax.ShapeDtypeStruct((b, seq, 256), BF16),
        grid=(b, rows_total // NA_ROWS_PER_STEP),
        in_specs=[pl.BlockSpec((1, qrows, 256), lambda bi, r: (bi, r + off, 0)), full, full,
                  pl.BlockSpec(bias_tab.shape, lambda bi, r: (0, 0, 0, 0))],
        out_specs=pl.BlockSpec((1, qrows, 256), lambda bi, r: (bi, r, 0)),
        compiler_params=_cparams(("parallel", "arbitrary"), VMEM_LIMIT),
        name="na",
    )(dq, dk, dv, bias_tab)


def _ctx_attn_kernel(sink_ref, q_ref, k_ref, v_ref, o_ref, *, has_sink):
    n = CTX_LEN
    head_row = lax.broadcasted_iota(jnp.int32, (2 * n, 1), 0) < n
    for g in range(2):
        ls = slice(128 * g, 128 * (g + 1))
        q_st = _stack_heads(q_ref[0, :, ls])
        s = _dot_nt(q_st, k_ref[0, :, ls])
        m = jnp.max(s, axis=-1, keepdims=True)
        if has_sink:
            sink = jnp.where(head_row, sink_ref[2 * g], sink_ref[2 * g + 1])
            m = jnp.maximum(m, sink)
        p = jnp.exp2(s - m)
        den = jnp.sum(p, axis=-1, keepdims=True)
        if has_sink:
            den = den + jnp.exp2(sink - m)
        o = _dot(p.astype(BF16), v_ref[0, :, ls])
        o_ref[0, :, ls] = _unstack_heads(o / den, n).astype(BF16)


def _ctx_attn(sink, q, k, v, has_sink):
    b = q.shape[0]
    blk = pl.BlockSpec((1, CTX_LEN, 256), lambda bi: (bi, 0, 0))
    return pl.pallas_call(
        functools.partial(_ctx_attn_kernel, has_sink=has_sink),
        out_shape=jax.ShapeDtypeStruct((b, CTX_LEN, 256), BF16),
        grid=(b,),
        in_specs=[pl.BlockSpec(memory_space=pltpu.SMEM), blk, blk, blk],
        out_specs=blk,
        compiler_params=_cparams(("parallel",)),
        name="ctx_attn_sink" if has_sink else "ctx_attn",
    )(sink, q, k, v)


def _outproj_kernel(x_ref, of_ref, ob_ref, fz_ref, co_ref, do_ref, g_ref, mod_ref, gn_ref, seg_ref,
                    w_ref, o_ref):
    a = of_ref[0].astype(F32) + ob_ref[0].astype(F32)
    ms = _dot((a * a).astype(BF16), seg_ref[...]) * (1.0 / GLA_DV)
    a = a * lax.rsqrt(ms + EPS) * gn_ref[...]
    g = g_ref[0].astype(F32)
    mixed = jnp.concatenate([a * g[:, 0:256], fz_ref[0].astype(F32) * g[:, 256:512],
                             co_ref[0].astype(F32) * g[:, 512:768], do_ref[0].astype(F32) * g[:, 768:1024]],
                            axis=1)
    y = _dot(mixed.astype(BF16), w_ref[...])
    o_ref[0] = x_ref[0] + mod_ref[0, 2:3, :] * y


def _outproj(x, o_f, o_b, fz, c_o, d_o, g, mod, gla_norm, seg_ones, w_out, is_ctx):
    b, rows, d = x.shape
    tm = CTX_LEN if is_ctx else TM
    nblk = rows // tm
    comb = 0 if is_ctx else CTX_PAD // tm
    nbatch = b

    def own(width):
        return pl.BlockSpec((1, tm, width), lambda bi, i: (bi, i, 0))

    def combined(width):
        return pl.BlockSpec((1, tm, width), lambda bi, i: (bi, i + comb, 0))

    def const(shape):
        return pl.BlockSpec(shape, lambda bi, i: (0,) * len(shape))

    mod_spec = pl.BlockSpec((1, 3, d), (lambda bi, i: (nbatch, 0, 0)) if is_ctx else (lambda bi, i: (bi, 0, 0)))
    return pl.pallas_call(
        _outproj_kernel,
        out_shape=jax.ShapeDtypeStruct((b, rows, d), F32),
        grid=(b, nblk),
        in_specs=[own(d), combined(256), combined(256), own(256), own(256), own(256), combined(1024),
                  mod_spec, const((1, 256)), const((256, 256)), const((d, d))],
        out_specs=own(d),
        compiler_params=_cparams(("parallel", "arbitrary"), VMEM_LIMIT),
        name="outproj_ctx" if is_ctx else "outproj",
    )(x, o_f, o_b, fz, c_o, d_o, g, mod, gla_norm, seg_ones, w_out)


_W_IN_MOVES = (
    (C_A, 0, 512), (C_BV, 800, 256), (C_CQ, 1312, 256),
    (C_CK, 1568, 64), (C_CK + 64, 1568, 64), (C_CK + 128, 1632, 64), (C_CK + 192, 1632, 64),
    (C_CV, 1696, 64), (C_CV + 64, 1696, 64), (C_CV + 128, 1760, 64), (C_CV + 192, 1760, 64),
    (C_DQ, 2080, 256), (C_DK, 2336, 256), (C_DV, 2592, 256),
    (C_G, 512, 256), (C_G + 256, 1056, 256), (C_G + 512, 1824, 256), (C_G + 768, 2848, 256),
    (C_LR, 768, 2 * GLA_RANK),
)


def _w_in_prep_kernel(w_ref, o_ref):
    for dst, src, width in _W_IN_MOVES:
        o_ref[0, :, dst:dst + width] = w_ref[0, :, src:src + width].astype(BF16)
    pad0 = C_LR + 2 * GLA_RANK
    o_ref[0, :, pad0:W_COLS] = jnp.zeros((o_ref.shape[1], W_COLS - pad0), BF16)


def _permute_w_in(w_all):
    depth, d, n = w_all.shape
    rows = 256
    return pl.pallas_call(
        _w_in_prep_kernel,
        out_shape=jax.ShapeDtypeStruct((depth, d, W_COLS), BF16),
        grid=(depth, d // rows),
        in_specs=[pl.BlockSpec((1, rows, n), lambda l, i: (l, i, 0))],
        out_specs=pl.BlockSpec((1, rows, W_COLS), lambda l, i: (l, i, 0)),
        compiler_params=_cparams(("parallel", "parallel")),
        name="w_in_prep",
    )(w_all)


def _rope_tables(seq):
    t = np.arange(seq)
    axis_dim = HEAD_DIM // 2
    inv = ROPE_BASE ** (-np.arange(0, axis_dim, 2, dtype=np.float64) / axis_dim)
    row = (t // GRID_W).astype(np.float64)
    col = (t % GRID_W).astype(np.float64)
    ang = np.concatenate([row[:, None] * inv, row[:, None] * inv, col[:, None] * inv, col[:, None] * inv], axis=1)
    cos = np.cos(ang)
    sign = np.concatenate([-np.ones(16), np.ones(16), -np.ones(16), np.ones(16)])
    sin = np.sin(ang) * sign
    cos = np.concatenate([np.ones((CTX_PAD, HEAD_DIM)), cos], axis=0)
    sin = np.concatenate([np.zeros((CTX_PAD, HEAD_DIM)), sin], axis=0)
    return (jnp.asarray(np.tile(cos, (1, 4)), F32), jnp.asarray(np.tile(sin, (1, 4)), F32))


def _seg_ones(width, seg):
    i = np.arange(width)
    return jnp.asarray((i[:, None] // seg) == (i[None, :] // seg), BF16)


def _gla_ones2():
    r = np.arange(GLA_HEADS * GLA_DK) // GLA_DK
    c = np.arange(GLA_HEADS * GLA_DV) // GLA_DV
    return jnp.asarray(r[:, None] == c[None, :], BF16)


def _layer(x, ctx, mod, need_ctx, tables, norm_w, w_in_perm, w_out, gla_dec_w, gla_dec_b,
           gla_out_norm, fnet_w, swa_q_norm, swa_k_norm, swa_sink, na_q_norm, na_k_norm, na_rel_bias):
    b, seq, d = x.shape
    cos_t, sin_t, seg64, ones2 = tables

    qs = HEAD_DIM ** -0.5 * LOG2E
    swa_sink = swa_sink * LOG2E
    head_norms = jnp.stack([jnp.tile(swa_q_norm, 4) * qs, jnp.tile(swa_k_norm, 4),
                            jnp.tile(na_q_norm, 4) * qs, jnp.tile(na_k_norm, 4)])
    head_norms = jnp.concatenate([head_norms, jnp.zeros((4, 256), F32)], axis=0)
    wdec = jnp.zeros((128, 256), F32)
    wdec = wdec.at[0:GLA_RANK, 0:128].set(gla_dec_w[0]).at[GLA_RANK:2 * GLA_RANK, 128:256].set(gla_dec_w[1])
    bdec = gla_dec_b.reshape(1, 256)

    a, la, bv, cq, ck, cv, dq, dk, dv, g = _inproj(
        x, ctx, mod, norm_w[None, :], w_in_perm, cos_t, sin_t, head_norms, seg64,
        wdec.astype(BF16), bdec)

    o_f, o_b = _gla(a, la, ones2)
    wf = fnet_w.astype(BF16)
    fz = _fnet_grid(bv, wf, seq)
    c_o = _swa(swa_sink, cq, ck, cv, seq)
    d_o = _na(dq, dk, dv, _na_bias_table(na_rel_bias), seq)

    gn = jnp.tile(gla_out_norm, 4)[None, :]
    wo = w_out.astype(BF16)
    x_new = _outproj(x, o_f, o_b, fz, c_o, d_o, g, mod, gn, seg64, wo, is_ctx=False)
    ctx_new = None
    if need_ctx:
        fzc = _fnet(bv, wf, 0, CTX_LEN)
        c_oc = _ctx_attn(swa_sink, cq, ck, cv, has_sink=True)
        d_oc = _ctx_attn(swa_sink, dq, dk, dv, has_sink=False)
        ctx_new = _outproj(ctx, o_f, o_b, fzc, c_oc, d_oc, g, mod, gn, seg64, wo, is_ctx=True)
    return x_new, ctx_new


def kernel(x, c, ctx, c_ctx, norm_w, ada_w, ada_b, w_in, w_out, gla_dec_w, gla_dec_b, gla_out_norm,
           fnet_w, swa_q_norm, swa_k_norm, swa_sink, na_q_norm, na_k_norm, na_rel_bias):
    b, seq, d = x.shape
    depth = norm_w.shape[0]
    assert ctx.shape[1] == CTX_LEN and seq % TM == 0 and seq % (NA_KH * GRID_W) == 0
    tables = _rope_tables(seq) + (_seg_ones(256, HEAD_DIM), _gla_ones2())
    cc = jnp.concatenate([c, c_ctx[None, :], jnp.zeros((8 - (b + 1) % 8, d), F32)], axis=0)
    mod = _ada(cc, ada_w, ada_b[:, None, :])
    w_in_perm = _permute_w_in(w_in)
    for i in range(depth):
        x, ctx = _layer(x, ctx, mod[i], i < depth - 1, tables, norm_w[i], w_in_perm[i], w_out[i],
                        gla_dec_w[i], gla_dec_b[i], gla_out_norm[i], fnet_w[i], swa_q_norm[i],
                        swa_k_norm[i], swa_sink[i], na_q_norm[i], na_k_norm[i], na_rel_bias[i])
    return x
```

```python
import functools
import math

import numpy as np
import jax
import jax.numpy as jnp
from jax import lax
from jax.experimental import pallas as pl
from jax.experimental.pallas import tpu as pltpu

F32 = jnp.float32
BF16 = jnp.bfloat16

GRID_W = 64
CTX_LEN = 256
HEAD_DIM = 64
BRANCH_W = 256
EPS = 1e-6
ROPE_BASE = 10000.0
GLA_HEADS = 4
GLA_DK = 32
GLA_DV = 64
GLA_RANK = 16
GLA_TAU = 16.0
GLA_CHUNK = 64
GLA_SUB = 8
LOG2E = 1.4426950408889634
FNET_GW = 64
SWA_BLOCK = 128
SWA_WINDOW = 128
NA_KH = 8
NA_KW = 16
NA_HEADS = 4

TM = 512
CTX_PAD = TM
VMEM_LIMIT = 48 * 1024 * 1024

C_A = 0
C_BV = 512
C_CQ = 768
C_CK = 1024
C_CV = 1280
C_DQ = 1536
C_DK = 1792
C_DV = 2048
C_G = 2304
C_LR = 3328
W_COLS = 3456


def _dot(a, b):
    return jnp.dot(a, b, preferred_element_type=F32)


def _dot_nt(a, b):
    return lax.dot_general(a, b, (((1,), (1,)), ((), ())), preferred_element_type=F32)


def _dot_tn(a, b):
    return lax.dot_general(a, b, (((0,), (0,)), ((), ())), preferred_element_type=F32)


INPROJ_ROW_SPLIT = 2
INPROJ_GROUP = 2
ATTN_GROUP = 4
_DONE = object()


def _interleave(units, group=None):
    group = group or len(units)
    for i in range(0, len(units), group):
        live = units[i:i + group]
        while live:
            live = [u for u in live if next(u, _DONE) is not _DONE]


def _skew(units, depth):
    started = []
    for u in units:
        next(u)
        started.append(u)
        if len(started) > depth:
            next(started.pop(0), _DONE)
    for u in started:
        next(u, _DONE)


def _cparams(sem, vmem=None):
    return pltpu.CompilerParams(dimension_semantics=sem, vmem_limit_bytes=vmem)


def _ada_kernel(c_ref, w_ref, b_ref, o_ref):
    c = c_ref[...]
    s = c * (1.0 / (1.0 + jnp.exp(-c)))
    d = c.shape[1]
    for j in range(3):
        cols = slice(j * d, (j + 1) * d)
        o_ref[0, :, j, :] = _dot(s.astype(BF16), w_ref[0, :, cols].astype(BF16)) + b_ref[0, :, cols]


def _ada(cc, ada_w, ada_b):
    r, d = cc.shape
    depth, _, n = ada_w.shape
    return pl.pallas_call(
        _ada_kernel,
        out_shape=jax.ShapeDtypeStruct((depth, r, 3, d), F32),
        grid=(depth,),
        in_specs=[pl.BlockSpec((r, d), lambda l: (0, 0)),
                  pl.BlockSpec((1, d, n), lambda l: (l, 0, 0)),
                  pl.BlockSpec((1, 1, n), lambda l: (l, 0, 0))],
        out_specs=pl.BlockSpec((1, r, 3, d), lambda l: (l, 0, 0, 0)),
        compiler_params=_cparams(("parallel",), VMEM_LIMIT),
        name="ada",
    )(cc, ada_w, ada_b)


def _head_rms(x, seg_ones, w):
    ms = _dot((x * x).astype(BF16), seg_ones) * (1.0 / HEAD_DIM)
    return x * lax.rsqrt(ms + EPS) * w


def _rope(y, cos, sin_signed):
    lane = lax.broadcasted_iota(jnp.int32, y.shape, 1)
    first_half = (lane % 32) < 16
    n = y.shape[1]
    swapped = jnp.where(first_half, pltpu.roll(y, n - 16, 1), pltpu.roll(y, 16, 1))
    return y * cos + swapped * sin_signed


def _inproj_kernel(x_ref, ctx_ref, mod_ref, nw_ref, w_ref, cos_ref, sin_ref, hn_ref, seg_ref,
                   wdec_ref, bdec_ref,
                   a_ref, la_ref, bv_ref, cq_ref, ck_ref, cv_ref, dq_ref, dk_ref, dv_ref, g_ref,
                   xs_ref):
    i = pl.program_id(1)

    @pl.when(i == 0)
    def _():
        xs_ref[0:CTX_LEN, :] = ctx_ref[0]
        xs_ref[CTX_LEN:, :] = jnp.zeros((TM - CTX_LEN, xs_ref.shape[1]), F32)

    @pl.when(i > 0)
    def _():
        xs_ref[...] = x_ref[0]

    seg = seg_ref[...]
    half = TM // INPROJ_ROW_SPLIT
    hb = [None] * INPROJ_ROW_SPLIT

    def prologue(part):
        rs = slice(part * half, (part + 1) * half)
        x = xs_ref[rs, :]
        ms = jnp.mean(x * x, axis=-1, keepdims=True)
        y = x * lax.rsqrt(ms + EPS) * nw_ref[...]
        hb[part] = (y * (1.0 + mod_ref[0, 1:2, :]) + mod_ref[0, 0:1, :]).astype(BF16)
        yield

    def gla_qkv(part, rs):
        a = _dot(hb[part], w_ref[:, C_A:C_A + 512])
        yield
        a_ref[0, rs, 0:128] = a[:, 0:128] * (GLA_DK ** -0.5)
        a_ref[0, rs, 128:512] = a[:, 128:512]

    def plain(part, rs, c0, ref):
        r = _dot(hb[part], w_ref[:, c0:c0 + 256])
        yield
        ref[0, rs, :] = r.astype(BF16)

    def normed(part, rs, c0, ref, row, rope):
        r = _dot(hb[part], w_ref[:, c0:c0 + 256])
        yield
        y = _head_rms(r, seg, hn_ref[row:row + 1, :])
        ref[0, rs, :] = (_rope(y, cos_ref[rs, :], sin_ref[rs, :]) if rope else y).astype(BF16)

    def gates(part, rs, j):
        g = _dot(hb[part], w_ref[:, C_G + 256 * j:C_G + 256 * (j + 1)])
        yield
        g_ref[0, rs, 256 * j:256 * (j + 1)] = (g * (1.0 / (1.0 + jnp.exp(-g)))).astype(BF16)

    def decay(part, rs):
        lr = _dot(hb[part], w_ref[:, C_LR:C_LR + 128])
        yield
        z = _dot(lr.astype(BF16), wdec_ref[...]) + bdec_ref[...]
        la_ref[0, rs, :] = (jnp.minimum(z, 0.0) - jnp.log(1.0 + jnp.exp(-jnp.abs(z)))) * (LOG2E / GLA_TAU)

    def groups(part):
        rs = slice(part * half, (part + 1) * half)
        return [decay(part, rs), normed(part, rs, C_CQ, cq_ref, 0, True), normed(part, rs, C_CK, ck_ref, 1, True),
                normed(part, rs, C_DQ, dq_ref, 2, False), normed(part, rs, C_DK, dk_ref, 3, False),
                gates(part, rs, 0), gates(part, rs, 1), gates(part, rs, 2), gates(part, rs, 3),
                gla_qkv(part, rs), plain(part, rs, C_BV, bv_ref), plain(part, rs, C_CV, cv_ref),
                plain(part, rs, C_DV, dv_ref)]

    units = [prologue(0)]
    for part in range(INPROJ_ROW_SPLIT):
        g = groups(part)
        if part + 1 < INPROJ_ROW_SPLIT:
            g.insert(INPROJ_GROUP, prologue(part + 1))
        units += g
    _skew(units, INPROJ_GROUP)


def _inproj(x, ctx, mod, norm_w, w_perm, layer, cos_t, sin_t, head_norms, seg_ones, wdec, bdec):
    b, seq, d = x.shape
    t_all = CTX_PAD + seq
    nblk = t_all // TM

    def tok(width):
        return pl.BlockSpec((1, TM, width), lambda bi, i: (bi, i, 0))

    def const(shape):
        return pl.BlockSpec(shape, lambda bi, i: (0,) * len(shape))

    def out(width, dtype):
        return jax.ShapeDtypeStruct((b, t_all, width), dtype)

    return pl.pallas_call(
        _inproj_kernel,
        out_shape=(out(512, F32), out(256, F32), out(256, BF16), out(256, BF16), out(256, BF16),
                   out(256, BF16), out(256, BF16), out(256, BF16), out(256, BF16), out(1024, BF16)),
        grid=(b, nblk),
        in_specs=[
            pl.BlockSpec((1, TM, d), lambda bi, i: (bi, jnp.maximum(i - 1, 0), 0)),
            pl.BlockSpec((1, CTX_LEN, d), lambda bi, i: (bi, 0, 0)),
            pl.BlockSpec((1, 3, d), lambda bi, i: (jnp.where(i == 0, b, bi), 0, 0)),
            const((1, d)),
            pl.BlockSpec((None, d, W_COLS), lambda bi, i: (layer, 0, 0)),
            pl.BlockSpec((TM, 256), lambda bi, i: (i, 0)),
            pl.BlockSpec((TM, 256), lambda bi, i: (i, 0)),
            const((8, 256)),
            const((256, 256)),
            const((128, 256)),
            const((1, 256)),
        ],
        out_specs=(tok(512), tok(256), tok(256), tok(256), tok(256), tok(256), tok(256), tok(256),
                   tok(256), tok(1024)),
        scratch_shapes=[pltpu.VMEM((TM, d), F32)],
        compiler_params=_cparams(("parallel", "arbitrary"), VMEM_LIMIT),
        name="inproj",
    )(x, ctx, mod, norm_w, w_perm, cos_t, sin_t, head_norms, seg_ones, wdec, bdec)


def _gla_kernel(af_ref, ab_ref, laf_ref, lab_ref, ones2_ref, trif_ref, trib_ref, pmf_ref, pmb_ref,
                qmask_ref, stmask_ref, of_ref, ob_ref, st_ref, rb_ref, *, nbatch, nreal):
    @pl.when(pl.program_id(0) == 0)
    def _():
        st_ref[...] = jnp.zeros_like(st_ref)

    @pl.when(pl.program_id(0) < nreal)
    def _():
        lane_head = lax.broadcasted_iota(jnp.int32, (GLA_CHUNK, GLA_HEADS * GLA_DV), 1) // GLA_DV
        shared = (qmask_ref[...], lane_head, stmask_ref[...])
        consts_f = (ones2_ref[...], trif_ref[...], pmf_ref[...]) + shared
        consts_b = (ones2_ref[...], trib_ref[...], pmb_ref[...]) + shared
        units = []
        for bi in range(nbatch):
            units.append(_gla_chunk(af_ref, laf_ref, of_ref, st_ref.at[0], rb_ref.at[0], bi, False, consts_f))
            units.append(_gla_chunk(ab_ref, lab_ref, ob_ref, st_ref.at[1], rb_ref.at[1], bi, True, consts_b))
        _interleave(units)

    @pl.when(pl.program_id(0) >= nreal)
    def _():
        of_ref[...] = jnp.zeros_like(of_ref)
        ob_ref[...] = jnp.zeros_like(ob_ref)


def _gla_chunk(a_ref, la_ref, o_ref, st_ref, rb_ref, bi, rev, consts):
    ones2, tri, pmask, qmask, lane_head, stmask = consts
    c, sb = GLA_CHUNK, GLA_SUB
    nsub = c // sb
    hk = GLA_HEADS * GLA_DK
    hv = GLA_HEADS * GLA_DV

    q = a_ref[bi, :, 0:hk]
    k = a_ref[bi, :, hk:2 * hk]
    v = a_ref[bi, :, 2 * hk:2 * hk + hv]
    la = la_ref[bi]

    la_hi = la.astype(BF16)
    la_lo = (la - la_hi.astype(F32)).astype(BF16)
    cum = _dot(tri, la_hi) + _dot(tri, la_lo)
    cumx = cum - la

    rb_ref[bi, 0] = cum
    rb_ref[bi, 1] = cumx
    rb_ref[bi, 2] = k
    rb_ref[bi, 3] = v[:, 0:hk]
    rb_ref[bi, 4] = v[:, hk:hv]
    yield

    def brow(plane, r, n=sb):
        return jnp.broadcast_to(rb_ref[bi, plane, r:r + 1, :], (n, hk))

    end = 0 if rev else c - 1

    def first_row(blk):
        return sb * blk + (sb - 1 if rev else 0)

    def blk_of(p):
        return nsub - 1 - p if rev else p

    cref_rows = jnp.concatenate([brow(1, first_row(blk)) for blk in range(nsub)], axis=0)

    st = st_ref[bi]
    qi = q * jnp.exp2(cum)
    inter = _dot_nt(qi.astype(BF16), st.astype(BF16))

    qd = q * jnp.exp2(cum - cref_rows)
    q_st = jnp.concatenate([qd] * GLA_HEADS, axis=0) * qmask
    kt, vt = [], []
    for p in range(1, nsub):
        lo, hi = (c - sb * p, c) if rev else (0, sb * p)
        kt.append(k[lo:hi] * jnp.exp2(brow(1, first_row(blk_of(p)), hi - lo) - cum[lo:hi]))
        vt.append(v[lo:hi])
    npad = pmask.shape[1] - sb * (nsub * (nsub - 1) // 2)
    kt.append(jnp.zeros((npad, hk), F32))
    vt.append(jnp.zeros((npad, hv), F32))
    kt = jnp.concatenate(kt, axis=0).astype(BF16)
    vt = jnp.concatenate(vt, axis=0).astype(BF16)
    sc = _dot_nt(q_st.astype(BF16), kt)

    t_loc = lax.broadcasted_iota(jnp.int32, (sb, hk), 0)
    valid = [(t_loc <= s) if rev else (t_loc >= s) for s in range(sb)]
    es = []
    for blk in range(nsub):
        r0 = sb * blk
        cb = cum[r0:r0 + sb]
        qb = q[r0:r0 + sb]
        for s in range(sb):
            diff = jnp.where(valid[s], cb - brow(0, r0 + s), -jnp.inf)
            es.append(jnp.exp2(diff) * (qb * brow(2, r0 + s)))
    e = jnp.concatenate(es, axis=0).astype(BF16)
    r = _dot(e, ones2)
    cum_end = rb_ref[bi, 0, end:end + 1, :]
    kd = k * jnp.exp2(cum_end - cum)
    kv = _dot_tn(v.astype(BF16), kd.astype(BF16))
    yield

    off_st = _dot(sc.astype(BF16) * pmask, vt)
    yield

    off = off_st[(GLA_HEADS - 1) * c:GLA_HEADS * c]
    for h in range(GLA_HEADS - 2, -1, -1):
        off = jnp.where(lane_head == h, off_st[h * c:(h + 1) * c], off)
    dgs = []
    for blk in range(nsub):
        acc = None
        for s in range(sb):
            row0 = (blk * sb + s) * sb
            vrow = jnp.concatenate([brow(3, blk * sb + s), brow(4, blk * sb + s)], axis=1)
            term = r[row0:row0 + sb] * vrow
            acc = term if acc is None else acc + term
        dgs.append(acc)
    o_ref[bi] = (inter + off + jnp.concatenate(dgs, axis=0)).astype(BF16)

    st_ref[bi] = st * jnp.exp2(cum_end) + kv * stmask


def _gla_consts(rev):
    c, sb = GLA_CHUNK, GLA_SUB
    nsub = c // sb
    t = np.arange(c)
    tri = (t[None, :] >= t[:, None]) if rev else (t[None, :] <= t[:, None])
    blk = t // sb
    p_of_t = (nsub - 1 - blk) if rev else blk
    col_p = np.concatenate([np.full(sb * p, p) for p in range(1, nsub)])
    col_p = np.concatenate([col_p, np.full(-col_p.size % 128, -1)])
    pmask = np.tile(p_of_t, GLA_HEADS)[:, None] == col_p[None, :]
    row_h = np.repeat(np.arange(GLA_HEADS), c)
    qmask = row_h[:, None] == (np.arange(GLA_HEADS * GLA_DK) // GLA_DK)[None, :]
    stmask = (np.arange(GLA_HEADS * GLA_DV) // GLA_DV)[:, None] == (np.arange(GLA_HEADS * GLA_DK) // GLA_DK)[None, :]
    f = lambda m: jnp.asarray(m, F32)
    return f(tri).astype(BF16), f(pmask).astype(BF16), f(qmask), f(stmask)


def _gla(a, la, ones2):
    b, t_all, _ = a.shape
    nch = t_all // GLA_CHUNK
    nctx = CTX_LEN // GLA_CHUNK
    nskip = (CTX_PAD - CTX_LEN) // GLA_CHUNK
    tri_f, pm_f, qmask, stmask = _gla_consts(False)
    tri_b, pm_b, _, _ = _gla_consts(True)
    consts = (ones2, tri_f, tri_b, pm_f, pm_b, qmask, stmask)

    nreal = nch - nskip

    def fwd(j):
        return jnp.where(j < nctx, j, jnp.where(j < nreal, j + nskip, nctx + (j - nreal)))

    def bwd(j):
        return jnp.where(j < nctx, nctx - 1 - j, jnp.where(j < nreal, nch - 1 - (j - nctx), nctx + (j - nreal)))

    out = jax.ShapeDtypeStruct((b, t_all, 256), BF16)
    return pl.pallas_call(
        functools.partial(_gla_kernel, nbatch=b, nreal=nreal),
        out_shape=(out, out),
        grid=(nch,),
        in_specs=[pl.BlockSpec((b, GLA_CHUNK, 512), lambda j: (0, fwd(j), 0)),
                  pl.BlockSpec((b, GLA_CHUNK, 512), lambda j: (0, bwd(j), 0)),
                  pl.BlockSpec((b, GLA_CHUNK, 128), lambda j: (0, fwd(j), 0)),
                  pl.BlockSpec((b, GLA_CHUNK, 128), lambda j: (0, bwd(j), 1))]
                 + [pl.BlockSpec(t.shape, lambda j: (0, 0)) for t in consts],
        out_specs=(pl.BlockSpec((b, GLA_CHUNK, 256), lambda j: (0, fwd(j), 0)),
                   pl.BlockSpec((b, GLA_CHUNK, 256), lambda j: (0, bwd(j), 0))),
        scratch_shapes=[pltpu.VMEM((2, b, 256, 128), F32), pltpu.VMEM((2, b, 5, GLA_CHUNK, 128), F32)],
        compiler_params=_cparams(("arbitrary",)),
        name="gla",
    )(a, a, la, la, *consts)


def _fnet_chan_kernel(v_ref, wc_ref, ws_ref, o_ref):
    v = v_ref[0]
    o_ref[0, :, :] = _dot(v, wc_ref[...]).astype(BF16)
    o_ref[1, :, :] = _dot(v, ws_ref[...]).astype(BF16)


def _fnet_chan(bv, wc, ws, row_blk_off, length):
    b = bv.shape[0]
    tl = min(TM, length)
    return pl.pallas_call(
        _fnet_chan_kernel,
        out_shape=jax.ShapeDtypeStruct((2, length, b * 256), BF16),
        grid=(b, length // tl),
        in_specs=[pl.BlockSpec((1, tl, 256), lambda bi, i: (bi, i + row_blk_off, 0)),
                  pl.BlockSpec((256, 256), lambda bi, i: (0, 0)),
                  pl.BlockSpec((256, 256), lambda bi, i: (0, 0))],
        out_specs=pl.BlockSpec((2, tl, 256), lambda bi, i: (0, i, bi)),
        compiler_params=_cparams(("parallel", "arbitrary")),
        name="fnet_chan",
    )(bv, wc, ws)


def _fnet_pos_kernel(d_ref, y_ref, wf_ref, o_ref, acc_ref, *, nb):
    kk = pl.program_id(1)

    @pl.when(kk == 0)
    def _():
        acc_ref[...] = jnp.zeros_like(acc_ref)

    acc_ref[...] += _dot(d_ref[...], y_ref[...])

    @pl.when(kk == pl.num_programs(1) - 1)
    def _():
        wf = wf_ref[...]
        for bi in range(nb):
            o_ref[bi] = _dot(acc_ref[:, bi * 256:(bi + 1) * 256].astype(BF16), wf).astype(BF16)


def _fnet_pos(dmat, y2, wf, nb):
    length, k2 = dmat.shape
    tm = min(512, length)
    tk = min(1024, k2)
    return pl.pallas_call(
        functools.partial(_fnet_pos_kernel, nb=nb),
        out_shape=jax.ShapeDtypeStruct((nb, length, 256), BF16),
        grid=(length // tm, k2 // tk),
        in_specs=[pl.BlockSpec((tm, tk), lambda i, kk: (i, kk)),
                  pl.BlockSpec((tk, nb * 256), lambda i, kk: (kk, 0)),
                  pl.BlockSpec((256, 256), lambda i, kk: (0, 0))],
        out_specs=pl.BlockSpec((nb, tm, 256), lambda i, kk: (0, i, 0)),
        scratch_shapes=[pltpu.VMEM((tm, nb * 256), F32)],
        compiler_params=_cparams(("parallel", "arbitrary"), VMEM_LIMIT),
        name="fnet_pos",
    )(dmat, y2, wf)


def _dft_tables(length):
    j = np.arange(length)
    ang = ((j[:, None] * j[None, :]) % length) * (2.0 * np.pi / length)
    s = 1.0 / math.sqrt(length)
    return jnp.asarray(np.concatenate([np.cos(ang) * s, -np.sin(ang) * s], axis=1), F32).astype(BF16)


def _chan_tables():
    j = np.arange(FNET_GW)
    ang = 2.0 * np.pi * ((j[:, None] * j[None, :]) % FNET_GW) / FNET_GW
    s = 1.0 / math.sqrt(FNET_GW)
    eye = np.eye(BRANCH_W // FNET_GW)
    wc = np.kron(eye, np.cos(ang) * s)
    ws = np.kron(eye, np.sin(ang) * s)
    return jnp.asarray(wc, F32).astype(BF16), jnp.asarray(ws, F32).astype(BF16)


def _fnet(bv, wf, row_blk_off, length):
    b = bv.shape[0]
    wc, ws = _chan_tables()
    y = _fnet_chan(bv, wc, ws, row_blk_off, length)
    y2 = y.reshape(2 * length, b * 256)
    return _fnet_pos(_dft_tables(length), y2, wf, b)


FNET_UNROLL = 8


def _fnet_grid_kernel(v_ref, cg_ref, sg_ref, w1_ref, w2_ref, wf_ref, o_ref, z_ref, g_ref, f_ref, *, rows, row_off):
    length = rows * GRID_W
    v = v_ref[0, row_off:row_off + length, :]
    wf = wf_ref[...]
    zr = _dot(v, _dot(cg_ref[...], wf).astype(BF16))
    zi = _dot(v, _dot(sg_ref[...], wf).astype(BF16))
    z_ref[0] = zr[:, 0:128].reshape(rows, GRID_W, 128)
    z_ref[1] = zr[:, 128:256].reshape(rows, GRID_W, 128)
    z_ref[2] = zi[:, 0:128].reshape(rows, GRID_W, 128)
    z_ref[3] = zi[:, 128:256].reshape(rows, GRID_W, 128)

    n8 = rows * 8
    for w0 in range(0, GRID_W, 8):
        def slab(c):
            return z_ref[c, :, w0:w0 + 8, :].reshape(n8, 128)
        rhs = jnp.concatenate([jnp.concatenate([slab(0), slab(1)], axis=1),
                               jnp.concatenate([slab(2), slab(3)], axis=1)], axis=0).astype(BF16)
        g = _dot(w1_ref[...], rhs)
        g_ref[0, :, w0:w0 + 8, :] = g[0:n8, 0:128].reshape(rows, 8, 128)
        g_ref[1, :, w0:w0 + 8, :] = g[0:n8, 128:256].reshape(rows, 8, 128)
        g_ref[2, :, w0:w0 + 8, :] = g[n8:2 * n8, 0:128].reshape(rows, 8, 128)
        g_ref[3, :, w0:w0 + 8, :] = g[n8:2 * n8, 128:256].reshape(rows, 8, 128)

    def stage_w(ka, carry):
        r0 = pl.multiple_of(ka * GRID_W, GRID_W)
        rhs = jnp.concatenate([jnp.concatenate([g_ref[0, ka], g_ref[1, ka]], axis=1),
                               jnp.concatenate([g_ref[2, ka], g_ref[3, ka]], axis=1)], axis=0).astype(BF16)
        y = _dot(w2_ref[pl.ds(r0, GRID_W), :], rhs)
        f_ref[0, pl.ds(ka, GRID_W, stride=rows), :] = y[:, 0:128]
        f_ref[1, pl.ds(ka, GRID_W, stride=rows), :] = y[:, 128:256]
        return carry

    lax.fori_loop(0, rows, stage_w, 0, unroll=FNET_UNROLL)
    o_ref[0, :, 0:128] = f_ref[0].astype(BF16)
    o_ref[0, :, 128:256] = f_ref[1].astype(BF16)


def _fnet_grid_tables(rows):
    length = rows * GRID_W
    j = np.arange(rows)
    ang = 2.0 * np.pi * ((j[:, None] * j[None, :]) % rows) / rows
    c, s = np.cos(ang), np.sin(ang)
    w1 = np.kron(np.block([[c, -s], [-s, -c]]), np.eye(8))
    w1 = jnp.asarray(w1, F32).astype(BF16)
    ka = np.arange(rows)[:, None, None]
    kb = np.arange(GRID_W)[None, :, None]
    w = np.arange(GRID_W)[None, None, :]
    ang2 = ((w * (ka + rows * kb)) % length) * (2.0 * np.pi / length)
    scale = 1.0 / math.sqrt(length)
    w2 = np.concatenate([np.cos(ang2) * scale, np.sin(ang2) * scale], axis=2)
    return w1, jnp.asarray(w2.reshape(length, 2 * GRID_W), F32).astype(BF16)


def _fnet_grid(bv, wf, seq):
    b, t_all, _ = bv.shape
    rows = seq // GRID_W
    wc, ws = _chan_tables()
    w1, w2 = _fnet_grid_tables(rows)

    def const(shape):
        return pl.BlockSpec(shape, lambda bi: (0,) * len(shape))

    return pl.pallas_call(
        functools.partial(_fnet_grid_kernel, rows=rows, row_off=CTX_PAD),
        out_shape=jax.ShapeDtypeStruct((b, seq, 256), BF16),
        grid=(b,),
        in_specs=[pl.BlockSpec((1, t_all, 256), lambda bi: (bi, 0, 0)), const((256, 256)), const((256, 256)),
                  const((16 * rows, 16 * rows)), const((seq, 2 * GRID_W)), const((256, 256))],
        out_specs=pl.BlockSpec((1, seq, 256), lambda bi: (bi, 0, 0)),
        scratch_shapes=[pltpu.VMEM((4, rows, GRID_W, 128), F32), pltpu.VMEM((4, rows, GRID_W, 128), F32),
                        pltpu.VMEM((2, seq, 128), F32)],
        compiler_params=_cparams(("parallel",), VMEM_LIMIT),
        name="fnet_grid",
    )(bv, wc, ws, w1, w2, wf)


def _stack_heads(qg):
    lane = lax.broadcasted_iota(jnp.int32, qg.shape, 1)
    zero = jnp.zeros_like(qg)
    return jnp.concatenate([jnp.where(lane < HEAD_DIM, qg, zero),
                            jnp.where(lane >= HEAD_DIM, qg, zero)], axis=0)


def _unstack_heads(o, rows):
    lane = lax.broadcasted_iota(jnp.int32, (rows, 2 * HEAD_DIM), 1)
    return jnp.where(lane < HEAD_DIM, o[0:rows], o[rows:2 * rows])


def _swa_kernel(sink_ref, band_ref, q_ref, kp_ref, kc_ref, kn_ref, kx_ref, vp_ref, vc_ref, vn_ref, vx_ref,
                o_ref, *, nbatch):
    n = pl.program_id(0)
    nb = pl.num_programs(0)
    blk = SWA_BLOCK
    rows = 2 * blk
    col = lax.broadcasted_iota(jnp.int32, (1, 3 * blk), 1)
    col_lo = jnp.where(n > 0, 0, blk)
    col_hi = jnp.where(n < nb - 1, 3 * blk, 2 * blk)
    edge = jnp.where((col >= col_lo) & (col < col_hi), 0.0, -jnp.inf)
    bias = band_ref[...] + edge
    head_row = lax.broadcasted_iota(jnp.int32, (rows, 1), 0) < blk

    def unit(bi, g):
        ls = slice(128 * g, 128 * (g + 1))
        q_st = _stack_heads(q_ref[bi, :, ls])
        kw = jnp.concatenate([kp_ref[bi, :, ls], kc_ref[bi, :, ls], kn_ref[bi, :, ls]], axis=0)
        vw = jnp.concatenate([vp_ref[bi, :, ls], vc_ref[bi, :, ls], vn_ref[bi, :, ls]], axis=0)
        s_loc = _dot_nt(q_st, kw) + bias
        s_ctx = _dot_nt(q_st, kx_ref[bi, :, ls])
        yield
        sink =jnp.where(head_row, sink_ref[2 * g], sink_ref[2 * g + 1])
        m = jnp.maximum(jnp.maximum(jnp.max(s_loc, axis=-1, keepdims=True),
                                    jnp.max(s_ctx, axis=-1, keepdims=True)), sink)
        p_loc = jnp.exp2(s_loc - m)
        p_ctx = jnp.exp2(s_ctx - m)
        den = (jnp.sum(p_loc, axis=-1, keepdims=True) + jnp.sum(p_ctx, axis=-1, keepdims=True)
               + jnp.exp2(sink - m))
        o = _dot(p_loc.astype(BF16), vw) + _dot(p_ctx.astype(BF16), vx_ref[bi, :, ls])
        yield
        o_ref[bi, :, ls] = _unstack_heads(o / den, blk).astype(BF16)

    _interleave([unit(bi, g) for bi in range(nbatch) for g in range(2)], group=ATTN_GROUP)


def _swa_band():
    row = np.arange(2 * SWA_BLOCK)[:, None] % SWA_BLOCK
    col = np.arange(3 * SWA_BLOCK)[None, :]
    ok = np.abs(col - SWA_BLOCK - row) <= SWA_WINDOW
    return jnp.where(jnp.asarray(ok), 0.0, -jnp.inf).astype(F32)


def _swa(sink, cq, ck, cv, seq):
    b = cq.shape[0]
    nb = seq // SWA_BLOCK
    off = CTX_PAD // SWA_BLOCK

    def at(f):
        return pl.BlockSpec((b, SWA_BLOCK, 256), lambda n: (0, f(n) + off, 0))

    prev = at(lambda n: jnp.maximum(n - 1, 0))
    cur = at(lambda n: n)
    nxt = at(lambda n: jnp.minimum(n + 1, nb - 1))
    cx = pl.BlockSpec((b, CTX_LEN, 256), lambda n: (0, 0, 0))
    band = pl.BlockSpec((2 * SWA_BLOCK, 3 * SWA_BLOCK), lambda n: (0, 0))
    return pl.pallas_call(
        functools.partial(_swa_kernel, nbatch=b),
        out_shape=jax.ShapeDtypeStruct((b, seq, 256), BF16),
        grid=(nb,),
        in_specs=[pl.BlockSpec(memory_space=pltpu.SMEM), band, cur, prev, cur, nxt, cx, prev, cur, nxt, cx],
        out_specs=pl.BlockSpec((b, SWA_BLOCK, 256), lambda n: (0, n, 0)),
        compiler_params=_cparams(("arbitrary",)),
        name="swa",
    )(sink, _swa_band(), cq, ck, ck, ck, ck, cv, cv, cv, cv)


NA_ROWS_PER_STEP = 8


def _na_kernel(q_ref, k_ref, v_ref, bias_ref, o_ref, *, rows_total):
    kh = NA_KH
    nk = kh * GRID_W
    def unit(rr, g):
        r = pl.program_id(1) * NA_ROWS_PER_STEP + rr
        row_start = jnp.clip(r - kh // 2, 0, rows_total - kh)
        cls = r - row_start
        k0 = pl.multiple_of(CTX_PAD + row_start * GRID_W, GRID_W)
        qs = slice(GRID_W * rr, GRID_W * (rr + 1))
        ls = slice(128 * g, 128 * (g + 1))
        q_st = _stack_heads(q_ref[0, qs, ls])
        s_nb = _dot_nt(q_st, k_ref[0, pl.ds(k0, nk), ls]) + bias_ref[cls, g]
        s_ctx = _dot_nt(q_st, k_ref[0, 0:CTX_LEN, ls])
        yield
        m =jnp.maximum(jnp.max(s_nb, axis=-1, keepdims=True), jnp.max(s_ctx, axis=-1, keepdims=True))
        p_nb = jnp.exp2(s_nb - m)
        p_ctx = jnp.exp2(s_ctx - m)
        den = jnp.sum(p_nb, axis=-1, keepdims=True) + jnp.sum(p_ctx, axis=-1, keepdims=True)
        o = (_dot(p_nb.astype(BF16), v_ref[0, pl.ds(k0, nk), ls])
             + _dot(p_ctx.astype(BF16), v_ref[0, 0:CTX_LEN, ls]))
        yield
        o_ref[0, qs, ls] = _unstack_heads(o / den, GRID_W).astype(BF16)

    _interleave([unit(rr, g) for rr in range(NA_ROWS_PER_STEP) for g in range(2)], group=ATTN_GROUP)


def _bias_expand_kernel(rb_ref, oh_ref, o_ref):
    x = rb_ref[...]
    oh = oh_ref[...]
    hi = x.astype(BF16)
    r1 = x - hi.astype(F32)
    mid = r1.astype(BF16)
    lo = (r1 - mid.astype(F32)).astype(BF16)
    o_ref[...] = (_dot(hi, oh) + _dot(mid, oh) + _dot(lo, oh)) * LOG2E


def _na_bias_table(rel_bias):
    ndy, ndx = 2 * NA_KH - 1, 2 * NA_KW - 1
    cq = np.arange(GRID_W)
    col_start = np.clip(cq - NA_KW // 2, 0, GRID_W - NA_KW)
    col_ok = (cq[None, :] >= col_start[:, None]) & (cq[None, :] < col_start[:, None] + NA_KW)
    qi = jnp.arange(GRID_W, dtype=jnp.int32)
    dx = jnp.clip(qi[None, :] - qi[:, None], -(NA_KW - 1), NA_KW - 1) + (NA_KW - 1)
    onehot = (jnp.arange(128, dtype=jnp.int32)[:, None] == dx.reshape(1, GRID_W * GRID_W)).astype(BF16)
    rb = jnp.zeros((64, 128), F32).at[0:NA_HEADS * ndy, 0:ndx].set(rel_bias.reshape(NA_HEADS * ndy, ndx))
    p = pl.pallas_call(
        _bias_expand_kernel,
        out_shape=jax.ShapeDtypeStruct((64, GRID_W * GRID_W), F32),
        name="na_bias_expand",
    )(rb, onehot)
    p = p[0:NA_HEADS * ndy].reshape(NA_HEADS, ndy, GRID_W, GRID_W)
    p = jnp.where(col_ok[None, None], p, -jnp.inf)
    tabs = jnp.stack([p[:, NA_KH - 1 - c:2 * NA_KH - 1 - c] for c in range(NA_KH)])
    tabs = tabs.transpose(0, 1, 3, 2, 4)
    return tabs.reshape(NA_KH, NA_HEADS // 2, 2 * GRID_W, NA_KH * GRID_W)


def _na(dq, dk, dv, bias_tab, seq):
    b, t_all, _ = dq.shape
    rows_total = seq // GRID_W
    qrows = NA_ROWS_PER_STEP * GRID_W
    assert CTX_PAD % qrows == 0 and rows_total % NA_ROWS_PER_STEP == 0
    off = CTX_PAD // qrows
    full = pl.BlockSpec((1, t_all, 256), lambda bi, r: (bi, 0, 0))
    return pl.pallas_call(
        functools.partial(_na_kernel, rows_total=rows_total),
        out_shape=jax.ShapeDtypeStruct((b, seq, 256), BF16),
        grid=(b, rows_total // NA_ROWS_PER_STEP),
        in_specs=[pl.BlockSpec((1, qrows, 256), lambda bi, r: (bi, r + off, 0)), full, full,
                  pl.BlockSpec(bias_tab.shape, lambda bi, r: (0, 0, 0, 0))],
        out_specs=pl.BlockSpec((1, qrows, 256), lambda bi, r: (bi, r, 0)),
        compiler_params=_cparams(("parallel", "arbitrary"), VMEM_LIMIT),
        name="na",
    )(dq, dk, dv, bias_tab)


def _ctx_attn_kernel(sink_ref, q_ref, k_ref, v_ref, o_ref, *, has_sink):
    n = CTX_LEN
    head_row = lax.broadcasted_iota(jnp.int32, (2 * n, 1), 0) < n
    for g in range(2):
        ls = slice(128 * g, 128 * (g + 1))
        q_st = _stack_heads(q_ref[0, :, ls])
        s = _dot_nt(q_st, k_ref[0, :, ls])
        m = jnp.max(s, axis=-1, keepdims=True)
        if has_sink:
            sink = jnp.where(head_row, sink_ref[2 * g], sink_ref[2 * g + 1])
            m = jnp.maximum(m, sink)
        p = jnp.exp2(s - m)
        den = jnp.sum(p, axis=-1, keepdims=True)
        if has_sink:
            den = den + jnp.exp2(sink - m)
        o = _dot(p.astype(BF16), v_ref[0, :, ls])
        o_ref[0, :, ls] = _unstack_heads(o / den, n).astype(BF16)


def _ctx_attn(sink, q, k, v, has_sink):
    b = q.shape[0]
    blk = pl.BlockSpec((1, CTX_LEN, 256), lambda bi: (bi, 0, 0))
    return pl.pallas_call(
        functools.partial(_ctx_attn_kernel, has_sink=has_sink),
        out_shape=jax.ShapeDtypeStruct((b, CTX_LEN, 256), BF16),
        grid=(b,),
        in_specs=[pl.BlockSpec(memory_space=pltpu.SMEM), blk, blk, blk],
        out_specs=blk,
        compiler_params=_cparams(("parallel",)),
        name="ctx_attn_sink" if has_sink else "ctx_attn",
    )(sink, q, k, v)


def _outproj_kernel(x_ref, of_ref, ob_ref, fz_ref, co_ref, do_ref, g_ref, mod_ref, gn_ref, seg_ref,
                    w_ref, o_ref):
    a = of_ref[0].astype(F32) + ob_ref[0].astype(F32)
    ms = _dot((a * a).astype(BF16), seg_ref[...]) * (1.0 / GLA_DV)
    a = a * lax.rsqrt(ms + EPS) * gn_ref[...]
    g = g_ref[0].astype(F32)
    mixed = jnp.concatenate([a * g[:, 0:256], fz_ref[0].astype(F32) * g[:, 256:512],
                             co_ref[0].astype(F32) * g[:, 512:768], do_ref[0].astype(F32) * g[:, 768:1024]],
                            axis=1)
    y = _dot(mixed.astype(BF16), w_ref[...])
    o_ref[0] = x_ref[0] + mod_ref[0, 2:3, :] * y


def _outproj(x, o_f, o_b, fz, c_o, d_o, g, mod, gla_norm, seg_ones, w_out, is_ctx):
    b, rows, d = x.shape
    tm = CTX_LEN if is_ctx else TM
    nblk = rows // tm
    comb = 0 if is_ctx else CTX_PAD // tm
    nbatch = b

    def own(width):
        return pl.BlockSpec((1, tm, width), lambda bi, i: (bi, i, 0))

    def combined(width):
        return pl.BlockSpec((1, tm, width), lambda bi, i: (bi, i + comb, 0))

    def const(shape):
        return pl.BlockSpec(shape, lambda bi, i: (0,) * len(shape))

    mod_spec = pl.BlockSpec((1, 3, d), (lambda bi, i: (nbatch, 0, 0)) if is_ctx else (lambda bi, i: (bi, 0, 0)))
    return pl.pallas_call(
        _outproj_kernel,
        out_shape=jax.ShapeDtypeStruct((b, rows, d), F32),
        grid=(b, nblk),
        in_specs=[own(d), combined(256), combined(256), own(256), own(256), own(256), combined(1024),
                  mod_spec, const((1, 256)), const((256, 256)), const((d, d))],
        out_specs=own(d),
        compiler_params=_cparams(("parallel", "arbitrary"), VMEM_LIMIT),
        name="outproj_ctx" if is_ctx else "outproj",
    )(x, o_f, o_b, fz, c_o, d_o, g, mod, gla_norm, seg_ones, w_out)


_W_IN_MOVES = (
    (C_A, 0, 512), (C_BV, 800, 256), (C_CQ, 1312, 256),
    (C_CK, 1568, 64), (C_CK + 64, 1568, 64), (C_CK + 128, 1632, 64), (C_CK + 192, 1632, 64),
    (C_CV, 1696, 64), (C_CV + 64, 1696, 64), (C_CV + 128, 1760, 64), (C_CV + 192, 1760, 64),
    (C_DQ, 2080, 256), (C_DK, 2336, 256), (C_DV, 2592, 256),
    (C_G, 512, 256), (C_G + 256, 1056, 256), (C_G + 512, 1824, 256), (C_G + 768, 2848, 256),
    (C_LR, 768, 2 * GLA_RANK),
)


def _w_in_prep_kernel(wt_ref, o_ref):
    for dst, src, width in _W_IN_MOVES:
        o_ref[0, :, dst:dst + width] = wt_ref[0, src:src + width, :].T.astype(BF16)
    pad0 = C_LR + 2 * GLA_RANK
    o_ref[0, :, pad0:W_COLS] = jnp.zeros((o_ref.shape[1], W_COLS - pad0), BF16)


def _permute_w_in(w_all):
    depth, d, n = w_all.shape
    kblk = 256
    return pl.pallas_call(
        _w_in_prep_kernel,
        out_shape=jax.ShapeDtypeStruct((depth, d, W_COLS), BF16),
        grid=(depth, d // kblk),
        in_specs=[pl.BlockSpec((1, n, kblk), lambda l, i: (l, 0, i))],
        out_specs=pl.BlockSpec((1, kblk, W_COLS), lambda l, i: (l, i, 0)),
        compiler_params=_cparams(("parallel", "parallel")),
        name="w_in_prep",
    )(jnp.swapaxes(w_all, 1, 2))


def _rope_tables(seq):
    t = np.arange(seq)
    axis_dim = HEAD_DIM // 2
    inv = ROPE_BASE ** (-np.arange(0, axis_dim, 2, dtype=np.float64) / axis_dim)
    row = (t // GRID_W).astype(np.float64)
    col = (t % GRID_W).astype(np.float64)
    ang = np.concatenate([row[:, None] * inv, row[:, None] * inv, col[:, None] * inv, col[:, None] * inv], axis=1)
    cos = np.cos(ang)
    sign = np.concatenate([-np.ones(16), np.ones(16), -np.ones(16), np.ones(16)])
    sin = np.sin(ang) * sign
    cos = np.concatenate([np.ones((CTX_PAD, HEAD_DIM)), cos], axis=0)
    sin = np.concatenate([np.zeros((CTX_PAD, HEAD_DIM)), sin], axis=0)
    return (jnp.asarray(np.tile(cos, (1, 4)), F32), jnp.asarray(np.tile(sin, (1, 4)), F32))


def _seg_ones(width, seg):
    i = np.arange(width)
    return jnp.asarray((i[:, None] // seg) == (i[None, :] // seg), BF16)


def _gla_ones2():
    r = np.arange(GLA_HEADS * GLA_DK) // GLA_DK
    c = np.arange(GLA_HEADS * GLA_DV) // GLA_DV
    return jnp.asarray(r[:, None] == c[None, :], BF16)


def _layer(x, ctx, mod, need_ctx, tables, norm_w, w_in_perm, layer, w_out, gla_dec_w, gla_dec_b,
           gla_out_norm, fnet_w, swa_q_norm, swa_k_norm, swa_sink, na_q_norm, na_k_norm, na_rel_bias):
    b, seq, d = x.shape
    cos_t, sin_t, seg64, ones2 = tables

    qs = HEAD_DIM ** -0.5 * LOG2E
    swa_sink = swa_sink * LOG2E
    head_norms = jnp.stack([jnp.tile(swa_q_norm, 4) * qs, jnp.tile(swa_k_norm, 4),
                            jnp.tile(na_q_norm, 4) * qs, jnp.tile(na_k_norm, 4)])
    head_norms = jnp.concatenate([head_norms, jnp.zeros((4, 256), F32)], axis=0)
    wdec = jnp.zeros((128, 256), F32)
    wdec = wdec.at[0:GLA_RANK, 0:128].set(gla_dec_w[0]).at[GLA_RANK:2 * GLA_RANK, 128:256].set(gla_dec_w[1])
    bdec = gla_dec_b.reshape(1, 256)

    a, la, bv, cq, ck, cv, dq, dk, dv, g = _inproj(
        x, ctx, mod, norm_w[None, :], w_in_perm, layer, cos_t, sin_t, head_norms, seg64,
        wdec.astype(BF16), bdec)

    o_f, o_b = _gla(a, la, ones2)
    wf = fnet_w.astype(BF16)
    fz = _fnet_grid(bv, wf, seq)
    c_o = _swa(swa_sink, cq, ck, cv, seq)
    d_o = _na(dq, dk, dv, _na_bias_table(na_rel_bias), seq)

    gn = jnp.tile(gla_out_norm, 4)[None, :]
    wo = w_out.astype(BF16)
    x_new = _outproj(x, o_f, o_b, fz, c_o, d_o, g, mod, gn, seg64, wo, is_ctx=False)
    ctx_new = None
    if need_ctx:
        fzc = _fnet(bv, wf, 0, CTX_LEN)
        c_oc = _ctx_attn(swa_sink, cq, ck, cv, has_sink=True)
        d_oc = _ctx_attn(swa_sink, dq, dk, dv, has_sink=False)
        ctx_new = _outproj(ctx, o_f, o_b, fzc, c_oc, d_oc, g, mod, gn, seg64, wo, is_ctx=True)
    return x_new, ctx_new


def kernel(x, c, ctx, c_ctx, norm_w, ada_w, ada_b, w_in, w_out, gla_dec_w, gla_dec_b, gla_out_norm,
           fnet_w, swa_q_norm, swa_k_norm, swa_sink, na_q_norm, na_k_norm, na_rel_bias):
    b, seq, d = x.shape
    depth = norm_w.shape[0]
    assert ctx.shape[1] == CTX_LEN and seq % TM == 0 and seq % (NA_KH * GRID_W) == 0
    tables = _rope_tables(seq) + (_seg_ones(256, HEAD_DIM), _gla_ones2())
    cc = jnp.concatenate([c, c_ctx[None, :], jnp.zeros((8 - (b + 1) % 8, d), F32)], axis=0)
    mod = _ada(cc, ada_w, ada_b[:, None, :])
    w_in_perm = _permute_w_in(w_in)
    for i in range(depth):
        x, ctx = _layer(x, ctx, mod[i], i < depth - 1, tables, norm_w[i], w_in_perm, i, w_out[i],
                        gla_dec_w[i], gla_dec_b[i], gla_out_norm[i], fnet_w[i], swa_q_norm[i],
                        swa_k_norm[i], swa_sink[i], na_q_norm[i], na_k_norm[i], na_rel_bias[i])
    return x
```

```python
import functools
import math

import numpy as np
import jax
import jax.numpy as jnp
from jax import lax
from jax.experimental import pallas as pl
from jax.experimental.pallas import tpu as pltpu

F32 = jnp.float32
BF16 = jnp.bfloat16

GRID_W = 64
CTX_LEN = 256
HEAD_DIM = 64
BRANCH_W = 256
EPS = 1e-6
ROPE_BASE = 10000.0
GLA_HEADS = 4
GLA_DK = 32
GLA_DV = 64
GLA_RANK = 16
GLA_TAU = 16.0
GLA_CHUNK = 64
GLA_SUB = 8
LOG2E = 1.4426950408889634
FNET_GW = 64
SWA_BLOCK = 128
SWA_WINDOW = 128
NA_KH = 8
NA_KW = 16
NA_HEADS = 4

TM = 512
CTX_PAD = TM
VMEM_LIMIT = 48 * 1024 * 1024

C_A = 0
C_BV = 512
C_CQ = 768
C_CK = 1024
C_CV = 1152
C_DQ = 1280
C_DK = 1536
C_DV = 1792
C_G = 2048
C_LR = 3072
W_COLS = 3200


def _dot(a, b):
    return jnp.dot(a, b, preferred_element_type=F32)


def _dot_nt(a, b):
    return lax.dot_general(a, b, (((1,), (1,)), ((), ())), preferred_element_type=F32)


def _dot_tn(a, b):
    return lax.dot_general(a, b, (((0,), (0,)), ((), ())), preferred_element_type=F32)


GLA_GROUP = 8
INPROJ_ROW_SPLIT = 2
INPROJ_GROUP = 2
ATTN_GROUP = 4
_DONE = object()


def _interleave(units, group=None):
    group = group or len(units)
    for i in range(0, len(units), group):
        live = units[i:i + group]
        while live:
            live = [u for u in live if next(u, _DONE) is not _DONE]


def _skew(units, depth):
    started = []
    for u in units:
        next(u)
        started.append(u)
        if len(started) > depth:
            next(started.pop(0), _DONE)
    for u in started:
        next(u, _DONE)


def _cparams(sem, vmem=None):
    return pltpu.CompilerParams(dimension_semantics=sem, vmem_limit_bytes=vmem)


def _ada_kernel(c_ref, w_ref, b_ref, o_ref):
    c = c_ref[...]
    s = c * (1.0 / (1.0 + jnp.exp(-c)))
    d = c.shape[1]
    for j in range(3):
        cols = slice(j * d, (j + 1) * d)
        o_ref[0, :, j, :] = _dot(s.astype(BF16), w_ref[0, :, cols].astype(BF16)) + b_ref[0, :, cols]


def _ada(cc, ada_w, ada_b):
    r, d = cc.shape
    depth, _, n = ada_w.shape
    return pl.pallas_call(
        _ada_kernel,
        out_shape=jax.ShapeDtypeStruct((depth, r, 3, d), F32),
        grid=(depth,),
        in_specs=[pl.BlockSpec((r, d), lambda l: (0, 0)),
                  pl.BlockSpec((1, d, n), lambda l: (l, 0, 0)),
                  pl.BlockSpec((1, 1, n), lambda l: (l, 0, 0))],
        out_specs=pl.BlockSpec((1, r, 3, d), lambda l: (l, 0, 0, 0)),
        compiler_params=_cparams(("parallel",), VMEM_LIMIT),
        name="ada",
    )(cc, ada_w, ada_b)


def _head_rms(x, seg_ones, w):
    ms = _dot((x * x).astype(BF16), seg_ones) * (1.0 / HEAD_DIM)
    return x * lax.rsqrt(ms + EPS) * w


def _rope(y, cos, sin_signed):
    lane = lax.broadcasted_iota(jnp.int32, y.shape, 1)
    first_half = (lane % 32) < 16
    n = y.shape[1]
    swapped = jnp.where(first_half, pltpu.roll(y, n - 16, 1), pltpu.roll(y, 16, 1))
    return y * cos + swapped * sin_signed


def _inproj_kernel(x_ref, ctx_ref, mod_ref, nw_ref, w_ref, cos_ref, sin_ref, hn_ref, seg_ref,
                   wdec_ref, bdec_ref,
                   a_ref, la_ref, bv_ref, cq_ref, ck_ref, cv_ref, dq_ref, dk_ref, dv_ref, g_ref,
                   xs_ref):
    i = pl.program_id(1)

    @pl.when(i == 0)
    def _():
        xs_ref[0:CTX_LEN, :] = ctx_ref[0]
        xs_ref[CTX_LEN:, :] = jnp.zeros((TM - CTX_LEN, xs_ref.shape[1]), F32)

    @pl.when(i > 0)
    def _():
        xs_ref[...] = x_ref[0]

    seg = seg_ref[...]
    half = TM // INPROJ_ROW_SPLIT
    hb = [None] * INPROJ_ROW_SPLIT

    def prologue(part):
        rs = slice(part * half, (part + 1) * half)
        x = xs_ref[rs, :]
        ms = jnp.mean(x * x, axis=-1, keepdims=True)
        y = x * lax.rsqrt(ms + EPS) * nw_ref[...]
        hb[part] = (y * (1.0 + mod_ref[0, 1:2, :]) + mod_ref[0, 0:1, :]).astype(BF16)
        yield

    def gla_qkv(part, rs):
        a = _dot(hb[part], w_ref[:, C_A:C_A + 512])
        yield
        a_ref[0, rs, 0:128] = a[:, 0:128] * (GLA_DK ** -0.5)
        a_ref[0, rs, 128:512] = a[:, 128:512]

    def plain(part, rs, c0, ref, width=256):
        r = _dot(hb[part], w_ref[:, c0:c0 + width])
        yield
        ref[0, rs, :] = r.astype(BF16)

    def normed(part, rs, c0, ref, row, rope, width=256):
        r = _dot(hb[part], w_ref[:, c0:c0 + width])
        yield
        y = _head_rms(r, seg[0:width, 0:width], hn_ref[row:row + 1, 0:width])
        ref[0, rs, :] = (_rope(y, cos_ref[rs, 0:width], sin_ref[rs, 0:width]) if rope else y).astype(BF16)

    def swa_kv(part, rs):
        r = _dot(hb[part], w_ref[:, C_CK:C_CK + 256])
        yield
        y = _head_rms(r[:, 0:128], seg[0:128, 0:128], hn_ref[1:2, 0:128])
        ck_ref[0, rs, :] = _rope(y, cos_ref[rs, 0:128], sin_ref[rs, 0:128]).astype(BF16)
        cv_ref[0, rs, :] = r[:, 128:256].astype(BF16)

    def swa_q(part, rs):
        r = _dot(hb[part], w_ref[:, C_CQ:C_CQ + 256])
        yield
        y = _rope(_head_rms(r, seg, hn_ref[0:1, :]), cos_ref[rs, :], sin_ref[rs, :])
        up = pltpu.roll(y, 64, 1)
        down = pltpu.roll(y, 192, 1)
        low = lax.broadcasted_iota(jnp.int32, (y.shape[0], 128), 1) < HEAD_DIM
        zero = jnp.zeros((y.shape[0], 128), F32)
        parts = [jnp.where(low, y[:, 0:128], zero), jnp.where(low, down[:, 0:128], zero),
                 jnp.where(low, zero, up[:, 128:256]), jnp.where(low, zero, y[:, 128:256])]
        cq_ref[0, rs, :] = jnp.concatenate(parts, axis=1).astype(BF16)

    def gates(part, rs, j):
        g = _dot(hb[part], w_ref[:, C_G + 256 * j:C_G + 256 * (j + 1)])
        yield
        g_ref[0, rs, 256 * j:256 * (j + 1)] = (g * (1.0 / (1.0 + jnp.exp(-g)))).astype(BF16)

    def decay(part, rs):
        lr = _dot(hb[part], w_ref[:, C_LR:C_LR + 128])
        yield
        z = _dot(lr.astype(BF16), wdec_ref[...]) + bdec_ref[...]
        la_ref[0, rs, :] = (jnp.minimum(z, 0.0) - jnp.log(1.0 + jnp.exp(-jnp.abs(z)))) * (LOG2E / GLA_TAU)

    def groups(part):
        rs = slice(part * half, (part + 1) * half)
        return [decay(part, rs), swa_q(part, rs), swa_kv(part, rs),
                normed(part, rs, C_DQ, dq_ref, 2, False), normed(part, rs, C_DK, dk_ref, 3, False),
                gates(part, rs, 0), gates(part, rs, 1), gates(part, rs, 2), gates(part, rs, 3),
                gla_qkv(part, rs), plain(part, rs, C_BV, bv_ref), plain(part, rs, C_DV, dv_ref)]

    units = [prologue(0)]
    for part in range(INPROJ_ROW_SPLIT):
        g = groups(part)
        if part + 1 < INPROJ_ROW_SPLIT:
            g.insert(INPROJ_GROUP, prologue(part + 1))
        units += g
    _skew(units, INPROJ_GROUP)


def _inproj(x, ctx, mod, norm_w, w_perm, layer, cos_t, sin_t, head_norms, seg_ones, wdec, bdec):
    b, seq, d = x.shape
    t_all = CTX_PAD + seq
    nblk = t_all // TM

    def tok(width):
        return pl.BlockSpec((1, TM, width), lambda bi, i: (bi, i, 0))

    def const(shape):
        return pl.BlockSpec(shape, lambda bi, i: (0,) * len(shape))

    def out(width, dtype):
        return jax.ShapeDtypeStruct((b, t_all, width), dtype)

    return pl.pallas_call(
        _inproj_kernel,
        out_shape=(out(512, F32), out(256, F32), out(256, BF16), out(512, BF16), out(128, BF16),
                   out(128, BF16), out(256, BF16), out(256, BF16), out(256, BF16), out(1024, BF16)),
        grid=(b, nblk),
        in_specs=[
            pl.BlockSpec((1, TM, d), lambda bi, i: (bi, jnp.maximum(i - 1, 0), 0)),
            pl.BlockSpec((1, CTX_LEN, d), lambda bi, i: (bi, 0, 0)),
            pl.BlockSpec((1, 3, d), lambda bi, i: (jnp.where(i == 0, b, bi), 0, 0)),
            const((1, d)),
            pl.BlockSpec((None, d, W_COLS), lambda bi, i: (layer, 0, 0)),
            pl.BlockSpec((TM, 256), lambda bi, i: (i, 0)),
            pl.BlockSpec((TM, 256), lambda bi, i: (i, 0)),
            const((8, 256)),
            const((256, 256)),
            const((128, 256)),
            const((1, 256)),
        ],
        out_specs=(tok(512), tok(256), tok(256), tok(512), tok(128), tok(128), tok(256), tok(256),
                   tok(256), tok(1024)),
        scratch_shapes=[pltpu.VMEM((TM, d), F32)],
        compiler_params=_cparams(("parallel", "arbitrary"), VMEM_LIMIT),
        name="inproj",
    )(x, ctx, mod, norm_w, w_perm, cos_t, sin_t, head_norms, seg_ones, wdec, bdec)


def _gla_kernel(af_ref, ab_ref, laf_ref, lab_ref, ones2_ref, trif_ref, trib_ref, pmf_ref, pmb_ref,
                qmask_ref, stmask_ref, of_ref, ob_ref, st_ref, rb_ref, *, nbatch, nreal):
    @pl.when(pl.program_id(0) == 0)
    def _():
        st_ref[...] = jnp.zeros_like(st_ref)

    @pl.when(pl.program_id(0) < nreal)
    def _():
        lane_head = lax.broadcasted_iota(jnp.int32, (GLA_CHUNK, GLA_HEADS * GLA_DV), 1) // GLA_DV
        shared = (qmask_ref[...], lane_head, stmask_ref[...])
        consts_f = (ones2_ref[...], trif_ref[...], pmf_ref[...]) + shared
        consts_b = (ones2_ref[...], trib_ref[...], pmb_ref[...]) + shared
        units = []
        for bi in range(nbatch):
            units.append(_gla_chunk(af_ref, laf_ref, of_ref, st_ref.at[0], rb_ref.at[0], bi, False, consts_f))
            units.append(_gla_chunk(ab_ref, lab_ref, ob_ref, st_ref.at[1], rb_ref.at[1], bi, True, consts_b))
        _interleave(units, group=GLA_GROUP)

    @pl.when(pl.program_id(0) >= nreal)
    def _():
        of_ref[...] = jnp.zeros_like(of_ref)
        ob_ref[...] = jnp.zeros_like(ob_ref)


def _gla_chunk(a_ref, la_ref, o_ref, st_ref, rb_ref, bi, rev, consts):
    ones2, tri, pmask, qmask, lane_head, stmask = consts
    c, sb = GLA_CHUNK, GLA_SUB
    nsub = c // sb
    hk = GLA_HEADS * GLA_DK
    hv = GLA_HEADS * GLA_DV

    q = a_ref[bi, :, 0:hk]
    k = a_ref[bi, :, hk:2 * hk]
    v = a_ref[bi, :, 2 * hk:2 * hk + hv]
    la = la_ref[bi]

    la_hi = la.astype(BF16)
    la_lo = (la - la_hi.astype(F32)).astype(BF16)
    cum = _dot(tri, la_hi) + _dot(tri, la_lo)
    cumx = cum - la

    rb_ref[bi, 0] = cum
    rb_ref[bi, 1] = cumx
    rb_ref[bi, 2] = k
    rb_ref[bi, 3] = v[:, 0:hk]
    rb_ref[bi, 4] = v[:, hk:hv]
    yield

    def brow(plane, r, n=sb):
        return jnp.broadcast_to(rb_ref[bi, plane, r:r + 1, :], (n, hk))

    end = 0 if rev else c - 1

    def first_row(blk):
        return sb * blk + (sb - 1 if rev else 0)

    def blk_of(p):
        return nsub - 1 - p if rev else p

    cref_rows = jnp.concatenate([brow(1, first_row(blk)) for blk in range(nsub)], axis=0)

    st = st_ref[bi]
    qi = q * jnp.exp2(cum)
    inter = _dot_nt(qi.astype(BF16), st.astype(BF16))

    qd = q * jnp.exp2(cum - cref_rows)
    q_st = jnp.concatenate([qd] * GLA_HEADS, axis=0) * qmask
    kt, vt = [], []
    for p in range(1, nsub):
        lo, hi = (c - sb * p, c) if rev else (0, sb * p)
        kt.append(k[lo:hi] * jnp.exp2(brow(1, first_row(blk_of(p)), hi - lo) - cum[lo:hi]))
        vt.append(v[lo:hi])
    npad = pmask.shape[1] - sb * (nsub * (nsub - 1) // 2)
    kt.append(jnp.zeros((npad, hk), F32))
    vt.append(jnp.zeros((npad, hv), F32))
    kt = jnp.concatenate(kt, axis=0).astype(BF16)
    vt = jnp.concatenate(vt, axis=0).astype(BF16)
    sc = _dot_nt(q_st.astype(BF16), kt)

    t_loc = lax.broadcasted_iota(jnp.int32, (sb, hk), 0)
    valid = [(t_loc <= s) if rev else (t_loc >= s) for s in range(sb)]
    es = []
    for blk in range(nsub):
        r0 = sb * blk
        cb = cum[r0:r0 + sb]
        qb = q[r0:r0 + sb]
        for s in range(sb):
            diff = jnp.where(valid[s], cb - brow(0, r0 + s), -jnp.inf)
            es.append(jnp.exp2(diff) * (qb * brow(2, r0 + s)))
    e = jnp.concatenate(es, axis=0).astype(BF16)
    r = _dot(e, ones2)
    cum_end = rb_ref[bi, 0, end:end + 1, :]
    kd = k * jnp.exp2(cum_end - cum)
    kv = _dot_tn(v.astype(BF16), kd.astype(BF16))
    yield

    off_st = _dot(sc.astype(BF16) * pmask, vt)
    dgs = []
    for blk in range(nsub):
        acc = None
        for s in range(sb):
            row0 = (blk * sb + s) * sb
            vrow = jnp.concatenate([brow(3, blk * sb + s), brow(4, blk * sb + s)], axis=1)
            term = r[row0:row0 + sb] * vrow
            acc = term if acc is None else acc + term
        dgs.append(acc)
    st_ref[bi] = st * jnp.exp2(cum_end) + kv * stmask
    yield

    off = off_st[(GLA_HEADS - 1) * c:GLA_HEADS * c]
    for h in range(GLA_HEADS - 2, -1, -1):
        off = jnp.where(lane_head == h, off_st[h * c:(h + 1) * c], off)
    o_ref[bi] = (inter + off + jnp.concatenate(dgs, axis=0)).astype(BF16)


def _gla_consts(rev):
    c, sb = GLA_CHUNK, GLA_SUB
    nsub = c // sb
    t = np.arange(c)
    tri = (t[None, :] >= t[:, None]) if rev else (t[None, :] <= t[:, None])
    blk = t // sb
    p_of_t = (nsub - 1 - blk) if rev else blk
    col_p = np.concatenate([np.full(sb * p, p) for p in range(1, nsub)])
    col_p = np.concatenate([col_p, np.full(-col_p.size % 128, -1)])
    pmask = np.tile(p_of_t, GLA_HEADS)[:, None] == col_p[None, :]
    row_h = np.repeat(np.arange(GLA_HEADS), c)
    qmask = row_h[:, None] == (np.arange(GLA_HEADS * GLA_DK) // GLA_DK)[None, :]
    stmask = (np.arange(GLA_HEADS * GLA_DV) // GLA_DV)[:, None] == (np.arange(GLA_HEADS * GLA_DK) // GLA_DK)[None, :]
    f = lambda m: jnp.asarray(m, F32)
    return f(tri).astype(BF16), f(pmask).astype(BF16), f(qmask), f(stmask)


def _gla(a, la, ones2):
    b, t_all, _ = a.shape
    nch = t_all // GLA_CHUNK
    nctx = CTX_LEN // GLA_CHUNK
    nskip = (CTX_PAD - CTX_LEN) // GLA_CHUNK
    tri_f, pm_f, qmask, stmask = _gla_consts(False)
    tri_b, pm_b, _, _ = _gla_consts(True)
    consts = (ones2, tri_f, tri_b, pm_f, pm_b, qmask, stmask)

    nreal = nch - nskip

    def fwd(j):
        return jnp.where(j < nctx, j, jnp.where(j < nreal, j + nskip, nctx + (j - nreal)))

    def bwd(j):
        return jnp.where(j < nctx, nctx - 1 - j, jnp.where(j < nreal, nch - 1 - (j - nctx), nctx + (j - nreal)))

    out = jax.ShapeDtypeStruct((b, t_all, 256), BF16)
    return pl.pallas_call(
        functools.partial(_gla_kernel, nbatch=b, nreal=nreal),
        out_shape=(out, out),
        grid=(nch,),
        in_specs=[pl.BlockSpec((b, GLA_CHUNK, 512), lambda j: (0, fwd(j), 0)),
                  pl.BlockSpec((b, GLA_CHUNK, 512), lambda j: (0, bwd(j), 0)),
                  pl.BlockSpec((b, GLA_CHUNK, 128), lambda j: (0, fwd(j), 0)),
                  pl.BlockSpec((b, GLA_CHUNK, 128), lambda j: (0, bwd(j), 1))]
                 + [pl.BlockSpec(t.shape, lambda j: (0, 0)) for t in consts],
        out_specs=(pl.BlockSpec((b, GLA_CHUNK, 256), lambda j: (0, fwd(j), 0)),
                   pl.BlockSpec((b, GLA_CHUNK, 256), lambda j: (0, bwd(j), 0))),
        scratch_shapes=[pltpu.VMEM((2, b, 256, 128), F32), pltpu.VMEM((2, b, 5, GLA_CHUNK, 128), F32)],
        compiler_params=_cparams(("arbitrary",)),
        name="gla",
    )(a, a, la, la, *consts)


def _fnet_chan_kernel(v_ref, wc_ref, ws_ref, o_ref):
    v = v_ref[0]
    o_ref[0, :, :] = _dot(v, wc_ref[...]).astype(BF16)
    o_ref[1, :, :] = _dot(v, ws_ref[...]).astype(BF16)


def _fnet_chan(bv, wc, ws, row_blk_off, length):
    b = bv.shape[0]
    tl = min(TM, length)
    return pl.pallas_call(
        _fnet_chan_kernel,
        out_shape=jax.ShapeDtypeStruct((2, length, b * 256), BF16),
        grid=(b, length // tl),
        in_specs=[pl.BlockSpec((1, tl, 256), lambda bi, i: (bi, i + row_blk_off, 0)),
                  pl.BlockSpec((256, 256), lambda bi, i: (0, 0)),
                  pl.BlockSpec((256, 256), lambda bi, i: (0, 0))],
        out_specs=pl.BlockSpec((2, tl, 256), lambda bi, i: (0, i, bi)),
        compiler_params=_cparams(("parallel", "arbitrary")),
        name="fnet_chan",
    )(bv, wc, ws)


def _fnet_pos_kernel(d_ref, y_ref, wf_ref, o_ref, acc_ref, *, nb):
    kk = pl.program_id(1)

    @pl.when(kk == 0)
    def _():
        acc_ref[...] = jnp.zeros_like(acc_ref)

    acc_ref[...] += _dot(d_ref[...], y_ref[...])

    @pl.when(kk == pl.num_programs(1) - 1)
    def _():
        wf = wf_ref[...]
        for bi in range(nb):
            o_ref[bi] = _dot(acc_ref[:, bi * 256:(bi + 1) * 256].astype(BF16), wf).astype(BF16)


def _fnet_pos(dmat, y2, wf, nb):
    length, k2 = dmat.shape
    tm = min(512, length)
    tk = min(1024, k2)
    return pl.pallas_call(
        functools.partial(_fnet_pos_kernel, nb=nb),
        out_shape=jax.ShapeDtypeStruct((nb, length, 256), BF16),
        grid=(length // tm, k2 // tk),
        in_specs=[pl.BlockSpec((tm, tk), lambda i, kk: (i, kk)),
                  pl.BlockSpec((tk, nb * 256), lambda i, kk: (kk, 0)),
                  pl.BlockSpec((256, 256), lambda i, kk: (0, 0))],
        out_specs=pl.BlockSpec((nb, tm, 256), lambda i, kk: (0, i, 0)),
        scratch_shapes=[pltpu.VMEM((tm, nb * 256), F32)],
        compiler_params=_cparams(("parallel", "arbitrary"), VMEM_LIMIT),
        name="fnet_pos",
    )(dmat, y2, wf)


def _dft_tables(length):
    j = np.arange(length)
    ang = ((j[:, None] * j[None, :]) % length) * (2.0 * np.pi / length)
    s = 1.0 / math.sqrt(length)
    return jnp.asarray(np.concatenate([np.cos(ang) * s, -np.sin(ang) * s], axis=1), F32).astype(BF16)


def _chan_tables():
    j = np.arange(FNET_GW)
    ang = 2.0 * np.pi * ((j[:, None] * j[None, :]) % FNET_GW) / FNET_GW
    s = 1.0 / math.sqrt(FNET_GW)
    eye = np.eye(BRANCH_W // FNET_GW)
    wc = np.kron(eye, np.cos(ang) * s)
    ws = np.kron(eye, np.sin(ang) * s)
    return jnp.asarray(wc, F32).astype(BF16), jnp.asarray(ws, F32).astype(BF16)


def _fnet(bv, wf, row_blk_off, length):
    b = bv.shape[0]
    wc, ws = _chan_tables()
    y = _fnet_chan(bv, wc, ws, row_blk_off, length)
    y2 = y.reshape(2 * length, b * 256)
    return _fnet_pos(_dft_tables(length), y2, wf, b)


FNET_UNROLL = 8


def _fnet_grid_kernel(v_ref, cg_ref, sg_ref, w1_ref, w2_ref, wf_ref, o_ref, z_ref, g_ref, f_ref, *, rows, row_off):
    length = rows * GRID_W
    v = v_ref[0, row_off:row_off + length, :]
    wf = wf_ref[...]
    zr = _dot(v, _dot(cg_ref[...], wf).astype(BF16))
    zi = _dot(v, _dot(sg_ref[...], wf).astype(BF16))
    z_ref[0] = zr[:, 0:128].reshape(rows, GRID_W, 128)
    z_ref[1] = zr[:, 128:256].reshape(rows, GRID_W, 128)
    z_ref[2] = zi[:, 0:128].reshape(rows, GRID_W, 128)
    z_ref[3] = zi[:, 128:256].reshape(rows, GRID_W, 128)

    n8 = rows * 8
    for w0 in range(0, GRID_W, 8):
        def slab(c):
            return z_ref[c, :, w0:w0 + 8, :].reshape(n8, 128)
        rhs = jnp.concatenate([jnp.concatenate([slab(0), slab(1)], axis=1),
                               jnp.concatenate([slab(2), slab(3)], axis=1)], axis=0).astype(BF16)
        g = _dot(w1_ref[...], rhs)
        g_ref[0, :, w0:w0 + 8, :] = g[0:n8, 0:128].reshape(rows, 8, 128)
        g_ref[1, :, w0:w0 + 8, :] = g[0:n8, 128:256].reshape(rows, 8, 128)
        g_ref[2, :, w0:w0 + 8, :] = g[n8:2 * n8, 0:128].reshape(rows, 8, 128)
        g_ref[3, :, w0:w0 + 8, :] = g[n8:2 * n8, 128:256].reshape(rows, 8, 128)

    def stage_w(ka, carry):
        r0 = pl.multiple_of(ka * GRID_W, GRID_W)
        rhs = jnp.concatenate([jnp.concatenate([g_ref[0, ka], g_ref[1, ka]], axis=1),
                               jnp.concatenate([g_ref[2, ka], g_ref[3, ka]], axis=1)], axis=0).astype(BF16)
        y = _dot(w2_ref[pl.ds(r0, GRID_W), :], rhs)
        f_ref[0, pl.ds(ka, GRID_W, stride=rows), :] = y[:, 0:128]
        f_ref[1, pl.ds(ka, GRID_W, stride=rows), :] = y[:, 128:256]
        return carry

    lax.fori_loop(0, rows, stage_w, 0, unroll=FNET_UNROLL)
    o_ref[0, :, 0:128] = f_ref[0].astype(BF16)
    o_ref[0, :, 128:256] = f_ref[1].astype(BF16)


def _fnet_grid_tables(rows):
    length = rows * GRID_W
    j = np.arange(rows)
    ang = 2.0 * np.pi * ((j[:, None] * j[None, :]) % rows) / rows
    c, s = np.cos(ang), np.sin(ang)
    w1 = np.kron(np.block([[c, -s], [-s, -c]]), np.eye(8))
    w1 = jnp.asarray(w1, F32).astype(BF16)
    ka = np.arange(rows)[:, None, None]
    kb = np.arange(GRID_W)[None, :, None]
    w = np.arange(GRID_W)[None, None, :]
    ang2 = ((w * (ka + rows * kb)) % length) * (2.0 * np.pi / length)
    scale = 1.0 / math.sqrt(length)
    w2 = np.concatenate([np.cos(ang2) * scale, np.sin(ang2) * scale], axis=2)
    return w1, jnp.asarray(w2.reshape(length, 2 * GRID_W), F32).astype(BF16)


def _fnet_grid(bv, wf, seq):
    b, t_all, _ = bv.shape
    rows = seq // GRID_W
    wc, ws = _chan_tables()
    w1, w2 = _fnet_grid_tables(rows)

    def const(shape):
        return pl.BlockSpec(shape, lambda bi: (0,) * len(shape))

    return pl.pallas_call(
        functools.partial(_fnet_grid_kernel, rows=rows, row_off=CTX_PAD),
        out_shape=jax.ShapeDtypeStruct((b, seq, 256), BF16),
        grid=(b,),
        in_specs=[pl.BlockSpec((1, t_all, 256), lambda bi: (bi, 0, 0)), const((256, 256)), const((256, 256)),
                  const((16 * rows, 16 * rows)), const((seq, 2 * GRID_W)), const((256, 256))],
        out_specs=pl.BlockSpec((1, seq, 256), lambda bi: (bi, 0, 0)),
        scratch_shapes=[pltpu.VMEM((4, rows, GRID_W, 128), F32), pltpu.VMEM((4, rows, GRID_W, 128), F32),
                        pltpu.VMEM((2, seq, 128), F32)],
        compiler_params=_cparams(("parallel",), VMEM_LIMIT),
        name="fnet_grid",
    )(bv, wc, ws, w1, w2, wf)


def _stack_heads(qg):
    lane = lax.broadcasted_iota(jnp.int32, qg.shape, 1)
    zero = jnp.zeros_like(qg)
    return jnp.concatenate([jnp.where(lane < HEAD_DIM, qg, zero),
                            jnp.where(lane >= HEAD_DIM, qg, zero)], axis=0)


def _gqa_unstack(o, rows, g):
    lane = lax.broadcasted_iota(jnp.int32, (rows, 2 * HEAD_DIM), 1)
    first, second = o[0:rows], o[rows:2 * rows]
    if g == 0:
        return jnp.where(lane < HEAD_DIM, first, pltpu.roll(second, HEAD_DIM, 1))
    return jnp.where(lane < HEAD_DIM, pltpu.roll(first, HEAD_DIM, 1), second)


def _unstack_heads(o, rows):
    lane = lax.broadcasted_iota(jnp.int32, (rows, 2 * HEAD_DIM), 1)
    return jnp.where(lane < HEAD_DIM, o[0:rows], o[rows:2 * rows])


def _swa_kernel(sink_ref, band_ref, q_ref, kp_ref, kc_ref, kn_ref, kx_ref, vp_ref, vc_ref, vn_ref, vx_ref,
                o_ref, *, nbatch):
    n = pl.program_id(0)
    nb = pl.num_programs(0)
    blk = SWA_BLOCK
    rows = 2 * blk
    col = lax.broadcasted_iota(jnp.int32, (1, 3 * blk), 1)
    col_lo = jnp.where(n > 0, 0, blk)
    col_hi = jnp.where(n < nb - 1, 3 * blk, 2 * blk)
    edge = jnp.where((col >= col_lo) & (col < col_hi), 0.0, -jnp.inf)
    bias = band_ref[...] + edge
    head_row = lax.broadcasted_iota(jnp.int32, (rows, 1), 0) < blk

    def unit(bi, g):
        q_st = jnp.concatenate([q_ref[bi, :, 256 * g:256 * g + 128],
                                q_ref[bi, :, 256 * g + 128:256 * g + 256]], axis=0)
        kw = jnp.concatenate([kp_ref[bi], kc_ref[bi], kn_ref[bi]], axis=0)
        vw = jnp.concatenate([vp_ref[bi], vc_ref[bi], vn_ref[bi]], axis=0)
        s_loc = _dot_nt(q_st, kw) + bias
        s_ctx = _dot_nt(q_st, kx_ref[bi])
        yield
        sink = jnp.where(head_row, sink_ref[2 * g], sink_ref[2 * g + 1])
        m = jnp.maximum(jnp.maximum(jnp.max(s_loc, axis=-1, keepdims=True),
                                    jnp.max(s_ctx, axis=-1, keepdims=True)), sink)
        p_loc = jnp.exp2(s_loc - m)
        p_ctx = jnp.exp2(s_ctx - m)
        den = (jnp.sum(p_loc, axis=-1, keepdims=True) + jnp.sum(p_ctx, axis=-1, keepdims=True)
               + jnp.exp2(sink - m))
        o = _dot(p_loc.astype(BF16), vw) + _dot(p_ctx.astype(BF16), vx_ref[bi])
        yield
        o_ref[bi, :, 128 * g:128 * (g + 1)] = _gqa_unstack(o / den, blk, g).astype(BF16)

    _interleave([unit(bi, g) for bi in range(nbatch) for g in range(2)], group=ATTN_GROUP)


def _swa_band():
    row = np.arange(2 * SWA_BLOCK)[:, None] % SWA_BLOCK
    col = np.arange(3 * SWA_BLOCK)[None, :]
    ok = np.abs(col - SWA_BLOCK - row) <= SWA_WINDOW
    return jnp.where(jnp.asarray(ok), 0.0, -jnp.inf).astype(F32)


def _swa(sink, cq, ck, cv, seq):
    b = cq.shape[0]
    nb = seq // SWA_BLOCK
    off = CTX_PAD // SWA_BLOCK

    def at(f, width=128):
        return pl.BlockSpec((b, SWA_BLOCK, width), lambda n: (0, f(n) + off, 0))

    prev = at(lambda n: jnp.maximum(n - 1, 0))
    cur = at(lambda n: n)
    nxt = at(lambda n: jnp.minimum(n + 1, nb - 1))
    cx = pl.BlockSpec((b, CTX_LEN, 128), lambda n: (0, 0, 0))
    band = pl.BlockSpec((2 * SWA_BLOCK, 3 * SWA_BLOCK), lambda n: (0, 0))
    return pl.pallas_call(
        functools.partial(_swa_kernel, nbatch=b),
        out_shape=jax.ShapeDtypeStruct((b, seq, 256), BF16),
        grid=(nb,),
        in_specs=[pl.BlockSpec(memory_space=pltpu.SMEM), band, at(lambda n: n, 512),
                  prev, cur, nxt, cx, prev, cur, nxt, cx],
        out_specs=pl.BlockSpec((b, SWA_BLOCK, 256), lambda n: (0, n, 0)),
        compiler_params=_cparams(("arbitrary",)),
        name="swa",
    )(sink, _swa_band(), cq, ck, ck, ck, ck, cv, cv, cv, cv)


NA_ROWS_PER_STEP = 8


def _na_kernel(q_ref, k_ref, v_ref, bias_ref, o_ref, *, rows_total):
    kh = NA_KH
    nk = kh * GRID_W
    def unit(rr, g):
        r = pl.program_id(1) * NA_ROWS_PER_STEP + rr
        row_start = jnp.clip(r - kh // 2, 0, rows_total - kh)
        cls = r - row_start
        k0 = pl.multiple_of(CTX_PAD + row_start * GRID_W, GRID_W)
        qs = slice(GRID_W * rr, GRID_W * (rr + 1))
        ls = slice(128 * g, 128 * (g + 1))
        q_st = _stack_heads(q_ref[0, qs, ls])
        s_nb = _dot_nt(q_st, k_ref[0, pl.ds(k0, nk), ls]) + bias_ref[cls, g]
        s_ctx = _dot_nt(q_st, k_ref[0, 0:CTX_LEN, ls])
        yield
        m =jnp.maximum(jnp.max(s_nb, axis=-1, keepdims=True), jnp.max(s_ctx, axis=-1, keepdims=True))
        p_nb = jnp.exp2(s_nb - m)
        p_ctx = jnp.exp2(s_ctx - m)
        den = jnp.sum(p_nb, axis=-1, keepdims=True) + jnp.sum(p_ctx, axis=-1, keepdims=True)
        o = (_dot(p_nb.astype(BF16), v_ref[0, pl.ds(k0, nk), ls])
             + _dot(p_ctx.astype(BF16), v_ref[0, 0:CTX_LEN, ls]))
        yield
        o_ref[0, qs, ls] = _unstack_heads(o / den, GRID_W).astype(BF16)

    _interleave([unit(rr, g) for rr in range(NA_ROWS_PER_STEP) for g in range(2)], group=ATTN_GROUP)


def _bias_expand_kernel(rb_ref, oh_ref, o_ref):
    x = rb_ref[...]
    oh = oh_ref[...]
    hi = x.astype(BF16)
    r1 = x - hi.astype(F32)
    mid = r1.astype(BF16)
    lo = (r1 - mid.astype(F32)).astype(BF16)
    o_ref[...] = (_dot(hi, oh) + _dot(mid, oh) + _dot(lo, oh)) * LOG2E


def _na_bias_table(rel_bias):
    ndy, ndx = 2 * NA_KH - 1, 2 * NA_KW - 1
    cq = np.arange(GRID_W)
    col_start = np.clip(cq - NA_KW // 2, 0, GRID_W - NA_KW)
    col_ok = (cq[None, :] >= col_start[:, None]) & (cq[None, :] < col_start[:, None] + NA_KW)
    qi = jnp.arange(GRID_W, dtype=jnp.int32)
    dx = jnp.clip(qi[None, :] - qi[:, None], -(NA_KW - 1), NA_KW - 1) + (NA_KW - 1)
    onehot = (jnp.arange(128, dtype=jnp.int32)[:, None] == dx.reshape(1, GRID_W * GRID_W)).astype(BF16)
    rb = jnp.zeros((64, 128), F32).at[0:NA_HEADS * ndy, 0:ndx].set(rel_bias.reshape(NA_HEADS * ndy, ndx))
    p = pl.pallas_call(
        _bias_expand_kernel,
        out_shape=jax.ShapeDtypeStruct((64, GRID_W * GRID_W), F32),
        name="na_bias_expand",
    )(rb, onehot)
    p = p[0:NA_HEADS * ndy].reshape(NA_HEADS, ndy, GRID_W, GRID_W)
    p = jnp.where(col_ok[None, None], p, -jnp.inf)
    tabs = jnp.stack([p[:, NA_KH - 1 - c:2 * NA_KH - 1 - c] for c in range(NA_KH)])
    tabs = tabs.transpose(0, 1, 3, 2, 4)
    return tabs.reshape(NA_KH, NA_HEADS // 2, 2 * GRID_W, NA_KH * GRID_W)


def _na(dq, dk, dv, bias_tab, seq):
    b, t_all, _ = dq.shape
    rows_total = seq // GRID_W
    qrows = NA_ROWS_PER_STEP * GRID_W
    assert CTX_PAD % qrows == 0 and rows_total % NA_ROWS_PER_STEP == 0
    off = CTX_PAD // qrows
    full = pl.BlockSpec((1, t_all, 256), lambda bi, r: (bi, 0, 0))
    return pl.pallas_call(
        functools.partial(_na_kernel, rows_total=rows_total),
        out_shape=jax.ShapeDtypeStruct((b, seq, 256), BF16),
        grid=(b, rows_total // NA_ROWS_PER_STEP),
        in_specs=[pl.BlockSpec((1, qrows, 256), lambda bi, r: (bi, r + off, 0)), full, full,
                  pl.BlockSpec(bias_tab.shape, lambda bi, r: (0, 0, 0, 0))],
        out_specs=pl.BlockSpec((1, qrows, 256), lambda bi, r: (bi, r, 0)),
        compiler_params=_cparams(("parallel", "arbitrary"), VMEM_LIMIT),
        name="na",
    )(dq, dk, dv, bias_tab)


def _ctx_attn_kernel(sink_ref, q_ref, k_ref, v_ref, o_ref, *, gqa):
    n = CTX_LEN
    head_row = lax.broadcasted_iota(jnp.int32, (2 * n, 1), 0) < n
    for g in range(2):
        ls = slice(128 * g, 128 * (g + 1))
        if gqa:
            q_st = jnp.concatenate([q_ref[0, :, 256 * g:256 * g + 128],
                                    q_ref[0, :, 256 * g + 128:256 * g + 256]], axis=0)
            k, v = k_ref[0], v_ref[0]
        else:
            q_st = _stack_heads(q_ref[0, :, ls])
            k, v = k_ref[0, :, ls], v_ref[0, :, ls]
        s = _dot_nt(q_st, k)
        m = jnp.max(s, axis=-1, keepdims=True)
        if gqa:
            sink = jnp.where(head_row, sink_ref[2 * g], sink_ref[2 * g + 1])
            m = jnp.maximum(m, sink)
        p = jnp.exp2(s - m)
        den = jnp.sum(p, axis=-1, keepdims=True)
        if gqa:
            den = den + jnp.exp2(sink - m)
        o = _dot(p.astype(BF16), v) / den
        o_ref[0, :, ls] = (_gqa_unstack(o, n, g) if gqa else _unstack_heads(o, n)).astype(BF16)


def _ctx_attn(sink, q, k, v, gqa):
    b = q.shape[0]

    def blk(arr):
        return pl.BlockSpec((1, CTX_LEN, arr.shape[2]), lambda bi: (bi, 0, 0))

    return pl.pallas_call(
        functools.partial(_ctx_attn_kernel, gqa=gqa),
        out_shape=jax.ShapeDtypeStruct((b, CTX_LEN, 256), BF16),
        grid=(b,),
        in_specs=[pl.BlockSpec(memory_space=pltpu.SMEM), blk(q), blk(k), blk(v)],
        out_specs=pl.BlockSpec((1, CTX_LEN, 256), lambda bi: (bi, 0, 0)),
        compiler_params=_cparams(("parallel",)),
        name="ctx_attn_sink" if gqa else "ctx_attn",
    )(sink, q, k, v)


def _outproj_kernel(x_ref, of_ref, ob_ref, fz_ref, co_ref, do_ref, g_ref, mod_ref, gn_ref, seg_ref,
                    w_ref, o_ref):
    a = of_ref[0].astype(F32) + ob_ref[0].astype(F32)
    ms = _dot((a * a).astype(BF16), seg_ref[...]) * (1.0 / GLA_DV)
    a = a * lax.rsqrt(ms + EPS) * gn_ref[...]
    g = g_ref[0].astype(F32)
    mixed =jnp.concatenate([a * g[:, 0:256], fz_ref[0].astype(F32) * g[:, 256:512],
                             co_ref[0].astype(F32) * g[:, 512:768], do_ref[0].astype(F32) * g[:, 768:1024]],
                            axis=1)
    y = _dot(mixed.astype(BF16), w_ref[...])
    o_ref[0] = x_ref[0] + mod_ref[0, 2:3, :] * y


def _outproj(x, o_f, o_b, fz, c_o, d_o, g, mod, gla_norm, seg_ones, w_out, is_ctx):
    b, rows, d = x.shape
    tm = CTX_LEN if is_ctx else TM
    nblk = rows // tm
    comb = 0 if is_ctx else CTX_PAD // tm
    nbatch = b

    def own(width):
        return pl.BlockSpec((1, tm, width), lambda bi, i: (bi, i, 0))

    def combined(width):
        return pl.BlockSpec((1, tm, width), lambda bi, i: (bi, i + comb, 0))

    def const(shape):
        return pl.BlockSpec(shape, lambda bi, i: (0,) * len(shape))

    mod_spec = pl.BlockSpec((1, 3, d), (lambda bi, i: (nbatch, 0, 0)) if is_ctx else (lambda bi, i: (bi, 0, 0)))
    return pl.pallas_call(
        _outproj_kernel,
        out_shape=jax.ShapeDtypeStruct((b, rows, d), F32),
        grid=(b, nblk),
        in_specs=[own(d), combined(256), combined(256), own(256), own(256), own(256), combined(1024),
                  mod_spec, const((1, 256)), const((256, 256)), const((d, d))],
        out_specs=own(d),
        compiler_params=_cparams(("parallel", "arbitrary"), VMEM_LIMIT),
        name="outproj_ctx" if is_ctx else "outproj",
    )(x, o_f, o_b, fz, c_o, d_o, g, mod, gla_norm, seg_ones, w_out)


_W_IN_MOVES = (
    (C_A, 0, 512), (C_BV, 800, 256), (C_CQ, 1312, 256),
    (C_CK, 1568, 128), (C_CV, 1696, 128),
    (C_DQ, 2080, 256), (C_DK, 2336, 256), (C_DV, 2592, 256),
    (C_G, 512, 256), (C_G + 256, 1056, 256), (C_G + 512, 1824, 256), (C_G + 768, 2848, 256),
    (C_LR, 768, 2 * GLA_RANK),
)


def _w_in_prep_kernel(wt_ref, o_ref):
    for dst, src, width in _W_IN_MOVES:
        o_ref[0, :, dst:dst + width] = wt_ref[0, src:src + width, :].T.astype(BF16)
    pad0 = C_LR + 2 * GLA_RANK
    o_ref[0, :, pad0:W_COLS] = jnp.zeros((o_ref.shape[1], W_COLS - pad0), BF16)


def _permute_w_in(w_all):
    depth, d, n = w_all.shape
    kblk = 256
    return pl.pallas_call(
        _w_in_prep_kernel,
        out_shape=jax.ShapeDtypeStruct((depth, d, W_COLS), BF16),
        grid=(depth, d // kblk),
        in_specs=[pl.BlockSpec((1, n, kblk), lambda l, i: (l, 0, i))],
        out_specs=pl.BlockSpec((1, kblk, W_COLS), lambda l, i: (l, i, 0)),
        compiler_params=_cparams(("parallel", "parallel")),
        name="w_in_prep",
    )(jnp.swapaxes(w_all, 1, 2))


def _rope_tables(seq):
    t = np.arange(seq)
    axis_dim = HEAD_DIM // 2
    inv = ROPE_BASE ** (-np.arange(0, axis_dim, 2, dtype=np.float64) / axis_dim)
    row = (t // GRID_W).astype(np.float64)
    col = (t % GRID_W).astype(np.float64)
    ang = np.concatenate([row[:, None] * inv, row[:, None] * inv, col[:, None] * inv, col[:, None] * inv], axis=1)
    cos = np.cos(ang)
    sign = np.concatenate([-np.ones(16), np.ones(16), -np.ones(16), np.ones(16)])
    sin = np.sin(ang) * sign
    cos = np.concatenate([np.ones((CTX_PAD, HEAD_DIM)), cos], axis=0)
    sin = np.concatenate([np.zeros((CTX_PAD, HEAD_DIM)), sin], axis=0)
    return (jnp.asarray(np.tile(cos, (1, 4)), F32), jnp.asarray(np.tile(sin, (1, 4)), F32))


def _seg_ones(width, seg):
    i = np.arange(width)
    return jnp.asarray((i[:, None] // seg) == (i[None, :] // seg), BF16)


def _gla_ones2():
    r = np.arange(GLA_HEADS * GLA_DK) // GLA_DK
    c = np.arange(GLA_HEADS * GLA_DV) // GLA_DV
    return jnp.asarray(r[:, None] == c[None, :], BF16)


def _layer(x, ctx, mod, need_ctx, tables, norm_w, w_in_perm, layer, w_out, gla_dec_w, gla_dec_b,
           gla_out_norm, fnet_w, swa_q_norm, swa_k_norm, swa_sink, na_q_norm, na_k_norm, na_rel_bias):
    b, seq, d = x.shape
    cos_t, sin_t, seg64, ones2 = tables

    qs = HEAD_DIM ** -0.5 * LOG2E
    swa_sink = swa_sink * LOG2E
    head_norms = jnp.stack([jnp.tile(swa_q_norm, 4) * qs, jnp.tile(swa_k_norm, 4),
                            jnp.tile(na_q_norm, 4) * qs, jnp.tile(na_k_norm, 4)])
    head_norms = jnp.concatenate([head_norms, jnp.zeros((4, 256), F32)], axis=0)
    wdec = jnp.zeros((128, 256), F32)
    wdec = wdec.at[0:GLA_RANK, 0:128].set(gla_dec_w[0]).at[GLA_RANK:2 * GLA_RANK, 128:256].set(gla_dec_w[1])
    bdec = gla_dec_b.reshape(1, 256)

    a, la, bv, cq, ck, cv, dq, dk, dv, g = _inproj(
        x, ctx, mod, norm_w[None, :], w_in_perm, layer, cos_t, sin_t, head_norms, seg64,
        wdec.astype(BF16), bdec)

    o_f, o_b = _gla(a, la, ones2)
    wf = fnet_w.astype(BF16)
    fz = _fnet_grid(bv, wf, seq)
    c_o = _swa(swa_sink, cq, ck, cv, seq)
    d_o = _na(dq, dk, dv, _na_bias_table(na_rel_bias), seq)

    gn = jnp.tile(gla_out_norm, 4)[None, :]
    wo = w_out.astype(BF16)
    x_new = _outproj(x, o_f, o_b, fz, c_o, d_o, g, mod, gn, seg64, wo, is_ctx=False)
    ctx_new = None
    if need_ctx:
        fzc = _fnet(bv, wf, 0, CTX_LEN)
        c_oc = _ctx_attn(swa_sink, cq, ck, cv, gqa=True)
        d_oc = _ctx_attn(swa_sink, dq, dk, dv, gqa=False)
        ctx_new = _outproj(ctx, o_f, o_b, fzc, c_oc, d_oc, g, mod, gn, seg64, wo, is_ctx=True)
    return x_new, ctx_new


def kernel(x, c, ctx, c_ctx, norm_w, ada_w, ada_b, w_in, w_out, gla_dec_w, gla_dec_b, gla_out_norm,
           fnet_w, swa_q_norm, swa_k_norm, swa_sink, na_q_norm, na_k_norm, na_rel_bias):
    b, seq, d = x.shape
    depth = norm_w.shape[0]
    assert ctx.shape[1] == CTX_LEN and seq % TM == 0 and seq % (NA_KH * GRID_W) == 0
    tables = _rope_tables(seq) + (_seg_ones(256, HEAD_DIM), _gla_ones2())
    cc = jnp.concatenate([c, c_ctx[None, :], jnp.zeros((8 - (b + 1) % 8, d), F32)], axis=0)
    mod = _ada(cc, ada_w, ada_b[:, None, :])
    w_in_perm = _permute_w_in(w_in)
    for i in range(depth):
        x, ctx = _layer(x, ctx, mod[i], i < depth - 1, tables, norm_w[i], w_in_perm, i, w_out[i],
                        gla_dec_w[i], gla_dec_b[i], gla_out_norm[i], fnet_w[i], swa_q_norm[i],
                        swa_k_norm[i], swa_sink[i], na_q_norm[i], na_k_norm[i], na_rel_bias[i])
    return x
```

```python
import functools
import math

import numpy as np
import jax
import jax.numpy as jnp
from jax import lax
from jax.experimental import pallas as pl
from jax.experimental.pallas import tpu as pltpu

F32 = jnp.float32
BF16 = jnp.bfloat16

GRID_W = 64
CTX_LEN = 256
HEAD_DIM = 64
BRANCH_W = 256
EPS = 1e-6
ROPE_BASE = 10000.0
GLA_HEADS = 4
GLA_DK = 32
GLA_DV = 64
GLA_RANK = 16
GLA_TAU = 16.0
GLA_CHUNK = 64
GLA_SUB = 8
LOG2E = 1.4426950408889634
FNET_GW = 64
SWA_BLOCK = 128
SWA_WINDOW = 128
NA_KH = 8
NA_KW = 16
NA_HEADS = 4

TM = 512
CTX_PAD = TM
VMEM_LIMIT = 48 * 1024 * 1024

C_A = 0
C_BV = 512
C_CQ = 768
C_CK = 1024
C_CV = 1152
C_DQ = 1280
C_DK = 1536
C_DV = 1792
C_G = 2048
C_LR = 3072
W_COLS = 3200


def _dot(a, b):
    return jnp.dot(a, b, preferred_element_type=F32)


def _dot_nt(a, b):
    return lax.dot_general(a, b, (((1,), (1,)), ((), ())), preferred_element_type=F32)


def _dot_tn(a, b):
    return lax.dot_general(a, b, (((0,), (0,)), ((), ())), preferred_element_type=F32)


GLA_GROUP = 8
INPROJ_ROW_SPLIT = 2
INPROJ_GROUP = 3
ATTN_GROUP = 4
_DONE = object()


def _interleave(units, group=None):
    group = group or len(units)
    for i in range(0, len(units), group):
        live = units[i:i + group]
        while live:
            live = [u for u in live if next(u, _DONE) is not _DONE]


def _skew(units, depth):
    started = []
    for u in units:
        next(u)
        started.append(u)
        if len(started) > depth:
            next(started.pop(0), _DONE)
    for u in started:
        next(u, _DONE)


def _cparams(sem, vmem=None):
    return pltpu.CompilerParams(dimension_semantics=sem, vmem_limit_bytes=vmem)


def _ada_kernel(c_ref, w_ref, b_ref, o_ref):
    c = c_ref[...]
    s = c * (1.0 / (1.0 + jnp.exp(-c)))
    d = c.shape[1]
    for j in range(3):
        cols = slice(j * d, (j + 1) * d)
        o_ref[0, :, j, :] = _dot(s.astype(BF16), w_ref[0, :, cols].astype(BF16)) + b_ref[0, :, cols]


def _ada(cc, ada_w, ada_b):
    r, d = cc.shape
    depth, _, n = ada_w.shape
    return pl.pallas_call(
        _ada_kernel,
        out_shape=jax.ShapeDtypeStruct((depth, r, 3, d), F32),
        grid=(depth,),
        in_specs=[pl.BlockSpec((r, d), lambda l: (0, 0)),
                  pl.BlockSpec((1, d, n), lambda l: (l, 0, 0)),
                  pl.BlockSpec((1, 1, n), lambda l: (l, 0, 0))],
        out_specs=pl.BlockSpec((1, r, 3, d), lambda l: (l, 0, 0, 0)),
        compiler_params=_cparams(("parallel",), VMEM_LIMIT),
        name="ada",
    )(cc, ada_w, ada_b)


def _head_rms(x, seg_ones, w):
    ms = _dot((x * x).astype(BF16), seg_ones) * (1.0 / HEAD_DIM)
    return x * lax.rsqrt(ms + EPS) * w


def _rope(y, cos, sin_signed):
    lane = lax.broadcasted_iota(jnp.int32, y.shape, 1)
    first_half = (lane % 32) < 16
    n = y.shape[1]
    swapped = jnp.where(first_half, pltpu.roll(y, n - 16, 1), pltpu.roll(y, 16, 1))
    return y * cos + swapped * sin_signed


def _inproj_kernel(x_ref, ctx_ref, mod_ref, nw_ref, w_ref, cos_ref, sin_ref, hn_ref, seg_ref,
                   wdec_ref, bdec_ref,
                   a_ref, la_ref, bv_ref, cq_ref, ck_ref, cv_ref, dq_ref, dk_ref, dv_ref, g_ref,
                   xs_ref):
    i = pl.program_id(1)

    @pl.when(i == 0)
    def _():
        xs_ref[0:CTX_LEN, :] = ctx_ref[0]
        xs_ref[CTX_LEN:, :] = jnp.zeros((TM - CTX_LEN, xs_ref.shape[1]), F32)

    @pl.when(i > 0)
    def _():
        xs_ref[...] = x_ref[0]

    seg = seg_ref[...]
    half = TM // INPROJ_ROW_SPLIT
    hb = [None] * INPROJ_ROW_SPLIT

    def prologue(part):
        rs = slice(part * half, (part + 1) * half)
        x = xs_ref[rs, :]
        ms = jnp.mean(x * x, axis=-1, keepdims=True)
        y = x * lax.rsqrt(ms + EPS) * nw_ref[...]
        hb[part] = (y * (1.0 + mod_ref[0, 1:2, :]) + mod_ref[0, 0:1, :]).astype(BF16)
        yield

    def gla_qkv(part, rs):
        a = _dot(hb[part], w_ref[:, C_A:C_A + 512])
        yield
        a_ref[0, rs, 0:128] = a[:, 0:128] * (GLA_DK ** -0.5)
        a_ref[0, rs, 128:512] = a[:, 128:512]

    def plain(part, rs, c0, ref, width=256):
        r = _dot(hb[part], w_ref[:, c0:c0 + width])
        yield
        ref[0, rs, :] = r.astype(BF16)

    def normed(part, rs, c0, ref, row, rope, width=256):
        r = _dot(hb[part], w_ref[:, c0:c0 + width])
        yield
        y = _head_rms(r, seg[0:width, 0:width], hn_ref[row:row + 1, 0:width])
        ref[0, rs, :] = (_rope(y, cos_ref[rs, 0:width], sin_ref[rs, 0:width]) if rope else y).astype(BF16)

    def swa_kv(part, rs):
        r = _dot(hb[part], w_ref[:, C_CK:C_CK + 256])
        yield
        y = _head_rms(r[:, 0:128], seg[0:128, 0:128], hn_ref[1:2, 0:128])
        ck_ref[0, rs, :] = _rope(y, cos_ref[rs, 0:128], sin_ref[rs, 0:128]).astype(BF16)
        cv_ref[0, rs, :] = r[:, 128:256].astype(BF16)

    def swa_q(part, rs):
        r = _dot(hb[part], w_ref[:, C_CQ:C_CQ + 256])
        yield
        y = _rope(_head_rms(r, seg, hn_ref[0:1, :]), cos_ref[rs, :], sin_ref[rs, :])
        up = pltpu.roll(y, 64, 1)
        down = pltpu.roll(y, 192, 1)
        low = lax.broadcasted_iota(jnp.int32, (y.shape[0], 128), 1) < HEAD_DIM
        zero = jnp.zeros((y.shape[0], 128), F32)
        parts = [jnp.where(low, y[:, 0:128], zero), jnp.where(low, down[:, 0:128], zero),
                 jnp.where(low, zero, up[:, 128:256]), jnp.where(low, zero, y[:, 128:256])]
        cq_ref[0, rs, :] = jnp.concatenate(parts, axis=1).astype(BF16)

    def gates(part, rs, j):
        g = _dot(hb[part], w_ref[:, C_G + 256 * j:C_G + 256 * (j + 1)])
        yield
        g_ref[0, rs, 256 * j:256 * (j + 1)] = (g * (1.0 / (1.0 + jnp.exp(-g)))).astype(BF16)

    def decay(part, rs):
        lr = _dot(hb[part], w_ref[:, C_LR:C_LR + 128])
        yield
        z = _dot(lr.astype(BF16), wdec_ref[...]) + bdec_ref[...]
        la_ref[0, rs, :] = (jnp.minimum(z, 0.0) - jnp.log(1.0 + jnp.exp(-jnp.abs(z)))) * (LOG2E / GLA_TAU)

    def groups(part):
        rs = slice(part * half, (part + 1) * half)
        return [decay(part, rs), gates(part, rs, 0), swa_q(part, rs), gates(part, rs, 1), swa_kv(part, rs),
                gates(part, rs, 2), normed(part, rs, C_DQ, dq_ref, 2, False), gates(part, rs, 3),
                normed(part, rs, C_DK, dk_ref, 3, False),
                gla_qkv(part, rs), plain(part, rs, C_BV, bv_ref), plain(part, rs, C_DV, dv_ref)]

    units = [prologue(0)]
    for part in range(INPROJ_ROW_SPLIT):
        g = groups(part)
        if part + 1 < INPROJ_ROW_SPLIT:
            g.insert(INPROJ_GROUP, prologue(part + 1))
        units += g
    _skew(units, INPROJ_GROUP)


def _inproj(x, ctx, mod, norm_w, w_perm, layer, cos_t, sin_t, head_norms, seg_ones, wdec, bdec):
    b, seq, d = x.shape
    t_all = CTX_PAD + seq
    nblk = t_all // TM

    def tok(width):
        return pl.BlockSpec((1, TM, width), lambda bi, i: (bi, i, 0))

    def const(shape):
        return pl.BlockSpec(shape, lambda bi, i: (0,) * len(shape))

    def out(width, dtype):
        return jax.ShapeDtypeStruct((b, t_all, width), dtype)

    return pl.pallas_call(
        _inproj_kernel,
        out_shape=(out(512, F32), out(256, F32), out(256, BF16), out(512, BF16), out(128, BF16),
                   out(128, BF16), out(256, BF16), out(256, BF16), out(256, BF16), out(1024, BF16)),
        grid=(b, nblk),
        in_specs=[
            pl.BlockSpec((1, TM, d), lambda bi, i: (bi, jnp.maximum(i - 1, 0), 0)),
            pl.BlockSpec((1, CTX_LEN, d), lambda bi, i: (bi, 0, 0)),
            pl.BlockSpec((1, 3, d), lambda bi, i: (jnp.where(i == 0, b, bi), 0, 0)),
            const((1, d)),
            pl.BlockSpec((None, d, W_COLS), lambda bi, i: (layer, 0, 0)),
            pl.BlockSpec((TM, 256), lambda bi, i: (i, 0)),
            pl.BlockSpec((TM, 256), lambda bi, i: (i, 0)),
            const((8, 256)),
            const((256, 256)),
            const((128, 256)),
            const((1, 256)),
        ],
        out_specs=(tok(512), tok(256), tok(256), tok(512), tok(128), tok(128), tok(256), tok(256),
                   tok(256), tok(1024)),
        scratch_shapes=[pltpu.VMEM((TM, d), F32)],
        compiler_params=_cparams(("parallel", "arbitrary"), VMEM_LIMIT),
        name="inproj",
    )(x, ctx, mod, norm_w, w_perm, cos_t, sin_t, head_norms, seg_ones, wdec, bdec)


def _gla_kernel(af_ref, ab_ref, laf_ref, lab_ref, ones2_ref, trif_ref, trib_ref, pmf_ref, pmb_ref,
                qmask_ref, stmask_ref, of_ref, ob_ref, st_ref, rb_ref, *, nbatch, nreal):
    @pl.when(pl.program_id(0) == 0)
    def _():
        st_ref[...] = jnp.zeros_like(st_ref)

    @pl.when(pl.program_id(0) < nreal)
    def _():
        lane_head = lax.broadcasted_iota(jnp.int32, (GLA_CHUNK, GLA_HEADS * GLA_DV), 1) // GLA_DV
        shared = (qmask_ref[...], lane_head, stmask_ref[...])
        consts_f = (ones2_ref[...], trif_ref[...], pmf_ref[...]) + shared
        consts_b = (ones2_ref[...], trib_ref[...], pmb_ref[...]) + shared
        units = []
        for bi in range(nbatch):
            units.append(_gla_chunk(af_ref, laf_ref, of_ref, st_ref.at[0], rb_ref.at[0], bi, False, consts_f))
            units.append(_gla_chunk(ab_ref, lab_ref, ob_ref, st_ref.at[1], rb_ref.at[1], bi, True, consts_b))
        _interleave(units, group=GLA_GROUP)

    @pl.when(pl.program_id(0) >= nreal)
    def _():
        of_ref[...] = jnp.zeros_like(of_ref)
        ob_ref[...] = jnp.zeros_like(ob_ref)


def _gla_chunk(a_ref, la_ref, o_ref, st_ref, rb_ref, bi, rev, consts):
    ones2, tri, pmask, qmask, lane_head, stmask = consts
    c, sb = GLA_CHUNK, GLA_SUB
    nsub = c // sb
    hk = GLA_HEADS * GLA_DK
    hv = GLA_HEADS * GLA_DV

    q = a_ref[bi, :, 0:hk]
    k = a_ref[bi, :, hk:2 * hk]
    v = a_ref[bi, :, 2 * hk:2 * hk + hv]
    la = la_ref[bi]

    la_hi = la.astype(BF16)
    la_lo = (la - la_hi.astype(F32)).astype(BF16)
    cum = _dot(tri, la_hi) + _dot(tri, la_lo)
    cumx = cum - la

    rb_ref[bi, 0] = cum
    rb_ref[bi, 1] = cumx
    rb_ref[bi, 2] = k
    rb_ref[bi, 3] = v[:, 0:hk]
    rb_ref[bi, 4] = v[:, hk:hv]
    yield

    def brow(plane, r, n=sb):
        return jnp.broadcast_to(rb_ref[bi, plane, r:r + 1, :], (n, hk))

    end = 0 if rev else c - 1

    def first_row(blk):
        return sb * blk + (sb - 1 if rev else 0)

    def blk_of(p):
        return nsub - 1 - p if rev else p

    cref_rows = jnp.concatenate([brow(1, first_row(blk)) for blk in range(nsub)], axis=0)

    st = st_ref[bi]
    qi = q * jnp.exp2(cum)
    inter = _dot_nt(qi.astype(BF16), st.astype(BF16))

    qd = q * jnp.exp2(cum - cref_rows)
    q_st = jnp.concatenate([qd] * GLA_HEADS, axis=0) * qmask
    kt, vt = [], []
    for p in range(1, nsub):
        lo, hi = (c - sb * p, c) if rev else (0, sb * p)
        kt.append(k[lo:hi] * jnp.exp2(brow(1, first_row(blk_of(p)), hi - lo) - cum[lo:hi]))
        vt.append(v[lo:hi])
    npad = pmask.shape[1] - sb * (nsub * (nsub - 1) // 2)
    kt.append(jnp.zeros((npad, hk), F32))
    vt.append(jnp.zeros((npad, hv), F32))
    kt = jnp.concatenate(kt, axis=0).astype(BF16)
    vt = jnp.concatenate(vt, axis=0).astype(BF16)
    sc = _dot_nt(q_st.astype(BF16), kt)
    cum_end = rb_ref[bi, 0, end:end + 1, :]
    kd = k * jnp.exp2(cum_end - cum)
    kv = _dot_tn(v.astype(BF16), kd.astype(BF16))

    t_loc = lax.broadcasted_iota(jnp.int32, (sb, hk), 0)
    valid = [(t_loc <= s) if rev else (t_loc >= s) for s in range(sb)]
    es = []
    for blk in range(nsub):
        r0 = sb * blk
        cb = cum[r0:r0 + sb]
        qb = q[r0:r0 + sb]
        for s in range(sb):
            diff = jnp.where(valid[s], cb - brow(0, r0 + s), -jnp.inf)
            es.append(jnp.exp2(diff) * (qb * brow(2, r0 + s)))
    e = jnp.concatenate(es, axis=0).astype(BF16)
    r = _dot(e, ones2)
    yield

    off_st = _dot(sc.astype(BF16) * pmask, vt)
    dgs = []
    for blk in range(nsub):
        acc = None
        for s in range(sb):
            row0 = (blk * sb + s) * sb
            vrow = jnp.concatenate([brow(3, blk * sb + s), brow(4, blk * sb + s)], axis=1)
            term = r[row0:row0 + sb] * vrow
            acc = term if acc is None else acc + term
        dgs.append(acc)
    st_ref[bi] = st * jnp.exp2(cum_end) + kv * stmask
    yield

    off = off_st[(GLA_HEADS - 1) * c:GLA_HEADS * c]
    for h in range(GLA_HEADS - 2, -1, -1):
        off = jnp.where(lane_head == h, off_st[h * c:(h + 1) * c], off)
    o_ref[bi] = (inter + off + jnp.concatenate(dgs, axis=0)).astype(BF16)


def _gla_consts(rev):
    c, sb = GLA_CHUNK, GLA_SUB
    nsub = c // sb
    t = np.arange(c)
    tri = (t[None, :] >= t[:, None]) if rev else (t[None, :] <= t[:, None])
    blk = t // sb
    p_of_t = (nsub - 1 - blk) if rev else blk
    col_p = np.concatenate([np.full(sb * p, p) for p in range(1, nsub)])
    col_p = np.concatenate([col_p, np.full(-col_p.size % 128, -1)])
    pmask = np.tile(p_of_t, GLA_HEADS)[:, None] == col_p[None, :]
    row_h = np.repeat(np.arange(GLA_HEADS), c)
    qmask = row_h[:, None] == (np.arange(GLA_HEADS * GLA_DK) // GLA_DK)[None, :]
    stmask = (np.arange(GLA_HEADS * GLA_DV) // GLA_DV)[:, None] == (np.arange(GLA_HEADS * GLA_DK) // GLA_DK)[None, :]
    f = lambda m: jnp.asarray(m, F32)
    return f(tri).astype(BF16), f(pmask).astype(BF16), f(qmask), f(stmask)


def _gla(a, la, ones2):
    b, t_all, _ = a.shape
    nch = t_all // GLA_CHUNK
    nctx = CTX_LEN // GLA_CHUNK
    nskip = (CTX_PAD - CTX_LEN) // GLA_CHUNK
    tri_f, pm_f, qmask, stmask = _gla_consts(False)
    tri_b, pm_b, _, _ = _gla_consts(True)
    consts = (ones2, tri_f, tri_b, pm_f, pm_b, qmask, stmask)

    nreal = nch - nskip

    def fwd(j):
        return jnp.where(j < nctx, j, jnp.where(j < nreal, j + nskip, nctx + (j - nreal)))

    def bwd(j):
        return jnp.where(j < nctx, nctx - 1 - j, jnp.where(j < nreal, nch - 1 - (j - nctx), nctx + (j - nreal)))

    out = jax.ShapeDtypeStruct((b, t_all, 256), BF16)
    return pl.pallas_call(
        functools.partial(_gla_kernel, nbatch=b, nreal=nreal),
        out_shape=(out, out),
        grid=(nch,),
        in_specs=[pl.BlockSpec((b, GLA_CHUNK, 512), lambda j: (0, fwd(j), 0)),
                  pl.BlockSpec((b, GLA_CHUNK, 512), lambda j: (0, bwd(j), 0)),
                  pl.BlockSpec((b, GLA_CHUNK, 128), lambda j: (0, fwd(j), 0)),
                  pl.BlockSpec((b, GLA_CHUNK, 128), lambda j: (0, bwd(j), 1))]
                 + [pl.BlockSpec(t.shape, lambda j: (0, 0)) for t in consts],
        out_specs=(pl.BlockSpec((b, GLA_CHUNK, 256), lambda j: (0, fwd(j), 0)),
                   pl.BlockSpec((b, GLA_CHUNK, 256), lambda j: (0, bwd(j), 0))),
        scratch_shapes=[pltpu.VMEM((2, b, 256, 128), F32), pltpu.VMEM((2, b, 5, GLA_CHUNK, 128), F32)],
        compiler_params=_cparams(("arbitrary",)),
        name="gla",
    )(a, a, la, la, *consts)


def _fnet_chan_kernel(v_ref, wc_ref, ws_ref, o_ref):
    v = v_ref[0]
    o_ref[0, :, :] = _dot(v, wc_ref[...]).astype(BF16)
    o_ref[1, :, :] = _dot(v, ws_ref[...]).astype(BF16)


def _fnet_chan(bv, wc, ws, row_blk_off, length):
    b = bv.shape[0]
    tl = min(TM, length)
    return pl.pallas_call(
        _fnet_chan_kernel,
        out_shape=jax.ShapeDtypeStruct((2, length, b * 256), BF16),
        grid=(b, length // tl),
        in_specs=[pl.BlockSpec((1, tl, 256), lambda bi, i: (bi, i + row_blk_off, 0)),
                  pl.BlockSpec((256, 256), lambda bi, i: (0, 0)),
                  pl.BlockSpec((256, 256), lambda bi, i: (0, 0))],
        out_specs=pl.BlockSpec((2, tl, 256), lambda bi, i: (0, i, bi)),
        compiler_params=_cparams(("parallel", "arbitrary")),
        name="fnet_chan",
    )(bv, wc, ws)


def _fnet_pos_kernel(d_ref, y_ref, wf_ref, o_ref, acc_ref, *, nb):
    kk = pl.program_id(1)

    @pl.when(kk == 0)
    def _():
        acc_ref[...] = jnp.zeros_like(acc_ref)

    acc_ref[...] += _dot(d_ref[...], y_ref[...])

    @pl.when(kk == pl.num_programs(1) - 1)
    def _():
        wf = wf_ref[...]
        for bi in range(nb):
            o_ref[bi] = _dot(acc_ref[:, bi * 256:(bi + 1) * 256].astype(BF16), wf).astype(BF16)


def _fnet_pos(dmat, y2, wf, nb):
    length, k2 = dmat.shape
    tm = min(512, length)
    tk = min(1024, k2)
    return pl.pallas_call(
        functools.partial(_fnet_pos_kernel, nb=nb),
        out_shape=jax.ShapeDtypeStruct((nb, length, 256), BF16),
        grid=(length // tm, k2 // tk),
        in_specs=[pl.BlockSpec((tm, tk), lambda i, kk: (i, kk)),
                  pl.BlockSpec((tk, nb * 256), lambda i, kk: (kk, 0)),
                  pl.BlockSpec((256, 256), lambda i, kk: (0, 0))],
        out_specs=pl.BlockSpec((nb, tm, 256), lambda i, kk: (0, i, 0)),
        scratch_shapes=[pltpu.VMEM((tm, nb * 256), F32)],
        compiler_params=_cparams(("parallel", "arbitrary"), VMEM_LIMIT),
        name="fnet_pos",
    )(dmat, y2, wf)


def _dft_tables(length):
    j = np.arange(length)
    ang = ((j[:, None] * j[None, :]) % length) * (2.0 * np.pi / length)
    s = 1.0 / math.sqrt(length)
    return jnp.asarray(np.concatenate([np.cos(ang) * s, -np.sin(ang) * s], axis=1), F32).astype(BF16)


def _chan_tables():
    j = np.arange(FNET_GW)
    ang = 2.0 * np.pi * ((j[:, None] * j[None, :]) % FNET_GW) / FNET_GW
    s = 1.0 / math.sqrt(FNET_GW)
    eye = np.eye(BRANCH_W // FNET_GW)
    wc = np.kron(eye, np.cos(ang) * s)
    ws = np.kron(eye, np.sin(ang) * s)
    return jnp.asarray(wc, F32).astype(BF16), jnp.asarray(ws, F32).astype(BF16)


def _fnet(bv, wf, row_blk_off, length):
    b = bv.shape[0]
    wc, ws = _chan_tables()
    y = _fnet_chan(bv, wc, ws, row_blk_off, length)
    y2 = y.reshape(2 * length, b * 256)
    return _fnet_pos(_dft_tables(length), y2, wf, b)


FNET_UNROLL = 16


def _fnet_grid_kernel(v_ref, cg_ref, sg_ref, w1_ref, w2_ref, wf_ref, o_ref, z_ref, g_ref, f_ref, *, rows, row_off):
    length = rows * GRID_W
    v = v_ref[0, row_off:row_off + length, :]
    wf = wf_ref[...]
    zr = _dot(v, _dot(cg_ref[...], wf).astype(BF16))
    zi = _dot(v, _dot(sg_ref[...], wf).astype(BF16))
    z_ref[0] = zr[:, 0:128].reshape(rows, GRID_W, 128)
    z_ref[1] = zr[:, 128:256].reshape(rows, GRID_W, 128)
    z_ref[2] = zi[:, 0:128].reshape(rows, GRID_W, 128)
    z_ref[3] = zi[:, 128:256].reshape(rows, GRID_W, 128)

    n8 = rows * 8
    for w0 in range(0, GRID_W, 8):
        def slab(c):
            return z_ref[c, :, w0:w0 + 8, :].reshape(n8, 128)
        rhs = jnp.concatenate([jnp.concatenate([slab(0), slab(1)], axis=1),
                               jnp.concatenate([slab(2), slab(3)], axis=1)], axis=0).astype(BF16)
        g = _dot(w1_ref[...], rhs)
        g_ref[0, :, w0:w0 + 8, :] = g[0:n8, 0:128].reshape(rows, 8, 128)
        g_ref[1, :, w0:w0 + 8, :] = g[0:n8, 128:256].reshape(rows, 8, 128)
        g_ref[2, :, w0:w0 + 8, :] = g[n8:2 * n8, 0:128].reshape(rows, 8, 128)
        g_ref[3, :, w0:w0 + 8, :] = g[n8:2 * n8, 128:256].reshape(rows, 8, 128)

    def stage_w(ka, carry):
        r0 = pl.multiple_of(ka * GRID_W, GRID_W)
        rhs = jnp.concatenate([jnp.concatenate([g_ref[0, ka], g_ref[1, ka]], axis=1),
                               jnp.concatenate([g_ref[2, ka], g_ref[3, ka]], axis=1)], axis=0).astype(BF16)
        y = _dot(w2_ref[pl.ds(r0, GRID_W), :], rhs)
        f_ref[0, pl.ds(ka, GRID_W, stride=rows), :] = y[:, 0:128]
        f_ref[1, pl.ds(ka, GRID_W, stride=rows), :] = y[:, 128:256]
        return carry

    lax.fori_loop(0, rows, stage_w, 0, unroll=FNET_UNROLL)
    o_ref[0, :, 0:128] = f_ref[0].astype(BF16)
    o_ref[0, :, 128:256] = f_ref[1].astype(BF16)


def _fnet_grid_tables(rows):
    length = rows * GRID_W
    j = np.arange(rows)
    ang = 2.0 * np.pi * ((j[:, None] * j[None, :]) % rows) / rows
    c, s = np.cos(ang), np.sin(ang)
    w1 = np.kron(np.block([[c, -s], [-s, -c]]), np.eye(8))
    w1 = jnp.asarray(w1, F32).astype(BF16)
    ka = np.arange(rows)[:, None, None]
    kb = np.arange(GRID_W)[None, :, None]
    w = np.arange(GRID_W)[None, None, :]
    ang2 = ((w * (ka + rows * kb)) % length) * (2.0 * np.pi / length)
    scale = 1.0 / math.sqrt(length)
    w2 = np.concatenate([np.cos(ang2) * scale, np.sin(ang2) * scale], axis=2)
    return w1, jnp.asarray(w2.reshape(length, 2 * GRID_W), F32).astype(BF16)


def _fnet_grid(bv, wf, seq):
    b, t_all, _ = bv.shape
    rows = seq // GRID_W
    wc, ws = _chan_tables()
    w1, w2 = _fnet_grid_tables(rows)

    def const(shape):
        return pl.BlockSpec(shape, lambda bi: (0,) * len(shape))

    return pl.pallas_call(
        functools.partial(_fnet_grid_kernel, rows=rows, row_off=CTX_PAD),
        out_shape=jax.ShapeDtypeStruct((b, seq, 256), BF16),
        grid=(b,),
        in_specs=[pl.BlockSpec((1, t_all, 256), lambda bi: (bi, 0, 0)), const((256, 256)), const((256, 256)),
                  const((16 * rows, 16 * rows)), const((seq, 2 * GRID_W)), const((256, 256))],
        out_specs=pl.BlockSpec((1, seq, 256), lambda bi: (bi, 0, 0)),
        scratch_shapes=[pltpu.VMEM((4, rows, GRID_W, 128), F32), pltpu.VMEM((4, rows, GRID_W, 128), F32),
                        pltpu.VMEM((2, seq, 128), F32)],
        compiler_params=_cparams(("parallel",), VMEM_LIMIT),
        name="fnet_grid",
    )(bv, wc, ws, w1, w2, wf)


def _stack_heads(qg):
    lane = lax.broadcasted_iota(jnp.int32, qg.shape, 1)
    zero = jnp.zeros_like(qg)
    return jnp.concatenate([jnp.where(lane < HEAD_DIM, qg, zero),
                            jnp.where(lane >= HEAD_DIM, qg, zero)], axis=0)


def _gqa_unstack(o, rows, g):
    lane = lax.broadcasted_iota(jnp.int32, (rows, 2 * HEAD_DIM), 1)
    first, second = o[0:rows], o[rows:2 * rows]
    if g == 0:
        return jnp.where(lane < HEAD_DIM, first, pltpu.roll(second, HEAD_DIM, 1))
    return jnp.where(lane < HEAD_DIM, pltpu.roll(first, HEAD_DIM, 1), second)


def _unstack_heads(o, rows):
    lane = lax.broadcasted_iota(jnp.int32, (rows, 2 * HEAD_DIM), 1)
    return jnp.where(lane < HEAD_DIM, o[0:rows], o[rows:2 * rows])


def _swa_kernel(sink_ref, band_ref, q_ref, kp_ref, kc_ref, kn_ref, kx_ref, vp_ref, vc_ref, vn_ref, vx_ref,
                o_ref, *, nbatch):
    n = pl.program_id(0)
    nb = pl.num_programs(0)
    blk = SWA_BLOCK
    rows = 2 * blk
    col = lax.broadcasted_iota(jnp.int32, (1, 3 * blk), 1)
    col_lo = jnp.where(n > 0, 0, blk)
    col_hi = jnp.where(n < nb - 1, 3 * blk, 2 * blk)
    edge = jnp.where((col >= col_lo) & (col < col_hi), 0.0, -jnp.inf)
    bias = band_ref[...] + edge
    head_row = lax.broadcasted_iota(jnp.int32, (rows, 1), 0) < blk

    def unit(bi, g):
        q_st = jnp.concatenate([q_ref[bi, :, 256 * g:256 * g + 128],
                                q_ref[bi, :, 256 * g + 128:256 * g + 256]], axis=0)
        kw = jnp.concatenate([kp_ref[bi], kc_ref[bi], kn_ref[bi]], axis=0)
        s_loc = _dot_nt(q_st, kw) + bias
        s_ctx = _dot_nt(q_st, kx_ref[bi])
        yield
        vw = jnp.concatenate([vp_ref[bi], vc_ref[bi], vn_ref[bi]], axis=0)
        sink = jnp.where(head_row, sink_ref[2 * g], sink_ref[2 * g + 1])
        m = jnp.maximum(jnp.maximum(jnp.max(s_loc, axis=-1, keepdims=True),
                                    jnp.max(s_ctx, axis=-1, keepdims=True)), sink)
        p_loc = jnp.exp2(s_loc - m)
        p_ctx = jnp.exp2(s_ctx - m)
        den = (jnp.sum(p_loc, axis=-1, keepdims=True) + jnp.sum(p_ctx, axis=-1, keepdims=True)
               + jnp.exp2(sink - m))
        o = _dot(p_loc.astype(BF16), vw) + _dot(p_ctx.astype(BF16), vx_ref[bi])
        yield
        o_ref[bi, :, 128 * g:128 * (g + 1)] = _gqa_unstack(o / den, blk, g).astype(BF16)

    _interleave([unit(bi, g) for bi in range(nbatch) for g in range(2)], group=ATTN_GROUP)


def _swa_band():
    row = np.arange(2 * SWA_BLOCK)[:, None] % SWA_BLOCK
    col = np.arange(3 * SWA_BLOCK)[None, :]
    ok = np.abs(col - SWA_BLOCK - row) <= SWA_WINDOW
    return jnp.where(jnp.asarray(ok), 0.0, -jnp.inf).astype(F32)


def _swa(sink, cq, ck, cv, seq):
    b = cq.shape[0]
    nb = seq // SWA_BLOCK
    off = CTX_PAD // SWA_BLOCK

    def at(f, width=128):
        return pl.BlockSpec((b, SWA_BLOCK, width), lambda n: (0, f(n) + off, 0))

    prev = at(lambda n: jnp.maximum(n - 1, 0))
    cur = at(lambda n: n)
    nxt = at(lambda n: jnp.minimum(n + 1, nb - 1))
    cx = pl.BlockSpec((b, CTX_LEN, 128), lambda n: (0, 0, 0))
    band = pl.BlockSpec((2 * SWA_BLOCK, 3 * SWA_BLOCK), lambda n: (0, 0))
    return pl.pallas_call(
        functools.partial(_swa_kernel, nbatch=b),
        out_shape=jax.ShapeDtypeStruct((b, seq, 256), BF16),
        grid=(nb,),
        in_specs=[pl.BlockSpec(memory_space=pltpu.SMEM), band, at(lambda n: n, 512),
                  prev, cur, nxt, cx, prev, cur, nxt, cx],
        out_specs=pl.BlockSpec((b, SWA_BLOCK, 256), lambda n: (0, n, 0)),
        compiler_params=_cparams(("arbitrary",)),
        name="swa",
    )(sink, _swa_band(), cq, ck, ck, ck, ck, cv, cv, cv, cv)


NA_ROWS_PER_STEP = 8


def _na_kernel(q_ref, k_ref, v_ref, bias_ref, o_ref, *, rows_total):
    kh = NA_KH
    nk = kh * GRID_W
    def unit(rr, g):
        r = pl.program_id(1) * NA_ROWS_PER_STEP + rr
        row_start = jnp.clip(r - kh // 2, 0, rows_total - kh)
        cls = r - row_start
        k0 = pl.multiple_of(CTX_PAD + row_start * GRID_W, GRID_W)
        qs = slice(GRID_W * rr, GRID_W * (rr + 1))
        ls = slice(128 * g, 128 * (g + 1))
        q_st = _stack_heads(q_ref[0, qs, ls])
        s_nb = _dot_nt(q_st, k_ref[0, pl.ds(k0, nk), ls]) + bias_ref[cls, g]
        s_ctx = _dot_nt(q_st, k_ref[0, 0:CTX_LEN, ls])
        yield
        m = jnp.maximum(jnp.max(s_nb, axis=-1, keepdims=True), jnp.max(s_ctx, axis=-1, keepdims=True))
        p_nb = jnp.exp2(s_nb - m)
        p_ctx = jnp.exp2(s_ctx - m)
        den = jnp.sum(p_nb, axis=-1, keepdims=True) + jnp.sum(p_ctx, axis=-1, keepdims=True)
        o = (_dot(p_nb.astype(BF16), v_ref[0, pl.ds(k0, nk), ls])
             + _dot(p_ctx.astype(BF16), v_ref[0, 0:CTX_LEN, ls]))
        yield
        o_ref[0, qs, ls] = _unstack_heads(o / den, GRID_W).astype(BF16)

    _interleave([unit(rr, g) for rr in range(NA_ROWS_PER_STEP) for g in range(2)], group=ATTN_GROUP)


def _bias_expand_kernel(rb_ref, oh_ref, o_ref):
    x = rb_ref[...]
    oh = oh_ref[...]
    hi = x.astype(BF16)
    r1 = x - hi.astype(F32)
    mid = r1.astype(BF16)
    lo = (r1 - mid.astype(F32)).astype(BF16)
    o_ref[...] = (_dot(hi, oh) + _dot(mid, oh) + _dot(lo, oh)) * LOG2E


def _na_bias_table(rel_bias):
    ndy, ndx = 2 * NA_KH - 1, 2 * NA_KW - 1
    cq = np.arange(GRID_W)
    col_start = np.clip(cq - NA_KW // 2, 0, GRID_W - NA_KW)
    col_ok = (cq[None, :] >= col_start[:, None]) & (cq[None, :] < col_start[:, None] + NA_KW)
    qi = jnp.arange(GRID_W, dtype=jnp.int32)
    dx = jnp.clip(qi[None, :] - qi[:, None], -(NA_KW - 1), NA_KW - 1) + (NA_KW - 1)
    onehot = (jnp.arange(128, dtype=jnp.int32)[:, None] == dx.reshape(1, GRID_W * GRID_W)).astype(BF16)
    rb = jnp.zeros((64, 128), F32).at[0:NA_HEADS * ndy, 0:ndx].set(rel_bias.reshape(NA_HEADS * ndy, ndx))
    p = pl.pallas_call(
        _bias_expand_kernel,
        out_shape=jax.ShapeDtypeStruct((64, GRID_W * GRID_W), F32),
        name="na_bias_expand",
    )(rb, onehot)
    p = p[0:NA_HEADS * ndy].reshape(NA_HEADS, ndy, GRID_W, GRID_W)
    p = jnp.where(col_ok[None, None], p, -jnp.inf)
    tabs = jnp.stack([p[:, NA_KH - 1 - c:2 * NA_KH - 1 - c] for c in range(NA_KH)])
    tabs = tabs.transpose(0, 1, 3, 2, 4)
    return tabs.reshape(NA_KH, NA_HEADS // 2, 2 * GRID_W, NA_KH * GRID_W)


def _na(dq, dk, dv, bias_tab, seq):
    b, t_all, _ = dq.shape
    rows_total = seq // GRID_W
    qrows = NA_ROWS_PER_STEP * GRID_W
    assert CTX_PAD % qrows == 0 and rows_total % NA_ROWS_PER_STEP == 0
    off = CTX_PAD // qrows
    full = pl.BlockSpec((1, t_all, 256), lambda bi, r: (bi, 0, 0))
    return pl.pallas_call(
        functools.partial(_na_kernel, rows_total=rows_total),
        out_shape=jax.ShapeDtypeStruct((b, seq, 256), BF16),
        grid=(b, rows_total // NA_ROWS_PER_STEP),
        in_specs=[pl.BlockSpec((1, qrows, 256), lambda bi, r: (bi, r + off, 0)), full, full,
                  pl.BlockSpec(bias_tab.shape, lambda bi, r: (0, 0, 0, 0))],
        out_specs=pl.BlockSpec((1, qrows, 256), lambda bi, r: (bi, r, 0)),
        compiler_params=_cparams(("parallel", "arbitrary"), VMEM_LIMIT),
        name="na",
    )(dq, dk, dv, bias_tab)


def _ctx_attn_kernel(sink_ref, q_ref, k_ref, v_ref, o_ref, *, gqa):
    n = CTX_LEN
    head_row = lax.broadcasted_iota(jnp.int32, (2 * n, 1), 0) < n
    for g in range(2):
        ls = slice(128 * g, 128 * (g + 1))
        if gqa:
            q_st = jnp.concatenate([q_ref[0, :, 256 * g:256 * g + 128],
                                    q_ref[0, :, 256 * g + 128:256 * g + 256]], axis=0)
            k, v = k_ref[0], v_ref[0]
        else:
            q_st = _stack_heads(q_ref[0, :, ls])
            k, v = k_ref[0, :, ls], v_ref[0, :, ls]
        s = _dot_nt(q_st, k)
        m = jnp.max(s, axis=-1, keepdims=True)
        if gqa:
            sink = jnp.where(head_row, sink_ref[2 * g], sink_ref[2 * g + 1])
            m = jnp.maximum(m, sink)
        p = jnp.exp2(s - m)
        den = jnp.sum(p, axis=-1, keepdims=True)
        if gqa:
            den = den + jnp.exp2(sink - m)
        o = _dot(p.astype(BF16), v) / den
        o_ref[0, :, ls] = (_gqa_unstack(o, n, g) if gqa else _unstack_heads(o, n)).astype(BF16)


def _ctx_attn(sink, q, k, v, gqa):
    b = q.shape[0]

    def blk(arr):
        return pl.BlockSpec((1, CTX_LEN, arr.shape[2]), lambda bi: (bi, 0, 0))

    return pl.pallas_call(
        functools.partial(_ctx_attn_kernel, gqa=gqa),
        out_shape=jax.ShapeDtypeStruct((b, CTX_LEN, 256), BF16),
        grid=(b,),
        in_specs=[pl.BlockSpec(memory_space=pltpu.SMEM), blk(q), blk(k), blk(v)],
        out_specs=pl.BlockSpec((1, CTX_LEN, 256), lambda bi: (bi, 0, 0)),
        compiler_params=_cparams(("parallel",)),
        name="ctx_attn_sink" if gqa else "ctx_attn",
    )(sink, q, k, v)


def _outproj_kernel(x_ref, of_ref, ob_ref, fz_ref, co_ref, do_ref, g_ref, mod_ref, gn_ref, seg_ref,
                    w_ref, o_ref):
    a = of_ref[0].astype(F32) + ob_ref[0].astype(F32)
    ms = _dot((a * a).astype(BF16), seg_ref[...]) * (1.0 / GLA_DV)
    a = a * lax.rsqrt(ms + EPS) * gn_ref[...]
    g = g_ref[0].astype(F32)
    mixed =jnp.concatenate([a * g[:, 0:256], fz_ref[0].astype(F32) * g[:, 256:512],
                             co_ref[0].astype(F32) * g[:, 512:768], do_ref[0].astype(F32) * g[:, 768:1024]],
                            axis=1)
    y = _dot(mixed.astype(BF16), w_ref[...])
    o_ref[0] = x_ref[0] + mod_ref[0, 2:3, :] * y


def _outproj(x, o_f, o_b, fz, c_o, d_o, g, mod, gla_norm, seg_ones, w_out, is_ctx):
    b, rows, d = x.shape
    tm = CTX_LEN if is_ctx else TM
    nblk = rows // tm
    comb = 0 if is_ctx else CTX_PAD // tm
    nbatch = b

    def own(width):
        return pl.BlockSpec((1, tm, width), lambda bi, i: (bi, i, 0))

    def combined(width):
        return pl.BlockSpec((1, tm, width), lambda bi, i: (bi, i + comb, 0))

    def const(shape):
        return pl.BlockSpec(shape, lambda bi, i: (0,) * len(shape))

    mod_spec = pl.BlockSpec((1, 3, d), (lambda bi, i: (nbatch, 0, 0)) if is_ctx else (lambda bi, i: (bi, 0, 0)))
    return pl.pallas_call(
        _outproj_kernel,
        out_shape=jax.ShapeDtypeStruct((b, rows, d), F32),
        grid=(b, nblk),
        in_specs=[own(d), combined(256), combined(256), own(256), own(256), own(256), combined(1024),
                  mod_spec, const((1, 256)), const((256, 256)), const((d, d))],
        out_specs=own(d),
        compiler_params=_cparams(("parallel", "arbitrary"), VMEM_LIMIT),
        name="outproj_ctx" if is_ctx else "outproj",
    )(x, o_f, o_b, fz, c_o, d_o, g, mod, gla_norm, seg_ones, w_out)


_W_IN_MOVES = (
    (C_A, 0, 512), (C_BV, 800, 256), (C_CQ, 1312, 256),
    (C_CK, 1568, 128), (C_CV, 1696, 128),
    (C_DQ, 2080, 256), (C_DK, 2336, 256), (C_DV, 2592, 256),
    (C_G, 512, 256), (C_G + 256, 1056, 256), (C_G + 512, 1824, 256), (C_G + 768, 2848, 256),
    (C_LR, 768, 2 * GLA_RANK),
)


def _w_in_prep_kernel(wt_ref, o_ref):
    for dst, src, width in _W_IN_MOVES:
        o_ref[0, :, dst:dst + width] = wt_ref[0, src:src + width, :].T.astype(BF16)
    pad0 = C_LR + 2 * GLA_RANK
    o_ref[0, :, pad0:W_COLS] = jnp.zeros((o_ref.shape[1], W_COLS - pad0), BF16)


def _permute_w_in(w_all):
    depth, d, n = w_all.shape
    kblk = 256
    return pl.pallas_call(
        _w_in_prep_kernel,
        out_shape=jax.ShapeDtypeStruct((depth, d, W_COLS), BF16),
        grid=(depth, d // kblk),
        in_specs=[pl.BlockSpec((1, n, kblk), lambda l, i: (l, 0, i))],
        out_specs=pl.BlockSpec((1, kblk, W_COLS), lambda l, i: (l, i, 0)),
        compiler_params=_cparams(("parallel", "parallel")),
        name="w_in_prep",
    )(jnp.swapaxes(w_all, 1, 2))


def _rope_tables(seq):
    t = np.arange(seq)
    axis_dim = HEAD_DIM // 2
    inv = ROPE_BASE ** (-np.arange(0, axis_dim, 2, dtype=np.float64) / axis_dim)
    row = (t // GRID_W).astype(np.float64)
    col = (t % GRID_W).astype(np.float64)
    ang = np.concatenate([row[:, None] * inv, row[:, None] * inv, col[:, None] * inv, col[:, None] * inv], axis=1)
    cos = np.cos(ang)
    sign = np.concatenate([-np.ones(16), np.ones(16), -np.ones(16), np.ones(16)])
    sin = np.sin(ang) * sign
    cos = np.concatenate([np.ones((CTX_PAD, HEAD_DIM)), cos], axis=0)
    sin = np.concatenate([np.zeros((CTX_PAD, HEAD_DIM)), sin], axis=0)
    return (jnp.asarray(np.tile(cos, (1, 4)), F32), jnp.asarray(np.tile(sin, (1, 4)), F32))


def _seg_ones(width, seg):
    i = np.arange(width)
    return jnp.asarray((i[:, None] // seg) == (i[None, :] // seg), BF16)


def _gla_ones2():
    r = np.arange(GLA_HEADS * GLA_DK) // GLA_DK
    c = np.arange(GLA_HEADS * GLA_DV) // GLA_DV
    return jnp.asarray(r[:, None] == c[None, :], BF16)


def _layer(x, ctx, mod, need_ctx, tables, norm_w, w_in_perm, layer, w_out, gla_dec_w, gla_dec_b,
           gla_out_norm, fnet_w, swa_q_norm, swa_k_norm, swa_sink, na_q_norm, na_k_norm, na_rel_bias):
    b, seq, d = x.shape
    cos_t, sin_t, seg64, ones2 = tables

    qs = HEAD_DIM ** -0.5 * LOG2E
    swa_sink = swa_sink * LOG2E
    head_norms = jnp.stack([jnp.tile(swa_q_norm, 4) * qs, jnp.tile(swa_k_norm, 4),
                            jnp.tile(na_q_norm, 4) * qs, jnp.tile(na_k_norm, 4)])
    head_norms = jnp.concatenate([head_norms, jnp.zeros((4, 256), F32)], axis=0)
    wdec = jnp.zeros((128, 256), F32)
    wdec = wdec.at[0:GLA_RANK, 0:128].set(gla_dec_w[0]).at[GLA_RANK:2 * GLA_RANK, 128:256].set(gla_dec_w[1])
    bdec = gla_dec_b.reshape(1, 256)

    a, la, bv, cq, ck, cv, dq, dk, dv, g = _inproj(
        x, ctx, mod, norm_w[None, :], w_in_perm, layer, cos_t, sin_t, head_norms, seg64,
        wdec.astype(BF16), bdec)

    o_f, o_b = _gla(a, la, ones2)
    wf = fnet_w.astype(BF16)
    fz = _fnet_grid(bv, wf, seq)
    c_o = _swa(swa_sink, cq, ck, cv, seq)
    d_o = _na(dq, dk, dv, _na_bias_table(na_rel_bias), seq)

    gn = jnp.tile(gla_out_norm, 4)[None, :]
    wo = w_out.astype(BF16)
    x_new = _outproj(x, o_f, o_b, fz, c_o, d_o, g, mod, gn, seg64, wo, is_ctx=False)
    ctx_new = None
    if need_ctx:
        fzc = _fnet(bv, wf, 0, CTX_LEN)
        c_oc = _ctx_attn(swa_sink, cq, ck, cv, gqa=True)
        d_oc = _ctx_attn(swa_sink, dq, dk, dv, gqa=False)
        ctx_new = _outproj(ctx, o_f, o_b, fzc, c_oc, d_oc, g, mod, gn, seg64, wo, is_ctx=True)
    return x_new, ctx_new


def kernel(x, c, ctx, c_ctx, norm_w, ada_w, ada_b, w_in, w_out, gla_dec_w, gla_dec_b, gla_out_norm,
           fnet_w, swa_q_norm, swa_k_norm, swa_sink, na_q_norm, na_k_norm, na_rel_bias):
    b, seq, d = x.shape
    depth = norm_w.shape[0]
    assert ctx.shape[1] == CTX_LEN and seq % TM == 0 and seq % (NA_KH * GRID_W) == 0
    tables = _rope_tables(seq) + (_seg_ones(256, HEAD_DIM), _gla_ones2())
    cc = jnp.concatenate([c, c_ctx[None, :], jnp.zeros((8 - (b + 1) % 8, d), F32)], axis=0)
    mod = _ada(cc, ada_w, ada_b[:, None, :])
    w_in_perm = _permute_w_in(w_in)
    for i in range(depth):
        x, ctx = _layer(x, ctx, mod[i], i < depth - 1, tables, norm_w[i], w_in_perm, i, w_out[i],
                        gla_dec_w[i], gla_dec_b[i], gla_out_norm[i], fnet_w[i], swa_q_norm[i],
                        swa_k_norm[i], swa_sink[i], na_q_norm[i], na_k_norm[i], na_rel_bias[i])
    return x
```

```python
import functools
import math

import numpy as np
import jax
import jax.numpy as jnp
from jax import lax
from jax.experimental import pallas as pl
from jax.experimental.pallas import tpu as pltpu

F32 = jnp.float32
BF16 = jnp.bfloat16

GRID_W = 64
CTX_LEN = 256
HEAD_DIM = 64
BRANCH_W = 256
EPS = 1e-6
ROPE_BASE = 10000.0
GLA_HEADS = 4
GLA_DK = 32
GLA_DV = 64
GLA_RANK = 16
GLA_TAU = 16.0
GLA_CHUNK = 64
GLA_SUB = 8
LOG2E = 1.4426950408889634
FNET_GW = 64
SWA_BLOCK = 128
SWA_WINDOW = 128
NA_KH = 8
NA_KW = 16
NA_HEADS = 4

TM = 512
CTX_PAD = TM
VMEM_LIMIT = 48 * 1024 * 1024

C_A = 0
C_BV = 512
C_CQ = 768
C_CK = 1024
C_CV = 1152
C_DQ = 1280
C_DK = 1536
C_DV = 1792
C_G = 2048
C_LR = 3072
W_COLS = 3200


def _dot(a, b):
    return jnp.dot(a, b, preferred_element_type=F32)


def _dot_nt(a, b):
    return lax.dot_general(a, b, (((1,), (1,)), ((), ())), preferred_element_type=F32)


def _dot_tn(a, b):
    return lax.dot_general(a, b, (((0,), (0,)), ((), ())), preferred_element_type=F32)


GLA_GROUP = 8
INPROJ_ROW_SPLIT = 2
INPROJ_GROUP = 4
ATTN_GROUP = 4
_DONE = object()


def _interleave(units, group=None):
    group = group or len(units)
    for i in range(0, len(units), group):
        live = units[i:i + group]
        while live:
            live = [u for u in live if next(u, _DONE) is not _DONE]


def _skew(units, depth):
    started = []
    for u in units:
        next(u)
        started.append(u)
        if len(started) > depth:
            next(started.pop(0), _DONE)
    for u in started:
        next(u, _DONE)


def _cparams(sem, vmem=None):
    return pltpu.CompilerParams(dimension_semantics=sem, vmem_limit_bytes=vmem)


def _ada_kernel(c_ref, w_ref, b_ref, o_ref):
    c = c_ref[...]
    s = c * (1.0 / (1.0 + jnp.exp(-c)))
    d = c.shape[1]
    for j in range(3):
        cols = slice(j * d, (j + 1) * d)
        o_ref[0, :, j, :] = _dot(s.astype(BF16), w_ref[0, :, cols].astype(BF16)) + b_ref[0, :, cols]


def _ada(cc, ada_w, ada_b):
    r, d = cc.shape
    depth, _, n = ada_w.shape
    return pl.pallas_call(
        _ada_kernel,
        out_shape=jax.ShapeDtypeStruct((depth, r, 3, d), F32),
        grid=(depth,),
        in_specs=[pl.BlockSpec((r, d), lambda l: (0, 0)),
                  pl.BlockSpec((1, d, n), lambda l: (l, 0, 0)),
                  pl.BlockSpec((1, 1, n), lambda l: (l, 0, 0))],
        out_specs=pl.BlockSpec((1, r, 3, d), lambda l: (l, 0, 0, 0)),
        compiler_params=_cparams(("parallel",), VMEM_LIMIT),
        name="ada",
    )(cc, ada_w, ada_b)


def _head_rms(x, seg_ones, w):
    ms = _dot((x * x).astype(BF16), seg_ones) * (1.0 / HEAD_DIM)
    return x * lax.rsqrt(ms + EPS) * w


def _rope(y, cos, sin_signed):
    lane = lax.broadcasted_iota(jnp.int32, y.shape, 1)
    first_half = (lane % 32) < 16
    n = y.shape[1]
    swapped = jnp.where(first_half, pltpu.roll(y, n - 16, 1), pltpu.roll(y, 16, 1))
    return y * cos + swapped * sin_signed


def _inproj_kernel(x_ref, ctx_ref, mod_ref, nw_ref, w_ref, cos_ref, sin_ref, hn_ref, seg_ref,
                   wdec_ref, bdec_ref,
                   a_ref, la_ref, bv_ref, cq_ref, ck_ref, cv_ref, dq_ref, dk_ref, dv_ref, g_ref,
                   xs_ref):
    i = pl.program_id(1)

    @pl.when(i == 0)
    def _():
        xs_ref[0:CTX_LEN, :] = ctx_ref[0]
        xs_ref[CTX_LEN:, :] = jnp.zeros((TM - CTX_LEN, xs_ref.shape[1]), F32)

    @pl.when(i > 0)
    def _():
        xs_ref[...] = x_ref[0]

    seg = seg_ref[...]
    half = TM // INPROJ_ROW_SPLIT
    hb = [None] * INPROJ_ROW_SPLIT

    def prologue(part):
        rs = slice(part * half, (part + 1) * half)
        x = xs_ref[rs, :]
        ms = jnp.mean(x * x, axis=-1, keepdims=True)
        y = x * lax.rsqrt(ms + EPS) * nw_ref[...]
        hb[part] = (y * (1.0 + mod_ref[0, 1:2, :]) + mod_ref[0, 0:1, :]).astype(BF16)
        yield

    def gla_qkv(part, rs):
        a = _dot(hb[part], w_ref[:, C_A:C_A + 512])
        yield
        a_ref[0, rs, 0:128] = a[:, 0:128] * (GLA_DK ** -0.5)
        a_ref[0, rs, 128:512] = a[:, 128:512]

    def plain(part, rs, c0, ref, width=256):
        r = _dot(hb[part], w_ref[:, c0:c0 + width])
        yield
        ref[0, rs, :] = r.astype(BF16)

    def normed(part, rs, c0, ref, row, rope, width=256):
        r = _dot(hb[part], w_ref[:, c0:c0 + width])
        yield
        y = _head_rms(r, seg[0:width, 0:width], hn_ref[row:row + 1, 0:width])
        ref[0, rs, :] = (_rope(y, cos_ref[rs, 0:width], sin_ref[rs, 0:width]) if rope else y).astype(BF16)

    def swa_kv(part, rs):
        r = _dot(hb[part], w_ref[:, C_CK:C_CK + 256])
        yield
        y = _head_rms(r[:, 0:128], seg[0:128, 0:128], hn_ref[1:2, 0:128])
        ck_ref[0, rs, :] = _rope(y, cos_ref[rs, 0:128], sin_ref[rs, 0:128]).astype(BF16)
        cv_ref[0, rs, :] = r[:, 128:256].astype(BF16)

    def swa_q(part, rs):
        r = _dot(hb[part], w_ref[:, C_CQ:C_CQ + 256])
        yield
        y = _rope(_head_rms(r, seg, hn_ref[0:1, :]), cos_ref[rs, :], sin_ref[rs, :])
        up = pltpu.roll(y, 64, 1)
        down = pltpu.roll(y, 192, 1)
        low = lax.broadcasted_iota(jnp.int32, (y.shape[0], 128), 1) < HEAD_DIM
        zero = jnp.zeros((y.shape[0], 128), F32)
        parts = [jnp.where(low, y[:, 0:128], zero), jnp.where(low, down[:, 0:128], zero),
                 jnp.where(low, zero, up[:, 128:256]), jnp.where(low, zero, y[:, 128:256])]
        cq_ref[0, rs, :] = jnp.concatenate(parts, axis=1).astype(BF16)

    def gates(part, rs, j):
        g = _dot(hb[part], w_ref[:, C_G + 256 * j:C_G + 256 * (j + 1)])
        yield
        g_ref[0, rs, 256 * j:256 * (j + 1)] = (g * (1.0 / (1.0 + jnp.exp(-g)))).astype(BF16)

    def decay(part, rs):
        lr = _dot(hb[part], w_ref[:, C_LR:C_LR + 128])
        yield
        z = _dot(lr.astype(BF16), wdec_ref[...]) + bdec_ref[...]
        la_ref[0, rs, :] = (jnp.minimum(z, 0.0) - jnp.log(1.0 + jnp.exp(-jnp.abs(z)))) * (LOG2E / GLA_TAU)

    def groups(part):
        rs = slice(part * half, (part + 1) * half)
        return [decay(part, rs), gates(part, rs, 0), swa_q(part, rs), gates(part, rs, 1), swa_kv(part, rs),
                gates(part, rs, 2), normed(part, rs, C_DQ, dq_ref, 2, False), gates(part, rs, 3),
                normed(part, rs, C_DK, dk_ref, 3, False),
                gla_qkv(part, rs), plain(part, rs, C_BV, bv_ref), plain(part, rs, C_DV, dv_ref)]

    units = [prologue(0)]
    for part in range(INPROJ_ROW_SPLIT):
        g = groups(part)
        if part + 1 < INPROJ_ROW_SPLIT:
            g.insert(INPROJ_GROUP, prologue(part + 1))
        units += g
    _skew(units, INPROJ_GROUP)


def _inproj(x, ctx, mod, norm_w, w_perm, layer, cos_t, sin_t, head_norms, seg_ones, wdec, bdec):
    b, seq, d = x.shape
    t_all = CTX_PAD + seq
    nblk = t_all // TM

    def tok(width):
        return pl.BlockSpec((1, TM, width), lambda bi, i: (bi, i, 0))

    def const(shape):
        return pl.BlockSpec(shape, lambda bi, i: (0,) * len(shape))

    def out(width, dtype):
        return jax.ShapeDtypeStruct((b, t_all, width), dtype)

    return pl.pallas_call(
        _inproj_kernel,
        out_shape=(out(512, F32), out(256, F32), out(256, BF16), out(512, BF16), out(128, BF16),
                   out(128, BF16), out(256, BF16), out(256, BF16), out(256, BF16), out(1024, BF16)),
        grid=(b, nblk),
        in_specs=[
            pl.BlockSpec((1, TM, d), lambda bi, i: (bi, jnp.maximum(i - 1, 0), 0)),
            pl.BlockSpec((1, CTX_LEN, d), lambda bi, i: (bi, 0, 0)),
            pl.BlockSpec((1, 3, d), lambda bi, i: (jnp.where(i == 0, b, bi), 0, 0)),
            const((1, d)),
            pl.BlockSpec((None, d, W_COLS), lambda bi, i: (layer, 0, 0)),
            pl.BlockSpec((TM, 256), lambda bi, i: (i, 0)),
            pl.BlockSpec((TM, 256), lambda bi, i: (i, 0)),
            const((8, 256)),
            const((256, 256)),
            const((128, 256)),
            const((1, 256)),
        ],
        out_specs=(tok(512), tok(256), tok(256), tok(512), tok(128), tok(128), tok(256), tok(256),
                   tok(256), tok(1024)),
        scratch_shapes=[pltpu.VMEM((TM, d), F32)],
        compiler_params=_cparams(("parallel", "arbitrary"), VMEM_LIMIT),
        name="inproj",
    )(x, ctx, mod, norm_w, w_perm, cos_t, sin_t, head_norms, seg_ones, wdec, bdec)


def _gla_kernel(af_ref, ab_ref, laf_ref, lab_ref, ones2_ref, trif_ref, trib_ref, pmf_ref, pmb_ref,
                qmask_ref, stmask_ref, of_ref, ob_ref, st_ref, rb_ref, *, nbatch, nreal):
    @pl.when(pl.program_id(0) == 0)
    def _():
        st_ref[...] = jnp.zeros_like(st_ref)

    @pl.when(pl.program_id(0) < nreal)
    def _():
        lane_head = lax.broadcasted_iota(jnp.int32, (GLA_CHUNK, GLA_HEADS * GLA_DV), 1) // GLA_DV
        shared = (qmask_ref[...], lane_head, stmask_ref[...])
        consts_f = (ones2_ref[...], trif_ref[...], pmf_ref[...]) + shared
        consts_b = (ones2_ref[...], trib_ref[...], pmb_ref[...]) + shared
        units = []
        for bi in range(nbatch):
            units.append(_gla_chunk(af_ref, laf_ref, of_ref, st_ref.at[0], rb_ref.at[0], bi, False, consts_f))
        for bi in range(nbatch):
            units.append(_gla_chunk(ab_ref, lab_ref, ob_ref, st_ref.at[1], rb_ref.at[1], bi, True, consts_b))
        _interleave(units, group=GLA_GROUP)

    @pl.when(pl.program_id(0) >= nreal)
    def _():
        of_ref[...] = jnp.zeros_like(of_ref)
        ob_ref[...] = jnp.zeros_like(ob_ref)


def _gla_chunk(a_ref, la_ref, o_ref, st_ref, rb_ref, bi, rev, consts):
    ones2, tri, pmask, qmask, lane_head, stmask = consts
    c, sb = GLA_CHUNK, GLA_SUB
    nsub = c // sb
    hk = GLA_HEADS * GLA_DK
    hv = GLA_HEADS * GLA_DV

    q = a_ref[bi, :, 0:hk]
    k = a_ref[bi, :, hk:2 * hk]
    v = a_ref[bi, :, 2 * hk:2 * hk + hv]
    la = la_ref[bi]

    la_hi = la.astype(BF16)
    la_lo = (la - la_hi.astype(F32)).astype(BF16)
    cum = _dot(tri, la_hi) + _dot(tri, la_lo)
    cumx = cum - la

    rb_ref[bi, 0] = cum
    rb_ref[bi, 1] = cumx
    rb_ref[bi, 2] = k
    rb_ref[bi, 3] = v[:, 0:hk]
    rb_ref[bi, 4] = v[:, hk:hv]
    yield

    def brow(plane, r, n=sb):
        return jnp.broadcast_to(rb_ref[bi, plane, r:r + 1, :], (n, hk))

    end = 0 if rev else c - 1

    def first_row(blk):
        return sb * blk + (sb - 1 if rev else 0)

    def blk_of(p):
        return nsub - 1 - p if rev else p

    cref_rows = jnp.concatenate([brow(1, first_row(blk)) for blk in range(nsub)], axis=0)

    st = st_ref[bi]
    qi = q * jnp.exp2(cum)
    inter = _dot_nt(qi.astype(BF16), st.astype(BF16))

    qd = q * jnp.exp2(cum - cref_rows)
    q_st = jnp.concatenate([qd] * GLA_HEADS, axis=0) * qmask
    kt, vt = [], []
    for p in range(1, nsub):
        lo, hi = (c - sb * p, c) if rev else (0, sb * p)
        kt.append(k[lo:hi] * jnp.exp2(brow(1, first_row(blk_of(p)), hi - lo) - cum[lo:hi]))
        vt.append(v[lo:hi])
    npad = pmask.shape[1] - sb * (nsub * (nsub - 1) // 2)
    kt.append(jnp.zeros((npad, hk), F32))
    vt.append(jnp.zeros((npad, hv), F32))
    kt = jnp.concatenate(kt, axis=0).astype(BF16)
    vt = jnp.concatenate(vt, axis=0).astype(BF16)
    sc = _dot_nt(q_st.astype(BF16), kt)
    cum_end = rb_ref[bi, 0, end:end + 1, :]
    kd = k * jnp.exp2(cum_end - cum)
    kv = _dot_tn(v.astype(BF16), kd.astype(BF16))

    t_loc = lax.broadcasted_iota(jnp.int32, (sb, hk), 0)
    valid = [(t_loc <= s) if rev else (t_loc >= s) for s in range(sb)]
    es = []
    for blk in range(nsub):
        r0 = sb * blk
        cb = cum[r0:r0 + sb]
        qb = q[r0:r0 + sb]
        for s in range(sb):
            diff = jnp.where(valid[s], cb - brow(0, r0 + s), -jnp.inf)
            es.append(jnp.exp2(diff) * (qb * brow(2, r0 + s)))
    e = jnp.concatenate(es, axis=0).astype(BF16)
    r = _dot(e, ones2)
    yield

    off_st = _dot(sc.astype(BF16) * pmask, vt)
    dgs = []
    for blk in range(nsub):
        acc = None
        for s in range(sb):
            row0 = (blk * sb + s) * sb
            vrow = jnp.concatenate([brow(3, blk * sb + s), brow(4, blk * sb + s)], axis=1)
            term = r[row0:row0 + sb] * vrow
            acc = term if acc is None else acc + term
        dgs.append(acc)
    st_ref[bi] = st * jnp.exp2(cum_end) + kv * stmask
    yield

    off = off_st[(GLA_HEADS - 1) * c:GLA_HEADS * c]
    for h in range(GLA_HEADS - 2, -1, -1):
        off = jnp.where(lane_head == h, off_st[h * c:(h + 1) * c], off)
    o_ref[bi] = (inter + off + jnp.concatenate(dgs, axis=0)).astype(BF16)


def _gla_consts(rev):
    c, sb = GLA_CHUNK, GLA_SUB
    nsub = c // sb
    t = np.arange(c)
    tri = (t[None, :] >= t[:, None]) if rev else (t[None, :] <= t[:, None])
    blk = t // sb
    p_of_t = (nsub - 1 - blk) if rev else blk
    col_p = np.concatenate([np.full(sb * p, p) for p in range(1, nsub)])
    col_p = np.concatenate([col_p, np.full(-col_p.size % 128, -1)])
    pmask = np.tile(p_of_t, GLA_HEADS)[:, None] == col_p[None, :]
    row_h = np.repeat(np.arange(GLA_HEADS), c)
    qmask = row_h[:, None] == (np.arange(GLA_HEADS * GLA_DK) // GLA_DK)[None, :]
    stmask = (np.arange(GLA_HEADS * GLA_DV) // GLA_DV)[:, None] == (np.arange(GLA_HEADS * GLA_DK) // GLA_DK)[None, :]
    f = lambda m: jnp.asarray(m, F32)
    return f(tri).astype(BF16), f(pmask).astype(BF16), f(qmask), f(stmask)


def _gla(a, la, ones2):
    b, t_all, _ = a.shape
    nch = t_all // GLA_CHUNK
    nctx = CTX_LEN // GLA_CHUNK
    nskip = (CTX_PAD - CTX_LEN) // GLA_CHUNK
    tri_f, pm_f, qmask, stmask = _gla_consts(False)
    tri_b, pm_b, _, _ = _gla_consts(True)
    consts = (ones2, tri_f, tri_b, pm_f, pm_b, qmask, stmask)

    nreal = nch - nskip

    def fwd(j):
        return jnp.where(j < nctx, j, jnp.where(j < nreal, j + nskip, nctx + (j - nreal)))

    def bwd(j):
        return jnp.where(j < nctx, nctx - 1 - j, jnp.where(j < nreal, nch - 1 - (j - nctx), nctx + (j - nreal)))

    out = jax.ShapeDtypeStruct((b, t_all, 256), BF16)
    return pl.pallas_call(
        functools.partial(_gla_kernel, nbatch=b, nreal=nreal),
        out_shape=(out, out),
        grid=(nch,),
        in_specs=[pl.BlockSpec((b, GLA_CHUNK, 512), lambda j: (0, fwd(j), 0)),
                  pl.BlockSpec((b, GLA_CHUNK, 512), lambda j: (0, bwd(j), 0)),
                  pl.BlockSpec((b, GLA_CHUNK, 128), lambda j: (0, fwd(j), 0)),
                  pl.BlockSpec((b, GLA_CHUNK, 128), lambda j: (0, bwd(j), 1))]
                 + [pl.BlockSpec(t.shape, lambda j: (0, 0)) for t in consts],
        out_specs=(pl.BlockSpec((b, GLA_CHUNK, 256), lambda j: (0, fwd(j), 0)),
                   pl.BlockSpec((b, GLA_CHUNK, 256), lambda j: (0, bwd(j), 0))),
        scratch_shapes=[pltpu.VMEM((2, b, 256, 128), F32), pltpu.VMEM((2, b, 5, GLA_CHUNK, 128), F32)],
        compiler_params=_cparams(("arbitrary",)),
        name="gla",
    )(a, a, la, la, *consts)


def _fnet_chan_kernel(v_ref, wc_ref, ws_ref, o_ref):
    v = v_ref[0]
    o_ref[0, :, :] = _dot(v, wc_ref[...]).astype(BF16)
    o_ref[1, :, :] = _dot(v, ws_ref[...]).astype(BF16)


def _fnet_chan(bv, wc, ws, row_blk_off, length):
    b = bv.shape[0]
    tl = min(TM, length)
    return pl.pallas_call(
        _fnet_chan_kernel,
        out_shape=jax.ShapeDtypeStruct((2, length, b * 256), BF16),
        grid=(b, length // tl),
        in_specs=[pl.BlockSpec((1, tl, 256), lambda bi, i: (bi, i + row_blk_off, 0)),
                  pl.BlockSpec((256, 256), lambda bi, i: (0, 0)),
                  pl.BlockSpec((256, 256), lambda bi, i: (0, 0))],
        out_specs=pl.BlockSpec((2, tl, 256), lambda bi, i: (0, i, bi)),
        compiler_params=_cparams(("parallel", "arbitrary")),
        name="fnet_chan",
    )(bv, wc, ws)


def _fnet_pos_kernel(d_ref, y_ref, wf_ref, o_ref, acc_ref, *, nb):
    kk = pl.program_id(1)

    @pl.when(kk == 0)
    def _():
        acc_ref[...] = jnp.zeros_like(acc_ref)

    acc_ref[...] += _dot(d_ref[...], y_ref[...])

    @pl.when(kk == pl.num_programs(1) - 1)
    def _():
        wf = wf_ref[...]
        for bi in range(nb):
            o_ref[bi] = _dot(acc_ref[:, bi * 256:(bi + 1) * 256].astype(BF16), wf).astype(BF16)


def _fnet_pos(dmat, y2, wf, nb):
    length, k2 = dmat.shape
    tm = min(512, length)
    tk = min(1024, k2)
    return pl.pallas_call(
        functools.partial(_fnet_pos_kernel, nb=nb),
        out_shape=jax.ShapeDtypeStruct((nb, length, 256), BF16),
        grid=(length // tm, k2 // tk),
        in_specs=[pl.BlockSpec((tm, tk), lambda i, kk: (i, kk)),
                  pl.BlockSpec((tk, nb * 256), lambda i, kk: (kk, 0)),
                  pl.BlockSpec((256, 256), lambda i, kk: (0, 0))],
        out_specs=pl.BlockSpec((nb, tm, 256), lambda i, kk: (0, i, 0)),
        scratch_shapes=[pltpu.VMEM((tm, nb * 256), F32)],
        compiler_params=_cparams(("parallel", "arbitrary"), VMEM_LIMIT),
        name="fnet_pos",
    )(dmat, y2, wf)


def _dft_tables(length):
    j = np.arange(length)
    ang = ((j[:, None] * j[None, :]) % length) * (2.0 * np.pi / length)
    s = 1.0 / math.sqrt(length)
    return jnp.asarray(np.concatenate([np.cos(ang) * s, -np.sin(ang) * s], axis=1), F32).astype(BF16)


def _chan_tables():
    j = np.arange(FNET_GW)
    ang = 2.0 * np.pi * ((j[:, None] * j[None, :]) % FNET_GW) / FNET_GW
    s = 1.0 / math.sqrt(FNET_GW)
    eye = np.eye(BRANCH_W // FNET_GW)
    wc = np.kron(eye, np.cos(ang) * s)
    ws = np.kron(eye, np.sin(ang) * s)
    return jnp.asarray(wc, F32).astype(BF16), jnp.asarray(ws, F32).astype(BF16)


def _fnet(bv, wf, row_blk_off, length):
    b = bv.shape[0]
    wc, ws = _chan_tables()
    y = _fnet_chan(bv, wc, ws, row_blk_off, length)
    y2 = y.reshape(2 * length, b * 256)
    return _fnet_pos(_dft_tables(length), y2, wf, b)


FNET_UNROLL = 16


def _fnet_grid_kernel(v_ref, cg_ref, sg_ref, w1_ref, w2_ref, wf_ref, o_ref, z_ref, g_ref, f_ref, *, rows, row_off):
    length = rows * GRID_W
    v = v_ref[0, row_off:row_off + length, :].astype(F32)
    z_ref[0] = v[:, 0:128].reshape(rows, GRID_W, 128)
    z_ref[1] = v[:, 128:256].reshape(rows, GRID_W, 128)
    wf = wf_ref[...]
    a = _dot(cg_ref[...], wf)
    b = _dot(sg_ref[...], wf)
    chan = jnp.concatenate([jnp.concatenate([a, -b], axis=1), jnp.concatenate([b, a], axis=1)], axis=0).astype(BF16)

    n8 = rows * 8
    for w0 in range(0, GRID_W, 8):
        rhs = jnp.concatenate([z_ref[0, :, w0:w0 + 8, :].reshape(n8, 128),
                               z_ref[1, :, w0:w0 + 8, :].reshape(n8, 128)], axis=1).astype(BF16)
        g = _dot(w1_ref[...], rhs)
        gc = jnp.concatenate([g[0:n8], g[n8:2 * n8]], axis=1).astype(BF16)
        y = _dot(gc, chan)
        for c in range(4):
            g_ref[c, :, w0:w0 + 8, :] = y[:, 128 * c:128 * (c + 1)].reshape(rows, 8, 128)

    def stage_w(ka, carry):
        r0 = pl.multiple_of(ka * GRID_W, GRID_W)
        rhs = jnp.concatenate([jnp.concatenate([g_ref[0, ka], g_ref[1, ka]], axis=1),
                               jnp.concatenate([g_ref[2, ka], g_ref[3, ka]], axis=1)], axis=0).astype(BF16)
        y = _dot(w2_ref[pl.ds(r0, GRID_W), :], rhs)
        f_ref[0, pl.ds(ka, GRID_W, stride=rows), :] = y[:, 0:128]
        f_ref[1, pl.ds(ka, GRID_W, stride=rows), :] = y[:, 128:256]
        return carry

    lax.fori_loop(0, rows, stage_w, 0, unroll=FNET_UNROLL)
    o_ref[0, :, 0:128] = f_ref[0].astype(BF16)
    o_ref[0, :, 128:256] = f_ref[1].astype(BF16)


def _fnet_grid_tables(rows):
    length = rows * GRID_W
    j = np.arange(rows)
    ang = 2.0 * np.pi * ((j[:, None] * j[None, :]) % rows) / rows
    c, s = np.cos(ang), np.sin(ang)
    w1 = np.kron(np.block([[c], [-s]]), np.eye(8))
    w1 = jnp.asarray(w1, F32).astype(BF16)
    ka = np.arange(rows)[:, None, None]
    kb = np.arange(GRID_W)[None, :, None]
    w = np.arange(GRID_W)[None, None, :]
    ang2 = ((w * (ka + rows * kb)) % length) * (2.0 * np.pi / length)
    scale = 1.0 / math.sqrt(length)
    w2 = np.concatenate([np.cos(ang2) * scale, np.sin(ang2) * scale], axis=2)
    return w1, jnp.asarray(w2.reshape(length, 2 * GRID_W), F32).astype(BF16)


def _fnet_grid(bv, wf, seq):
    b, t_all, _ = bv.shape
    rows = seq // GRID_W
    wc, ws = _chan_tables()
    w1, w2 = _fnet_grid_tables(rows)

    def const(shape):
        return pl.BlockSpec(shape, lambda bi: (0,) * len(shape))

    return pl.pallas_call(
        functools.partial(_fnet_grid_kernel, rows=rows, row_off=CTX_PAD),
        out_shape=jax.ShapeDtypeStruct((b, seq, 256), BF16),
        grid=(b,),
        in_specs=[pl.BlockSpec((1, t_all, 256), lambda bi: (bi, 0, 0)), const((256, 256)), const((256, 256)),
                  const((16 * rows, 8 * rows)), const((seq, 2 * GRID_W)), const((256, 256))],
        out_specs=pl.BlockSpec((1, seq, 256), lambda bi: (bi, 0, 0)),
        scratch_shapes=[pltpu.VMEM((2, rows, GRID_W, 128), F32), pltpu.VMEM((4, rows, GRID_W, 128), F32),
                        pltpu.VMEM((2, seq, 128), F32)],
        compiler_params=_cparams(("parallel",), VMEM_LIMIT),
        name="fnet_grid",
    )(bv, wc, ws, w1, w2, wf)


def _stack_heads(qg):
    lane = lax.broadcasted_iota(jnp.int32, qg.shape, 1)
    zero = jnp.zeros_like(qg)
    return jnp.concatenate([jnp.where(lane < HEAD_DIM, qg, zero),
                            jnp.where(lane >= HEAD_DIM, qg, zero)], axis=0)


def _gqa_unstack(o, rows, g):
    lane = lax.broadcasted_iota(jnp.int32, (rows, 2 * HEAD_DIM), 1)
    first, second = o[0:rows], o[rows:2 * rows]
    if g == 0:
        return jnp.where(lane < HEAD_DIM, first, pltpu.roll(second, HEAD_DIM, 1))
    return jnp.where(lane < HEAD_DIM, pltpu.roll(first, HEAD_DIM, 1), second)


def _unstack_heads(o, rows):
    lane = lax.broadcasted_iota(jnp.int32, (rows, 2 * HEAD_DIM), 1)
    return jnp.where(lane < HEAD_DIM, o[0:rows], o[rows:2 * rows])


def _swa_kernel(sink_ref, band_ref, q_ref, kp_ref, kc_ref, kn_ref, kx_ref, vp_ref, vc_ref, vn_ref, vx_ref,
                o_ref, *, nbatch):
    n = pl.program_id(0)
    nb = pl.num_programs(0)
    blk = SWA_BLOCK
    rows = 2 * blk
    col = lax.broadcasted_iota(jnp.int32, (1, 3 * blk), 1)
    col_lo = jnp.where(n > 0, 0, blk)
    col_hi = jnp.where(n < nb - 1, 3 * blk, 2 * blk)
    edge = jnp.where((col >= col_lo) & (col < col_hi), 0.0, -jnp.inf)
    bias = band_ref[...] + edge
    head_row = lax.broadcasted_iota(jnp.int32, (rows, 1), 0) < blk

    def unit(bi, g):
        q_st = jnp.concatenate([q_ref[bi, :, 256 * g:256 * g + 128],
                                q_ref[bi, :, 256 * g + 128:256 * g + 256]], axis=0)
        kw = jnp.concatenate([kp_ref[bi], kc_ref[bi], kn_ref[bi]], axis=0)
        s_loc = _dot_nt(q_st, kw) + bias
        s_ctx = _dot_nt(q_st, kx_ref[bi])
        yield
        vw = jnp.concatenate([vp_ref[bi], vc_ref[bi], vn_ref[bi]], axis=0)
        sink = jnp.where(head_row, sink_ref[2 * g], sink_ref[2 * g + 1])
        m = jnp.maximum(jnp.maximum(jnp.max(s_loc, axis=-1, keepdims=True),
                                    jnp.max(s_ctx, axis=-1, keepdims=True)), sink)
        p_loc = jnp.exp2(s_loc - m)
        p_ctx = jnp.exp2(s_ctx - m)
        den = (jnp.sum(p_loc, axis=-1, keepdims=True) + jnp.sum(p_ctx, axis=-1, keepdims=True)
               + jnp.exp2(sink - m))
        o = _dot(p_loc.astype(BF16), vw) + _dot(p_ctx.astype(BF16), vx_ref[bi])
        yield
        o_ref[bi, :, 128 * g:128 * (g + 1)] = _gqa_unstack(o / den, blk, g).astype(BF16)

    _interleave([unit(bi, g) for bi in range(nbatch) for g in range(2)], group=ATTN_GROUP)


def _swa_band():
    row = np.arange(2 * SWA_BLOCK)[:, None] % SWA_BLOCK
    col = np.arange(3 * SWA_BLOCK)[None, :]
    ok = np.abs(col - SWA_BLOCK - row) <= SWA_WINDOW
    return jnp.where(jnp.asarray(ok), 0.0, -jnp.inf).astype(F32)


def _swa(sink, cq, ck, cv, seq):
    b = cq.shape[0]
    nb = seq // SWA_BLOCK
    off = CTX_PAD // SWA_BLOCK

    def at(f, width=128):
        return pl.BlockSpec((b, SWA_BLOCK, width), lambda n: (0, f(n) + off, 0))

    prev = at(lambda n: jnp.maximum(n - 1, 0))
    cur = at(lambda n: n)
    nxt = at(lambda n: jnp.minimum(n + 1, nb - 1))
    cx = pl.BlockSpec((b, CTX_LEN, 128), lambda n: (0, 0, 0))
    band = pl.BlockSpec((2 * SWA_BLOCK, 3 * SWA_BLOCK), lambda n: (0, 0))
    return pl.pallas_call(
        functools.partial(_swa_kernel, nbatch=b),
        out_shape=jax.ShapeDtypeStruct((b, seq, 256), BF16),
        grid=(nb,),
        in_specs=[pl.BlockSpec(memory_space=pltpu.SMEM), band, at(lambda n: n, 512),
                  prev, cur, nxt, cx, prev, cur, nxt, cx],
        out_specs=pl.BlockSpec((b, SWA_BLOCK, 256), lambda n: (0, n, 0)),
        compiler_params=_cparams(("arbitrary",)),
        name="swa",
    )(sink, _swa_band(), cq, ck, ck, ck, ck, cv, cv, cv, cv)


NA_ROWS_PER_STEP = 8


def _na_kernel(q_ref, k_ref, v_ref, bias_ref, o_ref, *, rows_total):
    kh = NA_KH
    nk = kh * GRID_W
    def unit(rr, g):
        r = pl.program_id(1) * NA_ROWS_PER_STEP + rr
        row_start = jnp.clip(r - kh // 2, 0, rows_total - kh)
        cls = r - row_start
        k0 = pl.multiple_of(CTX_PAD + row_start * GRID_W, GRID_W)
        qs = slice(GRID_W * rr, GRID_W * (rr + 1))
        ls = slice(128 * g, 128 * (g + 1))
        q_st = _stack_heads(q_ref[0, qs, ls])
        s_nb = _dot_nt(q_st, k_ref[0, pl.ds(k0, nk), ls]) + bias_ref[cls, g]
        s_ctx = _dot_nt(q_st, k_ref[0, 0:CTX_LEN, ls])
        yield
        m = jnp.maximum(jnp.max(s_nb, axis=-1, keepdims=True), jnp.max(s_ctx, axis=-1, keepdims=True))
        p_nb = jnp.exp2(s_nb - m)
        p_ctx = jnp.exp2(s_ctx - m)
        den = jnp.sum(p_nb, axis=-1, keepdims=True) + jnp.sum(p_ctx, axis=-1, keepdims=True)
        o = (_dot(p_nb.astype(BF16), v_ref[0, pl.ds(k0, nk), ls])
             + _dot(p_ctx.astype(BF16), v_ref[0, 0:CTX_LEN, ls]))
        yield
        o_ref[0, qs, ls] = _unstack_heads(o / den, GRID_W).astype(BF16)

    _interleave([unit(rr, g) for rr in range(NA_ROWS_PER_STEP) for g in range(2)], group=ATTN_GROUP)


def _bias_expand_kernel(rb_ref, oh_ref, o_ref):
    x = rb_ref[...]
    oh = oh_ref[...]
    hi = x.astype(BF16)
    r1 = x - hi.astype(F32)
    mid = r1.astype(BF16)
    lo = (r1 - mid.astype(F32)).astype(BF16)
    o_ref[...] = (_dot(hi, oh) + _dot(mid, oh) + _dot(lo, oh)) * LOG2E


def _na_bias_table(rel_bias):
    ndy, ndx = 2 * NA_KH - 1, 2 * NA_KW - 1
    cq = np.arange(GRID_W)
    col_start = np.clip(cq - NA_KW // 2, 0, GRID_W - NA_KW)
    col_ok = (cq[None, :] >= col_start[:, None]) & (cq[None, :] < col_start[:, None] + NA_KW)
    qi = jnp.arange(GRID_W, dtype=jnp.int32)
    dx = jnp.clip(qi[None, :] - qi[:, None], -(NA_KW - 1), NA_KW - 1) + (NA_KW - 1)
    onehot = (jnp.arange(128, dtype=jnp.int32)[:, None] == dx.reshape(1, GRID_W * GRID_W)).astype(BF16)
    rb = jnp.zeros((64, 128), F32).at[0:NA_HEADS * ndy, 0:ndx].set(rel_bias.reshape(NA_HEADS * ndy, ndx))
    p = pl.pallas_call(
        _bias_expand_kernel,
        out_shape=jax.ShapeDtypeStruct((64, GRID_W * GRID_W), F32),
        name="na_bias_expand",
    )(rb, onehot)
    p = p[0:NA_HEADS * ndy].reshape(NA_HEADS, ndy, GRID_W, GRID_W)
    p = jnp.where(col_ok[None, None], p, -jnp.inf)
    tabs = jnp.stack([p[:, NA_KH - 1 - c:2 * NA_KH - 1 - c] for c in range(NA_KH)])
    tabs = tabs.transpose(0, 1, 3, 2, 4)
    return tabs.reshape(NA_KH, NA_HEADS // 2, 2 * GRID_W, NA_KH * GRID_W)


def _na(dq, dk, dv, bias_tab, seq):
    b, t_all, _ = dq.shape
    rows_total = seq // GRID_W
    qrows = NA_ROWS_PER_STEP * GRID_W
    assert CTX_PAD % qrows == 0 and rows_total % NA_ROWS_PER_STEP == 0
    off = CTX_PAD // qrows
    full = pl.BlockSpec((1, t_all, 256), lambda bi, r: (bi, 0, 0))
    return pl.pallas_call(
        functools.partial(_na_kernel, rows_total=rows_total),
        out_shape=jax.ShapeDtypeStruct((b, seq, 256), BF16),
        grid=(b, rows_total // NA_ROWS_PER_STEP),
        in_specs=[pl.BlockSpec((1, qrows, 256), lambda bi, r: (bi, r + off, 0)), full, full,
                  pl.BlockSpec(bias_tab.shape, lambda bi, r: (0, 0, 0, 0))],
        out_specs=pl.BlockSpec((1, qrows, 256), lambda bi, r: (bi, r, 0)),
        compiler_params=_cparams(("parallel", "arbitrary"), VMEM_LIMIT),
        name="na",
    )(dq, dk, dv, bias_tab)


def _ctx_attn_kernel(sink_ref, q_ref, k_ref, v_ref, o_ref, *, gqa):
    n = CTX_LEN
    head_row = lax.broadcasted_iota(jnp.int32, (2 * n, 1), 0) < n
    for g in range(2):
        ls = slice(128 * g, 128 * (g + 1))
        if gqa:
            q_st = jnp.concatenate([q_ref[0, :, 256 * g:256 * g + 128],
                                    q_ref[0, :, 256 * g + 128:256 * g + 256]], axis=0)
            k, v = k_ref[0], v_ref[0]
        else:
            q_st = _stack_heads(q_ref[0, :, ls])
            k, v = k_ref[0, :, ls], v_ref[0, :, ls]
        s = _dot_nt(q_st, k)
        m = jnp.max(s, axis=-1, keepdims=True)
        if gqa:
            sink = jnp.where(head_row, sink_ref[2 * g], sink_ref[2 * g + 1])
            m = jnp.maximum(m, sink)
        p = jnp.exp2(s - m)
        den = jnp.sum(p, axis=-1, keepdims=True)
        if gqa:
            den = den + jnp.exp2(sink - m)
        o = _dot(p.astype(BF16), v) / den
        o_ref[0, :, ls] = (_gqa_unstack(o, n, g) if gqa else _unstack_heads(o, n)).astype(BF16)


def _ctx_attn(sink, q, k, v, gqa):
    b = q.shape[0]

    def blk(arr):
        return pl.BlockSpec((1, CTX_LEN, arr.shape[2]), lambda bi: (bi, 0, 0))

    return pl.pallas_call(
        functools.partial(_ctx_attn_kernel, gqa=gqa),
        out_shape=jax.ShapeDtypeStruct((b, CTX_LEN, 256), BF16),
        grid=(b,),
        in_specs=[pl.BlockSpec(memory_space=pltpu.SMEM), blk(q), blk(k), blk(v)],
        out_specs=pl.BlockSpec((1, CTX_LEN, 256), lambda bi: (bi, 0, 0)),
        compiler_params=_cparams(("parallel",)),
        name="ctx_attn_sink" if gqa else "ctx_attn",
    )(sink, q, k, v)


def _outproj_kernel(x_ref, of_ref, ob_ref, fz_ref, co_ref, do_ref, g_ref, mod_ref, gn_ref, seg_ref,
                    w_ref, o_ref):
    a = of_ref[0].astype(F32) + ob_ref[0].astype(F32)
    ms = _dot((a * a).astype(BF16), seg_ref[...]) * (1.0 / GLA_DV)
    a = a * lax.rsqrt(ms + EPS) * gn_ref[...]
    g = g_ref[0].astype(F32)
    mixed =jnp.concatenate([a * g[:, 0:256], fz_ref[0].astype(F32) * g[:, 256:512],
                             co_ref[0].astype(F32) * g[:, 512:768], do_ref[0].astype(F32) * g[:, 768:1024]],
                            axis=1)
    y = _dot(mixed.astype(BF16), w_ref[...])
    o_ref[0] = x_ref[0] + mod_ref[0, 2:3, :] * y


def _outproj(x, o_f, o_b, fz, c_o, d_o, g, mod, gla_norm, seg_ones, w_out, is_ctx):
    b, rows, d = x.shape
    tm = CTX_LEN if is_ctx else TM
    nblk = rows // tm
    comb = 0 if is_ctx else CTX_PAD // tm
    nbatch = b

    def own(width):
        return pl.BlockSpec((1, tm, width), lambda bi, i: (bi, i, 0))

    def combined(width):
        return pl.BlockSpec((1, tm, width), lambda bi, i: (bi, i + comb, 0))

    def const(shape):
        return pl.BlockSpec(shape, lambda bi, i: (0,) * len(shape))

    mod_spec = pl.BlockSpec((1, 3, d), (lambda bi, i: (nbatch, 0, 0)) if is_ctx else (lambda bi, i: (bi, 0, 0)))
    return pl.pallas_call(
        _outproj_kernel,
        out_shape=jax.ShapeDtypeStruct((b, rows, d), F32),
        grid=(b, nblk),
        in_specs=[own(d), combined(256), combined(256), own(256), own(256), own(256), combined(1024),
                  mod_spec, const((1, 256)), const((256, 256)), const((d, d))],
        out_specs=own(d),
        compiler_params=_cparams(("parallel", "arbitrary"), VMEM_LIMIT),
        name="outproj_ctx" if is_ctx else "outproj",
    )(x, o_f, o_b, fz, c_o, d_o, g, mod, gla_norm, seg_ones, w_out)


_W_IN_MOVES = (
    (C_A, 0, 512), (C_BV, 800, 256), (C_CQ, 1312, 256),
    (C_CK, 1568, 128), (C_CV, 1696, 128),
    (C_DQ, 2080, 256), (C_DK, 2336, 256), (C_DV, 2592, 256),
    (C_G, 512, 256), (C_G + 256, 1056, 256), (C_G + 512, 1824, 256), (C_G + 768, 2848, 256),
    (C_LR, 768, 2 * GLA_RANK),
)


def _w_in_prep_kernel(wt_ref, o_ref):
    for dst, src, width in _W_IN_MOVES:
        o_ref[0, :, dst:dst + width] = wt_ref[0, src:src + width, :].T.astype(BF16)
    pad0 = C_LR + 2 * GLA_RANK
    o_ref[0, :, pad0:W_COLS] = jnp.zeros((o_ref.shape[1], W_COLS - pad0), BF16)


def _permute_w_in(w_all):
    depth, d, n = w_all.shape
    kblk = 256
    return pl.pallas_call(
        _w_in_prep_kernel,
        out_shape=jax.ShapeDtypeStruct((depth, d, W_COLS), BF16),
        grid=(depth, d // kblk),
        in_specs=[pl.BlockSpec((1, n, kblk), lambda l, i: (l, 0, i))],
        out_specs=pl.BlockSpec((1, kblk, W_COLS), lambda l, i: (l, i, 0)),
        compiler_params=_cparams(("parallel", "parallel")),
        name="w_in_prep",
    )(jnp.swapaxes(w_all, 1, 2))


def _rope_tables(seq):
    t = np.arange(seq)
    axis_dim = HEAD_DIM // 2
    inv = ROPE_BASE ** (-np.arange(0, axis_dim, 2, dtype=np.float64) / axis_dim)
    row = (t // GRID_W).astype(np.float64)
    col = (t % GRID_W).astype(np.float64)
    ang = np.concatenate([row[:, None] * inv, row[:, None] * inv, col[:, None] * inv, col[:, None] * inv], axis=1)
    cos = np.cos(ang)
    sign = np.concatenate([-np.ones(16), np.ones(16), -np.ones(16), np.ones(16)])
    sin = np.sin(ang) * sign
    cos = np.concatenate([np.ones((CTX_PAD, HEAD_DIM)), cos], axis=0)
    sin = np.concatenate([np.zeros((CTX_PAD, HEAD_DIM)), sin], axis=0)
    return (jnp.asarray(np.tile(cos, (1, 4)), F32), jnp.asarray(np.tile(sin, (1, 4)), F32))


def _seg_ones(width, seg):
    i = np.arange(width)
    return jnp.asarray((i[:, None] // seg) == (i[None, :] // seg), BF16)


def _gla_ones2():
    r = np.arange(GLA_HEADS * GLA_DK) // GLA_DK
    c = np.arange(GLA_HEADS * GLA_DV) // GLA_DV
    return jnp.asarray(r[:, None] == c[None, :], BF16)


def _layer(x, ctx, mod, need_ctx, tables, norm_w, w_in_perm, layer, w_out, gla_dec_w, gla_dec_b,
           gla_out_norm, fnet_w, swa_q_norm, swa_k_norm, swa_sink, na_q_norm, na_k_norm, na_rel_bias):
    b, seq, d = x.shape
    cos_t, sin_t, seg64, ones2 = tables

    qs = HEAD_DIM ** -0.5 * LOG2E
    swa_sink = swa_sink * LOG2E
    head_norms = jnp.stack([jnp.tile(swa_q_norm, 4) * qs, jnp.tile(swa_k_norm, 4),
                            jnp.tile(na_q_norm, 4) * qs, jnp.tile(na_k_norm, 4)])
    head_norms = jnp.concatenate([head_norms, jnp.zeros((4, 256), F32)], axis=0)
    wdec = jnp.zeros((128, 256), F32)
    wdec = wdec.at[0:GLA_RANK, 0:128].set(gla_dec_w[0]).at[GLA_RANK:2 * GLA_RANK, 128:256].set(gla_dec_w[1])
    bdec = gla_dec_b.reshape(1, 256)

    a, la, bv, cq, ck, cv, dq, dk, dv, g = _inproj(
        x, ctx, mod, norm_w[None, :], w_in_perm, layer, cos_t, sin_t, head_norms, seg64,
        wdec.astype(BF16), bdec)

    o_f, o_b = _gla(a, la, ones2)
    wf = fnet_w.astype(BF16)
    fz = _fnet_grid(bv, wf, seq)
    c_o = _swa(swa_sink, cq, ck, cv, seq)
    d_o = _na(dq, dk, dv, _na_bias_table(na_rel_bias), seq)

    gn = jnp.tile(gla_out_norm, 4)[None, :]
    wo = w_out.astype(BF16)
    x_new = _outproj(x, o_f, o_b, fz, c_o, d_o, g, mod, gn, seg64, wo, is_ctx=False)
    ctx_new = None
    if need_ctx:
        fzc = _fnet(bv, wf, 0, CTX_LEN)
        c_oc = _ctx_attn(swa_sink, cq, ck, cv, gqa=True)
        d_oc = _ctx_attn(swa_sink, dq, dk, dv, gqa=False)
        ctx_new = _outproj(ctx, o_f, o_b, fzc, c_oc, d_oc, g, mod, gn, seg64, wo, is_ctx=True)
    return x_new, ctx_new


def kernel(x, c, ctx, c_ctx, norm_w, ada_w, ada_b, w_in, w_out, gla_dec_w, gla_dec_b, gla_out_norm,
           fnet_w, swa_q_norm, swa_k_norm, swa_sink, na_q_norm, na_k_norm, na_rel_bias):
    b, seq, d = x.shape
    depth = norm_w.shape[0]
    assert ctx.shape[1] == CTX_LEN and seq % TM == 0 and seq % (NA_KH * GRID_W) == 0
    tables = _rope_tables(seq) + (_seg_ones(256, HEAD_DIM), _gla_ones2())
    cc = jnp.concatenate([c, c_ctx[None, :], jnp.zeros((8 - (b + 1) % 8, d), F32)], axis=0)
    mod = _ada(cc, ada_w, ada_b[:, None, :])
    w_in_perm = _permute_w_in(w_in)
    for i in range(depth):
        x, ctx = _layer(x, ctx, mod[i], i < depth - 1, tables, norm_w[i], w_in_perm, i, w_out[i],
                        gla_dec_w[i], gla_dec_b[i], gla_out_norm[i], fnet_w[i], swa_q_norm[i],
                        swa_k_norm[i], swa_sink[i], na_q_norm[i], na_k_norm[i], na_rel_bias[i])
    return x
```

```python
import functools
import math

import numpy as np
import jax
import jax.numpy as jnp
from jax import lax
from jax.experimental import pallas as pl
from jax.experimental.pallas import tpu as pltpu

F32 = jnp.float32
BF16 = jnp.bfloat16

GRID_W = 64
CTX_LEN = 256
HEAD_DIM = 64
BRANCH_W = 256
EPS = 1e-6
ROPE_BASE = 10000.0
GLA_HEADS = 4
GLA_DK = 32
GLA_DV = 64
GLA_RANK = 16
GLA_TAU = 16.0
GLA_CHUNK = 64
GLA_SUB = 8
LOG2E = 1.4426950408889634
FNET_GW = 64
SWA_BLOCK = 128
SWA_WINDOW = 128
NA_KH = 8
NA_KW = 16
NA_HEADS = 4

TM = 512
CTX_PAD = TM
VMEM_LIMIT = 48 * 1024 * 1024

C_A = 0
C_BV = 512
C_CQ = 768
C_CK = 1024
C_CV = 1152
C_DQ = 1280
C_DK = 1536
C_DV = 1792
C_G = 2048
C_LR = 3072
W_COLS = 3200


def _dot(a, b):
    return jnp.dot(a, b, preferred_element_type=F32)


def _dot_nt(a, b):
    return lax.dot_general(a, b, (((1,), (1,)), ((), ())), preferred_element_type=F32)


def _dot_tn(a, b):
    return lax.dot_general(a, b, (((0,), (0,)), ((), ())), preferred_element_type=F32)


GLA_GROUP = 8
INPROJ_ROW_SPLIT = 2
INPROJ_GROUP = 4
SWA_NA_GROUP = 3
ATTN_GROUP = 4
_DONE = object()


def _interleave(units, group=None):
    group = group or len(units)
    for i in range(0, len(units), group):
        live = units[i:i + group]
        while live:
            live = [u for u in live if next(u, _DONE) is not _DONE]


def _skew(units, depth):
    started = []
    for u in units:
        next(u)
        started.append(u)
        if len(started) > depth:
            next(started.pop(0), _DONE)
    for u in started:
        next(u, _DONE)


def _cparams(sem, vmem=None):
    return pltpu.CompilerParams(dimension_semantics=sem, vmem_limit_bytes=vmem)


def _ada_kernel(c_ref, w_ref, b_ref, o_ref):
    c = c_ref[...]
    s = c * (1.0 / (1.0 + jnp.exp(-c)))
    d = c.shape[1]
    for j in range(3):
        cols = slice(j * d, (j + 1) * d)
        o_ref[0, :, j, :] = _dot(s.astype(BF16), w_ref[0, :, cols].astype(BF16)) + b_ref[0, :, cols]


def _ada(cc, ada_w, ada_b):
    r, d = cc.shape
    depth, _, n = ada_w.shape
    return pl.pallas_call(
        _ada_kernel,
        out_shape=jax.ShapeDtypeStruct((depth, r, 3, d), F32),
        grid=(depth,),
        in_specs=[pl.BlockSpec((r, d), lambda l: (0, 0)),
                  pl.BlockSpec((1, d, n), lambda l: (l, 0, 0)),
                  pl.BlockSpec((1, 1, n), lambda l: (l, 0, 0))],
        out_specs=pl.BlockSpec((1, r, 3, d), lambda l: (l, 0, 0, 0)),
        compiler_params=_cparams(("parallel",), VMEM_LIMIT),
        name="ada",
    )(cc, ada_w, ada_b)


def _head_rms(x, seg_ones, w):
    ms = _dot((x * x).astype(BF16), seg_ones) * (1.0 / HEAD_DIM)
    return x * lax.rsqrt(ms + EPS) * w


def _rope(y, cos, sin_signed):
    lane = lax.broadcasted_iota(jnp.int32, y.shape, 1)
    first_half = (lane % 32) < 16
    n = y.shape[1]
    swapped = jnp.where(first_half, pltpu.roll(y, n - 16, 1), pltpu.roll(y, 16, 1))
    return y * cos + swapped * sin_signed


def _inproj_kernel(x_ref, ctx_ref, mod_ref, nw_ref, w_ref, cos_ref, sin_ref, hn_ref, seg_ref,
                   wdec_ref, bdec_ref,
                   a_ref, la_ref, bv_ref, cq_ref, ck_ref, cv_ref, dq_ref, dk_ref, dv_ref, g_ref,
                   xs_ref):
    i = pl.program_id(1)

    @pl.when(i == 0)
    def _():
        xs_ref[0:CTX_LEN, :] = ctx_ref[0]
        xs_ref[CTX_LEN:, :] = jnp.zeros((TM - CTX_LEN, xs_ref.shape[1]), F32)

    @pl.when(i > 0)
    def _():
        xs_ref[...] = x_ref[0]

    seg = seg_ref[...]
    half = TM // INPROJ_ROW_SPLIT
    hb = [None] * INPROJ_ROW_SPLIT

    def prologue(part):
        rs = slice(part * half, (part + 1) * half)
        x = xs_ref[rs, :]
        ms = jnp.mean(x * x, axis=-1, keepdims=True)
        y = x * lax.rsqrt(ms + EPS) * nw_ref[...]
        hb[part] = (y * (1.0 + mod_ref[0, 1:2, :]) + mod_ref[0, 0:1, :]).astype(BF16)
        yield

    def gla_qkv(part, rs):
        a = _dot(hb[part], w_ref[:, C_A:C_A + 512])
        yield
        a_ref[0, rs, 0:128] = a[:, 0:128] * (GLA_DK ** -0.5)
        a_ref[0, rs, 128:512] = a[:, 128:512]

    def plain(part, rs, c0, ref, width=256):
        r = _dot(hb[part], w_ref[:, c0:c0 + width])
        yield
        ref[0, rs, :] = r.astype(BF16)

    def normed(part, rs, c0, ref, row, rope, width=256):
        r = _dot(hb[part], w_ref[:, c0:c0 + width])
        yield
        y = _head_rms(r, seg[0:width, 0:width], hn_ref[row:row + 1, 0:width])
        ref[0, rs, :] = (_rope(y, cos_ref[rs, 0:width], sin_ref[rs, 0:width]) if rope else y).astype(BF16)

    def swa_kv(part, rs):
        r = _dot(hb[part], w_ref[:, C_CK:C_CK + 256])
        yield
        y = _head_rms(r[:, 0:128], seg[0:128, 0:128], hn_ref[1:2, 0:128])
        ck_ref[0, rs, :] = _rope(y, cos_ref[rs, 0:128], sin_ref[rs, 0:128]).astype(BF16)
        cv_ref[0, rs, :] = r[:, 128:256].astype(BF16)

    def swa_q(part, rs):
        r = _dot(hb[part], w_ref[:, C_CQ:C_CQ + 256])
        yield
        y = _rope(_head_rms(r, seg, hn_ref[0:1, :]), cos_ref[rs, :], sin_ref[rs, :])
        up = pltpu.roll(y, 64, 1)
        down = pltpu.roll(y, 192, 1)
        low = lax.broadcasted_iota(jnp.int32, (y.shape[0], 128), 1) < HEAD_DIM
        zero = jnp.zeros((y.shape[0], 128), F32)
        parts = [jnp.where(low, y[:, 0:128], zero), jnp.where(low, down[:, 0:128], zero),
                 jnp.where(low, zero, up[:, 128:256]), jnp.where(low, zero, y[:, 128:256])]
        cq_ref[0, rs, :] = jnp.concatenate(parts, axis=1).astype(BF16)

    def gates(part, rs, j):
        g = _dot(hb[part], w_ref[:, C_G + 256 * j:C_G + 256 * (j + 1)])
        yield
        g_ref[0, rs, 256 * j:256 * (j + 1)] = (g * (1.0 / (1.0 + jnp.exp(-g)))).astype(BF16)

    def decay(part, rs):
        lr = _dot(hb[part], w_ref[:, C_LR:C_LR + 128])
        yield
        z = _dot(lr.astype(BF16), wdec_ref[...]) + bdec_ref[...]
        la_ref[0, rs, :] = (jnp.minimum(z, 0.0) - jnp.log(1.0 + jnp.exp(-jnp.abs(z)))) * (LOG2E / GLA_TAU)

    def groups(part):
        rs = slice(part * half, (part + 1) * half)
        return [decay(part, rs), gates(part, rs, 0), swa_q(part, rs), gates(part, rs, 1), swa_kv(part, rs),
                gates(part, rs, 2), normed(part, rs, C_DQ, dq_ref, 2, False), gates(part, rs, 3),
                normed(part, rs, C_DK, dk_ref, 3, False),
                gla_qkv(part, rs), plain(part, rs, C_BV, bv_ref), plain(part, rs, C_DV, dv_ref)]

    units = [prologue(0)]
    for part in range(INPROJ_ROW_SPLIT):
        g = groups(part)
        if part + 1 < INPROJ_ROW_SPLIT:
            g.insert(INPROJ_GROUP, prologue(part + 1))
        units += g
    _skew(units, INPROJ_GROUP)


def _inproj(x, ctx, mod, norm_w, w_perm, layer, cos_t, sin_t, head_norms, seg_ones, wdec, bdec):
    b, seq, d = x.shape
    t_all = CTX_PAD + seq
    nblk = t_all // TM

    def tok(width):
        return pl.BlockSpec((1, TM, width), lambda bi, i: (bi, i, 0))

    def const(shape):
        return pl.BlockSpec(shape, lambda bi, i: (0,) * len(shape))

    def out(width, dtype):
        return jax.ShapeDtypeStruct((b, t_all, width), dtype)

    return pl.pallas_call(
        _inproj_kernel,
        out_shape=(out(512, F32), out(256, F32), out(256, BF16), out(512, BF16), out(128, BF16),
                   out(128, BF16), out(256, BF16), out(256, BF16), out(256, BF16), out(1024, BF16)),
        grid=(b, nblk),
        in_specs=[
            pl.BlockSpec((1, TM, d), lambda bi, i: (bi, jnp.maximum(i - 1, 0), 0)),
            pl.BlockSpec((1, CTX_LEN, d), lambda bi, i: (bi, 0, 0)),
            pl.BlockSpec((1, 3, d), lambda bi, i: (jnp.where(i == 0, b, bi), 0, 0)),
            const((1, d)),
            pl.BlockSpec((None, d, W_COLS), lambda bi, i: (layer, 0, 0)),
            pl.BlockSpec((TM, 256), lambda bi, i: (i, 0)),
            pl.BlockSpec((TM, 256), lambda bi, i: (i, 0)),
            const((8, 256)),
            const((256, 256)),
            const((128, 256)),
            const((1, 256)),
        ],
        out_specs=(tok(512), tok(256), tok(256), tok(512), tok(128), tok(128), tok(256), tok(256),
                   tok(256), tok(1024)),
        scratch_shapes=[pltpu.VMEM((TM, d), F32)],
        compiler_params=_cparams(("parallel", "arbitrary"), VMEM_LIMIT),
        name="inproj",
    )(x, ctx, mod, norm_w, w_perm, cos_t, sin_t, head_norms, seg_ones, wdec, bdec)


def _gla_kernel(af_ref, ab_ref, laf_ref, lab_ref, ones2_ref, trif_ref, trib_ref, pmf_ref, pmb_ref,
                qmask_ref, stmask_ref, of_ref, ob_ref, st_ref, rb_ref, *, nbatch, nreal):
    @pl.when(pl.program_id(0) == 0)
    def _():
        st_ref[...] = jnp.zeros_like(st_ref)

    @pl.when(pl.program_id(0) < nreal)
    def _():
        lane_head = lax.broadcasted_iota(jnp.int32, (GLA_CHUNK, GLA_HEADS * GLA_DV), 1) // GLA_DV
        shared = (qmask_ref[...], lane_head, stmask_ref[...])
        consts_f = (ones2_ref[...], trif_ref[...], pmf_ref[...]) + shared
        consts_b = (ones2_ref[...], trib_ref[...], pmb_ref[...]) + shared
        units = []
        for bi in range(nbatch):
            units.append(_gla_chunk(af_ref, laf_ref, of_ref, st_ref.at[0], rb_ref.at[0], bi, False, consts_f))
        for bi in range(nbatch):
            units.append(_gla_chunk(ab_ref, lab_ref, ob_ref, st_ref.at[1], rb_ref.at[1], bi, True, consts_b))
        _interleave(units, group=GLA_GROUP)

    @pl.when(pl.program_id(0) >= nreal)
    def _():
        of_ref[...] = jnp.zeros_like(of_ref)
        ob_ref[...] = jnp.zeros_like(ob_ref)


def _gla_chunk(a_ref, la_ref, o_ref, st_ref, rb_ref, bi, rev, consts):
    ones2, tri, pmask, qmask, lane_head, stmask = consts
    c, sb = GLA_CHUNK, GLA_SUB
    nsub = c // sb
    hk = GLA_HEADS * GLA_DK
    hv = GLA_HEADS * GLA_DV

    q = a_ref[bi, :, 0:hk]
    k = a_ref[bi, :, hk:2 * hk]
    v = a_ref[bi, :, 2 * hk:2 * hk + hv]
    la = la_ref[bi]

    la_hi = la.astype(BF16)
    la_lo = (la - la_hi.astype(F32)).astype(BF16)
    cum = _dot(tri, la_hi) + _dot(tri, la_lo)
    cumx = cum - la

    rb_ref[bi, 0] = cum
    rb_ref[bi, 1] = cumx
    rb_ref[bi, 2] = k
    rb_ref[bi, 3] = v[:, 0:hk]
    rb_ref[bi, 4] = v[:, hk:hv]
    yield

    def brow(plane, r, n=sb):
        return jnp.broadcast_to(rb_ref[bi, plane, r:r + 1, :], (n, hk))

    end = 0 if rev else c - 1

    def first_row(blk):
        return sb * blk + (sb - 1 if rev else 0)

    def blk_of(p):
        return nsub - 1 - p if rev else p

    cref_rows = jnp.concatenate([brow(1, first_row(blk)) for blk in range(nsub)], axis=0)

    st = st_ref[bi]
    qi = q * jnp.exp2(cum)
    inter = _dot_nt(qi.astype(BF16), st.astype(BF16))

    qd = q * jnp.exp2(cum - cref_rows)
    q_st = jnp.concatenate([qd] * GLA_HEADS, axis=0) * qmask
    kt, vt = [], []
    for p in range(1, nsub):
        lo, hi = (c - sb * p, c) if rev else (0, sb * p)
        kt.append(k[lo:hi] * jnp.exp2(brow(1, first_row(blk_of(p)), hi - lo) - cum[lo:hi]))
        vt.append(v[lo:hi])
    npad = pmask.shape[1] - sb * (nsub * (nsub - 1) // 2)
    kt.append(jnp.zeros((npad, hk), F32))
    vt.append(jnp.zeros((npad, hv), F32))
    kt = jnp.concatenate(kt, axis=0).astype(BF16)
    vt = jnp.concatenate(vt, axis=0).astype(BF16)
    sc = _dot_nt(q_st.astype(BF16), kt)
    cum_end = rb_ref[bi, 0, end:end + 1, :]
    kd = k * jnp.exp2(cum_end - cum)
    kv = _dot_tn(v.astype(BF16), kd.astype(BF16))

    t_loc = lax.broadcasted_iota(jnp.int32, (sb, hk), 0)
    valid = [(t_loc <= s) if rev else (t_loc >= s) for s in range(sb)]
    es = []
    for blk in range(nsub):
        r0 = sb * blk
        cb = cum[r0:r0 + sb]
        qb = q[r0:r0 + sb]
        for s in range(sb):
            diff = jnp.where(valid[s], cb - brow(0, r0 + s), -jnp.inf)
            es.append(jnp.exp2(diff) * (qb * brow(2, r0 + s)))
    e = jnp.concatenate(es, axis=0).astype(BF16)
    r = _dot(e, ones2)
    yield

    off_st = _dot(sc.astype(BF16) * pmask, vt)
    dgs = []
    for blk in range(nsub):
        acc = None
        for s in range(sb):
            row0 = (blk * sb + s) * sb
            vrow = jnp.concatenate([brow(3, blk * sb + s), brow(4, blk * sb + s)], axis=1)
            term = r[row0:row0 + sb] * vrow
            acc = term if acc is None else acc + term
        dgs.append(acc)
    st_ref[bi] = st * jnp.exp2(cum_end) + kv * stmask
    yield

    off = off_st[(GLA_HEADS - 1) * c:GLA_HEADS * c]
    for h in range(GLA_HEADS - 2, -1, -1):
        off = jnp.where(lane_head == h, off_st[h * c:(h + 1) * c], off)
    o_ref[bi] = (inter + off + jnp.concatenate(dgs, axis=0)).astype(BF16)


def _gla_consts(rev):
    c, sb = GLA_CHUNK, GLA_SUB
    nsub = c // sb
    t = np.arange(c)
    tri = (t[None, :] >= t[:, None]) if rev else (t[None, :] <= t[:, None])
    blk = t // sb
    p_of_t = (nsub - 1 - blk) if rev else blk
    col_p = np.concatenate([np.full(sb * p, p) for p in range(1, nsub)])
    col_p = np.concatenate([col_p, np.full(-col_p.size % 128, -1)])
    pmask = np.tile(p_of_t, GLA_HEADS)[:, None] == col_p[None, :]
    row_h = np.repeat(np.arange(GLA_HEADS), c)
    qmask = row_h[:, None] == (np.arange(GLA_HEADS * GLA_DK) // GLA_DK)[None, :]
    stmask = (np.arange(GLA_HEADS * GLA_DV) // GLA_DV)[:, None] == (np.arange(GLA_HEADS * GLA_DK) // GLA_DK)[None, :]
    f = lambda m: jnp.asarray(m, F32)
    return f(tri).astype(BF16), f(pmask).astype(BF16), f(qmask), f(stmask)


def _gla(a, la, ones2):
    b, t_all, _ = a.shape
    nch = t_all // GLA_CHUNK
    nctx = CTX_LEN // GLA_CHUNK
    nskip = (CTX_PAD - CTX_LEN) // GLA_CHUNK
    tri_f, pm_f, qmask, stmask = _gla_consts(False)
    tri_b, pm_b, _, _ = _gla_consts(True)
    consts = (ones2, tri_f, tri_b, pm_f, pm_b, qmask, stmask)

    nreal = nch - nskip

    def fwd(j):
        return jnp.where(j < nctx, j, jnp.where(j < nreal, j + nskip, nctx + (j - nreal)))

    def bwd(j):
        return jnp.where(j < nctx, nctx - 1 - j, jnp.where(j < nreal, nch - 1 - (j - nctx), nctx + (j - nreal)))

    out = jax.ShapeDtypeStruct((b, t_all, 256), BF16)
    return pl.pallas_call(
        functools.partial(_gla_kernel, nbatch=b, nreal=nreal),
        out_shape=(out, out),
        grid=(nch,),
        in_specs=[pl.BlockSpec((b, GLA_CHUNK, 512), lambda j: (0, fwd(j), 0)),
                  pl.BlockSpec((b, GLA_CHUNK, 512), lambda j: (0, bwd(j), 0)),
                  pl.BlockSpec((b, GLA_CHUNK, 128), lambda j: (0, fwd(j), 0)),
                  pl.BlockSpec((b, GLA_CHUNK, 128), lambda j: (0, bwd(j), 1))]
                 + [pl.BlockSpec(t.shape, lambda j: (0, 0)) for t in consts],
        out_specs=(pl.BlockSpec((b, GLA_CHUNK, 256), lambda j: (0, fwd(j), 0)),
                   pl.BlockSpec((b, GLA_CHUNK, 256), lambda j: (0, bwd(j), 0))),
        scratch_shapes=[pltpu.VMEM((2, b, 256, 128), F32), pltpu.VMEM((2, b, 5, GLA_CHUNK, 128), F32)],
        compiler_params=_cparams(("arbitrary",)),
        name="gla",
    )(a, a, la, la, *consts)


def _fnet_chan_kernel(v_ref, wc_ref, ws_ref, o_ref):
    v = v_ref[0]
    o_ref[0, :, :] = _dot(v, wc_ref[...]).astype(BF16)
    o_ref[1, :, :] = _dot(v, ws_ref[...]).astype(BF16)


def _fnet_chan(bv, wc, ws, row_blk_off, length):
    b = bv.shape[0]
    tl = min(TM, length)
    return pl.pallas_call(
        _fnet_chan_kernel,
        out_shape=jax.ShapeDtypeStruct((2, length, b * 256), BF16),
        grid=(b, length // tl),
        in_specs=[pl.BlockSpec((1, tl, 256), lambda bi, i: (bi, i + row_blk_off, 0)),
                  pl.BlockSpec((256, 256), lambda bi, i: (0, 0)),
                  pl.BlockSpec((256, 256), lambda bi, i: (0, 0))],
        out_specs=pl.BlockSpec((2, tl, 256), lambda bi, i: (0, i, bi)),
        compiler_params=_cparams(("parallel", "arbitrary")),
        name="fnet_chan",
    )(bv, wc, ws)


def _fnet_pos_kernel(d_ref, y_ref, wf_ref, o_ref, acc_ref, *, nb):
    kk = pl.program_id(1)

    @pl.when(kk == 0)
    def _():
        acc_ref[...] = jnp.zeros_like(acc_ref)

    acc_ref[...] += _dot(d_ref[...], y_ref[...])

    @pl.when(kk == pl.num_programs(1) - 1)
    def _():
        wf = wf_ref[...]
        for bi in range(nb):
            o_ref[bi] = _dot(acc_ref[:, bi * 256:(bi + 1) * 256].astype(BF16), wf).astype(BF16)


def _fnet_pos(dmat, y2, wf, nb):
    length, k2 = dmat.shape
    tm = min(512, length)
    tk = min(1024, k2)
    return pl.pallas_call(
        functools.partial(_fnet_pos_kernel, nb=nb),
        out_shape=jax.ShapeDtypeStruct((nb, length, 256), BF16),
        grid=(length // tm, k2 // tk),
        in_specs=[pl.BlockSpec((tm, tk), lambda i, kk: (i, kk)),
                  pl.BlockSpec((tk, nb * 256), lambda i, kk: (kk, 0)),
                  pl.BlockSpec((256, 256), lambda i, kk: (0, 0))],
        out_specs=pl.BlockSpec((nb, tm, 256), lambda i, kk: (0, i, 0)),
        scratch_shapes=[pltpu.VMEM((tm, nb * 256), F32)],
        compiler_params=_cparams(("parallel", "arbitrary"), VMEM_LIMIT),
        name="fnet_pos",
    )(dmat, y2, wf)


def _dft_tables(length):
    j = np.arange(length)
    ang = ((j[:, None] * j[None, :]) % length) * (2.0 * np.pi / length)
    s = 1.0 / math.sqrt(length)
    return jnp.asarray(np.concatenate([np.cos(ang) * s, -np.sin(ang) * s], axis=1), F32).astype(BF16)


def _chan_tables():
    j = np.arange(FNET_GW)
    ang = 2.0 * np.pi * ((j[:, None] * j[None, :]) % FNET_GW) / FNET_GW
    s = 1.0 / math.sqrt(FNET_GW)
    eye = np.eye(BRANCH_W // FNET_GW)
    wc = np.kron(eye, np.cos(ang) * s)
    ws = np.kron(eye, np.sin(ang) * s)
    return jnp.asarray(wc, F32).astype(BF16), jnp.asarray(ws, F32).astype(BF16)


def _fnet(bv, wf, row_blk_off, length):
    b = bv.shape[0]
    wc, ws = _chan_tables()
    y = _fnet_chan(bv, wc, ws, row_blk_off, length)
    y2 = y.reshape(2 * length, b * 256)
    return _fnet_pos(_dft_tables(length), y2, wf, b)


FNET_UNROLL = 16


def _fnet_grid_kernel(v_ref, cg_ref, sg_ref, w1_ref, w2_ref, wf_ref, o_ref, z_ref, g_ref, f_ref, *, rows, row_off):
    length = rows * GRID_W
    v = v_ref[0, row_off:row_off + length, :].astype(F32)
    z_ref[0] = v[:, 0:128].reshape(rows, GRID_W, 128)
    z_ref[1] = v[:, 128:256].reshape(rows, GRID_W, 128)
    wf = wf_ref[...]
    a = _dot(cg_ref[...], wf)
    b = _dot(sg_ref[...], wf)
    chan = jnp.concatenate([jnp.concatenate([a, -b], axis=1), jnp.concatenate([b, a], axis=1)], axis=0).astype(BF16)

    n8 = rows * 8
    for w0 in range(0, GRID_W, 8):
        rhs = jnp.concatenate([z_ref[0, :, w0:w0 + 8, :].reshape(n8, 128),
                               z_ref[1, :, w0:w0 + 8, :].reshape(n8, 128)], axis=1).astype(BF16)
        g = _dot(w1_ref[...], rhs)
        gc = jnp.concatenate([g[0:n8], g[n8:2 * n8]], axis=1).astype(BF16)
        y = _dot(gc, chan)
        for c in range(4):
            g_ref[c, :, w0:w0 + 8, :] = y[:, 128 * c:128 * (c + 1)].reshape(rows, 8, 128)

    def stage_w(ka, carry):
        r0 = pl.multiple_of(ka * GRID_W, GRID_W)
        rhs = jnp.concatenate([jnp.concatenate([g_ref[0, ka], g_ref[1, ka]], axis=1),
                               jnp.concatenate([g_ref[2, ka], g_ref[3, ka]], axis=1)], axis=0).astype(BF16)
        y = _dot(w2_ref[pl.ds(r0, GRID_W), :], rhs)
        f_ref[0, pl.ds(ka, GRID_W, stride=rows), :] = y[:, 0:128]
        f_ref[1, pl.ds(ka, GRID_W, stride=rows), :] = y[:, 128:256]
        return carry

    lax.fori_loop(0, rows, stage_w, 0, unroll=FNET_UNROLL)
    o_ref[0, :, 0:128] = f_ref[0].astype(BF16)
    o_ref[0, :, 128:256] = f_ref[1].astype(BF16)


def _fnet_grid_tables(rows):
    length = rows * GRID_W
    j = np.arange(rows)
    ang = 2.0 * np.pi * ((j[:, None] * j[None, :]) % rows) / rows
    c, s = np.cos(ang), np.sin(ang)
    w1 = np.kron(np.block([[c], [-s]]), np.eye(8))
    w1 = jnp.asarray(w1, F32).astype(BF16)
    ka = np.arange(rows)[:, None, None]
    kb = np.arange(GRID_W)[None, :, None]
    w = np.arange(GRID_W)[None, None, :]
    ang2 = ((w * (ka + rows * kb)) % length) * (2.0 * np.pi / length)
    scale = 1.0 / math.sqrt(length)
    w2 = np.concatenate([np.cos(ang2) * scale, np.sin(ang2) * scale], axis=2)
    return w1, jnp.asarray(w2.reshape(length, 2 * GRID_W), F32).astype(BF16)


def _fnet_grid(bv, wf, seq):
    b, t_all, _ = bv.shape
    rows = seq // GRID_W
    wc, ws = _chan_tables()
    w1, w2 = _fnet_grid_tables(rows)

    def const(shape):
        return pl.BlockSpec(shape, lambda bi: (0,) * len(shape))

    return pl.pallas_call(
        functools.partial(_fnet_grid_kernel, rows=rows, row_off=CTX_PAD),
        out_shape=jax.ShapeDtypeStruct((b, seq, 256), BF16),
        grid=(b,),
        in_specs=[pl.BlockSpec((1, t_all, 256), lambda bi: (bi, 0, 0)), const((256, 256)), const((256, 256)),
                  const((16 * rows, 8 * rows)), const((seq, 2 * GRID_W)), const((256, 256))],
        out_specs=pl.BlockSpec((1, seq, 256), lambda bi: (bi, 0, 0)),
        scratch_shapes=[pltpu.VMEM((2, rows, GRID_W, 128), F32), pltpu.VMEM((4, rows, GRID_W, 128), F32),
                        pltpu.VMEM((2, seq, 128), F32)],
        compiler_params=_cparams(("parallel",), VMEM_LIMIT),
        name="fnet_grid",
    )(bv, wc, ws, w1, w2, wf)


def _stack_heads(qg):
    lane = lax.broadcasted_iota(jnp.int32, qg.shape, 1)
    zero = jnp.zeros_like(qg)
    return jnp.concatenate([jnp.where(lane < HEAD_DIM, qg, zero),
                            jnp.where(lane >= HEAD_DIM, qg, zero)], axis=0)


def _gqa_unstack(o, rows, g):
    lane = lax.broadcasted_iota(jnp.int32, (rows, 2 * HEAD_DIM), 1)
    first, second = o[0:rows], o[rows:2 * rows]
    if g == 0:
        return jnp.where(lane < HEAD_DIM, first, pltpu.roll(second, HEAD_DIM, 1))
    return jnp.where(lane < HEAD_DIM, pltpu.roll(first, HEAD_DIM, 1), second)


def _unstack_heads(o, rows):
    lane = lax.broadcasted_iota(jnp.int32, (rows, 2 * HEAD_DIM), 1)
    return jnp.where(lane < HEAD_DIM, o[0:rows], o[rows:2 * rows])


def _swa_kernel(sink_ref, band_ref, q_ref, kp_ref, kc_ref, kn_ref, kx_ref, vp_ref, vc_ref, vn_ref, vx_ref,
                o_ref, *, nbatch):
    n = pl.program_id(0)
    nb = pl.num_programs(0)
    blk = SWA_BLOCK
    rows = 2 * blk
    col = lax.broadcasted_iota(jnp.int32, (1, 3 * blk), 1)
    col_lo = jnp.where(n > 0, 0, blk)
    col_hi = jnp.where(n < nb - 1, 3 * blk, 2 * blk)
    edge = jnp.where((col >= col_lo) & (col < col_hi), 0.0, -jnp.inf)
    bias = band_ref[...] + edge
    head_row = lax.broadcasted_iota(jnp.int32, (rows, 1), 0) < blk

    def unit(bi, g):
        q_st = jnp.concatenate([q_ref[bi, :, 256 * g:256 * g + 128],
                                q_ref[bi, :, 256 * g + 128:256 * g + 256]], axis=0)
        kw = jnp.concatenate([kp_ref[bi], kc_ref[bi], kn_ref[bi]], axis=0)
        s_loc = _dot_nt(q_st, kw) + bias
        s_ctx = _dot_nt(q_st, kx_ref[bi])
        yield
        vw = jnp.concatenate([vp_ref[bi], vc_ref[bi], vn_ref[bi]], axis=0)
        sink = jnp.where(head_row, sink_ref[2 * g], sink_ref[2 * g + 1])
        m = jnp.maximum(jnp.maximum(jnp.max(s_loc, axis=-1, keepdims=True),
                                    jnp.max(s_ctx, axis=-1, keepdims=True)), sink)
        p_loc = jnp.exp2(s_loc - m)
        p_ctx = jnp.exp2(s_ctx - m)
        den = (jnp.sum(p_loc, axis=-1, keepdims=True) + jnp.sum(p_ctx, axis=-1, keepdims=True)
               + jnp.exp2(sink - m))
        o = _dot(p_loc.astype(BF16), vw) + _dot(p_ctx.astype(BF16), vx_ref[bi])
        yield
        o_ref[bi, :, 128 * g:128 * (g + 1)] = _gqa_unstack(o / den, blk, g).astype(BF16)

    _interleave([unit(bi, g) for bi in range(nbatch) for g in range(2)], group=ATTN_GROUP)


def _swa_band():
    row = np.arange(2 * SWA_BLOCK)[:, None] % SWA_BLOCK
    col = np.arange(3 * SWA_BLOCK)[None, :]
    ok = np.abs(col - SWA_BLOCK - row) <= SWA_WINDOW
    return jnp.where(jnp.asarray(ok), 0.0, -jnp.inf).astype(F32)


def _swa(sink, cq, ck, cv, seq):
    b = cq.shape[0]
    nb = seq // SWA_BLOCK
    off = CTX_PAD // SWA_BLOCK

    def at(f, width=128):
        return pl.BlockSpec((b, SWA_BLOCK, width), lambda n: (0, f(n) + off, 0))

    prev = at(lambda n: jnp.maximum(n - 1, 0))
    cur = at(lambda n: n)
    nxt = at(lambda n: jnp.minimum(n + 1, nb - 1))
    cx = pl.BlockSpec((b, CTX_LEN, 128), lambda n: (0, 0, 0))
    band = pl.BlockSpec((2 * SWA_BLOCK, 3 * SWA_BLOCK), lambda n: (0, 0))
    return pl.pallas_call(
        functools.partial(_swa_kernel, nbatch=b),
        out_shape=jax.ShapeDtypeStruct((b, seq, 256), BF16),
        grid=(nb,),
        in_specs=[pl.BlockSpec(memory_space=pltpu.SMEM), band, at(lambda n: n, 512),
                  prev, cur, nxt, cx, prev, cur, nxt, cx],
        out_specs=pl.BlockSpec((b, SWA_BLOCK, 256), lambda n: (0, n, 0)),
        compiler_params=_cparams(("arbitrary",)),
        name="swa",
    )(sink, _swa_band(), cq, ck, ck, ck, ck, cv, cv, cv, cv)


NA_ROWS_PER_STEP = 8


def _na_kernel(q_ref, k_ref, v_ref, bias_ref, o_ref, *, rows_total):
    _interleave(_na_units(q_ref, k_ref, v_ref, bias_ref, o_ref, rows_total), group=ATTN_GROUP)


def _na_units(q_ref, k_ref, v_ref, bias_ref, o_ref, rows_total):
    kh = NA_KH
    nk = kh * GRID_W
    def unit(rr, g):
        r = pl.program_id(1) * NA_ROWS_PER_STEP + rr
        row_start = jnp.clip(r - kh // 2, 0, rows_total - kh)
        cls = r - row_start
        k0 = pl.multiple_of(CTX_PAD + row_start * GRID_W, GRID_W)
        qs = slice(GRID_W * rr, GRID_W * (rr + 1))
        ls = slice(128 * g, 128 * (g + 1))
        q_st = _stack_heads(q_ref[0, qs, ls])
        s_nb = _dot_nt(q_st, k_ref[0, pl.ds(k0, nk), ls]) + bias_ref[cls, g]
        s_ctx = _dot_nt(q_st, k_ref[0, 0:CTX_LEN, ls])
        yield
        m = jnp.maximum(jnp.max(s_nb, axis=-1, keepdims=True), jnp.max(s_ctx, axis=-1, keepdims=True))
        p_nb = jnp.exp2(s_nb - m)
        p_ctx = jnp.exp2(s_ctx - m)
        den = jnp.sum(p_nb, axis=-1, keepdims=True) + jnp.sum(p_ctx, axis=-1, keepdims=True)
        o = (_dot(p_nb.astype(BF16), v_ref[0, pl.ds(k0, nk), ls])
             + _dot(p_ctx.astype(BF16), v_ref[0, 0:CTX_LEN, ls]))
        yield
        o_ref[0, qs, ls] = _unstack_heads(o / den, GRID_W).astype(BF16)

    return [unit(rr, g) for rr in range(NA_ROWS_PER_STEP) for g in range(2)]


def _swa_row_units(sink_ref, band_ref, q_ref, kp_ref, kc_ref, kn_ref, kx_ref, vp_ref, vc_ref, vn_ref, vx_ref,
                   o_ref):
    r = pl.program_id(1)
    nsteps = pl.num_programs(1)
    blk = SWA_BLOCK
    nsub = q_ref.shape[1] // blk
    col = lax.broadcasted_iota(jnp.int32, (1, 3 * blk), 1)
    no_prev = jnp.where((col < blk) & (r == 0), -jnp.inf, 0.0)
    no_next = jnp.where((col >= 2 * blk) & (r == nsteps - 1), -jnp.inf, 0.0)
    band = band_ref[...]
    head_row = lax.broadcasted_iota(jnp.int32, (2 * blk, 1), 0) < blk
    k_all = jnp.concatenate([kp_ref[0], kc_ref[0], kn_ref[0]], axis=0)
    v_all = jnp.concatenate([vp_ref[0], vc_ref[0], vn_ref[0]], axis=0)

    def unit(j, g):
        rows = slice(blk * j, blk * (j + 1))
        q_st = jnp.concatenate([q_ref[0, rows, 256 * g:256 * g + 128],
                                q_ref[0, rows, 256 * g + 128:256 * g + 256]], axis=0)
        bias = band + (no_prev if j == 0 else 0.0) + (no_next if j == nsub - 1 else 0.0)
        s_loc = _dot_nt(q_st, k_all[blk * j:blk * (j + 3)]) + bias
        s_ctx = _dot_nt(q_st, kx_ref[0])
        yield
        sink = jnp.where(head_row, sink_ref[2 * g], sink_ref[2 * g + 1])
        m = jnp.maximum(jnp.maximum(jnp.max(s_loc, axis=-1, keepdims=True),
                                    jnp.max(s_ctx, axis=-1, keepdims=True)), sink)
        p_loc = jnp.exp2(s_loc - m)
        p_ctx = jnp.exp2(s_ctx - m)
        den = (jnp.sum(p_loc, axis=-1, keepdims=True) + jnp.sum(p_ctx, axis=-1, keepdims=True)
               + jnp.exp2(sink - m))
        o = _dot(p_loc.astype(BF16), v_all[blk * j:blk * (j + 3)]) + _dot(p_ctx.astype(BF16), vx_ref[0])
        yield
        o_ref[0, rows, 128 * g:128 * (g + 1)] = _gqa_unstack(o / den, blk, g).astype(BF16)

    return [unit(j, g) for j in range(nsub) for g in range(2)]


def _swa_na_kernel(sink_ref, band_ref, cq_ref, kp_ref, kc_ref, kn_ref, kx_ref, vp_ref, vc_ref, vn_ref, vx_ref,
                   dq_ref, dk_ref, dv_ref, bias_ref, co_ref, do_ref, *, rows_total):
    swa = _swa_row_units(sink_ref, band_ref, cq_ref, kp_ref, kc_ref, kn_ref, kx_ref, vp_ref, vc_ref, vn_ref,
                         vx_ref, co_ref)
    na = _na_units(dq_ref, dk_ref, dv_ref, bias_ref, do_ref, rows_total)
    units = []
    for i in range(len(swa)):
        units += [na[2 * i], na[2 * i + 1], swa[i]]
    _interleave(units, group=SWA_NA_GROUP)


def _swa_na(sink, cq, ck, cv, dq, dk, dv, bias_tab, seq):
    b, t_all, _ = dq.shape
    rows_total = seq // GRID_W
    qrows = NA_ROWS_PER_STEP * GRID_W
    assert CTX_PAD % qrows == 0 and seq % qrows == 0 and qrows % SWA_BLOCK == 0
    nsteps = seq // qrows
    off = CTX_PAD // qrows
    per = qrows // SWA_BLOCK
    koff = CTX_PAD // SWA_BLOCK
    nblk = seq // SWA_BLOCK

    def rows_spec(width):
        return pl.BlockSpec((1, qrows, width), lambda bi, r: (bi, r + off, 0))

    prev = pl.BlockSpec((1, SWA_BLOCK, 128), lambda bi, r: (bi, jnp.maximum(per * r - 1, 0) + koff, 0))
    nxt = pl.BlockSpec((1, SWA_BLOCK, 128), lambda bi, r: (bi, jnp.minimum(per * r + per, nblk - 1) + koff, 0))
    cx = pl.BlockSpec((1, CTX_LEN, 128), lambda bi, r: (bi, 0, 0))
    full = pl.BlockSpec((1, t_all, 256), lambda bi, r: (bi, 0, 0))
    out = jax.ShapeDtypeStruct((b, seq, 256), BF16)
    out_spec = pl.BlockSpec((1, qrows, 256), lambda bi, r: (bi, r, 0))
    return pl.pallas_call(
        functools.partial(_swa_na_kernel, rows_total=rows_total),
        out_shape=(out, out),
        grid=(b, nsteps),
        in_specs=[pl.BlockSpec(memory_space=pltpu.SMEM),
                  pl.BlockSpec((2 * SWA_BLOCK, 3 * SWA_BLOCK), lambda bi, r: (0, 0)),
                  rows_spec(512), prev, rows_spec(128), nxt, cx, prev, rows_spec(128), nxt, cx,
                  rows_spec(256), full, full, pl.BlockSpec(bias_tab.shape, lambda bi, r: (0, 0, 0, 0))],
        out_specs=(out_spec, out_spec),
        compiler_params=_cparams(("parallel", "arbitrary"), VMEM_LIMIT),
        name="swa_na",
    )(sink, _swa_band(), cq, ck, ck, ck, ck, cv, cv, cv, cv, dq, dk, dv, bias_tab)


def _bias_expand_kernel(rb_ref, oh_ref, o_ref):
    x = rb_ref[...]
    oh = oh_ref[...]
    hi = x.astype(BF16)
    r1 = x - hi.astype(F32)
    mid = r1.astype(BF16)
    lo = (r1 - mid.astype(F32)).astype(BF16)
    o_ref[...] = (_dot(hi, oh) + _dot(mid, oh) + _dot(lo, oh)) * LOG2E


def _na_bias_table(rel_bias):
    ndy, ndx = 2 * NA_KH - 1, 2 * NA_KW - 1
    cq = np.arange(GRID_W)
    col_start = np.clip(cq - NA_KW // 2, 0, GRID_W - NA_KW)
    col_ok = (cq[None, :] >= col_start[:, None]) & (cq[None, :] < col_start[:, None] + NA_KW)
    qi = jnp.arange(GRID_W, dtype=jnp.int32)
    dx = jnp.clip(qi[None, :] - qi[:, None], -(NA_KW - 1), NA_KW - 1) + (NA_KW - 1)
    onehot = (jnp.arange(128, dtype=jnp.int32)[:, None] == dx.reshape(1, GRID_W * GRID_W)).astype(BF16)
    rb = jnp.zeros((64, 128), F32).at[0:NA_HEADS * ndy, 0:ndx].set(rel_bias.reshape(NA_HEADS * ndy, ndx))
    p = pl.pallas_call(
        _bias_expand_kernel,
        out_shape=jax.ShapeDtypeStruct((64, GRID_W * GRID_W), F32),
        name="na_bias_expand",
    )(rb, onehot)
    p = p[0:NA_HEADS * ndy].reshape(NA_HEADS, ndy, GRID_W, GRID_W)
    p = jnp.where(col_ok[None, None], p, -jnp.inf)
    tabs = jnp.stack([p[:, NA_KH - 1 - c:2 * NA_KH - 1 - c] for c in range(NA_KH)])
    tabs = tabs.transpose(0, 1, 3, 2, 4)
    return tabs.reshape(NA_KH, NA_HEADS // 2, 2 * GRID_W, NA_KH * GRID_W)


def _na(dq, dk, dv, bias_tab, seq):
    b, t_all, _ = dq.shape
    rows_total = seq // GRID_W
    qrows = NA_ROWS_PER_STEP * GRID_W
    assert CTX_PAD % qrows == 0 and rows_total % NA_ROWS_PER_STEP == 0
    off = CTX_PAD // qrows
    full = pl.BlockSpec((1, t_all, 256), lambda bi, r: (bi, 0, 0))
    return pl.pallas_call(
        functools.partial(_na_kernel, rows_total=rows_total),
        out_shape=jax.ShapeDtypeStruct((b, seq, 256), BF16),
        grid=(b, rows_total // NA_ROWS_PER_STEP),
        in_specs=[pl.BlockSpec((1, qrows, 256), lambda bi, r: (bi, r + off, 0)), full, full,
                  pl.BlockSpec(bias_tab.shape, lambda bi, r: (0, 0, 0, 0))],
        out_specs=pl.BlockSpec((1, qrows, 256), lambda bi, r: (bi, r, 0)),
        compiler_params=_cparams(("parallel", "arbitrary"), VMEM_LIMIT),
        name="na",
    )(dq, dk, dv, bias_tab)


def _ctx_attn_kernel(sink_ref, q_ref, k_ref, v_ref, o_ref, *, gqa):
    n = CTX_LEN
    head_row = lax.broadcasted_iota(jnp.int32, (2 * n, 1), 0) < n
    for g in range(2):
        ls = slice(128 * g, 128 * (g + 1))
        if gqa:
            q_st = jnp.concatenate([q_ref[0, :, 256 * g:256 * g + 128],
                                    q_ref[0, :, 256 * g + 128:256 * g + 256]], axis=0)
            k, v = k_ref[0], v_ref[0]
        else:
            q_st = _stack_heads(q_ref[0, :, ls])
            k, v = k_ref[0, :, ls], v_ref[0, :, ls]
        s = _dot_nt(q_st, k)
        m = jnp.max(s, axis=-1, keepdims=True)
        if gqa:
            sink = jnp.where(head_row, sink_ref[2 * g], sink_ref[2 * g + 1])
            m = jnp.maximum(m, sink)
        p = jnp.exp2(s - m)
        den = jnp.sum(p, axis=-1, keepdims=True)
        if gqa:
            den = den + jnp.exp2(sink - m)
        o = _dot(p.astype(BF16), v) / den
        o_ref[0, :, ls] = (_gqa_unstack(o, n, g) if gqa else _unstack_heads(o, n)).astype(BF16)


def _ctx_attn(sink, q, k, v, gqa):
    b = q.shape[0]

    def blk(arr):
        return pl.BlockSpec((1, CTX_LEN, arr.shape[2]), lambda bi: (bi, 0, 0))

    return pl.pallas_call(
        functools.partial(_ctx_attn_kernel, gqa=gqa),
        out_shape=jax.ShapeDtypeStruct((b, CTX_LEN, 256), BF16),
        grid=(b,),
        in_specs=[pl.BlockSpec(memory_space=pltpu.SMEM), blk(q), blk(k), blk(v)],
        out_specs=pl.BlockSpec((1, CTX_LEN, 256), lambda bi: (bi, 0, 0)),
        compiler_params=_cparams(("parallel",)),
        name="ctx_attn_sink" if gqa else "ctx_attn",
    )(sink, q, k, v)


def _outproj_kernel(x_ref, of_ref, ob_ref, fz_ref, co_ref, do_ref, g_ref, mod_ref, gn_ref, seg_ref,
                    w_ref, o_ref):
    a = of_ref[0].astype(F32) + ob_ref[0].astype(F32)
    ms = _dot((a * a).astype(BF16), seg_ref[...]) * (1.0 / GLA_DV)
    a = a * lax.rsqrt(ms + EPS) * gn_ref[...]
    g = g_ref[0].astype(F32)
    mixed =jnp.concatenate([a * g[:, 0:256], fz_ref[0].astype(F32) * g[:, 256:512],
                             co_ref[0].astype(F32) * g[:, 512:768], do_ref[0].astype(F32) * g[:, 768:1024]],
                            axis=1)
    y = _dot(mixed.astype(BF16), w_ref[...])
    o_ref[0] = x_ref[0] + mod_ref[0, 2:3, :] * y


def _outproj(x, o_f, o_b, fz, c_o, d_o, g, mod, gla_norm, seg_ones, w_out, is_ctx):
    b, rows, d = x.shape
    tm = CTX_LEN if is_ctx else TM
    nblk = rows // tm
    comb = 0 if is_ctx else CTX_PAD // tm
    nbatch = b

    def own(width):
        return pl.BlockSpec((1, tm, width), lambda bi, i: (bi, i, 0))

    def combined(width):
        return pl.BlockSpec((1, tm, width), lambda bi, i: (bi, i + comb, 0))

    def const(shape):
        return pl.BlockSpec(shape, lambda bi, i: (0,) * len(shape))

    mod_spec = pl.BlockSpec((1, 3, d), (lambda bi, i: (nbatch, 0, 0)) if is_ctx else (lambda bi, i: (bi, 0, 0)))
    return pl.pallas_call(
        _outproj_kernel,
        out_shape=jax.ShapeDtypeStruct((b, rows, d), F32),
        grid=(b, nblk),
        in_specs=[own(d), combined(256), combined(256), own(256), own(256), own(256), combined(1024),
                  mod_spec, const((1, 256)), const((256, 256)), const((d, d))],
        out_specs=own(d),
        compiler_params=_cparams(("parallel", "arbitrary"), VMEM_LIMIT),
        name="outproj_ctx" if is_ctx else "outproj",
    )(x, o_f, o_b, fz, c_o, d_o, g, mod, gla_norm, seg_ones, w_out)


_W_IN_MOVES = (
    (C_A, 0, 512), (C_BV, 800, 256), (C_CQ, 1312, 256),
    (C_CK, 1568, 128), (C_CV, 1696, 128),
    (C_DQ, 2080, 256), (C_DK, 2336, 256), (C_DV, 2592, 256),
    (C_G, 512, 256), (C_G + 256, 1056, 256), (C_G + 512, 1824, 256), (C_G + 768, 2848, 256),
    (C_LR, 768, 2 * GLA_RANK),
)


def _w_in_prep_kernel(wt_ref, o_ref):
    for dst, src, width in _W_IN_MOVES:
        o_ref[0, :, dst:dst + width] = wt_ref[0, src:src + width, :].T.astype(BF16)
    pad0 = C_LR + 2 * GLA_RANK
    o_ref[0, :, pad0:W_COLS] = jnp.zeros((o_ref.shape[1], W_COLS - pad0), BF16)


def _permute_w_in(w_all):
    depth, d, n = w_all.shape
    kblk = 256
    return pl.pallas_call(
        _w_in_prep_kernel,
        out_shape=jax.ShapeDtypeStruct((depth, d, W_COLS), BF16),
        grid=(depth, d // kblk),
        in_specs=[pl.BlockSpec((1, n, kblk), lambda l, i: (l, 0, i))],
        out_specs=pl.BlockSpec((1, kblk, W_COLS), lambda l, i: (l, i, 0)),
        compiler_params=_cparams(("parallel", "parallel")),
        name="w_in_prep",
    )(jnp.swapaxes(w_all, 1, 2))


def _rope_tables(seq):
    t = np.arange(seq)
    axis_dim = HEAD_DIM // 2
    inv = ROPE_BASE ** (-np.arange(0, axis_dim, 2, dtype=np.float64) / axis_dim)
    row = (t // GRID_W).astype(np.float64)
    col = (t % GRID_W).astype(np.float64)
    ang = np.concatenate([row[:, None] * inv, row[:, None] * inv, col[:, None] * inv, col[:, None] * inv], axis=1)
    cos = np.cos(ang)
    sign = np.concatenate([-np.ones(16), np.ones(16), -np.ones(16), np.ones(16)])
    sin = np.sin(ang) * sign
    cos = np.concatenate([np.ones((CTX_PAD, HEAD_DIM)), cos], axis=0)
    sin = np.concatenate([np.zeros((CTX_PAD, HEAD_DIM)), sin], axis=0)
    return (jnp.asarray(np.tile(cos, (1, 4)), F32), jnp.asarray(np.tile(sin, (1, 4)), F32))


def _seg_ones(width, seg):
    i = np.arange(width)
    return jnp.asarray((i[:, None] // seg) == (i[None, :] // seg), BF16)


def _gla_ones2():
    r = np.arange(GLA_HEADS * GLA_DK) // GLA_DK
    c = np.arange(GLA_HEADS * GLA_DV) // GLA_DV
    return jnp.asarray(r[:, None] == c[None, :], BF16)


def _layer(x, ctx, mod, need_ctx, tables, norm_w, w_in_perm, layer, w_out, gla_dec_w, gla_dec_b,
           gla_out_norm, fnet_w, swa_q_norm, swa_k_norm, swa_sink, na_q_norm, na_k_norm, na_rel_bias):
    b, seq, d = x.shape
    cos_t, sin_t, seg64, ones2 = tables

    qs = HEAD_DIM ** -0.5 * LOG2E
    swa_sink = swa_sink * LOG2E
    head_norms = jnp.stack([jnp.tile(swa_q_norm, 4) * qs, jnp.tile(swa_k_norm, 4),
                            jnp.tile(na_q_norm, 4) * qs, jnp.tile(na_k_norm, 4)])
    head_norms = jnp.concatenate([head_norms, jnp.zeros((4, 256), F32)], axis=0)
    wdec = jnp.zeros((128, 256), F32)
    wdec = wdec.at[0:GLA_RANK, 0:128].set(gla_dec_w[0]).at[GLA_RANK:2 * GLA_RANK, 128:256].set(gla_dec_w[1])
    bdec = gla_dec_b.reshape(1, 256)

    a, la, bv, cq, ck, cv, dq, dk, dv, g = _inproj(
        x, ctx, mod, norm_w[None, :], w_in_perm, layer, cos_t, sin_t, head_norms, seg64,
        wdec.astype(BF16), bdec)

    o_f, o_b = _gla(a, la, ones2)
    wf = fnet_w.astype(BF16)
    fz = _fnet_grid(bv, wf, seq)
    c_o, d_o = _swa_na(swa_sink, cq, ck, cv, dq, dk, dv, _na_bias_table(na_rel_bias), seq)

    gn = jnp.tile(gla_out_norm, 4)[None, :]
    wo = w_out.astype(BF16)
    x_new = _outproj(x, o_f, o_b, fz, c_o, d_o, g, mod, gn, seg64, wo, is_ctx=False)
    ctx_new = None
    if need_ctx:
        fzc = _fnet(bv, wf, 0, CTX_LEN)
        c_oc = _ctx_attn(swa_sink, cq, ck, cv, gqa=True)
        d_oc = _ctx_attn(swa_sink, dq, dk, dv, gqa=False)
        ctx_new = _outproj(ctx, o_f, o_b, fzc, c_oc, d_oc, g, mod, gn, seg64, wo, is_ctx=True)
    return x_new, ctx_new


def kernel(x, c, ctx, c_ctx, norm_w, ada_w, ada_b, w_in, w_out, gla_dec_w, gla_dec_b, gla_out_norm,
           fnet_w, swa_q_norm, swa_k_norm, swa_sink, na_q_norm, na_k_norm, na_rel_bias):
    b, seq, d = x.shape
    depth = norm_w.shape[0]
    assert ctx.shape[1] == CTX_LEN and seq % TM == 0 and seq % (NA_KH * GRID_W) == 0
    tables = _rope_tables(seq) + (_seg_ones(256, HEAD_DIM), _gla_ones2())
    cc = jnp.concatenate([c, c_ctx[None, :], jnp.zeros((8 - (b + 1) % 8, d), F32)], axis=0)
    mod = _ada(cc, ada_w, ada_b[:, None, :])
    w_in_perm = _permute_w_in(w_in)
    for i in range(depth):
        x, ctx = _layer(x, ctx, mod[i], i < depth - 1, tables, norm_w[i], w_in_perm, i, w_out[i],
                        gla_dec_w[i], gla_dec_b[i], gla_out_norm[i], fnet_w[i], swa_q_norm[i],
                        swa_k_norm[i], swa_sink[i], na_q_norm[i], na_k_norm[i], na_rel_bias[i])
    return x
```

```python
import functools
import math

import numpy as np
import jax
import jax.numpy as jnp
from jax import lax
from jax.experimental import pallas as pl
from jax.experimental.pallas import tpu as pltpu

F32 = jnp.float32
BF16 = jnp.bfloat16

GRID_W = 64
CTX_LEN = 256
HEAD_DIM = 64
BRANCH_W = 256
EPS = 1e-6
ROPE_BASE = 10000.0
GLA_HEADS = 4
GLA_DK = 32
GLA_DV = 64
GLA_RANK = 16
GLA_TAU = 16.0
GLA_CHUNK = 64
GLA_SUB = 8
LOG2E = 1.4426950408889634
FNET_GW = 64
SWA_BLOCK = 128
SWA_WINDOW = 128
NA_KH = 8
NA_KW = 16
NA_HEADS = 4

TM = 512
CTX_PAD = TM
VMEM_LIMIT = 48 * 1024 * 1024

C_A = 0
C_BV = 512
C_CQ = 768
C_CK = 1024
C_CV = 1152
C_DQ = 1280
C_DK = 1536
C_DV = 1792
C_G = 2048
C_LR = 3072
W_COLS = 3200


def _dot(a, b):
    return jnp.dot(a, b, preferred_element_type=F32)


def _dot_nt(a, b):
    return lax.dot_general(a, b, (((1,), (1,)), ((), ())), preferred_element_type=F32)


def _dot_tn(a, b):
    return lax.dot_general(a, b, (((0,), (0,)), ((), ())), preferred_element_type=F32)


GLA_GROUP = 8
INPROJ_ROW_SPLIT = 2
INPROJ_GROUP = 4
SWA_NA_GROUP = 3
ATTN_GROUP = 4
_DONE = object()


def _interleave(units, group=None):
    group = group or len(units)
    for i in range(0, len(units), group):
        live = units[i:i + group]
        while live:
            live = [u for u in live if next(u, _DONE) is not _DONE]


def _skew(units, depth):
    started = []
    for u in units:
        next(u)
        started.append(u)
        if len(started) > depth:
            next(started.pop(0), _DONE)
    for u in started:
        next(u, _DONE)


def _cparams(sem, vmem=None):
    return pltpu.CompilerParams(dimension_semantics=sem, vmem_limit_bytes=vmem)


def _ada_kernel(c_ref, w_ref, b_ref, o_ref):
    c = c_ref[...]
    s = c * (1.0 / (1.0 + jnp.exp(-c)))
    d = c.shape[1]
    for j in range(3):
        cols = slice(j * d, (j + 1) * d)
        o_ref[0, :, j, :] = _dot(s.astype(BF16), w_ref[0, :, cols].astype(BF16)) + b_ref[0, :, cols]


def _ada(cc, ada_w, ada_b):
    r, d = cc.shape
    depth, _, n = ada_w.shape
    return pl.pallas_call(
        _ada_kernel,
        out_shape=jax.ShapeDtypeStruct((depth, r, 3, d), F32),
        grid=(depth,),
        in_specs=[pl.BlockSpec((r, d), lambda l: (0, 0)),
                  pl.BlockSpec((1, d, n), lambda l: (l, 0, 0)),
                  pl.BlockSpec((1, 1, n), lambda l: (l, 0, 0))],
        out_specs=pl.BlockSpec((1, r, 3, d), lambda l: (l, 0, 0, 0)),
        compiler_params=_cparams(("parallel",), VMEM_LIMIT),
        name="ada",
    )(cc, ada_w, ada_b)


def _head_rms(x, seg_ones, w):
    ms = _dot((x * x).astype(BF16), seg_ones) * (1.0 / HEAD_DIM)
    return x * lax.rsqrt(ms + EPS) * w


def _rope(y, cos, sin_signed):
    lane = lax.broadcasted_iota(jnp.int32, y.shape, 1)
    first_half = (lane % 32) < 16
    n = y.shape[1]
    swapped = jnp.where(first_half, pltpu.roll(y, n - 16, 1), pltpu.roll(y, 16, 1))
    return y * cos + swapped * sin_signed


def _inproj_kernel(x_ref, ctx_ref, mod_ref, nw_ref, w_ref, cos_ref, sin_ref, hn_ref, seg_ref,
                   wdec_ref, bdec_ref,
                   a_ref, la_ref, bv_ref, cq_ref, ck_ref, cv_ref, dq_ref, dk_ref, dv_ref, g_ref,
                   xs_ref):
    i = pl.program_id(1)

    @pl.when(i == 0)
    def _():
        xs_ref[0:CTX_LEN, :] = ctx_ref[0]
        xs_ref[CTX_LEN:, :] = jnp.zeros((TM - CTX_LEN, xs_ref.shape[1]), F32)

    @pl.when(i > 0)
    def _():
        xs_ref[...] = x_ref[0]

    seg = seg_ref[...]
    half = TM // INPROJ_ROW_SPLIT
    hb = [None] * INPROJ_ROW_SPLIT

    def prologue(part):
        rs = slice(part * half, (part + 1) * half)
        x = xs_ref[rs, :]
        ms = jnp.mean(x * x, axis=-1, keepdims=True)
        y = x * lax.rsqrt(ms + EPS) * nw_ref[...]
        hb[part] = (y * (1.0 + mod_ref[0, 1:2, :]) + mod_ref[0, 0:1, :]).astype(BF16)
        yield

    def gla_qkv(part, rs):
        a = _dot(hb[part], w_ref[:, C_A:C_A + 512])
        yield
        a_ref[0, rs, 0:128] = a[:, 0:128] * (GLA_DK ** -0.5)
        a_ref[0, rs, 128:512] = a[:, 128:512]

    def plain(part, rs, c0, ref, width=256):
        r = _dot(hb[part], w_ref[:, c0:c0 + width])
        yield
        ref[0, rs, :] = r.astype(BF16)

    def normed(part, rs, c0, ref, row, rope, width=256):
        r = _dot(hb[part], w_ref[:, c0:c0 + width])
        yield
        y = _head_rms(r, seg[0:width, 0:width], hn_ref[row:row + 1, 0:width])
        ref[0, rs, :] = (_rope(y, cos_ref[rs, 0:width], sin_ref[rs, 0:width]) if rope else y).astype(BF16)

    def swa_kv(part, rs):
        r = _dot(hb[part], w_ref[:, C_CK:C_CK + 256])
        yield
        y = _head_rms(r[:, 0:128], seg[0:128, 0:128], hn_ref[1:2, 0:128])
        ck_ref[0, rs, :] = _rope(y, cos_ref[rs, 0:128], sin_ref[rs, 0:128]).astype(BF16)
        cv_ref[0, rs, :] = r[:, 128:256].astype(BF16)

    def swa_q(part, rs):
        r = _dot(hb[part], w_ref[:, C_CQ:C_CQ + 256])
        yield
        y = _rope(_head_rms(r, seg, hn_ref[0:1, :]), cos_ref[rs, :], sin_ref[rs, :])
        up = pltpu.roll(y, 64, 1)
        down = pltpu.roll(y, 192, 1)
        low = lax.broadcasted_iota(jnp.int32, (y.shape[0], 128), 1) < HEAD_DIM
        zero = jnp.zeros((y.shape[0], 128), F32)
        parts = [jnp.where(low, y[:, 0:128], zero), jnp.where(low, down[:, 0:128], zero),
                 jnp.where(low, zero, up[:, 128:256]), jnp.where(low, zero, y[:, 128:256])]
        cq_ref[0, rs, :] = jnp.concatenate(parts, axis=1).astype(BF16)

    def gates(part, rs, j):
        g = _dot(hb[part], w_ref[:, C_G + 256 * j:C_G + 256 * (j + 1)])
        yield
        g_ref[0, rs, 256 * j:256 * (j + 1)] = (g * (1.0 / (1.0 + jnp.exp(-g)))).astype(BF16)

    def decay(part, rs):
        lr = _dot(hb[part], w_ref[:, C_LR:C_LR + 128])
        yield
        z = _dot(lr.astype(BF16), wdec_ref[...]) + bdec_ref[...]
        la_ref[0, rs, :] = (jnp.minimum(z, 0.0) - jnp.log(1.0 + jnp.exp(-jnp.abs(z)))) * (LOG2E / GLA_TAU)

    def groups(part):
        rs = slice(part * half, (part + 1) * half)
        return [decay(part, rs), gates(part, rs, 0), swa_q(part, rs), gates(part, rs, 1), swa_kv(part, rs),
                gates(part, rs, 2), normed(part, rs, C_DQ, dq_ref, 2, False), gates(part, rs, 3),
                normed(part, rs, C_DK, dk_ref, 3, False),
                gla_qkv(part, rs), plain(part, rs, C_BV, bv_ref), plain(part, rs, C_DV, dv_ref)]

    units = [prologue(0)]
    for part in range(INPROJ_ROW_SPLIT):
        g = groups(part)
        if part + 1 < INPROJ_ROW_SPLIT:
            g.insert(INPROJ_GROUP, prologue(part + 1))
        units += g
    _skew(units, INPROJ_GROUP)


def _inproj(x, ctx, mod, norm_w, w_perm, layer, cos_t, sin_t, head_norms, seg_ones, wdec, bdec):
    b, seq, d = x.shape
    t_all = CTX_PAD + seq
    nblk = t_all // TM

    def tok(width):
        return pl.BlockSpec((1, TM, width), lambda bi, i: (bi, i, 0))

    def const(shape):
        return pl.BlockSpec(shape, lambda bi, i: (0,) * len(shape))

    def out(width, dtype):
        return jax.ShapeDtypeStruct((b, t_all, width), dtype)

    return pl.pallas_call(
        _inproj_kernel,
        out_shape=(out(512, F32), out(256, F32), out(256, BF16), out(512, BF16), out(128, BF16),
                   out(128, BF16), out(256, BF16), out(256, BF16), out(256, BF16), out(1024, BF16)),
        grid=(b, nblk),
        in_specs=[
            pl.BlockSpec((1, TM, d), lambda bi, i: (bi, jnp.maximum(i - 1, 0), 0)),
            pl.BlockSpec((1, CTX_LEN, d), lambda bi, i: (bi, 0, 0)),
            pl.BlockSpec((1, 3, d), lambda bi, i: (jnp.where(i == 0, b, bi), 0, 0)),
            const((1, d)),
            pl.BlockSpec((None, d, W_COLS), lambda bi, i: (layer, 0, 0)),
            pl.BlockSpec((TM, 256), lambda bi, i: (i, 0)),
            pl.BlockSpec((TM, 256), lambda bi, i: (i, 0)),
            const((8, 256)),
            const((256, 256)),
            const((128, 256)),
            const((1, 256)),
        ],
        out_specs=(tok(512), tok(256), tok(256), tok(512), tok(128), tok(128), tok(256), tok(256),
                   tok(256), tok(1024)),
        scratch_shapes=[pltpu.VMEM((TM, d), F32)],
        compiler_params=_cparams(("parallel", "arbitrary"), VMEM_LIMIT),
        name="inproj",
    )(x, ctx, mod, norm_w, w_perm, cos_t, sin_t, head_norms, seg_ones, wdec, bdec)


def _gla_kernel(af_ref, ab_ref, laf_ref, lab_ref, ones2_ref, trif_ref, trib_ref, pmf_ref, pmb_ref,
                qmask_ref, stmask_ref, of_ref, ob_ref, st_ref, rb_ref, *, nbatch, nreal):
    @pl.when(pl.program_id(0) == 0)
    def _():
        st_ref[...] = jnp.zeros_like(st_ref)

    @pl.when(pl.program_id(0) < nreal)
    def _():
        lane_head = lax.broadcasted_iota(jnp.int32, (GLA_CHUNK, GLA_HEADS * GLA_DV), 1) // GLA_DV
        shared = (qmask_ref[...], lane_head, stmask_ref[...])
        consts_f = (ones2_ref[...], trif_ref[...], pmf_ref[...]) + shared
        consts_b = (ones2_ref[...], trib_ref[...], pmb_ref[...]) + shared
        units = []
        for bi in range(nbatch):
            units.append(_gla_chunk(af_ref, laf_ref, of_ref, st_ref.at[0], rb_ref.at[0], bi, False, consts_f))
        for bi in range(nbatch):
            units.append(_gla_chunk(ab_ref, lab_ref, ob_ref, st_ref.at[1], rb_ref.at[1], bi, True, consts_b))
        _interleave(units, group=GLA_GROUP)

    @pl.when(pl.program_id(0) >= nreal)
    def _():
        of_ref[...] = jnp.zeros_like(of_ref)
        ob_ref[...] = jnp.zeros_like(ob_ref)


def _gla_chunk(a_ref, la_ref, o_ref, st_ref, rb_ref, bi, rev, consts):
    ones2, tri, pmask, qmask, lane_head, stmask = consts
    c, sb = GLA_CHUNK, GLA_SUB
    nsub = c // sb
    hk = GLA_HEADS * GLA_DK
    hv = GLA_HEADS * GLA_DV

    q = a_ref[bi, :, 0:hk]
    k = a_ref[bi, :, hk:2 * hk]
    v = a_ref[bi, :, 2 * hk:2 * hk + hv]
    la = la_ref[bi]

    la_hi = la.astype(BF16)
    la_lo = (la - la_hi.astype(F32)).astype(BF16)
    cum = _dot(tri, la_hi) + _dot(tri, la_lo)
    cumx = cum - la

    rb_ref[bi, 0] = cum
    rb_ref[bi, 1] = cumx
    rb_ref[bi, 2] = k
    rb_ref[bi, 3] = v[:, 0:hk]
    rb_ref[bi, 4] = v[:, hk:hv]
    yield

    def brow(plane, r, n=sb):
        return jnp.broadcast_to(rb_ref[bi, plane, r:r + 1, :], (n, hk))

    end = 0 if rev else c - 1

    def first_row(blk):
        return sb * blk + (sb - 1 if rev else 0)

    def blk_of(p):
        return nsub - 1 - p if rev else p

    cref_rows = jnp.concatenate([brow(1, first_row(blk)) for blk in range(nsub)], axis=0)

    st = st_ref[bi]
    qi = q * jnp.exp2(cum)
    inter = _dot_nt(qi.astype(BF16), st.astype(BF16))

    qd = q * jnp.exp2(cum - cref_rows)
    q_st = jnp.concatenate([qd] * GLA_HEADS, axis=0) * qmask
    kt, vt = [], []
    for p in range(1, nsub):
        lo, hi = (c - sb * p, c) if rev else (0, sb * p)
        kt.append(k[lo:hi] * jnp.exp2(brow(1, first_row(blk_of(p)), hi - lo) - cum[lo:hi]))
        vt.append(v[lo:hi])
    npad = pmask.shape[1] - sb * (nsub * (nsub - 1) // 2)
    kt.append(jnp.zeros((npad, hk), F32))
    vt.append(jnp.zeros((npad, hv), F32))
    kt = jnp.concatenate(kt, axis=0).astype(BF16)
    vt = jnp.concatenate(vt, axis=0).astype(BF16)
    sc = _dot_nt(q_st.astype(BF16), kt)
    cum_end = rb_ref[bi, 0, end:end + 1, :]
    kd = k * jnp.exp2(cum_end - cum)
    kv = _dot_tn(v.astype(BF16), kd.astype(BF16))

    t_loc = lax.broadcasted_iota(jnp.int32, (sb, hk), 0)
    valid = [(t_loc <= s) if rev else (t_loc >= s) for s in range(sb)]
    es = []
    for blk in range(nsub):
        r0 = sb * blk
        cb = cum[r0:r0 + sb]
        qb = q[r0:r0 + sb]
        for s in range(sb):
            diff = jnp.where(valid[s], cb - brow(0, r0 + s), -jnp.inf)
            es.append(jnp.exp2(diff) * (qb * brow(2, r0 + s)))
    e = jnp.concatenate(es, axis=0).astype(BF16)
    r = _dot(e, ones2)
    yield

    off_st = _dot(sc.astype(BF16) * pmask, vt)
    dgs = []
    for blk in range(nsub):
        acc = None
        for s in range(sb):
            row0 = (blk * sb + s) * sb
            vrow = jnp.concatenate([brow(3, blk * sb + s), brow(4, blk * sb + s)], axis=1)
            term = r[row0:row0 + sb] * vrow
            acc = term if acc is None else acc + term
        dgs.append(acc)
    st_ref[bi] = st * jnp.exp2(cum_end) + kv * stmask
    yield

    off = off_st[(GLA_HEADS - 1) * c:GLA_HEADS * c]
    for h in range(GLA_HEADS - 2, -1, -1):
        off = jnp.where(lane_head == h, off_st[h * c:(h + 1) * c], off)
    o_ref[bi] = (inter + off + jnp.concatenate(dgs, axis=0)).astype(BF16)


def _gla_consts(rev):
    c, sb = GLA_CHUNK, GLA_SUB
    nsub = c // sb
    t = np.arange(c)
    tri = (t[None, :] >= t[:, None]) if rev else (t[None, :] <= t[:, None])
    blk = t // sb
    p_of_t = (nsub - 1 - blk) if rev else blk
    col_p = np.concatenate([np.full(sb * p, p) for p in range(1, nsub)])
    col_p = np.concatenate([col_p, np.full(-col_p.size % 128, -1)])
    pmask = np.tile(p_of_t, GLA_HEADS)[:, None] == col_p[None, :]
    row_h = np.repeat(np.arange(GLA_HEADS), c)
    qmask = row_h[:, None] == (np.arange(GLA_HEADS * GLA_DK) // GLA_DK)[None, :]
    stmask = (np.arange(GLA_HEADS * GLA_DV) // GLA_DV)[:, None] == (np.arange(GLA_HEADS * GLA_DK) // GLA_DK)[None, :]
    f = lambda m: jnp.asarray(m, F32)
    return f(tri).astype(BF16), f(pmask).astype(BF16), f(qmask), f(stmask)


def _gla(a, la, ones2):
    b, t_all, _ = a.shape
    nch = t_all // GLA_CHUNK
    nctx = CTX_LEN // GLA_CHUNK
    nskip = (CTX_PAD - CTX_LEN) // GLA_CHUNK
    tri_f, pm_f, qmask, stmask = _gla_consts(False)
    tri_b, pm_b, _, _ = _gla_consts(True)
    consts = (ones2, tri_f, tri_b, pm_f, pm_b, qmask, stmask)

    nreal = nch - nskip

    def fwd(j):
        return jnp.where(j < nctx, j, jnp.where(j < nreal, j + nskip, nctx + (j - nreal)))

    def bwd(j):
        return jnp.where(j < nctx, nctx - 1 - j, jnp.where(j < nreal, nch - 1 - (j - nctx), nctx + (j - nreal)))

    out = jax.ShapeDtypeStruct((b, t_all, 256), BF16)
    return pl.pallas_call(
        functools.partial(_gla_kernel, nbatch=b, nreal=nreal),
        out_shape=(out, out),
        grid=(nch,),
        in_specs=[pl.BlockSpec((b, GLA_CHUNK, 512), lambda j: (0, fwd(j), 0)),
                  pl.BlockSpec((b, GLA_CHUNK, 512), lambda j: (0, bwd(j), 0)),
                  pl.BlockSpec((b, GLA_CHUNK, 128), lambda j: (0, fwd(j), 0)),
                  pl.BlockSpec((b, GLA_CHUNK, 128), lambda j: (0, bwd(j), 1))]
                 + [pl.BlockSpec(t.shape, lambda j: (0, 0)) for t in consts],
        out_specs=(pl.BlockSpec((b, GLA_CHUNK, 256), lambda j: (0, fwd(j), 0)),
                   pl.BlockSpec((b, GLA_CHUNK, 256), lambda j: (0, bwd(j), 0))),
        scratch_shapes=[pltpu.VMEM((2, b, 256, 128), F32), pltpu.VMEM((2, b, 5, GLA_CHUNK, 128), F32)],
        compiler_params=_cparams(("arbitrary",)),
        name="gla",
    )(a, a, la, la, *consts)


def _fnet_chan_kernel(v_ref, wc_ref, ws_ref, o_ref):
    v = v_ref[0]
    o_ref[0, :, :] = _dot(v, wc_ref[...]).astype(BF16)
    o_ref[1, :, :] = _dot(v, ws_ref[...]).astype(BF16)


def _fnet_chan(bv, wc, ws, row_blk_off, length):
    b = bv.shape[0]
    tl = min(TM, length)
    return pl.pallas_call(
        _fnet_chan_kernel,
        out_shape=jax.ShapeDtypeStruct((2, length, b * 256), BF16),
        grid=(b, length // tl),
        in_specs=[pl.BlockSpec((1, tl, 256), lambda bi, i: (bi, i + row_blk_off, 0)),
                  pl.BlockSpec((256, 256), lambda bi, i: (0, 0)),
                  pl.BlockSpec((256, 256), lambda bi, i: (0, 0))],
        out_specs=pl.BlockSpec((2, tl, 256), lambda bi, i: (0, i, bi)),
        compiler_params=_cparams(("parallel", "arbitrary")),
        name="fnet_chan",
    )(bv, wc, ws)


def _fnet_pos_kernel(d_ref, y_ref, wf_ref, o_ref, acc_ref, *, nb):
    kk = pl.program_id(1)

    @pl.when(kk == 0)
    def _():
        acc_ref[...] = jnp.zeros_like(acc_ref)

    acc_ref[...] += _dot(d_ref[...], y_ref[...])

    @pl.when(kk == pl.num_programs(1) - 1)
    def _():
        wf = wf_ref[...]
        for bi in range(nb):
            o_ref[bi] = _dot(acc_ref[:, bi * 256:(bi + 1) * 256].astype(BF16), wf).astype(BF16)


def _fnet_pos(dmat, y2, wf, nb):
    length, k2 = dmat.shape
    tm = min(512, length)
    tk = min(1024, k2)
    return pl.pallas_call(
        functools.partial(_fnet_pos_kernel, nb=nb),
        out_shape=jax.ShapeDtypeStruct((nb, length, 256), BF16),
        grid=(length // tm, k2 // tk),
        in_specs=[pl.BlockSpec((tm, tk), lambda i, kk: (i, kk)),
                  pl.BlockSpec((tk, nb * 256), lambda i, kk: (kk, 0)),
                  pl.BlockSpec((256, 256), lambda i, kk: (0, 0))],
        out_specs=pl.BlockSpec((nb, tm, 256), lambda i, kk: (0, i, 0)),
        scratch_shapes=[pltpu.VMEM((tm, nb * 256), F32)],
        compiler_params=_cparams(("parallel", "arbitrary"), VMEM_LIMIT),
        name="fnet_pos",
    )(dmat, y2, wf)


def _dft_tables(length):
    j = np.arange(length)
    ang = ((j[:, None] * j[None, :]) % length) * (2.0 * np.pi / length)
    s = 1.0 / math.sqrt(length)
    return jnp.asarray(np.concatenate([np.cos(ang) * s, -np.sin(ang) * s], axis=1), F32).astype(BF16)


def _chan_tables():
    j = np.arange(FNET_GW)
    ang = 2.0 * np.pi * ((j[:, None] * j[None, :]) % FNET_GW) / FNET_GW
    s = 1.0 / math.sqrt(FNET_GW)
    eye = np.eye(BRANCH_W // FNET_GW)
    wc = np.kron(eye, np.cos(ang) * s)
    ws = np.kron(eye, np.sin(ang) * s)
    return jnp.asarray(wc, F32).astype(BF16), jnp.asarray(ws, F32).astype(BF16)


def _fnet(bv, wf, row_blk_off, length):
    b = bv.shape[0]
    wc, ws = _chan_tables()
    y = _fnet_chan(bv, wc, ws, row_blk_off, length)
    y2 = y.reshape(2 * length, b * 256)
    return _fnet_pos(_dft_tables(length), y2, wf, b)


FNET_UNROLL = 16


def _fnet_grid_kernel(v_ref, cg_ref, sg_ref, w1_ref, w2_ref, wf_ref, o_ref, z_ref, g_ref, f_ref, *, rows, row_off):
    length = rows * GRID_W
    v = v_ref[0, row_off:row_off + length, :].astype(F32)
    z_ref[0] = v[:, 0:128].reshape(rows, GRID_W, 128)
    z_ref[1] = v[:, 128:256].reshape(rows, GRID_W, 128)
    wf = wf_ref[...]
    a = _dot(cg_ref[...], wf)
    b = _dot(sg_ref[...], wf)
    chan = jnp.concatenate([jnp.concatenate([a, -b], axis=1), jnp.concatenate([b, a], axis=1)], axis=0).astype(BF16)

    n8 = rows * 8
    for w0 in range(0, GRID_W, 8):
        rhs = jnp.concatenate([z_ref[0, :, w0:w0 + 8, :].reshape(n8, 128),
                               z_ref[1, :, w0:w0 + 8, :].reshape(n8, 128)], axis=1).astype(BF16)
        g = _dot(w1_ref[...], rhs)
        gc = jnp.concatenate([g[0:n8], g[n8:2 * n8]], axis=1).astype(BF16)
        y = _dot(gc, chan)
        for c in range(4):
            g_ref[c, :, w0:w0 + 8, :] = y[:, 128 * c:128 * (c + 1)].reshape(rows, 8, 128)

    def stage_w(ka, carry):
        r0 = pl.multiple_of(ka * GRID_W, GRID_W)
        rhs = jnp.concatenate([jnp.concatenate([g_ref[0, ka], g_ref[1, ka]], axis=1),
                               jnp.concatenate([g_ref[2, ka], g_ref[3, ka]], axis=1)], axis=0).astype(BF16)
        y = _dot(w2_ref[pl.ds(r0, GRID_W), :], rhs)
        f_ref[0, pl.ds(ka, GRID_W, stride=rows), :] = y[:, 0:128]
        f_ref[1, pl.ds(ka, GRID_W, stride=rows), :] = y[:, 128:256]
        return carry

    lax.fori_loop(0, rows, stage_w, 0, unroll=FNET_UNROLL)
    o_ref[0, :, 0:128] = f_ref[0].astype(BF16)
    o_ref[0, :, 128:256] = f_ref[1].astype(BF16)


def _fnet_grid_tables(rows):
    length = rows * GRID_W
    j = np.arange(rows)
    ang = 2.0 * np.pi * ((j[:, None] * j[None, :]) % rows) / rows
    c, s = np.cos(ang), np.sin(ang)
    w1 = np.kron(np.block([[c], [-s]]), np.eye(8))
    w1 = jnp.asarray(w1, F32).astype(BF16)
    ka = np.arange(rows)[:, None, None]
    kb = np.arange(GRID_W)[None, :, None]
    w = np.arange(GRID_W)[None, None, :]
    ang2 = ((w * (ka + rows * kb)) % length) * (2.0 * np.pi / length)
    scale = 1.0 / math.sqrt(length)
    w2 = np.concatenate([np.cos(ang2) * scale, np.sin(ang2) * scale], axis=2)
    return w1, jnp.asarray(w2.reshape(length, 2 * GRID_W), F32).astype(BF16)


def _fnet_grid(bv, wf, seq):
    b, t_all, _ = bv.shape
    rows = seq // GRID_W
    wc, ws = _chan_tables()
    w1, w2 = _fnet_grid_tables(rows)

    def const(shape):
        return pl.BlockSpec(shape, lambda bi: (0,) * len(shape))

    return pl.pallas_call(
        functools.partial(_fnet_grid_kernel, rows=rows, row_off=CTX_PAD),
        out_shape=jax.ShapeDtypeStruct((b, seq, 256), BF16),
        grid=(b,),
        in_specs=[pl.BlockSpec((1, t_all, 256), lambda bi: (bi, 0, 0)), const((256, 256)), const((256, 256)),
                  const((16 * rows, 8 * rows)), const((seq, 2 * GRID_W)), const((256, 256))],
        out_specs=pl.BlockSpec((1, seq, 256), lambda bi: (bi, 0, 0)),
        scratch_shapes=[pltpu.VMEM((2, rows, GRID_W, 128), F32), pltpu.VMEM((4, rows, GRID_W, 128), F32),
                        pltpu.VMEM((2, seq, 128), F32)],
        compiler_params=_cparams(("parallel",), VMEM_LIMIT),
        name="fnet_grid",
    )(bv, wc, ws, w1, w2, wf)


def _stack_heads(qg):
    lane = lax.broadcasted_iota(jnp.int32, qg.shape, 1)
    zero = jnp.zeros_like(qg)
    return jnp.concatenate([jnp.where(lane < HEAD_DIM, qg, zero),
                            jnp.where(lane >= HEAD_DIM, qg, zero)], axis=0)


def _gqa_unstack(o, rows, g):
    lane = lax.broadcasted_iota(jnp.int32, (rows, 2 * HEAD_DIM), 1)
    first, second = o[0:rows], o[rows:2 * rows]
    if g == 0:
        return jnp.where(lane < HEAD_DIM, first, pltpu.roll(second, HEAD_DIM, 1))
    return jnp.where(lane < HEAD_DIM, pltpu.roll(first, HEAD_DIM, 1), second)


def _unstack_heads(o, rows):
    lane = lax.broadcasted_iota(jnp.int32, (rows, 2 * HEAD_DIM), 1)
    return jnp.where(lane < HEAD_DIM, o[0:rows], o[rows:2 * rows])


def _swa_kernel(sink_ref, band_ref, q_ref, kp_ref, kc_ref, kn_ref, kx_ref, vp_ref, vc_ref, vn_ref, vx_ref,
                o_ref, *, nbatch):
    n = pl.program_id(0)
    nb = pl.num_programs(0)
    blk = SWA_BLOCK
    rows = 2 * blk
    col = lax.broadcasted_iota(jnp.int32, (1, 3 * blk), 1)
    col_lo = jnp.where(n > 0, 0, blk)
    col_hi = jnp.where(n < nb - 1, 3 * blk, 2 * blk)
    edge = jnp.where((col >= col_lo) & (col < col_hi), 0.0, -jnp.inf)
    bias = band_ref[...] + edge
    head_row = lax.broadcasted_iota(jnp.int32, (rows, 1), 0) < blk

    def unit(bi, g):
        q_st = jnp.concatenate([q_ref[bi, :, 256 * g:256 * g + 128],
                                q_ref[bi, :, 256 * g + 128:256 * g + 256]], axis=0)
        kw = jnp.concatenate([kp_ref[bi], kc_ref[bi], kn_ref[bi]], axis=0)
        s_loc = _dot_nt(q_st, kw) + bias
        s_ctx = _dot_nt(q_st, kx_ref[bi])
        yield
        vw = jnp.concatenate([vp_ref[bi], vc_ref[bi], vn_ref[bi]], axis=0)
        sink = jnp.where(head_row, sink_ref[2 * g], sink_ref[2 * g + 1])
        m = jnp.maximum(jnp.maximum(jnp.max(s_loc, axis=-1, keepdims=True),
                                    jnp.max(s_ctx, axis=-1, keepdims=True)), sink)
        p_loc = jnp.exp2(s_loc - m)
        p_ctx = jnp.exp2(s_ctx - m)
        den = (jnp.sum(p_loc, axis=-1, keepdims=True) + jnp.sum(p_ctx, axis=-1, keepdims=True)
               + jnp.exp2(sink - m))
        o = _dot(p_loc.astype(BF16), vw) + _dot(p_ctx.astype(BF16), vx_ref[bi])
        yield
        o_ref[bi, :, 128 * g:128 * (g + 1)] = _gqa_unstack(o / den, blk, g).astype(BF16)

    _interleave([unit(bi, g) for bi in range(nbatch) for g in range(2)], group=ATTN_GROUP)


def _swa_band():
    row = np.arange(2 * SWA_BLOCK)[:, None] % SWA_BLOCK
    col = np.arange(3 * SWA_BLOCK)[None, :]
    ok = np.abs(col - SWA_BLOCK - row) <= SWA_WINDOW
    return jnp.where(jnp.asarray(ok), 0.0, -jnp.inf).astype(F32)


def _swa(sink, cq, ck, cv, seq):
    b = cq.shape[0]
    nb = seq // SWA_BLOCK
    off = CTX_PAD // SWA_BLOCK

    def at(f, width=128):
        return pl.BlockSpec((b, SWA_BLOCK, width), lambda n: (0, f(n) + off, 0))

    prev = at(lambda n: jnp.maximum(n - 1, 0))
    cur = at(lambda n: n)
    nxt = at(lambda n: jnp.minimum(n + 1, nb - 1))
    cx = pl.BlockSpec((b, CTX_LEN, 128), lambda n: (0, 0, 0))
    band = pl.BlockSpec((2 * SWA_BLOCK, 3 * SWA_BLOCK), lambda n: (0, 0))
    return pl.pallas_call(
        functools.partial(_swa_kernel, nbatch=b),
        out_shape=jax.ShapeDtypeStruct((b, seq, 256), BF16),
        grid=(nb,),
        in_specs=[pl.BlockSpec(memory_space=pltpu.SMEM), band, at(lambda n: n, 512),
                  prev, cur, nxt, cx, prev, cur, nxt, cx],
        out_specs=pl.BlockSpec((b, SWA_BLOCK, 256), lambda n: (0, n, 0)),
        compiler_params=_cparams(("arbitrary",)),
        name="swa",
    )(sink, _swa_band(), cq, ck, ck, ck, ck, cv, cv, cv, cv)


NA_ROWS_PER_STEP = 8


def _na_kernel(q_ref, k_ref, v_ref, bias_ref, o_ref, *, rows_total):
    _interleave(_na_units(q_ref, k_ref, v_ref, bias_ref, o_ref, rows_total), group=ATTN_GROUP)


def _na_units(q_ref, k_ref, v_ref, bias_ref, o_ref, rows_total):
    kh = NA_KH
    nk = kh * GRID_W
    def unit(rr, g):
        r = pl.program_id(1) * NA_ROWS_PER_STEP + rr
        row_start = jnp.clip(r - kh // 2, 0, rows_total - kh)
        cls = r - row_start
        k0 = pl.multiple_of(CTX_PAD + row_start * GRID_W, GRID_W)
        qs = slice(GRID_W * rr, GRID_W * (rr + 1))
        ls = slice(128 * g, 128 * (g + 1))
        q_st = _stack_heads(q_ref[0, qs, ls])
        s_nb = _dot_nt(q_st, k_ref[0, pl.ds(k0, nk), ls]) + bias_ref[cls, g]
        s_ctx = _dot_nt(q_st, k_ref[0, 0:CTX_LEN, ls])
        yield
        m = jnp.maximum(jnp.max(s_nb, axis=-1, keepdims=True), jnp.max(s_ctx, axis=-1, keepdims=True))
        p_nb = jnp.exp2(s_nb - m)
        p_ctx = jnp.exp2(s_ctx - m)
        den = jnp.sum(p_nb, axis=-1, keepdims=True) + jnp.sum(p_ctx, axis=-1, keepdims=True)
        o = (_dot(p_nb.astype(BF16), v_ref[0, pl.ds(k0, nk), ls])
             + _dot(p_ctx.astype(BF16), v_ref[0, 0:CTX_LEN, ls]))
        yield
        o_ref[0, qs, ls] = _unstack_heads(o / den, GRID_W).astype(BF16)

    return [unit(rr, g) for rr in range(NA_ROWS_PER_STEP) for g in range(2)]


def _swa_row_units(sink_ref, band_ref, q_ref, kp_ref, kc_ref, kn_ref, kx_ref, vp_ref, vc_ref, vn_ref, vx_ref,
                   o_ref):
    r = pl.program_id(1)
    nsteps = pl.num_programs(1)
    blk = SWA_BLOCK
    nsub = q_ref.shape[1] // blk
    col = lax.broadcasted_iota(jnp.int32, (1, 3 * blk), 1)
    no_prev = jnp.where((col < blk) & (r == 0), -jnp.inf, 0.0)
    no_next = jnp.where((col >= 2 * blk) & (r == nsteps - 1), -jnp.inf, 0.0)
    band = band_ref[...]
    head_row = lax.broadcasted_iota(jnp.int32, (2 * blk, 1), 0) < blk
    k_all = jnp.concatenate([kp_ref[0], kc_ref[0], kn_ref[0]], axis=0)
    v_all = jnp.concatenate([vp_ref[0], vc_ref[0], vn_ref[0]], axis=0)

    def unit(j, g):
        rows = slice(blk * j, blk * (j + 1))
        q_st = jnp.concatenate([q_ref[0, rows, 256 * g:256 * g + 128],
                                q_ref[0, rows, 256 * g + 128:256 * g + 256]], axis=0)
        bias = band + (no_prev if j == 0 else 0.0) + (no_next if j == nsub - 1 else 0.0)
        s_loc = _dot_nt(q_st, k_all[blk * j:blk * (j + 3)]) + bias
        s_ctx = _dot_nt(q_st, kx_ref[0])
        yield
        sink = jnp.where(head_row, sink_ref[2 * g], sink_ref[2 * g + 1])
        m = jnp.maximum(jnp.maximum(jnp.max(s_loc, axis=-1, keepdims=True),
                                    jnp.max(s_ctx, axis=-1, keepdims=True)), sink)
        p_loc = jnp.exp2(s_loc - m)
        p_ctx = jnp.exp2(s_ctx - m)
        den = (jnp.sum(p_loc, axis=-1, keepdims=True) + jnp.sum(p_ctx, axis=-1, keepdims=True)
               + jnp.exp2(sink - m))
        o = _dot(p_loc.astype(BF16), v_all[blk * j:blk * (j + 3)]) + _dot(p_ctx.astype(BF16), vx_ref[0])
        yield
        o_ref[0, rows, 128 * g:128 * (g + 1)] = _gqa_unstack(o / den, blk, g).astype(BF16)

    return [unit(j, g) for j in range(nsub) for g in range(2)]


def _swa_na_kernel(sink_ref, band_ref, cq_ref, kp_ref, kc_ref, kn_ref, kx_ref, vp_ref, vc_ref, vn_ref, vx_ref,
                   dq_ref, dk_ref, dv_ref, bias_ref, co_ref, do_ref, *, rows_total):
    swa = _swa_row_units(sink_ref, band_ref, cq_ref, kp_ref, kc_ref, kn_ref, kx_ref, vp_ref, vc_ref, vn_ref,
                         vx_ref, co_ref)
    na = _na_units(dq_ref, dk_ref, dv_ref, bias_ref, do_ref, rows_total)
    units = []
    for i in range(len(swa)):
        units += [na[2 * i], na[2 * i + 1], swa[i]]
    _interleave(units, group=SWA_NA_GROUP)


def _swa_na(sink, cq, ck, cv, dq, dk, dv, bias_tab, layer, seq):
    b, t_all, _ = dq.shape
    rows_total = seq // GRID_W
    qrows = NA_ROWS_PER_STEP * GRID_W
    assert CTX_PAD % qrows == 0 and seq % qrows == 0 and qrows % SWA_BLOCK == 0
    nsteps = seq // qrows
    off = CTX_PAD // qrows
    per = qrows // SWA_BLOCK
    koff = CTX_PAD // SWA_BLOCK
    nblk = seq // SWA_BLOCK

    def rows_spec(width):
        return pl.BlockSpec((1, qrows, width), lambda bi, r: (bi, r + off, 0))

    prev = pl.BlockSpec((1, SWA_BLOCK, 128), lambda bi, r: (bi, jnp.maximum(per * r - 1, 0) + koff, 0))
    nxt = pl.BlockSpec((1, SWA_BLOCK, 128), lambda bi, r: (bi, jnp.minimum(per * r + per, nblk - 1) + koff, 0))
    cx = pl.BlockSpec((1, CTX_LEN, 128), lambda bi, r: (bi, 0, 0))
    full = pl.BlockSpec((1, t_all, 256), lambda bi, r: (bi, 0, 0))
    out = jax.ShapeDtypeStruct((b, seq, 256), BF16)
    out_spec = pl.BlockSpec((1, qrows, 256), lambda bi, r: (bi, r, 0))
    return pl.pallas_call(
        functools.partial(_swa_na_kernel, rows_total=rows_total),
        out_shape=(out, out),
        grid=(b, nsteps),
        in_specs=[pl.BlockSpec(memory_space=pltpu.SMEM),
                  pl.BlockSpec((2 * SWA_BLOCK, 3 * SWA_BLOCK), lambda bi, r: (0, 0)),
                  rows_spec(512), prev, rows_spec(128), nxt, cx, prev, rows_spec(128), nxt, cx,
                  rows_spec(256), full, full,
                  pl.BlockSpec((None,) + bias_tab.shape[1:], lambda bi, r: (layer, 0, 0, 0, 0))],
        out_specs=(out_spec, out_spec),
        compiler_params=_cparams(("parallel", "arbitrary"), VMEM_LIMIT),
        name="swa_na",
    )(sink, _swa_band(), cq, ck, ck, ck, ck, cv, cv, cv, cv, dq, dk, dv, bias_tab)


def _bias_expand_kernel(rb_ref, oh_ref, o_ref):
    x = rb_ref[...]
    oh = oh_ref[...]
    hi = x.astype(BF16)
    r1 = x - hi.astype(F32)
    mid = r1.astype(BF16)
    lo = (r1 - mid.astype(F32)).astype(BF16)
    o_ref[...] = (_dot(hi, oh) + _dot(mid, oh) + _dot(lo, oh)) * LOG2E


def _na_bias_table(rel_bias):
    ndy, ndx = 2 * NA_KH - 1, 2 * NA_KW - 1
    cq = np.arange(GRID_W)
    col_start = np.clip(cq - NA_KW // 2, 0, GRID_W - NA_KW)
    col_ok = (cq[None, :] >= col_start[:, None]) & (cq[None, :] < col_start[:, None] + NA_KW)
    qi = jnp.arange(GRID_W, dtype=jnp.int32)
    dx = jnp.clip(qi[None, :] - qi[:, None], -(NA_KW - 1), NA_KW - 1) + (NA_KW - 1)
    onehot = (jnp.arange(128, dtype=jnp.int32)[:, None] == dx.reshape(1, GRID_W * GRID_W)).astype(BF16)
    depth = rel_bias.shape[0]
    rb = jnp.zeros((depth, 64, 128), F32).at[:, 0:NA_HEADS * ndy, 0:ndx].set(
        rel_bias.reshape(depth, NA_HEADS * ndy, ndx))
    p = pl.pallas_call(
        _bias_expand_kernel,
        out_shape=jax.ShapeDtypeStruct((depth, 64, GRID_W * GRID_W), F32),
        grid=(depth,),
        in_specs=[pl.BlockSpec((None, 64, 128), lambda l: (l, 0, 0)),
                  pl.BlockSpec((128, GRID_W * GRID_W), lambda l: (0, 0))],
        out_specs=pl.BlockSpec((None, 64, GRID_W * GRID_W), lambda l: (l, 0, 0)),
        name="na_bias_expand",
    )(rb, onehot)
    p = p[:, 0:NA_HEADS * ndy].reshape(depth, NA_HEADS, ndy, GRID_W, GRID_W)
    p = jnp.where(col_ok[None, None, None], p, -jnp.inf)
    tabs = jnp.stack([p[:, :, NA_KH - 1 - c:2 * NA_KH - 1 - c] for c in range(NA_KH)], axis=1)
    tabs = tabs.transpose(0, 1, 2, 4, 3, 5)
    return tabs.reshape(depth, NA_KH, NA_HEADS // 2, 2 * GRID_W, NA_KH * GRID_W)


def _na(dq, dk, dv, bias_tab, seq):
    b, t_all, _ = dq.shape
    rows_total = seq // GRID_W
    qrows = NA_ROWS_PER_STEP * GRID_W
    assert CTX_PAD % qrows == 0 and rows_total % NA_ROWS_PER_STEP == 0
    off = CTX_PAD // qrows
    full = pl.BlockSpec((1, t_all, 256), lambda bi, r: (bi, 0, 0))
    return pl.pallas_call(
        functools.partial(_na_kernel, rows_total=rows_total),
        out_shape=jax.ShapeDtypeStruct((b, seq, 256), BF16),
        grid=(b, rows_total // NA_ROWS_PER_STEP),
        in_specs=[pl.BlockSpec((1, qrows, 256), lambda bi, r: (bi, r + off, 0)), full, full,
                  pl.BlockSpec(bias_tab.shape, lambda bi, r: (0, 0, 0, 0))],
        out_specs=pl.BlockSpec((1, qrows, 256), lambda bi, r: (bi, r, 0)),
        compiler_params=_cparams(("parallel", "arbitrary"), VMEM_LIMIT),
        name="na",
    )(dq, dk, dv, bias_tab)


def _ctx_attn_kernel(sink_ref, q_ref, k_ref, v_ref, o_ref, *, gqa):
    n = CTX_LEN
    head_row = lax.broadcasted_iota(jnp.int32, (2 * n, 1), 0) < n
    for g in range(2):
        ls = slice(128 * g, 128 * (g + 1))
        if gqa:
            q_st = jnp.concatenate([q_ref[0, :, 256 * g:256 * g + 128],
                                    q_ref[0, :, 256 * g + 128:256 * g + 256]], axis=0)
            k, v = k_ref[0], v_ref[0]
        else:
            q_st = _stack_heads(q_ref[0, :, ls])
            k, v = k_ref[0, :, ls], v_ref[0, :, ls]
        s = _dot_nt(q_st, k)
        m = jnp.max(s, axis=-1, keepdims=True)
        if gqa:
            sink = jnp.where(head_row, sink_ref[2 * g], sink_ref[2 * g + 1])
            m = jnp.maximum(m, sink)
        p = jnp.exp2(s - m)
        den = jnp.sum(p, axis=-1, keepdims=True)
        if gqa:
            den = den + jnp.exp2(sink - m)
        o = _dot(p.astype(BF16), v) / den
        o_ref[0, :, ls] = (_gqa_unstack(o, n, g) if gqa else _unstack_heads(o, n)).astype(BF16)


def _ctx_attn(sink, q, k, v, gqa):
    b = q.shape[0]

    def blk(arr):
        return pl.BlockSpec((1, CTX_LEN, arr.shape[2]), lambda bi: (bi, 0, 0))

    return pl.pallas_call(
        functools.partial(_ctx_attn_kernel, gqa=gqa),
        out_shape=jax.ShapeDtypeStruct((b, CTX_LEN, 256), BF16),
        grid=(b,),
        in_specs=[pl.BlockSpec(memory_space=pltpu.SMEM), blk(q), blk(k), blk(v)],
        out_specs=pl.BlockSpec((1, CTX_LEN, 256), lambda bi: (bi, 0, 0)),
        compiler_params=_cparams(("parallel",)),
        name="ctx_attn_sink" if gqa else "ctx_attn",
    )(sink, q, k, v)


def _outproj_kernel(x_ref, of_ref, ob_ref, fz_ref, co_ref, do_ref, g_ref, mod_ref, gn_ref, seg_ref,
                    w_ref, o_ref):
    a = of_ref[0].astype(F32) + ob_ref[0].astype(F32)
    ms = _dot((a * a).astype(BF16), seg_ref[...]) * (1.0 / GLA_DV)
    a = a * lax.rsqrt(ms + EPS) * gn_ref[...]
    g = g_ref[0].astype(F32)
    mixed =jnp.concatenate([a * g[:, 0:256], fz_ref[0].astype(F32) * g[:, 256:512],
                             co_ref[0].astype(F32) * g[:, 512:768], do_ref[0].astype(F32) * g[:, 768:1024]],
                            axis=1)
    y = _dot(mixed.astype(BF16), w_ref[...])
    o_ref[0] = x_ref[0] + mod_ref[0, 2:3, :] * y


def _outproj(x, o_f, o_b, fz, c_o, d_o, g, mod, gla_norm, seg_ones, w_out, is_ctx):
    b, rows, d = x.shape
    tm = CTX_LEN if is_ctx else TM
    nblk = rows // tm
    comb = 0 if is_ctx else CTX_PAD // tm
    nbatch = b

    def own(width):
        return pl.BlockSpec((1, tm, width), lambda bi, i: (bi, i, 0))

    def combined(width):
        return pl.BlockSpec((1, tm, width), lambda bi, i: (bi, i + comb, 0))

    def const(shape):
        return pl.BlockSpec(shape, lambda bi, i: (0,) * len(shape))

    mod_spec = pl.BlockSpec((1, 3, d), (lambda bi, i: (nbatch, 0, 0)) if is_ctx else (lambda bi, i: (bi, 0, 0)))
    return pl.pallas_call(
        _outproj_kernel,
        out_shape=jax.ShapeDtypeStruct((b, rows, d), F32),
        grid=(b, nblk),
        in_specs=[own(d), combined(256), combined(256), own(256), own(256), own(256), combined(1024),
                  mod_spec, const((1, 256)), const((256, 256)), const((d, d))],
        out_specs=own(d),
        compiler_params=_cparams(("parallel", "arbitrary"), VMEM_LIMIT),
        name="outproj_ctx" if is_ctx else "outproj",
    )(x, o_f, o_b, fz, c_o, d_o, g, mod, gla_norm, seg_ones, w_out)


_W_IN_MOVES = (
    (C_A, 0, 512), (C_BV, 800, 256), (C_CQ, 1312, 256),
    (C_CK, 1568, 128), (C_CV, 1696, 128),
    (C_DQ, 2080, 256), (C_DK, 2336, 256), (C_DV, 2592, 256),
    (C_G, 512, 256), (C_G + 256, 1056, 256), (C_G + 512, 1824, 256), (C_G + 768, 2848, 256),
    (C_LR, 768, 2 * GLA_RANK),
)


def _w_in_prep_kernel(wt_ref, o_ref):
    for dst, src, width in _W_IN_MOVES:
        o_ref[0, :, dst:dst + width] = wt_ref[0, src:src + width, :].T.astype(BF16)
    pad0 = C_LR + 2 * GLA_RANK
    o_ref[0, :, pad0:W_COLS] = jnp.zeros((o_ref.shape[1], W_COLS - pad0), BF16)


def _permute_w_in(w_all):
    depth, d, n = w_all.shape
    kblk = 256
    return pl.pallas_call(
        _w_in_prep_kernel,
        out_shape=jax.ShapeDtypeStruct((depth, d, W_COLS), BF16),
        grid=(depth, d // kblk),
        in_specs=[pl.BlockSpec((1, n, kblk), lambda l, i: (l, 0, i))],
        out_specs=pl.BlockSpec((1, kblk, W_COLS), lambda l, i: (l, i, 0)),
        compiler_params=_cparams(("parallel", "parallel")),
        name="w_in_prep",
    )(jnp.swapaxes(w_all, 1, 2))


def _rope_tables(seq):
    t = np.arange(seq)
    axis_dim = HEAD_DIM // 2
    inv = ROPE_BASE ** (-np.arange(0, axis_dim, 2, dtype=np.float64) / axis_dim)
    row = (t // GRID_W).astype(np.float64)
    col = (t % GRID_W).astype(np.float64)
    ang = np.concatenate([row[:, None] * inv, row[:, None] * inv, col[:, None] * inv, col[:, None] * inv], axis=1)
    cos = np.cos(ang)
    sign = np.concatenate([-np.ones(16), np.ones(16), -np.ones(16), np.ones(16)])
    sin = np.sin(ang) * sign
    cos = np.concatenate([np.ones((CTX_PAD, HEAD_DIM)), cos], axis=0)
    sin = np.concatenate([np.zeros((CTX_PAD, HEAD_DIM)), sin], axis=0)
    return (jnp.asarray(np.tile(cos, (1, 4)), F32), jnp.asarray(np.tile(sin, (1, 4)), F32))


def _seg_ones(width, seg):
    i = np.arange(width)
    return jnp.asarray((i[:, None] // seg) == (i[None, :] // seg), BF16)


def _gla_ones2():
    r = np.arange(GLA_HEADS * GLA_DK) // GLA_DK
    c = np.arange(GLA_HEADS * GLA_DV) // GLA_DV
    return jnp.asarray(r[:, None] == c[None, :], BF16)


def _layer(x, ctx, mod, need_ctx, tables, norm_w, w_in_perm, layer, w_out, gla_dec_w, gla_dec_b,
           gla_out_norm, fnet_w, swa_q_norm, swa_k_norm, swa_sink, na_q_norm, na_k_norm, na_bias_all):
    b, seq, d = x.shape
    cos_t, sin_t, seg64, ones2 = tables

    qs = HEAD_DIM ** -0.5 * LOG2E
    swa_sink = swa_sink * LOG2E
    head_norms = jnp.stack([jnp.tile(swa_q_norm, 4) * qs, jnp.tile(swa_k_norm, 4),
                            jnp.tile(na_q_norm, 4) * qs, jnp.tile(na_k_norm, 4)])
    head_norms = jnp.concatenate([head_norms, jnp.zeros((4, 256), F32)], axis=0)
    wdec = jnp.zeros((128, 256), F32)
    wdec = wdec.at[0:GLA_RANK, 0:128].set(gla_dec_w[0]).at[GLA_RANK:2 * GLA_RANK, 128:256].set(gla_dec_w[1])
    bdec = gla_dec_b.reshape(1, 256)

    a, la, bv, cq, ck, cv, dq, dk, dv, g = _inproj(
        x, ctx, mod, norm_w[None, :], w_in_perm, layer, cos_t, sin_t, head_norms, seg64,
        wdec.astype(BF16), bdec)

    o_f, o_b = _gla(a, la, ones2)
    wf = fnet_w.astype(BF16)
    fz = _fnet_grid(bv, wf, seq)
    c_o, d_o = _swa_na(swa_sink, cq, ck, cv, dq, dk, dv, na_bias_all, layer, seq)

    gn = jnp.tile(gla_out_norm, 4)[None, :]
    wo = w_out.astype(BF16)
    x_new = _outproj(x, o_f, o_b, fz, c_o, d_o, g, mod, gn, seg64, wo, is_ctx=False)
    ctx_new = None
    if need_ctx:
        fzc = _fnet(bv, wf, 0, CTX_LEN)
        c_oc = _ctx_attn(swa_sink, cq, ck, cv, gqa=True)
        d_oc = _ctx_attn(swa_sink, dq, dk, dv, gqa=False)
        ctx_new = _outproj(ctx, o_f, o_b, fzc, c_oc, d_oc, g, mod, gn, seg64, wo, is_ctx=True)
    return x_new, ctx_new


def kernel(x, c, ctx, c_ctx, norm_w, ada_w, ada_b, w_in, w_out, gla_dec_w, gla_dec_b, gla_out_norm,
           fnet_w, swa_q_norm, swa_k_norm, swa_sink, na_q_norm, na_k_norm, na_rel_bias):
    b, seq, d = x.shape
    depth = norm_w.shape[0]
    assert ctx.shape[1] == CTX_LEN and seq % TM == 0 and seq % (NA_KH * GRID_W) == 0
    tables = _rope_tables(seq) + (_seg_ones(256, HEAD_DIM), _gla_ones2())
    cc = jnp.concatenate([c, c_ctx[None, :], jnp.zeros((8 - (b + 1) % 8, d), F32)], axis=0)
    mod = _ada(cc, ada_w, ada_b[:, None, :])
    w_in_perm = _permute_w_in(w_in)
    na_bias_all = _na_bias_table(na_rel_bias)
    for i in range(depth):
        x, ctx = _layer(x, ctx, mod[i], i < depth - 1, tables, norm_w[i], w_in_perm, i, w_out[i],
                        gla_dec_w[i], gla_dec_b[i], gla_out_norm[i], fnet_w[i], swa_q_norm[i],
                        swa_k_norm[i], swa_sink[i], na_q_norm[i], na_k_norm[i], na_bias_all)
    return x
```
